```python
import math
import jax
import jax.numpy as jnp
from jax import lax
import numpy as np

D_MODEL = 1024
BATCH = 4
SEQ = 4096
DEPTH = 1

HG_HEADS = 4
HG_DK = 128
HG_DV = 128
HG_WIDTH = HG_HEADS * HG_DK
HG_CHUNK = 64
ATT_BRANCHES = ((128, 1), (512, 4), (2048, 16))
ATT_HEADS_PER_BRANCH = 4
ATT_HEAD_DIM = 64
ATT_HEADS = ATT_HEADS_PER_BRANCH * len(ATT_BRANCHES)
ATT_WIDTH = ATT_HEADS * ATT_HEAD_DIM
ATT_OUT_WIDTH = ATT_HEADS_PER_BRANCH * ATT_HEAD_DIM
REL_BUCKETS = 32
REL_MAX_DIST = 2048
N_EXPERTS = 256
TOP_K = 8
N_GROUPS = 8
TOPK_GROUPS = 4
EXPERT_DIM = 256
SHARED_DIM = 256
ROUTED_SCALE = 2.5
MOE_BLOCK = 128
DN_ALPHA = (2 * DEPTH) ** 0.25
DN_BETA = (8 * DEPTH) ** -0.25
LN_EPS = 1e-5
RMS_EPS = 1e-6
IN_WIDTH = 4 * HG_WIDTH + 3 * ATT_WIDTH
IN_SPLITS = (HG_WIDTH, 2 * HG_WIDTH, 3 * HG_WIDTH, 4 * HG_WIDTH,
             4 * HG_WIDTH + ATT_WIDTH, 4 * HG_WIDTH + 2 * ATT_WIDTH)
MIX_OUT = HG_WIDTH + ATT_OUT_WIDTH

kernel_name = 'hybrid_hgrn2_dilated_attn_moe_deepnorm'


def _layer_norm(x, gain=None, bias=None):
    xf = x.astype(jnp.float32)
    mu = xf.mean(-1, keepdims=True)
    var = jnp.square(xf - mu).mean(-1, keepdims=True)
    y = (xf - mu) * lax.rsqrt(var + LN_EPS)
    if gain is not None:
        y = y * gain.astype(jnp.float32) + bias.astype(jnp.float32)
    return y.astype(x.dtype)


def _t5_causal_bucket(dist):
    max_exact = REL_BUCKETS // 2
    n = jnp.maximum(dist, 0)
    nf = jnp.maximum(n, 1).astype(jnp.float32)
    large = max_exact + (jnp.log(nf / max_exact) / math.log(REL_MAX_DIST / max_exact)
                         * (REL_BUCKETS - max_exact)).astype(jnp.int32)
    large = jnp.minimum(large, REL_BUCKETS - 1)
    return jnp.where(n < max_exact, n, large)


def _hgrn2(q, f_logit, i, g, lb, norm_w):
    B, S, H, DK = q.shape
    DV = i.shape[-1]
    f32 = jnp.float32
    lb = lb.reshape(H, DK).astype(f32)
    z = f_logit.astype(f32)
    log_f = jnp.log(lb + (1.0 - lb) * jax.nn.sigmoid(z))
    k = (1.0 - lb) * jax.nn.sigmoid(-z)
    nc = S // HG_CHUNK

    def chunks(t):
        return t.astype(f32).reshape(B, nc, HG_CHUNK, H, t.shape[-1]).transpose(1, 0, 3, 2, 4)

    causal = jnp.tril(jnp.ones((HG_CHUNK, HG_CHUNK), bool))

    def step(state, xs):
        qc, kc, ic, lfc = xs
        b = jnp.cumsum(lfc, axis=2)
        inter = jnp.einsum('bhtd,bhde->bhte', qc * jnp.exp(b), state)
        diff = b[:, :, :, None, :] - b[:, :, None, :, :]
        decay = jnp.exp(jnp.where(causal[:, :, None], diff, -jnp.inf))
        scores = jnp.einsum('bhtd,bhsd,bhtsd->bhts', qc, kc, decay)
        intra = jnp.einsum('bhts,bhse->bhte', scores, ic)
        b_last = b[:, :, -1:]
        new_state = (jnp.exp(b_last[:, :, 0])[..., None] * state
                     + jnp.einsum('bhsd,bhse->bhde', kc * jnp.exp(b_last - b), ic))
        return new_state, inter + intra

    state0 = jnp.zeros((B, H, DK, DV), f32)
    _, o = lax.scan(step, state0, (chunks(q), chunks(k), chunks(i), chunks(log_f)))
    o = o.transpose(1, 0, 3, 2, 4).reshape(B, S, H, DV)
    o = o * lax.rsqrt(jnp.mean(o * o, -1, keepdims=True) + RMS_EPS)
    o = o.reshape(B, S, H * DV) * norm_w.astype(f32) * jax.nn.silu(g.astype(f32))
    return o.astype(q.dtype)


def _dilated_branch(q, k, v, rel_bias, window, dilation):
    B, S, H, Dh = q.shape
    W = window // dilation
    span = dilation * W
    S_pad = -(-S // span) * span
    L = S_pad // dilation
    nb = L // W

    def to_blocks(t):
        t = jnp.pad(t, ((0, 0), (0, S_pad - S), (0, 0), (0, 0)))
        t = t.reshape(B, L, dilation, H, Dh).transpose(0, 2, 3, 1, 4)
        return t.reshape(B, dilation, H, nb, W, Dh)

    def with_prev(t):
        prev = jnp.concatenate([jnp.zeros_like(t[:, :, :, :1]), t[:, :, :, :-1]], axis=3)
        return jnp.concatenate([prev, t], axis=4)

    qb = to_blocks(q)
    kk = with_prev(to_blocks(k))
    vv = with_prev(to_blocks(v))
    qi = jnp.arange(W)[:, None]
    ki = jnp.arange(2 * W)[None, :]
    m = W + qi - ki
    band = (m >= 0) & (m <= W)
    blk = jnp.arange(nb)[:, None, None]
    valid = band[None] & ((blk > 0) | (ki >= W)[None])
    bias = rel_bias[_t5_causal_bucket(m * dilation)].astype(jnp.float32).transpose(2, 0, 1)
    s = jnp.einsum('brhnqd,brhnkd->brhnqk', qb, kk).astype(jnp.float32) * (Dh ** -0.5)
    s = jnp.where(valid[None, None, None], s + bias[None, None, :, None], -jnp.inf)
    mx = s.max(-1, keepdims=True)
    p = jnp.exp(s - mx)
    l = p.sum(-1, keepdims=True)
    o = jnp.einsum('brhnqk,brhnkd->brhnqd', (p / l).astype(v.dtype), vv)
    lse = (mx + jnp.log(l))[..., 0]

    def from_blocks(t):
        extra = t.shape[5:]
        t = jnp.moveaxis(t.reshape((B, dilation, H, L) + extra), 3, 1)
        return t.reshape((B, S_pad, H) + extra)[:, :S]

    return from_blocks(o), from_blocks(lse)


def _dilated_mixture(q, k, v, rel_bias):
    B, S, _ = q.shape
    Hb = ATT_HEADS_PER_BRANCH
    outs, lses = [], []
    for g, (window, dilation) in enumerate(ATT_BRANCHES):
        lo, hi = g * Hb * ATT_HEAD_DIM, (g + 1) * Hb * ATT_HEAD_DIM
        qg = q[..., lo:hi].reshape(B, S, Hb, ATT_HEAD_DIM)
        kg = k[..., lo:hi].reshape(B, S, Hb, ATT_HEAD_DIM)
        vg = v[..., lo:hi].reshape(B, S, Hb, ATT_HEAD_DIM)
        o, lse = _dilated_branch(qg, kg, vg, rel_bias[:, g * Hb:(g + 1) * Hb], window, dilation)
        outs.append(o)
        lses.append(lse)
    wts = jax.nn.softmax(jnp.stack(lses, 0), axis=0)
    out = jnp.einsum('gbsh,gbshd->bshd', wts, jnp.stack(outs, 0).astype(jnp.float32))
    return out.reshape(B, S, ATT_OUT_WIDTH).astype(q.dtype)


def _moe(h, w_router, router_bias, w_e_gate, w_e_up, w_e_down, w_sh_gate, w_sh_up, w_sh_down):
    B, S, D = h.shape
    T = B * S
    xt = h.reshape(T, D)
    scores = jax.nn.sigmoid((xt @ w_router).astype(jnp.float32))
    biased = scores + router_bias.astype(jnp.float32)
    grp_score = lax.top_k(biased.reshape(T, N_GROUPS, N_EXPERTS // N_GROUPS), 2)[0].sum(-1)
    _, grp_idx = lax.top_k(grp_score, TOPK_GROUPS)
    grp_mask = jnp.any(grp_idx[..., None] == jnp.arange(N_GROUPS), axis=1)
    masked = jnp.where(jnp.repeat(grp_mask, N_EXPERTS // N_GROUPS, axis=1), biased, -jnp.inf)
    _, top_idx = lax.top_k(masked, TOP_K)
    top_w = jnp.take_along_axis(scores, top_idx, axis=1)
    top_w = top_w / top_w.sum(-1, keepdims=True) * ROUTED_SCALE
    A = T * TOP_K
    flat_e = top_idx.reshape(A)
    flat_w = top_w.reshape(A)
    order = jnp.argsort(flat_e)
    sorted_e = flat_e[order]
    counts = jnp.bincount(flat_e, length=N_EXPERTS)
    start = jnp.cumsum(counts) - counts
    padded = (counts + MOE_BLOCK - 1) // MOE_BLOCK * MOE_BLOCK
    pend = jnp.cumsum(padded)
    pstart = pend - padded
    dest = pstart[sorted_e] + jnp.arange(A, dtype=jnp.int32) - start[sorted_e]
    n_blocks = -(-A // MOE_BLOCK) + N_EXPERTS
    R = n_blocks * MOE_BLOCK
    row_token = jnp.zeros((R,), jnp.int32).at[dest].set((order // TOP_K).astype(jnp.int32))
    row_w = jnp.zeros((R,), jnp.float32).at[dest].set(flat_w[order])
    block_e = jnp.minimum(jnp.searchsorted(pend, jnp.arange(n_blocks) * MOE_BLOCK, side='right'),
                          N_EXPERTS - 1).astype(jnp.int32)

    def expert_block(args):
        e, tok, w = args
        xb = xt[tok]
        hb = jax.nn.silu(xb @ w_e_gate[e]) * (xb @ w_e_up[e])
        yb = hb @ w_e_down[e]
        return yb * w[:, None].astype(yb.dtype)

    y_rows = lax.map(expert_block, (block_e, row_token.reshape(n_blocks, MOE_BLOCK),
                                    row_w.reshape(n_blocks, MOE_BLOCK)))
    routed = jax.ops.segment_sum(y_rows.reshape(R, D), row_token, num_segments=T)
    shared = (jax.nn.silu(xt @ w_sh_gate) * (xt @ w_sh_up)) @ w_sh_down
    return (routed + shared).reshape(B, S, D)


def setup_inputs(seed: int = 0) -> dict:
    key = jax.random.key(seed)
    ks = jax.random.split(key, 24)
    f32 = jnp.float32

    def nrm(k, shape, scale):
        return jax.random.normal(k, shape, f32) * scale

    D = D_MODEL
    return {
        'x': nrm(ks[0], (BATCH, SEQ, D), 1.0),
        'c': nrm(ks[1], (BATCH, D), 1.0),
        'w_ada': nrm(ks[2], (DEPTH, D, 6 * D), 0.5 * D ** -0.5),
        'b_ada': nrm(ks[3], (DEPTH, 6 * D), 0.02),
        'w_in': nrm(ks[4], (DEPTH, D, IN_WIDTH), D ** -0.5),
        'hg_lower_bound': nrm(ks[5], (DEPTH + 1, HG_WIDTH), 0.5),
        'hg_norm_w': 1.0 + nrm(ks[6], (DEPTH, HG_WIDTH), 0.02),
        'rel_bias': nrm(ks[7], (REL_BUCKETS, ATT_HEADS), 0.5),
        'w_out': nrm(ks[8], (DEPTH, MIX_OUT, D), MIX_OUT ** -0.5 * DN_BETA),
        'ln1_g': 1.0 + nrm(ks[9], (DEPTH, D), 0.02),
        'ln1_b': nrm(ks[10], (DEPTH, D), 0.02),
        'w_router': nrm(ks[11], (DEPTH, D, N_EXPERTS), D ** -0.5),
        'router_bias': nrm(ks[12], (DEPTH, N_EXPERTS), 0.01),
        'w_e_gate': nrm(ks[13], (DEPTH, N_EXPERTS, D, EXPERT_DIM), D ** -0.5),
        'w_e_up': nrm(ks[14], (DEPTH, N_EXPERTS, D, EXPERT_DIM), D ** -0.5),
        'w_e_down': nrm(ks[15], (DEPTH, N_EXPERTS, EXPERT_DIM, D), EXPERT_DIM ** -0.5 * DN_BETA),
        'w_sh_gate': nrm(ks[16], (DEPTH, D, SHARED_DIM), D ** -0.5),
        'w_sh_up': nrm(ks[17], (DEPTH, D, SHARED_DIM), D ** -0.5),
        'w_sh_down': nrm(ks[18], (DEPTH, SHARED_DIM, D), SHARED_DIM ** -0.5 * DN_BETA),
        'ln2_g': 1.0 + nrm(ks[19], (DEPTH, D), 0.02),
        'ln2_b': nrm(ks[20], (DEPTH, D), 0.02),
    }


def reference(x, c, w_ada, b_ada, w_in, hg_lower_bound, hg_norm_w, rel_bias, w_out, ln1_g, ln1_b,
              w_router, router_bias, w_e_gate, w_e_up, w_e_down, w_sh_gate, w_sh_up, w_sh_down,
              ln2_g, ln2_b):
    B, S, D = x.shape
    lower_bounds = jnp.cumsum(jax.nn.softmax(hg_lower_bound.astype(jnp.float32), axis=0), axis=0)
    cond = jax.nn.silu(c)
    for l in range(DEPTH):
        mod = (cond @ w_ada[l] + b_ada[l])[:, None, :]
        sh1, sc1, g1, sh2, sc2, g2 = jnp.split(mod, 6, axis=-1)
        h = _layer_norm(x) * (1.0 + sc1) + sh1
        proj = h @ w_in[l]
        hq, hf, hi, hg, aq, ak, av = jnp.split(proj, IN_SPLITS, axis=-1)
        y_hg = _hgrn2(hq.reshape(B, S, HG_HEADS, HG_DK), hf.reshape(B, S, HG_HEADS, HG_DK),
                      hi.reshape(B, S, HG_HEADS, HG_DV), hg, lower_bounds[l], hg_norm_w[l])
        y_att = _dilated_mixture(aq, ak, av, rel_bias)
        mix = jnp.concatenate([y_hg, y_att], axis=-1) @ w_out[l]
        x = _layer_norm(DN_ALPHA * x + g1 * mix, ln1_g[l], ln1_b[l])
        h = _layer_norm(x) * (1.0 + sc2) + sh2
        ffn = _moe(h, w_router[l], router_bias[l], w_e_gate[l], w_e_up[l], w_e_down[l],
                   w_sh_gate[l], w_sh_up[l], w_sh_down[l])
        x = _layer_norm(DN_ALPHA * x + g2 * ffn, ln2_g[l], ln2_b[l])
    return x
```

```python
import functools
import math

import jax
import jax.numpy as jnp
import numpy as np
from jax import lax
from jax.experimental import pallas as pl
from jax.experimental.pallas import tpu as pltpu

HG_HEADS = 4
HG_DK = 128
HG_WIDTH = HG_HEADS * HG_DK
ATT_BRANCHES = ((128, 1), (512, 4), (2048, 16))
ATT_HEADS_PER_BRANCH = 4
ATT_HEAD_DIM = 64
ATT_BW = ATT_HEADS_PER_BRANCH * ATT_HEAD_DIM
ATT_BLOCK = 128
REL_BUCKETS = 32
REL_MAX_DIST = 2048
N_EXPERTS = 256
TOP_K = 8
N_GROUPS = 8
TOPK_GROUPS = 4
ROUTED_SCALE = 2.5
MOE_BLOCK = 128
DEPTH = 1
DN_ALPHA = (2 * DEPTH) ** 0.25
LN_EPS = 1e-5
RMS_EPS = 1e-6

LANES = 128
SUBLANES = 8
VMEM_LIMIT_BYTES = 56 * 1024 * 1024

F32 = jnp.float32
BF16 = jnp.bfloat16
NEG_INF = float("-inf")


def _cparams(sem):
    return pltpu.CompilerParams(dimension_semantics=sem, vmem_limit_bytes=VMEM_LIMIT_BYTES)


def _ln_rows(x):
    mu = jnp.mean(x, axis=-1, keepdims=True)
    xc = x - mu
    var = jnp.mean(xc * xc, axis=-1, keepdims=True)
    return xc * lax.rsqrt(var + LN_EPS)


def _dot(a, b):
    return jnp.dot(a, b, preferred_element_type=F32)


def _dot_nt(a, b):
    return lax.dot_general(a, b, (((1,), (1,)), ((), ())), preferred_element_type=F32)


def _dot_tn(a, b):
    return lax.dot_general(a, b, (((0,), (0,)), ((), ())), preferred_element_type=F32)


def _cast_kernel(w_ref, o_ref):
    o_ref[...] = w_ref[...].astype(o_ref.dtype)


def _cast_bf16(w, rows_per_step):
    r, c = w.shape
    return pl.pallas_call(
        _cast_kernel,
        grid=(r // rows_per_step,),
        in_specs=[pl.BlockSpec((rows_per_step, c), lambda i: (i, 0))],
        out_specs=pl.BlockSpec((rows_per_step, c), lambda i: (i, 0)),
        out_shape=jax.ShapeDtypeStruct((r, c), BF16),
        compiler_params=_cparams(("parallel",)),
        name="cast_bf16",
    )(w)


def _mod_kernel(c_ref, w_ref, b_ref, o_ref):
    c = c_ref[...]
    cond = c * jax.nn.sigmoid(c)
    o_ref[...] = jnp.dot(cond, w_ref[...], preferred_element_type=F32,
                         precision=lax.Precision.HIGHEST) + b_ref[...]


def _modulation(c, w_ada, b_ada):
    bsz, d = c.shape
    n = w_ada.shape[1]
    rows = -(-bsz // SUBLANES) * SUBLANES
    cpad = jnp.zeros((rows, d), F32).at[:bsz].set(c)
    tn = 1024
    out = pl.pallas_call(
        _mod_kernel,
        grid=(n // tn,),
        in_specs=[pl.BlockSpec((rows, d), lambda j: (0, 0)),
                  pl.BlockSpec((d, tn), lambda j: (0, j)),
                  pl.BlockSpec((1, tn), lambda j: (0, j))],
        out_specs=pl.BlockSpec((rows, tn), lambda j: (0, j)),
        out_shape=jax.ShapeDtypeStruct((rows, n), F32),
        compiler_params=_cparams(("parallel",)),
        name="adaln_modulation",
    )(cpad, w_ada, b_ada.reshape(1, n))
    return out[:bsz]


_IN_HG = 4
_IN_ATT = 3 * len(ATT_BRANCHES)


def _inproj_kernel(x_ref, sc_ref, sh_ref, w_ref, *outs):
    x = x_ref[...]
    h = _ln_rows(x) * (1.0 + sc_ref[...]) + sh_ref[...]
    hb = h.astype(BF16)
    col = 0
    for k, o_ref in enumerate(outs):
        width = o_ref.shape[-1]
        y = _dot(hb, w_ref[:, col:col + width])
        if _IN_HG <= k < _IN_HG + len(ATT_BRANCHES):
            y = y * (ATT_HEAD_DIM ** -0.5)
        o_ref[...] = y.astype(o_ref.dtype)
        col += width


def _in_projection(x2, mod4, w_in_bf, seq, tm):
    t, d = x2.shape
    steps_per_batch = seq // tm
    widths = [HG_WIDTH] * _IN_HG + [ATT_BW] * _IN_ATT
    dtypes = [BF16, F32, BF16, BF16] + [BF16] * _IN_ATT
    mod_spec = lambda row: pl.BlockSpec((None, None, 1, d),
                                        lambda i, row=row: (i // steps_per_batch, row, 0, 0))
    outs = pl.pallas_call(
        _inproj_kernel,
        grid=(t // tm,),
        in_specs=[pl.BlockSpec((tm, d), lambda i: (i, 0)),
                  mod_spec(1), mod_spec(0),
                  pl.BlockSpec(w_in_bf.shape, lambda i: (0, 0))],
        out_specs=[pl.BlockSpec((tm, w), lambda i: (i, 0)) for w in widths],
        out_shape=[jax.ShapeDtypeStruct((t, w), dt) for w, dt in zip(widths, dtypes)],
        compiler_params=_cparams(("parallel",)),
        name="ln_in_projection",
    )(x2, mod4, mod4, w_in_bf)
    return outs


HG_CHUNK = 64
HG_SUB = 8
HG_LEVELS = (64, 32, 16)


def _hgrn_chunk(q, z, iv, lb, st_t):
    c = HG_CHUNK
    f = lb + (1.0 - lb) * jax.nn.sigmoid(z)
    lf = jnp.log(f)
    kk = (1.0 - lb) * jax.nn.sigmoid(-z)
    r_i = lax.broadcasted_iota(jnp.int32, (c, c), 0)
    c_i = lax.broadcasted_iota(jnp.int32, (c, c), 1)
    tril = (c_i <= r_i).astype(F32)
    b = jnp.dot(tril, lf, preferred_element_type=F32, precision=lax.Precision.HIGHEST)

    row = lax.broadcasted_iota(jnp.int32, (c, HG_DK), 0)
    scores = jnp.zeros((c, c), F32)
    for m in HG_LEVELS:
        nb = c // m
        b3 = b.reshape(nb, m, HG_DK)
        piv = jnp.broadcast_to(b3[:, m // 2 - 1:m // 2, :], (nb, m, HG_DK)).reshape(c, HG_DK)
        second = (row % m) >= (m // 2)
        qt = jnp.where(second, q * jnp.exp(b - piv), 0.0)
        kt = jnp.where(second, 0.0, kk * jnp.exp(piv - b))
        s_m = _dot_nt(qt.astype(BF16), kt.astype(BF16))
        if nb > 1:
            s_m = jnp.where((r_i // m) == (c_i // m), s_m, 0.0)
        scores = scores + s_m
    sub = HG_SUB
    t_i = lax.broadcasted_iota(jnp.int32, (sub, 1), 0)
    lane = lax.broadcasted_iota(jnp.int32, (sub, c), 1)
    diag_rows = []
    for j in range(c // sub):
        qb = q[j * sub:(j + 1) * sub]
        kb = kk[j * sub:(j + 1) * sub]
        bb = b[j * sub:(j + 1) * sub]
        a_j = jnp.zeros((sub, c), F32)
        for s in range(sub):
            w = qb * kb[s:s + 1] * jnp.exp(bb - bb[s:s + 1])
            col = jnp.sum(w, axis=-1, keepdims=True)
            col = jnp.where(t_i >= s, col, 0.0)
            a_j = jnp.where(lane == j * sub + s, col, a_j)
        diag_rows.append(a_j)
    scores = scores + jnp.concatenate(diag_rows, axis=0)

    stb = st_t.astype(BF16)
    inter = _dot_nt((q * jnp.exp(b)).astype(BF16), stb)
    ivb = iv.astype(BF16)
    o = inter + _dot(scores.astype(BF16), ivb)
    b_last = b[c - 1:c]
    kdec = (kk * jnp.exp(b_last - b)).astype(BF16)
    st_new = st_t * jnp.exp(b_last) + _dot_tn(ivb, kdec)
    return o, st_new


def _hgrn_kernel(q_ref, f_ref, i_ref, g_ref, lbp_ref, nw_ref, o_ref, st_ref):
    @pl.when(pl.program_id(1) == 0)
    def _():
        st_ref[...] = jnp.zeros_like(st_ref)

    lbp = lbp_ref[...]
    e = jnp.exp(lbp - jnp.max(lbp, axis=0, keepdims=True))
    lb_all = e[0:1] / jnp.sum(e, axis=0, keepdims=True)
    outs = []
    for h in range(HG_HEADS):
        sl = slice(h * HG_DK, (h + 1) * HG_DK)
        o, st_new = _hgrn_chunk(q_ref[:, sl].astype(F32), f_ref[:, sl], i_ref[:, sl].astype(F32),
                                lb_all[:, sl], st_ref[h])
        st_ref[h] = st_new
        o = o * lax.rsqrt(jnp.mean(o * o, axis=-1, keepdims=True) + RMS_EPS)
        outs.append(o)
    o_all = jnp.concatenate(outs, axis=-1)
    g = g_ref[...].astype(F32)
    o_ref[...] = (o_all * nw_ref[...] * (g * jax.nn.sigmoid(g))).astype(o_ref.dtype)


def _hgrn2(hq, hf, hi, hg, lb_param, norm_w, bsz, seq):
    t = hq.shape[0]
    nc = seq // HG_CHUNK
    tok = lambda b, n: (b * nc + n, 0)
    spec = pl.BlockSpec((HG_CHUNK, HG_WIDTH), tok)
    return pl.pallas_call(
        _hgrn_kernel,
        grid=(bsz, nc),
        in_specs=[spec, spec, spec, spec,
                  pl.BlockSpec(lb_param.shape, lambda b, n: (0, 0)),
                  pl.BlockSpec((1, HG_WIDTH), lambda b, n: (0, 0))],
        out_specs=spec,
        out_shape=jax.ShapeDtypeStruct((t, HG_WIDTH), BF16),
        scratch_shapes=[pltpu.VMEM((HG_HEADS, HG_DK, HG_DK), F32)],
        compiler_params=_cparams(("parallel", "arbitrary")),
        name="hgrn2_scan",
    )(hq, hf, hi, hg, lb_param, norm_w.reshape(1, HG_WIDTH))


def _t5_bucket_np(dist):
    max_exact = REL_BUCKETS // 2
    n = np.maximum(dist, 0)
    nf = np.maximum(n, 1).astype(np.float32)
    large = max_exact + (np.log(nf / np.float32(max_exact)) / np.float32(math.log(REL_MAX_DIST / max_exact))
                         * np.float32(REL_BUCKETS - max_exact)).astype(np.int32)
    large = np.minimum(large, REL_BUCKETS - 1)
    return np.where(n < max_exact, n, large).astype(np.int32)


def _band_tables():
    w = ATT_BLOCK
    qi = np.arange(w)[:, None]
    ki = np.arange(2 * w)[None, :]
    m = w + qi - ki
    band = (m >= 0) & (m <= w)
    buckets = np.stack([_t5_bucket_np(m * dil) for _, dil in ATT_BRANCHES])
    return buckets, band


def _bias_kernel(rb_ref, bucket_ref, o_ref):
    g = pl.program_id(0)
    w = ATT_BLOCK
    bucket = bucket_ref[...]
    qi = lax.broadcasted_iota(jnp.int32, (w, 2 * w), 0)
    ki = lax.broadcasted_iota(jnp.int32, (w, 2 * w), 1)
    m = w + qi - ki
    band = (m >= 0) & (m <= w)
    for h in range(ATT_HEADS_PER_BRANCH):
        acc = jnp.zeros((w, 2 * w), F32)
        for c in range(REL_BUCKETS):
            acc = jnp.where(bucket == c, rb_ref[c, g * ATT_HEADS_PER_BRANCH + h], acc)
        full = jnp.where(band, acc, NEG_INF)
        o_ref[1, h] = full
        o_ref[0, h] = jnp.where(ki >= w, full, NEG_INF)


def _bias_tables(rel_bias):
    buckets, _ = _band_tables()
    g = len(ATT_BRANCHES)
    w = ATT_BLOCK
    return pl.pallas_call(
        _bias_kernel,
        grid=(g,),
        in_specs=[pl.BlockSpec(memory_space=pltpu.SMEM),
                  pl.BlockSpec((None, w, 2 * w), lambda i: (i, 0, 0))],
        out_specs=pl.BlockSpec((None, 2, ATT_HEADS_PER_BRANCH, w, 2 * w), lambda i: (i, 0, 0, 0, 0)),
        out_shape=jax.ShapeDtypeStruct((g, 2, ATT_HEADS_PER_BRANCH, w, 2 * w), F32),
        compiler_params=_cparams(("parallel",)),
        name="rel_bias_tables",
    )(rel_bias, jnp.asarray(buckets))


def _attn_kernel(q_ref, kp_ref, kc_ref, vp_ref, vc_ref, bias_ref, o_ref, lse_ref):
    n = pl.program_id(2)
    w = ATT_BLOCK
    hb = ATT_HEADS_PER_BRANCH
    q = q_ref[...]
    lane = lax.broadcasted_iota(jnp.int32, (w, ATT_BW), 1) // ATT_HEAD_DIM
    q4 = jnp.concatenate([jnp.where(lane == h, q, jnp.zeros_like(q)) for h in range(hb)], axis=0)
    kk = jnp.concatenate([kp_ref[...], kc_ref[...]], axis=0)
    vv = jnp.concatenate([vp_ref[...], vc_ref[...]], axis=0)
    s4 = _dot_nt(q4, kk)
    bias = bias_ref[jnp.minimum(n, 1)]
    s4 = s4 + bias.reshape(hb * w, 2 * w)
    mx = jnp.max(s4, axis=-1, keepdims=True)
    p = jnp.exp(s4 - mx)
    l = jnp.sum(p, axis=-1, keepdims=True)
    o4 = _dot((p / l).astype(vv.dtype), vv)
    lse4 = mx + jnp.log(l)
    o = jnp.zeros((w, ATT_BW), F32)
    lse = jnp.zeros((w, ATT_BW), F32)
    for h in range(hb):
        o = jnp.where(lane == h, o4[h * w:(h + 1) * w], o)
        lse = jnp.where(lane == h, lse4[h * w:(h + 1) * w], lse)
    o_ref[...] = o.astype(o_ref.dtype)
    lse_ref[...] = lse


def _dilated_attention(q, k, v, bias_g, bsz, seq, dilation):
    t = q.shape[0]
    w = ATT_BLOCK
    l = seq // dilation
    nb = l // w
    view = lambda a: a.reshape(bsz, l, dilation * ATT_BW)
    cur = pl.BlockSpec((None, w, ATT_BW), lambda b, r, n: (b, n, r))
    prev = pl.BlockSpec((None, w, ATT_BW), lambda b, r, n: (b, jnp.maximum(n - 1, 0), r))
    o, lse = pl.pallas_call(
        _attn_kernel,
        grid=(bsz, dilation, nb),
        in_specs=[cur, prev, cur, prev, cur,
                  pl.BlockSpec(bias_g.shape, lambda b, r, n: (0, 0, 0, 0))],
        out_specs=[cur, cur],
        out_shape=[jax.ShapeDtypeStruct((bsz, l, dilation * ATT_BW), BF16),
                   jax.ShapeDtypeStruct((bsz, l, dilation * ATT_BW), F32)],
        compiler_params=_cparams(("parallel", "parallel", "arbitrary")),
        name=f"dilated_attention_d{dilation}",
    )(view(q), view(k), view(k), view(v), view(v), bias_g)
    return o.reshape(t, ATT_BW), lse.reshape(t, ATT_BW)


def _split_bf16(a):
    hi = a.astype(BF16)
    lo = (a - hi.astype(F32)).astype(BF16)
    return hi, lo


def _outproj_kernel(yhg_ref, o1_ref, o2_ref, o3_ref, l1_ref, l2_ref, l3_ref, x_ref,
                    g1_ref, sc2_ref, sh2_ref, wout_ref, lng_ref, lnb_ref, wrt_ref,
                    x1_ref, h2_ref, lgt_ref):
    l1, l2, l3 = l1_ref[...], l2_ref[...], l3_ref[...]
    mx = jnp.maximum(jnp.maximum(l1, l2), l3)
    e1, e2, e3 = jnp.exp(l1 - mx), jnp.exp(l2 - mx), jnp.exp(l3 - mx)
    den = e1 + e2 + e3
    att = (e1 / den) * o1_ref[...].astype(F32) + (e2 / den) * o2_ref[...].astype(F32) \
        + (e3 / den) * o3_ref[...].astype(F32)
    mix = _dot(yhg_ref[...], wout_ref[:HG_WIDTH, :]) + _dot(att.astype(BF16), wout_ref[HG_WIDTH:, :])
    x1 = _ln_rows(DN_ALPHA * x_ref[...] + g1_ref[...] * mix) * lng_ref[...] + lnb_ref[...]
    x1_ref[...] = x1
    h2 = _ln_rows(x1) * (1.0 + sc2_ref[...]) + sh2_ref[...]
    h2_ref[...] = h2
    h_hi, h_lo = _split_bf16(h2)
    w_hi, w_lo = _split_bf16(wrt_ref[...])
    lgt_ref[...] = _dot_nt(w_hi, h_hi) + (_dot_nt(w_hi, h_lo) + _dot_nt(w_lo, h_hi))


def _out_projection(yhg, os_, ls_, x2, mod4, w_out_bf, ln_g, ln_b, w_router_t, seq, tm):
    t, d = x2.shape
    spb = seq // tm
    ne = w_router_t.shape[0]
    row = lambda w: pl.BlockSpec((tm, w), lambda i: (i, 0))
    mod_spec = lambda r: pl.BlockSpec((None, None, 1, d), lambda i, r=r: (i // spb, r, 0, 0))
    full = lambda a: pl.BlockSpec(a.shape, lambda i: (0,) * a.ndim)
    ln_g2, ln_b2 = ln_g.reshape(1, d), ln_b.reshape(1, d)
    return pl.pallas_call(
        _outproj_kernel,
        grid=(t // tm,),
        in_specs=[row(HG_WIDTH)] + [row(ATT_BW)] * 6 + [row(d),
                  mod_spec(2), mod_spec(4), mod_spec(3),
                  full(w_out_bf), full(ln_g2), full(ln_b2), full(w_router_t)],
        out_specs=[row(d), row(d), pl.BlockSpec((ne, tm), lambda i: (0, i))],
        out_shape=[jax.ShapeDtypeStruct((t, d), F32), jax.ShapeDtypeStruct((t, d), F32),
                   jax.ShapeDtypeStruct((ne, t), F32)],
        compiler_params=_cparams(("parallel",)),
        name="merge_outproj_ln",
    )(yhg, *os_, *ls_, x2, mod4, mod4, mod4, w_out_bf, ln_g2, ln_b2, w_router_t)


def _argmax_rows(cur, iota, nrows):
    m = jnp.max(cur, axis=0, keepdims=True)
    idx = jnp.min(jnp.where(cur == m, iota, nrows), axis=0, keepdims=True)
    return m, idx, iota == idx


def _route_kernel(lgt_ref, rb_ref, eidx_ref, w_ref):
    ne = N_EXPERTS
    gsz = ne // N_GROUPS
    tt = lgt_ref.shape[1]
    sc = jax.nn.sigmoid(lgt_ref[...])
    biased = sc + rb_ref[...]
    g3 = biased.reshape(N_GROUPS, gsz, tt)
    io3 = lax.broadcasted_iota(jnp.int32, (N_GROUPS, gsz, tt), 1)
    m1 = jnp.max(g3, axis=1, keepdims=True)
    first = jnp.min(jnp.where(g3 == m1, io3, gsz), axis=1, keepdims=True)
    m2 = jnp.max(jnp.where(io3 == first, NEG_INF, g3), axis=1, keepdims=True)
    gs = (m1 + m2).reshape(N_GROUPS, tt)
    io8 = lax.broadcasted_iota(jnp.int32, (N_GROUPS, tt), 0)
    sel = jnp.zeros((N_GROUPS, tt), jnp.int32)
    cur = gs
    for _ in range(TOPK_GROUPS):
        _, _, pick = _argmax_rows(cur, io8, N_GROUPS)
        sel = jnp.where(pick, 1, sel)
        cur = jnp.where(pick, NEG_INF, cur)
    masked = jnp.where(sel.reshape(N_GROUPS, 1, tt) > 0, g3, NEG_INF).reshape(ne, tt)
    ioe = lax.broadcasted_iota(jnp.int32, (ne, tt), 0)
    cur = masked
    idxs, ws = [], []
    for _ in range(TOP_K):
        _, idx, pick = _argmax_rows(cur, ioe, ne)
        idxs.append(idx)
        ws.append(jnp.sum(jnp.where(pick, sc, 0.0), axis=0, keepdims=True))
        cur = jnp.where(pick, NEG_INF, cur)
    wk = jnp.concatenate(ws, axis=0)
    eidx_ref[...] = jnp.concatenate(idxs, axis=0)
    w_ref[...] = wk / jnp.sum(wk, axis=0, keepdims=True) * ROUTED_SCALE


def _route(logits_t, router_bias, tt):
    ne, t = logits_t.shape
    return pl.pallas_call(
        _route_kernel,
        grid=(t // tt,),
        in_specs=[pl.BlockSpec((ne, tt), lambda i: (0, i)),
                  pl.BlockSpec((ne, 1), lambda i: (0, 0))],
        out_specs=[pl.BlockSpec((TOP_K, tt), lambda i: (0, i))] * 2,
        out_shape=[jax.ShapeDtypeStruct((TOP_K, t), jnp.int32), jax.ShapeDtypeStruct((TOP_K, t), F32)],
        compiler_params=_cparams(("parallel",)),
        name="router_topk",
    )(logits_t, router_bias.reshape(ne, 1))


def _dispatch_plan(eidx, wk):
    k, t = eidx.shape
    a = t * k
    flat_e = eidx.T.reshape(a)
    flat_w = wk.T.reshape(a)
    order = jnp.argsort(flat_e)
    sorted_e = flat_e[order]
    counts = jnp.bincount(flat_e, length=N_EXPERTS)
    start = jnp.cumsum(counts) - counts
    padded = (counts + MOE_BLOCK - 1) // MOE_BLOCK * MOE_BLOCK
    pend = jnp.cumsum(padded)
    pstart = pend - padded
    dest = (pstart[sorted_e] + jnp.arange(a, dtype=jnp.int32) - start[sorted_e]).astype(jnp.int32)
    n_blocks = -(-a // MOE_BLOCK) + N_EXPERTS
    r = n_blocks * MOE_BLOCK
    row_token = jnp.zeros((r,), jnp.int32).at[dest].set((order // k).astype(jnp.int32))
    row_w = jnp.zeros((r,), F32).at[dest].set(flat_w[order])
    block_e = jnp.minimum(jnp.searchsorted(pend, jnp.arange(n_blocks) * MOE_BLOCK, side='right'),
                          N_EXPERTS - 1).astype(jnp.int32)
    dest_flat = jnp.zeros((a,), jnp.int32).at[order].set(dest)
    n_used = (pend[-1] // MOE_BLOCK).astype(jnp.int32).reshape(1)
    return row_token, row_w, block_e, dest_flat, n_used, n_blocks


ROW_TILE = (SUBLANES, LANES)


def _row_copies(src_hbm, idx_ref, buf, sem, n):
    return [pltpu.make_async_copy(src_hbm.at[idx_ref[0, j]], buf.at[j], sem) for j in range(n)]


def _ffn_kernel(be_ref, nu_ref, tokc_ref, tokn_ref, w_ref, h2_hbm, wg_ref, wu_ref, wd_ref,
                y_ref, xbuf, sems):
    i = pl.program_id(0)
    n_used = nu_ref[0]
    slot = i % 2
    nrow = MOE_BLOCK

    def issue(tok_ref, s):
        for cp in _row_copies(h2_hbm, tok_ref, xbuf.at[s], sems.at[s], nrow):
            cp.start()

    @pl.when(jnp.logical_and(i == 0, n_used > 0))
    def _():
        issue(tokc_ref, 0)

    @pl.when(i + 1 < n_used)
    def _():
        issue(tokn_ref, 1 - slot)

    @pl.when(i < n_used)
    def _():
        for cp in _row_copies(h2_hbm, tokc_ref, xbuf.at[slot], sems.at[slot], nrow):
            cp.wait()
        x3 = xbuf[slot]
        x = pltpu.einshape("rcl->r(cl)", x3).astype(BF16)
        g = _dot(x, wg_ref[...].astype(BF16))
        u = _dot(x, wu_ref[...].astype(BF16))
        hm = (g * jax.nn.sigmoid(g)) * u
        y = _dot(hm.astype(BF16), wd_ref[...].astype(BF16)) * w_ref[...]
        y_ref[...] = pltpu.einshape("r(cl)->rcl", y, c=SUBLANES)

    @pl.when(i >= n_used)
    def _():
        y_ref[...] = jnp.zeros_like(y_ref)


def _routed_experts(h2_rows, row_token, row_w, block_e, n_used, n_blocks, wg, wu, wd):
    d = wg.shape[1]
    de = wg.shape[2]
    nb = n_blocks
    tok3 = row_token.reshape(nb, 1, MOE_BLOCK)
    w3 = row_w.reshape(nb, MOE_BLOCK, 1)
    grid_spec = pltpu.PrefetchScalarGridSpec(
        num_scalar_prefetch=2,
        grid=(nb,),
        in_specs=[
            pl.BlockSpec((None, 1, MOE_BLOCK), lambda i, be, nu: (i, 0, 0), memory_space=pltpu.SMEM),
            pl.BlockSpec((None, 1, MOE_BLOCK), lambda i, be, nu: (jnp.minimum(i + 1, nb - 1), 0, 0),
                         memory_space=pltpu.SMEM),
            pl.BlockSpec((None, MOE_BLOCK, 1), lambda i, be, nu: (i, 0, 0)),
            pl.BlockSpec(memory_space=pl.ANY),
            pl.BlockSpec((None, d, de), lambda i, be, nu: (be[i], 0, 0)),
            pl.BlockSpec((None, d, de), lambda i, be, nu: (be[i], 0, 0)),
            pl.BlockSpec((None, de, d), lambda i, be, nu: (be[i], 0, 0)),
        ],
        out_specs=pl.BlockSpec((MOE_BLOCK,) + ROW_TILE,
                               lambda i, be, nu: (jnp.minimum(i, nu[0]), 0, 0)),
        scratch_shapes=[pltpu.VMEM((2, MOE_BLOCK) + ROW_TILE, F32),
                        pltpu.SemaphoreType.DMA((2,))],
    )
    return pl.pallas_call(
        _ffn_kernel,
        grid_spec=grid_spec,
        out_shape=jax.ShapeDtypeStruct(((nb + 1) * MOE_BLOCK,) + ROW_TILE, F32),
        compiler_params=_cparams(("arbitrary",)),
        name="routed_experts",
    )(block_e, n_used, tok3, tok3, w3, h2_rows, wg, wu, wd)


COMBINE_TOKENS = 32


def _final_kernel(dc_ref, dn_ref, x1_ref, h2_ref, g2_ref, wsg_ref, wsu_ref, wsd_ref, lng_ref, lnb_ref,
                  y_hbm, o_ref, ybuf, sems):
    i = pl.program_id(0)
    nsteps = pl.num_programs(0)
    slot = i % 2
    nrow = COMBINE_TOKENS * TOP_K

    def issue(d_ref, s):
        for cp in _row_copies(y_hbm, d_ref, ybuf.at[s], sems.at[s], nrow):
            cp.start()

    @pl.when(i == 0)
    def _():
        issue(dc_ref, 0)

    @pl.when(i + 1 < nsteps)
    def _():
        issue(dn_ref, 1 - slot)

    hb = h2_ref[...].astype(BF16)
    g = _dot(hb, wsg_ref[...])
    u = _dot(hb, wsu_ref[...])
    shared = _dot(((g * jax.nn.sigmoid(g)) * u).astype(BF16), wsd_ref[...])
    for cp in _row_copies(y_hbm, dc_ref, ybuf.at[slot], sems.at[slot], nrow):
        cp.wait()
    yb = ybuf[slot].reshape((COMBINE_TOKENS, TOP_K) + ROW_TILE)
    routed = pltpu.einshape("tcl->t(cl)", jnp.sum(yb, axis=1))
    x2 = DN_ALPHA * x1_ref[...] + g2_ref[...] * (routed + shared)
    o_ref[...] = _ln_rows(x2) * lng_ref[...] + lnb_ref[...]


def _combine_final(x1, h2, mod4, dest_flat, y_rows, wsg, wsu, wsd, ln_g, ln_b, seq):
    t, d = x1.shape
    tt = COMBINE_TOKENS
    nsteps = t // tt
    spb = seq // tt
    d3 = dest_flat.reshape(nsteps, 1, tt * TOP_K)
    row = pl.BlockSpec((tt, d), lambda i: (i, 0))
    full = lambda a: pl.BlockSpec(a.shape, lambda i: (0,) * a.ndim)
    ln_g2, ln_b2 = ln_g.reshape(1, d), ln_b.reshape(1, d)
    return pl.pallas_call(
        _final_kernel,
        grid=(nsteps,),
        in_specs=[pl.BlockSpec((None, 1, tt * TOP_K), lambda i: (i, 0, 0), memory_space=pltpu.SMEM),
                  pl.BlockSpec((None, 1, tt * TOP_K), lambda i: (jnp.minimum(i + 1, nsteps - 1), 0, 0),
                               memory_space=pltpu.SMEM),
                  row, row,
                  pl.BlockSpec((None, None, 1, d), lambda i: (i // spb, 5, 0, 0)),
                  full(wsg), full(wsu), full(wsd), full(ln_g2), full(ln_b2),
                  pl.BlockSpec(memory_space=pl.ANY)],
        out_specs=row,
        out_shape=jax.ShapeDtypeStruct((t, d), F32),
        scratch_shapes=[pltpu.VMEM((2, tt * TOP_K) + ROW_TILE, F32),
                        pltpu.SemaphoreType.DMA((2,))],
        compiler_params=_cparams(("arbitrary",)),
        name="shared_combine_ln",
    )(d3, d3, x1, h2, mod4, wsg, wsu, wsd, ln_g2, ln_b2, y_rows)


def kernel(x, c, w_ada, b_ada, w_in, hg_lower_bound, hg_norm_w, rel_bias, w_out, ln1_g, ln1_b, w_router,
           router_bias, w_e_gate, w_e_up, w_e_down, w_sh_gate, w_sh_up, w_sh_down, ln2_g, ln2_b):
    bsz, seq, d = x.shape
    t = bsz * seq
    assert w_ada.shape[0] == DEPTH and seq % (ATT_BRANCHES[-1][0]) == 0
    x2 = x.reshape(t, d)
    bias = _bias_tables(rel_bias)
    for l in range(DEPTH):
        mod4 = _modulation(c, w_ada[l], b_ada[l]).reshape(bsz, 6, 1, d)
        w_in_bf = _cast_bf16(w_in[l], 256)
        hq, hf, hi, hg, *qkv = _in_projection(x2, mod4, w_in_bf, seq, 512)
        nbr = len(ATT_BRANCHES)
        y_hg = _hgrn2(hq, hf, hi, hg, hg_lower_bound, hg_norm_w[l], bsz, seq)
        os_, ls_ = [], []
        for g, (_, dil) in enumerate(ATT_BRANCHES):
            o, lse = _dilated_attention(qkv[g], qkv[nbr + g], qkv[2 * nbr + g], bias[g], bsz, seq, dil)
            os_.append(o)
            ls_.append(lse)
        w_out_bf = _cast_bf16(w_out[l], 256)
        x1, h2, logits_t = _out_projection(y_hg, os_, ls_, x2, mod4, w_out_bf, ln1_g[l], ln1_b[l],
                                           w_router[l].T, seq, 512)
        eidx, wk = _route(logits_t, router_bias[l], 256)
        row_token, row_w, block_e, dest_flat, n_used, n_blocks = _dispatch_plan(eidx, wk)
        y_rows = _routed_experts(h2.reshape((t,) + ROW_TILE), row_token, row_w, block_e, n_used, n_blocks,
                                 w_e_gate[l], w_e_up[l], w_e_down[l])
        x2 = _combine_final(x1, h2, mod4, dest_flat, y_rows,
                            _cast_bf16(w_sh_gate[l], 256), _cast_bf16(w_sh_up[l], 256),
                            _cast_bf16(w_sh_down[l], 256), ln2_g[l], ln2_b[l], seq)
    return x2.reshape(bsz, seq, d)
```

```python
import dataclasses
import math

import jax
import jax.numpy as jnp
import numpy as np
from jax import lax
from jax.experimental import pallas as pl
from jax.experimental.pallas import tpu as pltpu
from jax.experimental.pallas import tpu_sc as plsc

HG_HEADS = 4
HG_DK = 128
HG_WIDTH = HG_HEADS * HG_DK
ATT_BRANCHES = ((128, 1), (512, 4), (2048, 16))
ATT_HEADS_PER_BRANCH = 4
ATT_HEAD_DIM = 64
ATT_BW = ATT_HEADS_PER_BRANCH * ATT_HEAD_DIM
ATT_BLOCK = 128
REL_BUCKETS = 32
REL_MAX_DIST = 2048
N_EXPERTS = 256
TOP_K = 8
N_GROUPS = 8
TOPK_GROUPS = 4
ROUTED_SCALE = 2.5
MOE_BLOCK = 128
DEPTH = 1
DN_ALPHA = (2 * DEPTH) ** 0.25
LN_EPS = 1e-5
RMS_EPS = 1e-6

LANES = 128
SUBLANES = 8
VMEM_LIMIT_BYTES = 56 * 1024 * 1024

F32 = jnp.float32
BF16 = jnp.bfloat16
NEG_INF = float("-inf")


def _cparams(sem):
    return pltpu.CompilerParams(dimension_semantics=sem, vmem_limit_bytes=VMEM_LIMIT_BYTES)


def _ln_rows(x):
    mu = jnp.mean(x, axis=-1, keepdims=True)
    xc = x - mu
    var = jnp.mean(xc * xc, axis=-1, keepdims=True)
    return xc * lax.rsqrt(var + LN_EPS)


def _dot(a, b):
    return jnp.dot(a, b, preferred_element_type=F32)


def _dot_nt(a, b):
    return lax.dot_general(a, b, (((1,), (1,)), ((), ())), preferred_element_type=F32)


def _dot_tn(a, b):
    return lax.dot_general(a, b, (((0,), (0,)), ((), ())), preferred_element_type=F32)


def _cast_kernel(w_ref, o_ref):
    o_ref[...] = w_ref[...].astype(o_ref.dtype)


def _cast_bf16(w, rows_per_step):
    r, c = w.shape
    return pl.pallas_call(
        _cast_kernel,
        grid=(r // rows_per_step,),
        in_specs=[pl.BlockSpec((rows_per_step, c), lambda i: (i, 0))],
        out_specs=pl.BlockSpec((rows_per_step, c), lambda i: (i, 0)),
        out_shape=jax.ShapeDtypeStruct((r, c), BF16),
        compiler_params=_cparams(("parallel",)),
        name="cast_bf16",
    )(w)


def _mod_kernel(c_ref, w_ref, b_ref, o_ref):
    c = c_ref[...]
    cond = c * jax.nn.sigmoid(c)
    o_ref[...] = jnp.dot(cond, w_ref[...], preferred_element_type=F32,
                         precision=lax.Precision.HIGHEST) + b_ref[...]


def _modulation(c, w_ada, b_ada):
    bsz, d = c.shape
    n = w_ada.shape[1]
    rows = -(-bsz // SUBLANES) * SUBLANES
    cpad = jnp.zeros((rows, d), F32).at[:bsz].set(c)
    tn = 1024
    out = pl.pallas_call(
        _mod_kernel,
        grid=(n // tn,),
        in_specs=[pl.BlockSpec((rows, d), lambda j: (0, 0)),
                  pl.BlockSpec((d, tn), lambda j: (0, j)),
                  pl.BlockSpec((1, tn), lambda j: (0, j))],
        out_specs=pl.BlockSpec((rows, tn), lambda j: (0, j)),
        out_shape=jax.ShapeDtypeStruct((rows, n), F32),
        compiler_params=_cparams(("parallel",)),
        name="adaln_modulation",
    )(cpad, w_ada, b_ada.reshape(1, n))
    return out[:bsz]


_IN_HG = 4
_IN_ATT = 3 * len(ATT_BRANCHES)


def _inproj_kernel(x_ref, sc_ref, sh_ref, w_ref, *outs):
    x = x_ref[...]
    h = _ln_rows(x) * (1.0 + sc_ref[...]) + sh_ref[...]
    hb = h.astype(BF16)
    col = 0
    for k, o_ref in enumerate(outs):
        width = o_ref.shape[-1]
        y = _dot(hb, w_ref[:, col:col + width])
        if _IN_HG <= k < _IN_HG + len(ATT_BRANCHES):
            y = y * (ATT_HEAD_DIM ** -0.5)
        o_ref[...] = y.astype(o_ref.dtype)
        col += width


def _in_projection(x2, mod4, w_in_bf, seq, tm):
    t, d = x2.shape
    steps_per_batch = seq // tm
    widths = [HG_WIDTH] * _IN_HG + [ATT_BW] * _IN_ATT
    dtypes = [BF16, F32, BF16, BF16] + [BF16] * _IN_ATT
    mod_spec = lambda row: pl.BlockSpec((None, None, 1, d),
                                        lambda i, row=row: (i // steps_per_batch, row, 0, 0))
    outs = pl.pallas_call(
        _inproj_kernel,
        grid=(t // tm,),
        in_specs=[pl.BlockSpec((tm, d), lambda i: (i, 0)),
                  mod_spec(1), mod_spec(0),
                  pl.BlockSpec(w_in_bf.shape, lambda i: (0, 0))],
        out_specs=[pl.BlockSpec((tm, w), lambda i: (i, 0)) for w in widths],
        out_shape=[jax.ShapeDtypeStruct((t, w), dt) for w, dt in zip(widths, dtypes)],
        compiler_params=_cparams(("parallel",)),
        name="ln_in_projection",
    )(x2, mod4, mod4, w_in_bf)
    return outs


HG_CHUNK = 64
HG_SUB = 8
HG_LEVELS = (64, 32, 16)


def _hgrn_chunk(q, z, iv, lb, st_t):
    c = HG_CHUNK
    f = lb + (1.0 - lb) * jax.nn.sigmoid(z)
    lf = jnp.log(f)
    kk = (1.0 - lb) * jax.nn.sigmoid(-z)
    r_i = lax.broadcasted_iota(jnp.int32, (c, c), 0)
    c_i = lax.broadcasted_iota(jnp.int32, (c, c), 1)
    tril = (c_i <= r_i).astype(F32)
    b = jnp.dot(tril, lf, preferred_element_type=F32, precision=lax.Precision.HIGHEST)

    row = lax.broadcasted_iota(jnp.int32, (c, HG_DK), 0)
    scores = jnp.zeros((c, c), F32)
    for m in HG_LEVELS:
        nb = c // m
        b3 = b.reshape(nb, m, HG_DK)
        piv = jnp.broadcast_to(b3[:, m // 2 - 1:m // 2, :], (nb, m, HG_DK)).reshape(c, HG_DK)
        second = (row % m) >= (m // 2)
        qt = jnp.where(second, q * jnp.exp(b - piv), 0.0)
        kt = jnp.where(second, 0.0, kk * jnp.exp(piv - b))
        s_m = _dot_nt(qt.astype(BF16), kt.astype(BF16))
        if nb > 1:
            s_m = jnp.where((r_i // m) == (c_i // m), s_m, 0.0)
        scores = scores + s_m
    sub = HG_SUB
    t_i = lax.broadcasted_iota(jnp.int32, (sub, 1), 0)
    lane = lax.broadcasted_iota(jnp.int32, (sub, c), 1)
    diag_rows = []
    for j in range(c // sub):
        qb = q[j * sub:(j + 1) * sub]
        kb = kk[j * sub:(j + 1) * sub]
        bb = b[j * sub:(j + 1) * sub]
        a_j = jnp.zeros((sub, c), F32)
        for s in range(sub):
            w = qb * kb[s:s + 1] * jnp.exp(bb - bb[s:s + 1])
            col = jnp.sum(w, axis=-1, keepdims=True)
            col = jnp.where(t_i >= s, col, 0.0)
            a_j = jnp.where(lane == j * sub + s, col, a_j)
        diag_rows.append(a_j)
    scores = scores + jnp.concatenate(diag_rows, axis=0)

    stb = st_t.astype(BF16)
    inter = _dot_nt((q * jnp.exp(b)).astype(BF16), stb)
    ivb = iv.astype(BF16)
    o = inter + _dot(scores.astype(BF16), ivb)
    b_last = b[c - 1:c]
    kdec = (kk * jnp.exp(b_last - b)).astype(BF16)
    st_new = st_t * jnp.exp(b_last) + _dot_tn(ivb, kdec)
    return o, st_new


def _hgrn_kernel(q_ref, f_ref, i_ref, g_ref, lbp_ref, nw_ref, o_ref, st_ref):
    @pl.when(pl.program_id(1) == 0)
    def _():
        st_ref[...] = jnp.zeros_like(st_ref)

    lbp = lbp_ref[...]
    e = jnp.exp(lbp - jnp.max(lbp, axis=0, keepdims=True))
    lb_all = e[0:1] / jnp.sum(e, axis=0, keepdims=True)
    outs = []
    for h in range(HG_HEADS):
        sl = slice(h * HG_DK, (h + 1) * HG_DK)
        o, st_new = _hgrn_chunk(q_ref[:, sl].astype(F32), f_ref[:, sl], i_ref[:, sl].astype(F32),
                                lb_all[:, sl], st_ref[h])
        st_ref[h] = st_new
        o = o * lax.rsqrt(jnp.mean(o * o, axis=-1, keepdims=True) + RMS_EPS)
        outs.append(o)
    o_all = jnp.concatenate(outs, axis=-1)
    g = g_ref[...].astype(F32)
    o_ref[...] = (o_all * nw_ref[...] * (g * jax.nn.sigmoid(g))).astype(o_ref.dtype)


def _hgrn2(hq, hf, hi, hg, lb_param, norm_w, bsz, seq):
    t = hq.shape[0]
    nc = seq // HG_CHUNK
    tok = lambda b, n: (b * nc + n, 0)
    spec = pl.BlockSpec((HG_CHUNK, HG_WIDTH), tok)
    return pl.pallas_call(
        _hgrn_kernel,
        grid=(bsz, nc),
        in_specs=[spec, spec, spec, spec,
                  pl.BlockSpec(lb_param.shape, lambda b, n: (0, 0)),
                  pl.BlockSpec((1, HG_WIDTH), lambda b, n: (0, 0))],
        out_specs=spec,
        out_shape=jax.ShapeDtypeStruct((t, HG_WIDTH), BF16),
        scratch_shapes=[pltpu.VMEM((HG_HEADS, HG_DK, HG_DK), F32)],
        compiler_params=_cparams(("parallel", "arbitrary")),
        name="hgrn2_scan",
    )(hq, hf, hi, hg, lb_param, norm_w.reshape(1, HG_WIDTH))


def _t5_bucket_np(dist):
    max_exact = REL_BUCKETS // 2
    n = np.maximum(dist, 0)
    nf = np.maximum(n, 1).astype(np.float32)
    large = max_exact + (np.log(nf / np.float32(max_exact)) / np.float32(math.log(REL_MAX_DIST / max_exact))
                         * np.float32(REL_BUCKETS - max_exact)).astype(np.int32)
    large = np.minimum(large, REL_BUCKETS - 1)
    return np.where(n < max_exact, n, large).astype(np.int32)


def _band_tables():
    w = ATT_BLOCK
    qi = np.arange(w)[:, None]
    ki = np.arange(2 * w)[None, :]
    m = w + qi - ki
    band = (m >= 0) & (m <= w)
    buckets = np.stack([_t5_bucket_np(m * dil) for _, dil in ATT_BRANCHES])
    return buckets, band


def _bias_kernel(rb_ref, bucket_ref, o_ref):
    g = pl.program_id(0)
    w = ATT_BLOCK
    bucket = bucket_ref[...]
    qi = lax.broadcasted_iota(jnp.int32, (w, 2 * w), 0)
    ki = lax.broadcasted_iota(jnp.int32, (w, 2 * w), 1)
    m = w + qi - ki
    band = (m >= 0) & (m <= w)
    for h in range(ATT_HEADS_PER_BRANCH):
        acc = jnp.zeros((w, 2 * w), F32)
        for c in range(REL_BUCKETS):
            acc = jnp.where(bucket == c, rb_ref[c, g * ATT_HEADS_PER_BRANCH + h], acc)
        full = jnp.where(band, acc, NEG_INF)
        o_ref[1, h] = full
        o_ref[0, h] = jnp.where(ki >= w, full, NEG_INF)


def _bias_tables(rel_bias):
    buckets, _ = _band_tables()
    g = len(ATT_BRANCHES)
    w = ATT_BLOCK
    return pl.pallas_call(
        _bias_kernel,
        grid=(g,),
        in_specs=[pl.BlockSpec(memory_space=pltpu.SMEM),
                  pl.BlockSpec((None, w, 2 * w), lambda i: (i, 0, 0))],
        out_specs=pl.BlockSpec((None, 2, ATT_HEADS_PER_BRANCH, w, 2 * w), lambda i: (i, 0, 0, 0, 0)),
        out_shape=jax.ShapeDtypeStruct((g, 2, ATT_HEADS_PER_BRANCH, w, 2 * w), F32),
        compiler_params=_cparams(("parallel",)),
        name="rel_bias_tables",
    )(rel_bias, jnp.asarray(buckets))


def _attn_kernel(q_ref, kp_ref, kc_ref, vp_ref, vc_ref, bias_ref, o_ref, lse_ref):
    n = pl.program_id(2)
    w = ATT_BLOCK
    hb = ATT_HEADS_PER_BRANCH
    q = q_ref[...]
    lane = lax.broadcasted_iota(jnp.int32, (w, ATT_BW), 1) // ATT_HEAD_DIM
    q4 = jnp.concatenate([jnp.where(lane == h, q, jnp.zeros_like(q)) for h in range(hb)], axis=0)
    kk = jnp.concatenate([kp_ref[...], kc_ref[...]], axis=0)
    vv = jnp.concatenate([vp_ref[...], vc_ref[...]], axis=0)
    s4 = _dot_nt(q4, kk)
    bias = bias_ref[jnp.minimum(n, 1)]
    s4 = s4 + bias.reshape(hb * w, 2 * w)
    mx = jnp.max(s4, axis=-1, keepdims=True)
    p = jnp.exp(s4 - mx)
    l = jnp.sum(p, axis=-1, keepdims=True)
    o4 = _dot((p / l).astype(vv.dtype), vv)
    lse4 = mx + jnp.log(l)
    o = jnp.zeros((w, ATT_BW), F32)
    lse = jnp.zeros((w, ATT_BW), F32)
    for h in range(hb):
        o = jnp.where(lane == h, o4[h * w:(h + 1) * w], o)
        lse = jnp.where(lane == h, lse4[h * w:(h + 1) * w], lse)
    o_ref[...] = o.astype(o_ref.dtype)
    lse_ref[...] = lse


def _dilated_attention(q, k, v, bias_g, bsz, seq, dilation):
    t = q.shape[0]
    w = ATT_BLOCK
    l = seq // dilation
    nb = l // w
    view = lambda a: a.reshape(bsz, l, dilation * ATT_BW)
    cur = pl.BlockSpec((None, w, ATT_BW), lambda b, r, n: (b, n, r))
    prev = pl.BlockSpec((None, w, ATT_BW), lambda b, r, n: (b, jnp.maximum(n - 1, 0), r))
    o, lse = pl.pallas_call(
        _attn_kernel,
        grid=(bsz, dilation, nb),
        in_specs=[cur, prev, cur, prev, cur,
                  pl.BlockSpec(bias_g.shape, lambda b, r, n: (0, 0, 0, 0))],
        out_specs=[cur, cur],
        out_shape=[jax.ShapeDtypeStruct((bsz, l, dilation * ATT_BW), BF16),
                   jax.ShapeDtypeStruct((bsz, l, dilation * ATT_BW), F32)],
        compiler_params=_cparams(("parallel", "parallel", "arbitrary")),
        name=f"dilated_attention_d{dilation}",
    )(view(q), view(k), view(k), view(v), view(v), bias_g)
    return o.reshape(t, ATT_BW), lse.reshape(t, ATT_BW)


def _split_bf16(a):
    hi = a.astype(BF16)
    lo = (a - hi.astype(F32)).astype(BF16)
    return hi, lo


H2P_CHUNKS = 4


def _bf16_bits(a):
    u = lax.bitcast_convert_type(a, jnp.uint32)
    return u + jnp.uint32(0x7FFF) + ((u >> 16) & jnp.uint32(1))


def _outproj_kernel(yhg_ref, o1_ref, o2_ref, o3_ref, l1_ref, l2_ref, l3_ref, x_ref,
                    g1_ref, sc2_ref, sh2_ref, wout_ref, lng_ref, lnb_ref, wrt_ref,
                    x1_ref, h2_ref, h2p_ref, lgt_ref):
    l1, l2, l3 = l1_ref[...], l2_ref[...], l3_ref[...]
    mx = jnp.maximum(jnp.maximum(l1, l2), l3)
    e1, e2, e3 = jnp.exp(l1 - mx), jnp.exp(l2 - mx), jnp.exp(l3 - mx)
    den = e1 + e2 + e3
    att = (e1 / den) * o1_ref[...].astype(F32) + (e2 / den) * o2_ref[...].astype(F32) \
        + (e3 / den) * o3_ref[...].astype(F32)
    mix = _dot(yhg_ref[...], wout_ref[:HG_WIDTH, :]) + _dot(att.astype(BF16), wout_ref[HG_WIDTH:, :])
    x1 = _ln_rows(DN_ALPHA * x_ref[...] + g1_ref[...] * mix) * lng_ref[...] + lnb_ref[...]
    x1_ref[...] = x1
    h2 = _ln_rows(x1) * (1.0 + sc2_ref[...]) + sh2_ref[...]
    h2_ref[...] = h2
    for cidx in range(H2P_CHUNKS):
        lo = _bf16_bits(h2[:, 2 * LANES * cidx:2 * LANES * cidx + LANES])
        hi = _bf16_bits(h2[:, 2 * LANES * cidx + LANES:2 * LANES * (cidx + 1)])
        word = (lo >> 16) | (hi & jnp.uint32(0xFFFF0000))
        h2p_ref[:, cidx, :] = lax.bitcast_convert_type(word, jnp.int32)
    h_hi, h_lo = _split_bf16(h2)
    w_hi, w_lo = _split_bf16(wrt_ref[...])
    lgt_ref[...] = _dot_nt(w_hi, h_hi) + (_dot_nt(w_hi, h_lo) + _dot_nt(w_lo, h_hi))


def _out_projection(yhg, os_, ls_, x2, mod4, w_out_bf, ln_g, ln_b, w_router_t, seq, tm):
    t, d = x2.shape
    spb = seq // tm
    ne = w_router_t.shape[0]
    row = lambda w: pl.BlockSpec((tm, w), lambda i: (i, 0))
    mod_spec = lambda r: pl.BlockSpec((None, None, 1, d), lambda i, r=r: (i // spb, r, 0, 0))
    full = lambda a: pl.BlockSpec(a.shape, lambda i: (0,) * a.ndim)
    ln_g2, ln_b2 = ln_g.reshape(1, d), ln_b.reshape(1, d)
    return pl.pallas_call(
        _outproj_kernel,
        grid=(t // tm,),
        in_specs=[row(HG_WIDTH)] + [row(ATT_BW)] * 6 + [row(d),
                  mod_spec(2), mod_spec(4), mod_spec(3),
                  full(w_out_bf), full(ln_g2), full(ln_b2), full(w_router_t)],
        out_specs=[row(d), row(d), pl.BlockSpec((tm, H2P_CHUNKS, LANES), lambda i: (i, 0, 0)),
                   pl.BlockSpec((ne, tm), lambda i: (0, i))],
        out_shape=[jax.ShapeDtypeStruct((t, d), F32), jax.ShapeDtypeStruct((t, d), F32),
                   jax.ShapeDtypeStruct((t, H2P_CHUNKS, LANES), jnp.int32),
                   jax.ShapeDtypeStruct((ne, t), F32)],
        compiler_params=_cparams(("parallel",)),
        name="merge_outproj_ln",
    )(yhg, *os_, *ls_, x2, mod4, mod4, mod4, w_out_bf, ln_g2, ln_b2, w_router_t)


def _argmax_rows(cur, iota, nrows):
    m = jnp.max(cur, axis=0, keepdims=True)
    idx = jnp.min(jnp.where(cur == m, iota, nrows), axis=0, keepdims=True)
    return m, idx, iota == idx


def _route_kernel(lgt_ref, rb_ref, eidx_ref, w_ref, rank_ref, cnt_ref, carry):
    ne = N_EXPERTS
    gsz = ne // N_GROUPS
    tt = lgt_ref.shape[1]

    @pl.when(pl.program_id(0) == 0)
    def _():
        carry[...] = jnp.zeros_like(carry)

    sc = jax.nn.sigmoid(lgt_ref[...])
    biased = sc + rb_ref[...]
    g3 = biased.reshape(N_GROUPS, gsz, tt)
    io3 = lax.broadcasted_iota(jnp.int32, (N_GROUPS, gsz, tt), 1)
    m1 = jnp.max(g3, axis=1, keepdims=True)
    first = jnp.min(jnp.where(g3 == m1, io3, gsz), axis=1, keepdims=True)
    m2 = jnp.max(jnp.where(io3 == first, NEG_INF, g3), axis=1, keepdims=True)
    gs = (m1 + m2).reshape(N_GROUPS, tt)
    io8 = lax.broadcasted_iota(jnp.int32, (N_GROUPS, tt), 0)
    sel = jnp.zeros((N_GROUPS, tt), jnp.int32)
    cur = gs
    for _ in range(TOPK_GROUPS):
        _, _, pick = _argmax_rows(cur, io8, N_GROUPS)
        sel = jnp.where(pick, 1, sel)
        cur = jnp.where(pick, NEG_INF, cur)
    masked = jnp.where(sel.reshape(N_GROUPS, 1, tt) > 0, g3, NEG_INF).reshape(ne, tt)
    ioe = lax.broadcasted_iota(jnp.int32, (ne, tt), 0)
    cur = masked
    idxs, ws, picks = [], [], []
    for _ in range(TOP_K):
        _, idx, pick = _argmax_rows(cur, ioe, ne)
        idxs.append(idx)
        picks.append(pick)
        ws.append(jnp.sum(jnp.where(pick, sc, 0.0), axis=0, keepdims=True))
        cur = jnp.where(pick, NEG_INF, cur)
    wk = jnp.concatenate(ws, axis=0)
    eidx_ref[...] = jnp.concatenate(idxs, axis=0)
    w_ref[...] = wk / jnp.sum(wk, axis=0, keepdims=True) * ROUTED_SCALE
    chosen = jnp.where(cur == NEG_INF, jnp.where(masked == NEG_INF, 0.0, 1.0), 0.0)
    r_i = lax.broadcasted_iota(jnp.int32, (tt, tt), 0)
    c_i = lax.broadcasted_iota(jnp.int32, (tt, tt), 1)
    before = jnp.where(r_i < c_i, 1.0, 0.0).astype(BF16)
    pref = _dot(chosen.astype(BF16), before) + carry[...]
    rank_ref[...] = jnp.concatenate(
        [jnp.sum(jnp.where(p, pref, 0.0), axis=0, keepdims=True) for p in picks], axis=0).astype(jnp.int32)
    carry[...] = carry[...] + jnp.sum(chosen, axis=1, keepdims=True)
    cnt_ref[...] = carry[...]


def _route(logits_t, router_bias, tt):
    ne, t = logits_t.shape
    tok = pl.BlockSpec((TOP_K, tt), lambda i: (0, i))
    return pl.pallas_call(
        _route_kernel,
        grid=(t // tt,),
        in_specs=[pl.BlockSpec((ne, tt), lambda i: (0, i)),
                  pl.BlockSpec((ne, 1), lambda i: (0, 0))],
        out_specs=[tok, tok, tok, pl.BlockSpec((ne, 1), lambda i: (0, 0))],
        out_shape=[jax.ShapeDtypeStruct((TOP_K, t), jnp.int32), jax.ShapeDtypeStruct((TOP_K, t), F32),
                   jax.ShapeDtypeStruct((TOP_K, t), jnp.int32), jax.ShapeDtypeStruct((ne, 1), F32)],
        scratch_shapes=[pltpu.VMEM((ne, 1), F32)],
        compiler_params=_cparams(("arbitrary",)),
        name="router_topk",
    )(logits_t, router_bias.reshape(ne, 1))


def _plan_kernel(cnt_ref, eidx_ref, rank_ref, dest_ref, be_ref, nu_ref):
    ne = N_EXPERTS
    tt = eidx_ref.shape[1]
    nblk = be_ref.shape[1]
    cnt = cnt_ref[...].astype(jnp.int32)
    padded = ((cnt + (MOE_BLOCK - 1)) // MOE_BLOCK) * MOE_BLOCK
    r_i = lax.broadcasted_iota(jnp.int32, (ne, ne), 0)
    c_i = lax.broadcasted_iota(jnp.int32, (ne, ne), 1)
    incl = jnp.where(c_i <= r_i, 1.0, 0.0)
    pend = jnp.dot(incl, jnp.broadcast_to(padded.astype(F32), (ne, LANES)),
                   preferred_element_type=F32, precision=lax.Precision.HIGHEST)[:, 0:1]
    pend = pend.astype(jnp.int32)
    pstart = pend - padded
    ioe = lax.broadcasted_iota(jnp.int32, (ne, tt), 0)
    rows = []
    for k in range(TOP_K):
        sel = ioe == eidx_ref[k:k + 1, :]
        rows.append(jnp.sum(jnp.where(sel, pstart, 0), axis=0, keepdims=True))
    dest_ref[...] = jnp.concatenate(rows, axis=0) + rank_ref[...]
    blk0 = lax.broadcasted_iota(jnp.int32, (ne, nblk), 1) * MOE_BLOCK
    be = jnp.sum(jnp.where(pend <= blk0, 1, 0), axis=0, keepdims=True)
    be_ref[...] = jnp.minimum(be, ne - 1)
    nu_ref[...] = jnp.max(pend, axis=0, keepdims=True) // MOE_BLOCK


def _dispatch_plan(cnt, eidx, rank, tt):
    k, t = eidx.shape
    n_blocks = -(-(t * k) // MOE_BLOCK) + N_EXPERTS
    tok = pl.BlockSpec((k, tt), lambda i: (0, i))
    dest, block_e, n_used = pl.pallas_call(
        _plan_kernel,
        grid=(t // tt,),
        in_specs=[pl.BlockSpec(cnt.shape, lambda i: (0, 0)), tok, tok],
        out_specs=[tok, pl.BlockSpec((1, n_blocks), lambda i: (0, 0)), pl.BlockSpec((1, 1), lambda i: (0, 0))],
        out_shape=[jax.ShapeDtypeStruct((k, t), jnp.int32), jax.ShapeDtypeStruct((1, n_blocks), jnp.int32),
                   jax.ShapeDtypeStruct((1, 1), jnp.int32)],
        compiler_params=_cparams(("arbitrary",)),
        name="dispatch_plan",
    )(cnt, eidx, rank)
    return dest, block_e.reshape(n_blocks), n_used.reshape(1), n_blocks


SC_CORES = 2
SC_SUBCORES = 16
SC_LANES = 16
SC_CHUNK = 4096


def _row_tables(dest, wk, n_rows):
    k, t = dest.shape
    a = k * t
    nw = SC_CORES * SC_SUBCORES
    per_w = n_rows // nw
    assert n_rows % (nw * SC_LANES) == 0 and t % SC_CHUNK == 0
    mesh = plsc.VectorSubcoreMesh(core_axis_name="c", subcore_axis_name="s")
    cp = pltpu.CompilerParams()
    if "needs_layout_passes" in pltpu.CompilerParams.__dataclass_fields__:
        cp = dataclasses.replace(cp, needs_layout_passes=False)

    def body(dest_hbm, w_hbm, tok_out, w_out, dbuf, wbuf, tloc, wloc):
        wid = lax.axis_index("s") * SC_CORES + lax.axis_index("c")
        base = wid * per_w

        @pl.loop(0, per_w // SC_LANES)
        def _(i):
            tloc[pl.ds(i * SC_LANES, SC_LANES)] = jnp.zeros((SC_LANES,), jnp.int32)
            wloc[pl.ds(i * SC_LANES, SC_LANES)] = jnp.zeros((SC_LANES,), F32)

        lane = lax.iota(jnp.int32, SC_LANES)

        @pl.loop(0, a // SC_CHUNK)
        def _(c):
            pltpu.sync_copy(dest_hbm.at[pl.ds(c * SC_CHUNK, SC_CHUNK)], dbuf)
            pltpu.sync_copy(w_hbm.at[pl.ds(c * SC_CHUNK, SC_CHUNK)], wbuf)
            tok0 = lax.rem(c * SC_CHUNK, t)

            @pl.loop(0, SC_CHUNK // SC_LANES)
            def _(j):
                loc = dbuf[pl.ds(j * SC_LANES, SC_LANES)] - base
                mine = jnp.logical_and(loc >= 0, loc < per_w)
                loc = jnp.where(mine, loc, 0)
                plsc.store_scatter(tloc, [loc], tok0 + j * SC_LANES + lane, mask=mine)
                plsc.store_scatter(wloc, [loc], wbuf[pl.ds(j * SC_LANES, SC_LANES)], mask=mine)

        pltpu.sync_copy(tloc, tok_out.at[pl.ds(base, per_w)])
        pltpu.sync_copy(wloc, w_out.at[pl.ds(base, per_w)])

    fn = pl.kernel(
        body,
        out_type=(jax.ShapeDtypeStruct((n_rows,), jnp.int32), jax.ShapeDtypeStruct((n_rows,), F32)),
        mesh=mesh,
        scratch_types=[pltpu.VMEM((SC_CHUNK,), jnp.int32), pltpu.VMEM((SC_CHUNK,), F32),
                       pltpu.VMEM((per_w,), jnp.int32), pltpu.VMEM((per_w,), F32)],
        compiler_params=cp,
        name="row_tables",
    )
    return fn(dest.reshape(a), wk.reshape(a))


ROW_TILE = (SUBLANES, LANES)


def _gather_rows(h2p_ref, tok_ref, sbuf, s):
    for j in range(MOE_BLOCK):
        sbuf[s, j] = h2p_ref[tok_ref[0, j]]


def _ffn_kernel(be_ref, nu_ref, tokc_ref, tokn_ref, w_ref, h2p_ref, wg_ref, wu_ref, wd_ref,
                y_ref, sbuf):
    i = pl.program_id(0)
    n_used = nu_ref[0]
    slot = i % 2

    @pl.when(i == 0)
    def _():
        _gather_rows(h2p_ref, tokc_ref, sbuf, 0)

    @pl.when(i < n_used)
    def _():
        _gather_rows(h2p_ref, tokn_ref, sbuf, 1 - slot)
        parts = []
        for cidx in range(H2P_CHUNKS):
            word = sbuf[slot, :, cidx, :]
            lo = lax.bitcast_convert_type(word << 16, F32)
            hi = lax.bitcast_convert_type(word & jnp.int32(-65536), F32)
            parts += [lo.astype(BF16), hi.astype(BF16)]
        x = jnp.concatenate(parts, axis=1)
        g = _dot(x, wg_ref[...].astype(BF16))
        u = _dot(x, wu_ref[...].astype(BF16))
        hm = (g * jax.nn.sigmoid(g)) * u
        y = _dot(hm.astype(BF16), wd_ref[...].astype(BF16)) * w_ref[...]
        y_ref[...] = pltpu.einshape("r(cl)->rcl", y, c=SUBLANES)

    @pl.when(i >= n_used)
    def _():
        y_ref[...] = jnp.zeros_like(y_ref)


def _routed_experts(h2p, row_token, row_w, block_e, n_used, n_blocks, wg, wu, wd):
    d = wg.shape[1]
    de = wg.shape[2]
    nb = n_blocks
    tok3 = row_token.reshape(nb, 1, MOE_BLOCK)
    w3 = row_w.reshape(nb, MOE_BLOCK, 1)
    grid_spec = pltpu.PrefetchScalarGridSpec(
        num_scalar_prefetch=2,
        grid=(nb,),
        in_specs=[
            pl.BlockSpec((None, 1, MOE_BLOCK), lambda i, be, nu: (i, 0, 0), memory_space=pltpu.SMEM),
            pl.BlockSpec((None, 1, MOE_BLOCK), lambda i, be, nu: (jnp.minimum(i + 1, nb - 1), 0, 0),
                         memory_space=pltpu.SMEM),
            pl.BlockSpec((None, MOE_BLOCK, 1), lambda i, be, nu: (i, 0, 0)),
            pl.BlockSpec(h2p.shape, lambda i, be, nu: (0, 0, 0), pipeline_mode=pl.Buffered(1)),
            pl.BlockSpec((None, d, de), lambda i, be, nu: (be[i], 0, 0)),
            pl.BlockSpec((None, d, de), lambda i, be, nu: (be[i], 0, 0)),
            pl.BlockSpec((None, de, d), lambda i, be, nu: (be[i], 0, 0)),
        ],
        out_specs=pl.BlockSpec((MOE_BLOCK,) + ROW_TILE,
                               lambda i, be, nu: (jnp.minimum(i, nu[0]), 0, 0)),
        scratch_shapes=[pltpu.VMEM((2, MOE_BLOCK, H2P_CHUNKS, LANES), jnp.int32)],
    )
    return pl.pallas_call(
        _ffn_kernel,
        grid_spec=grid_spec,
        out_shape=jax.ShapeDtypeStruct(((nb + 1) * MOE_BLOCK,) + ROW_TILE, F32),
        compiler_params=_cparams(("arbitrary",)),
        name="routed_experts",
    )(block_e, n_used, tok3, tok3, w3, h2p, wg, wu, wd)


COMBINE_TOKENS = 128


def _row_copies(src_hbm, idx_ref, buf, sem):
    return [pltpu.make_async_copy(src_hbm.at[idx_ref[k, j]], buf.at[k, j], sem)
            for k in range(TOP_K) for j in range(COMBINE_TOKENS)]


def _final_kernel(dc_ref, dn_ref, x1_ref, h2_ref, g2_ref, wsg_ref, wsu_ref, wsd_ref, lng_ref, lnb_ref,
                  y_hbm, o_ref, ybuf, sems):
    i = pl.program_id(0)
    nsteps = pl.num_programs(0)
    slot = i % 2

    def issue(d_ref, s):
        for n, cp in enumerate(_row_copies(y_hbm, d_ref, ybuf.at[s], sems.at[s])):
            cp.start(priority=n % 2)

    @pl.when(i == 0)
    def _():
        issue(dc_ref, 0)

    @pl.when(i + 1 < nsteps)
    def _():
        issue(dn_ref, 1 - slot)

    hb = h2_ref[...].astype(BF16)
    g = _dot(hb, wsg_ref[...])
    u = _dot(hb, wsu_ref[...])
    shared = _dot(((g * jax.nn.sigmoid(g)) * u).astype(BF16), wsd_ref[...])
    for cp in _row_copies(y_hbm, dc_ref, ybuf.at[slot], sems.at[slot]):
        cp.wait()
    routed = pltpu.einshape("tcl->t(cl)", jnp.sum(ybuf[slot], axis=0))
    x2 = DN_ALPHA * x1_ref[...] + g2_ref[...] * (routed + shared)
    o_ref[...] = _ln_rows(x2) * lng_ref[...] + lnb_ref[...]


def _combine_final(x1, h2, mod4, dest, y_rows, wsg, wsu, wsd, ln_g, ln_b, seq):
    t, d = x1.shape
    tt = COMBINE_TOKENS
    nsteps = t // tt
    spb = seq // tt
    row = pl.BlockSpec((tt, d), lambda i: (i, 0))
    full = lambda a: pl.BlockSpec(a.shape, lambda i: (0,) * a.ndim)
    ln_g2, ln_b2 = ln_g.reshape(1, d), ln_b.reshape(1, d)
    return pl.pallas_call(
        _final_kernel,
        grid=(nsteps,),
        in_specs=[pl.BlockSpec((TOP_K, tt), lambda i: (0, i), memory_space=pltpu.SMEM),
                  pl.BlockSpec((TOP_K, tt), lambda i: (0, jnp.minimum(i + 1, nsteps - 1)),
                               memory_space=pltpu.SMEM),
                  row, row,
                  pl.BlockSpec((None, None, 1, d), lambda i: (i // spb, 5, 0, 0)),
                  full(wsg), full(wsu), full(wsd), full(ln_g2), full(ln_b2),
                  pl.BlockSpec(memory_space=pl.ANY)],
        out_specs=row,
        out_shape=jax.ShapeDtypeStruct((t, d), F32),
        scratch_shapes=[pltpu.VMEM((2, TOP_K, tt) + ROW_TILE, F32),
                        pltpu.SemaphoreType.DMA((2,))],
        compiler_params=_cparams(("arbitrary",)),
        name="shared_combine_ln",
    )(dest, dest, x1, h2, mod4, wsg, wsu, wsd, ln_g2, ln_b2, y_rows)


def kernel(x, c, w_ada, b_ada, w_in, hg_lower_bound, hg_norm_w, rel_bias, w_out, ln1_g, ln1_b, w_router,
           router_bias, w_e_gate, w_e_up, w_e_down, w_sh_gate, w_sh_up, w_sh_down, ln2_g, ln2_b):
    bsz, seq, d = x.shape
    t = bsz * seq
    assert w_ada.shape[0] == DEPTH and seq % (ATT_BRANCHES[-1][0]) == 0
    x2 = x.reshape(t, d)
    bias = _bias_tables(rel_bias)
    for l in range(DEPTH):
        mod4 = _modulation(c, w_ada[l], b_ada[l]).reshape(bsz, 6, 1, d)
        w_in_bf = _cast_bf16(w_in[l], 256)
        hq, hf, hi, hg, *qkv = _in_projection(x2, mod4, w_in_bf, seq, 512)
        nbr = len(ATT_BRANCHES)
        y_hg = _hgrn2(hq, hf, hi, hg, hg_lower_bound, hg_norm_w[l], bsz, seq)
        os_, ls_ = [], []
        for g, (_, dil) in enumerate(ATT_BRANCHES):
            o, lse = _dilated_attention(qkv[g], qkv[nbr + g], qkv[2 * nbr + g], bias[g], bsz, seq, dil)
            os_.append(o)
            ls_.append(lse)
        w_out_bf = _cast_bf16(w_out[l], 256)
        x1, h2, h2p, logits_t = _out_projection(y_hg, os_, ls_, x2, mod4, w_out_bf, ln1_g[l], ln1_b[l],
                                                w_router[l].T, seq, 512)
        eidx, wk, rank, cnt = _route(logits_t, router_bias[l], 256)
        dest, block_e, n_used, n_blocks = _dispatch_plan(cnt, eidx, rank, 512)
        row_token, row_w = _row_tables(dest, wk, n_blocks * MOE_BLOCK)
        y_rows = _routed_experts(h2p, row_token, row_w, block_e, n_used, n_blocks,
                                 w_e_gate[l], w_e_up[l], w_e_down[l])
        x2 = _combine_final(x1, h2, mod4, dest, y_rows,
                            _cast_bf16(w_sh_gate[l], 256), _cast_bf16(w_sh_up[l], 256),
                            _cast_bf16(w_sh_down[l], 256), ln2_g[l], ln2_b[l], seq)
    return x2.reshape(bsz, seq, d)
```

```python
import dataclasses
import math

import jax
import jax.numpy as jnp
import numpy as np
from jax import lax
from jax.experimental import pallas as pl
from jax.experimental.pallas import tpu as pltpu
from jax.experimental.pallas import tpu_sc as plsc

HG_HEADS = 4
HG_DK = 128
HG_WIDTH = HG_HEADS * HG_DK
ATT_BRANCHES = ((128, 1), (512, 4), (2048, 16))
ATT_HEADS_PER_BRANCH = 4
ATT_HEAD_DIM = 64
ATT_BW = ATT_HEADS_PER_BRANCH * ATT_HEAD_DIM
ATT_BLOCK = 128
REL_BUCKETS = 32
REL_MAX_DIST = 2048
N_EXPERTS = 256
TOP_K = 8
N_GROUPS = 8
TOPK_GROUPS = 4
ROUTED_SCALE = 2.5
MOE_BLOCK = 128
DEPTH = 1
DN_ALPHA = (2 * DEPTH) ** 0.25
LN_EPS = 1e-5
RMS_EPS = 1e-6

LANES = 128
SUBLANES = 8
VMEM_LIMIT_BYTES = 56 * 1024 * 1024

F32 = jnp.float32
BF16 = jnp.bfloat16
NEG_INF = float("-inf")


def _cparams(sem):
    return pltpu.CompilerParams(dimension_semantics=sem, vmem_limit_bytes=VMEM_LIMIT_BYTES)


def _ln_rows(x):
    mu = jnp.mean(x, axis=-1, keepdims=True)
    xc = x - mu
    var = jnp.mean(xc * xc, axis=-1, keepdims=True)
    return xc * lax.rsqrt(var + LN_EPS)


def _dot(a, b):
    return jnp.dot(a, b, preferred_element_type=F32)


def _dot_nt(a, b):
    return lax.dot_general(a, b, (((1,), (1,)), ((), ())), preferred_element_type=F32)


def _dot_tn(a, b):
    return lax.dot_general(a, b, (((0,), (0,)), ((), ())), preferred_element_type=F32)


def _cast_kernel(w_ref, o_ref):
    o_ref[...] = w_ref[...].astype(o_ref.dtype)


def _cast_bf16(w, rows_per_step):
    r, c = w.shape
    return pl.pallas_call(
        _cast_kernel,
        grid=(r // rows_per_step,),
        in_specs=[pl.BlockSpec((rows_per_step, c), lambda i: (i, 0))],
        out_specs=pl.BlockSpec((rows_per_step, c), lambda i: (i, 0)),
        out_shape=jax.ShapeDtypeStruct((r, c), BF16),
        compiler_params=_cparams(("parallel",)),
        name="cast_bf16",
    )(w)


def _mod_kernel(c_ref, w_ref, b_ref, o_ref):
    c = c_ref[...]
    cond = c * jax.nn.sigmoid(c)
    o_ref[...] = jnp.dot(cond, w_ref[...], preferred_element_type=F32,
                         precision=lax.Precision.HIGHEST) + b_ref[...]


def _modulation(c, w_ada, b_ada):
    bsz, d = c.shape
    n = w_ada.shape[1]
    rows = -(-bsz // SUBLANES) * SUBLANES
    cpad = jnp.zeros((rows, d), F32).at[:bsz].set(c)
    tn = 1024
    out = pl.pallas_call(
        _mod_kernel,
        grid=(n // tn,),
        in_specs=[pl.BlockSpec((rows, d), lambda j: (0, 0)),
                  pl.BlockSpec((d, tn), lambda j: (0, j)),
                  pl.BlockSpec((1, tn), lambda j: (0, j))],
        out_specs=pl.BlockSpec((rows, tn), lambda j: (0, j)),
        out_shape=jax.ShapeDtypeStruct((rows, n), F32),
        compiler_params=_cparams(("parallel",)),
        name="adaln_modulation",
    )(cpad, w_ada, b_ada.reshape(1, n))
    return out[:bsz]


_IN_HG = 4
_IN_ATT = 3 * len(ATT_BRANCHES)


def _inproj_kernel(x_ref, sc_ref, sh_ref, w_ref, *outs):
    x = x_ref[...]
    h = _ln_rows(x) * (1.0 + sc_ref[...]) + sh_ref[...]
    hb = h.astype(BF16)
    col = 0
    for k, o_ref in enumerate(outs):
        width = o_ref.shape[-1]
        y = _dot(hb, w_ref[:, col:col + width])
        if _IN_HG <= k < _IN_HG + len(ATT_BRANCHES):
            y = y * (ATT_HEAD_DIM ** -0.5)
        o_ref[...] = y.astype(o_ref.dtype)
        col += width


def _in_projection(x2, mod4, w_in_bf, seq, tm):
    t, d = x2.shape
    steps_per_batch = seq // tm
    widths = [HG_WIDTH] * _IN_HG + [ATT_BW] * _IN_ATT
    dtypes = [BF16, F32, BF16, BF16] + [BF16] * _IN_ATT
    mod_spec = lambda row: pl.BlockSpec((None, None, 1, d),
                                        lambda i, row=row: (i // steps_per_batch, row, 0, 0))
    outs = pl.pallas_call(
        _inproj_kernel,
        grid=(t // tm,),
        in_specs=[pl.BlockSpec((tm, d), lambda i: (i, 0)),
                  mod_spec(1), mod_spec(0),
                  pl.BlockSpec(w_in_bf.shape, lambda i: (0, 0))],
        out_specs=[pl.BlockSpec((tm, w), lambda i: (i, 0)) for w in widths],
        out_shape=[jax.ShapeDtypeStruct((t, w), dt) for w, dt in zip(widths, dtypes)],
        compiler_params=_cparams(("parallel",)),
        name="ln_in_projection",
    )(x2, mod4, mod4, w_in_bf)
    return outs


HG_CHUNK = 64
HG_SUB = 8
HG_LEVELS = (64, 32, 16)


def _hgrn_chunk(q, z, iv, lb, st_t):
    c = HG_CHUNK
    f = lb + (1.0 - lb) * jax.nn.sigmoid(z)
    lf = jnp.log(f)
    kk = (1.0 - lb) * jax.nn.sigmoid(-z)
    r_i = lax.broadcasted_iota(jnp.int32, (c, c), 0)
    c_i = lax.broadcasted_iota(jnp.int32, (c, c), 1)
    tril = (c_i <= r_i).astype(F32)
    b = jnp.dot(tril, lf, preferred_element_type=F32, precision=lax.Precision.HIGHEST)

    row = lax.broadcasted_iota(jnp.int32, (c, HG_DK), 0)
    scores = jnp.zeros((c, c), F32)
    for m in HG_LEVELS:
        nb = c // m
        b3 = b.reshape(nb, m, HG_DK)
        piv = jnp.broadcast_to(b3[:, m // 2 - 1:m // 2, :], (nb, m, HG_DK)).reshape(c, HG_DK)
        second = (row % m) >= (m // 2)
        qt = jnp.where(second, q * jnp.exp(b - piv), 0.0)
        kt = jnp.where(second, 0.0, kk * jnp.exp(piv - b))
        s_m = _dot_nt(qt.astype(BF16), kt.astype(BF16))
        if nb > 1:
            s_m = jnp.where((r_i // m) == (c_i // m), s_m, 0.0)
        scores = scores + s_m
    sub = HG_SUB
    t_i = lax.broadcasted_iota(jnp.int32, (sub, 1), 0)
    lane = lax.broadcasted_iota(jnp.int32, (sub, c), 1)
    diag_rows = []
    for j in range(c // sub):
        qb = q[j * sub:(j + 1) * sub]
        kb = kk[j * sub:(j + 1) * sub]
        bb = b[j * sub:(j + 1) * sub]
        a_j = jnp.zeros((sub, c), F32)
        for s in range(sub):
            w = qb * kb[s:s + 1] * jnp.exp(bb - bb[s:s + 1])
            col = jnp.sum(w, axis=-1, keepdims=True)
            col = jnp.where(t_i >= s, col, 0.0)
            a_j = jnp.where(lane == j * sub + s, col, a_j)
        diag_rows.append(a_j)
    scores = scores + jnp.concatenate(diag_rows, axis=0)

    stb = st_t.astype(BF16)
    inter = _dot_nt((q * jnp.exp(b)).astype(BF16), stb)
    ivb = iv.astype(BF16)
    o = inter + _dot(scores.astype(BF16), ivb)
    b_last = b[c - 1:c]
    kdec = (kk * jnp.exp(b_last - b)).astype(BF16)
    st_new = st_t * jnp.exp(b_last) + _dot_tn(ivb, kdec)
    return o, st_new


def _hgrn_kernel(q_ref, f_ref, i_ref, g_ref, lbp_ref, nw_ref, o_ref, st_ref):
    @pl.when(pl.program_id(1) == 0)
    def _():
        st_ref[...] = jnp.zeros_like(st_ref)

    lbp = lbp_ref[...]
    e = jnp.exp(lbp - jnp.max(lbp, axis=0, keepdims=True))
    lb_all = e[0:1] / jnp.sum(e, axis=0, keepdims=True)
    outs = []
    for h in range(HG_HEADS):
        sl = slice(h * HG_DK, (h + 1) * HG_DK)
        o, st_new = _hgrn_chunk(q_ref[:, sl].astype(F32), f_ref[:, sl], i_ref[:, sl].astype(F32),
                                lb_all[:, sl], st_ref[h])
        st_ref[h] = st_new
        o = o * lax.rsqrt(jnp.mean(o * o, axis=-1, keepdims=True) + RMS_EPS)
        outs.append(o)
    o_all = jnp.concatenate(outs, axis=-1)
    g = g_ref[...].astype(F32)
    o_ref[...] = (o_all * nw_ref[...] * (g * jax.nn.sigmoid(g))).astype(o_ref.dtype)


def _hgrn2(hq, hf, hi, hg, lb_param, norm_w, bsz, seq):
    t = hq.shape[0]
    nc = seq // HG_CHUNK
    tok = lambda b, n: (b * nc + n, 0)
    spec = pl.BlockSpec((HG_CHUNK, HG_WIDTH), tok)
    return pl.pallas_call(
        _hgrn_kernel,
        grid=(bsz, nc),
        in_specs=[spec, spec, spec, spec,
                  pl.BlockSpec(lb_param.shape, lambda b, n: (0, 0)),
                  pl.BlockSpec((1, HG_WIDTH), lambda b, n: (0, 0))],
        out_specs=spec,
        out_shape=jax.ShapeDtypeStruct((t, HG_WIDTH), BF16),
        scratch_shapes=[pltpu.VMEM((HG_HEADS, HG_DK, HG_DK), F32)],
        compiler_params=_cparams(("parallel", "arbitrary")),
        name="hgrn2_scan",
    )(hq, hf, hi, hg, lb_param, norm_w.reshape(1, HG_WIDTH))


def _t5_bucket_np(dist):
    max_exact = REL_BUCKETS // 2
    n = np.maximum(dist, 0)
    nf = np.maximum(n, 1).astype(np.float32)
    large = max_exact + (np.log(nf / np.float32(max_exact)) / np.float32(math.log(REL_MAX_DIST / max_exact))
                         * np.float32(REL_BUCKETS - max_exact)).astype(np.int32)
    large = np.minimum(large, REL_BUCKETS - 1)
    return np.where(n < max_exact, n, large).astype(np.int32)


def _band_tables():
    w = ATT_BLOCK
    qi = np.arange(w)[:, None]
    ki = np.arange(2 * w)[None, :]
    m = w + qi - ki
    band = (m >= 0) & (m <= w)
    buckets = np.stack([_t5_bucket_np(m * dil) for _, dil in ATT_BRANCHES])
    return buckets, band


def _bias_kernel(rb_ref, bucket_ref, o_ref):
    g = pl.program_id(0)
    w = ATT_BLOCK
    bucket = bucket_ref[...]
    qi = lax.broadcasted_iota(jnp.int32, (w, 2 * w), 0)
    ki = lax.broadcasted_iota(jnp.int32, (w, 2 * w), 1)
    m = w + qi - ki
    band = (m >= 0) & (m <= w)
    for h in range(ATT_HEADS_PER_BRANCH):
        acc = jnp.zeros((w, 2 * w), F32)
        for c in range(REL_BUCKETS):
            acc = jnp.where(bucket == c, rb_ref[c, g * ATT_HEADS_PER_BRANCH + h], acc)
        full = jnp.where(band, acc, NEG_INF)
        o_ref[1, h] = full
        o_ref[0, h] = jnp.where(ki >= w, full, NEG_INF)


def _bias_tables(rel_bias):
    buckets, _ = _band_tables()
    g = len(ATT_BRANCHES)
    w = ATT_BLOCK
    return pl.pallas_call(
        _bias_kernel,
        grid=(g,),
        in_specs=[pl.BlockSpec(memory_space=pltpu.SMEM),
                  pl.BlockSpec((None, w, 2 * w), lambda i: (i, 0, 0))],
        out_specs=pl.BlockSpec((None, 2, ATT_HEADS_PER_BRANCH, w, 2 * w), lambda i: (i, 0, 0, 0, 0)),
        out_shape=jax.ShapeDtypeStruct((g, 2, ATT_HEADS_PER_BRANCH, w, 2 * w), F32),
        compiler_params=_cparams(("parallel",)),
        name="rel_bias_tables",
    )(rel_bias, jnp.asarray(buckets))


def _attn_kernel(q_ref, kp_ref, kc_ref, vp_ref, vc_ref, bias_ref, o_ref, lse_ref):
    n = pl.program_id(2)
    w = ATT_BLOCK
    hb = ATT_HEADS_PER_BRANCH
    q = q_ref[...]
    lane = lax.broadcasted_iota(jnp.int32, (w, ATT_BW), 1) // ATT_HEAD_DIM
    q4 = jnp.concatenate([jnp.where(lane == h, q, jnp.zeros_like(q)) for h in range(hb)], axis=0)
    kk = jnp.concatenate([kp_ref[...], kc_ref[...]], axis=0)
    vv = jnp.concatenate([vp_ref[...], vc_ref[...]], axis=0)
    s4 = _dot_nt(q4, kk)
    bias = bias_ref[jnp.minimum(n, 1)]
    s4 = s4 + bias.reshape(hb * w, 2 * w)
    mx = jnp.max(s4, axis=-1, keepdims=True)
    p = jnp.exp(s4 - mx)
    l = jnp.sum(p, axis=-1, keepdims=True)
    o4 = _dot((p / l).astype(vv.dtype), vv)
    lse4 = mx + jnp.log(l)
    o = jnp.zeros((w, ATT_BW), F32)
    lse = jnp.zeros((w, ATT_BW), F32)
    for h in range(hb):
        o = jnp.where(lane == h, o4[h * w:(h + 1) * w], o)
        lse = jnp.where(lane == h, lse4[h * w:(h + 1) * w], lse)
    o_ref[...] = o.astype(o_ref.dtype)
    lse_ref[...] = lse


def _dilated_attention(q, k, v, bias_g, bsz, seq, dilation):
    t = q.shape[0]
    w = ATT_BLOCK
    l = seq // dilation
    nb = l // w
    view = lambda a: a.reshape(bsz, l, dilation * ATT_BW)
    cur = pl.BlockSpec((None, w, ATT_BW), lambda b, r, n: (b, n, r))
    prev = pl.BlockSpec((None, w, ATT_BW), lambda b, r, n: (b, jnp.maximum(n - 1, 0), r))
    o, lse = pl.pallas_call(
        _attn_kernel,
        grid=(bsz, dilation, nb),
        in_specs=[cur, prev, cur, prev, cur,
                  pl.BlockSpec(bias_g.shape, lambda b, r, n: (0, 0, 0, 0))],
        out_specs=[cur, cur],
        out_shape=[jax.ShapeDtypeStruct((bsz, l, dilation * ATT_BW), BF16),
                   jax.ShapeDtypeStruct((bsz, l, dilation * ATT_BW), F32)],
        compiler_params=_cparams(("parallel", "parallel", "arbitrary")),
        name=f"dilated_attention_d{dilation}",
    )(view(q), view(k), view(k), view(v), view(v), bias_g)
    return o.reshape(t, ATT_BW), lse.reshape(t, ATT_BW)


def _split_bf16(a):
    hi = a.astype(BF16)
    lo = (a - hi.astype(F32)).astype(BF16)
    return hi, lo


H2P_CHUNKS = 4


def _bf16_bits(a):
    u = lax.bitcast_convert_type(a, jnp.uint32)
    return u + jnp.uint32(0x7FFF) + ((u >> 16) & jnp.uint32(1))


def _outproj_kernel(yhg_ref, o1_ref, o2_ref, o3_ref, l1_ref, l2_ref, l3_ref, x_ref,
                    g1_ref, sc2_ref, sh2_ref, wout_ref, lng_ref, lnb_ref, wrt_ref,
                    x1_ref, h2_ref, h2p_ref, lgt_ref):
    l1, l2, l3 = l1_ref[...], l2_ref[...], l3_ref[...]
    mx = jnp.maximum(jnp.maximum(l1, l2), l3)
    e1, e2, e3 = jnp.exp(l1 - mx), jnp.exp(l2 - mx), jnp.exp(l3 - mx)
    den = e1 + e2 + e3
    att = (e1 / den) * o1_ref[...].astype(F32) + (e2 / den) * o2_ref[...].astype(F32) \
        + (e3 / den) * o3_ref[...].astype(F32)
    mix = _dot(yhg_ref[...], wout_ref[:HG_WIDTH, :]) + _dot(att.astype(BF16), wout_ref[HG_WIDTH:, :])
    x1 = _ln_rows(DN_ALPHA * x_ref[...] + g1_ref[...] * mix) * lng_ref[...] + lnb_ref[...]
    x1_ref[...] = x1
    h2 = _ln_rows(x1) * (1.0 + sc2_ref[...]) + sh2_ref[...]
    h2_ref[...] = h2
    for cidx in range(H2P_CHUNKS):
        lo = _bf16_bits(h2[:, 2 * LANES * cidx:2 * LANES * cidx + LANES])
        hi = _bf16_bits(h2[:, 2 * LANES * cidx + LANES:2 * LANES * (cidx + 1)])
        word = (lo >> 16) | (hi & jnp.uint32(0xFFFF0000))
        h2p_ref[:, cidx, :] = lax.bitcast_convert_type(word, jnp.int32)
    h_hi, h_lo = _split_bf16(h2)
    w_hi, w_lo = _split_bf16(wrt_ref[...])
    lgt_ref[...] = _dot_nt(w_hi, h_hi) + (_dot_nt(w_hi, h_lo) + _dot_nt(w_lo, h_hi))


def _out_projection(yhg, os_, ls_, x2, mod4, w_out_bf, ln_g, ln_b, w_router_t, seq, tm):
    t, d = x2.shape
    spb = seq // tm
    ne = w_router_t.shape[0]
    row = lambda w: pl.BlockSpec((tm, w), lambda i: (i, 0))
    mod_spec = lambda r: pl.BlockSpec((None, None, 1, d), lambda i, r=r: (i // spb, r, 0, 0))
    full = lambda a: pl.BlockSpec(a.shape, lambda i: (0,) * a.ndim)
    ln_g2, ln_b2 = ln_g.reshape(1, d), ln_b.reshape(1, d)
    return pl.pallas_call(
        _outproj_kernel,
        grid=(t // tm,),
        in_specs=[row(HG_WIDTH)] + [row(ATT_BW)] * 6 + [row(d),
                  mod_spec(2), mod_spec(4), mod_spec(3),
                  full(w_out_bf), full(ln_g2), full(ln_b2), full(w_router_t)],
        out_specs=[row(d), row(d), pl.BlockSpec((tm, H2P_CHUNKS, LANES), lambda i: (i, 0, 0)),
                   pl.BlockSpec((ne, tm), lambda i: (0, i))],
        out_shape=[jax.ShapeDtypeStruct((t, d), F32), jax.ShapeDtypeStruct((t, d), F32),
                   jax.ShapeDtypeStruct((t, H2P_CHUNKS, LANES), jnp.int32),
                   jax.ShapeDtypeStruct((ne, t), F32)],
        compiler_params=_cparams(("parallel",)),
        name="merge_outproj_ln",
    )(yhg, *os_, *ls_, x2, mod4, mod4, mod4, w_out_bf, ln_g2, ln_b2, w_router_t)


def _argmax_rows(cur, iota, nrows):
    m = jnp.max(cur, axis=0, keepdims=True)
    idx = jnp.min(jnp.where(cur == m, iota, nrows), axis=0, keepdims=True)
    return m, idx, iota == idx


def _route_kernel(lgt_ref, rb_ref, eidx_ref, w_ref, rank_ref, cnt_ref, carry):
    ne = N_EXPERTS
    gsz = ne // N_GROUPS
    tt = lgt_ref.shape[1]

    @pl.when(pl.program_id(0) == 0)
    def _():
        carry[...] = jnp.zeros_like(carry)

    sc = jax.nn.sigmoid(lgt_ref[...])
    biased = sc + rb_ref[...]
    g3 = biased.reshape(N_GROUPS, gsz, tt)
    io3 = lax.broadcasted_iota(jnp.int32, (N_GROUPS, gsz, tt), 1)
    m1 = jnp.max(g3, axis=1, keepdims=True)
    first = jnp.min(jnp.where(g3 == m1, io3, gsz), axis=1, keepdims=True)
    m2 = jnp.max(jnp.where(io3 == first, NEG_INF, g3), axis=1, keepdims=True)
    gs = (m1 + m2).reshape(N_GROUPS, tt)
    io8 = lax.broadcasted_iota(jnp.int32, (N_GROUPS, tt), 0)
    sel = jnp.zeros((N_GROUPS, tt), jnp.int32)
    cur = gs
    for _ in range(TOPK_GROUPS):
        _, _, pick = _argmax_rows(cur, io8, N_GROUPS)
        sel = jnp.where(pick, 1, sel)
        cur = jnp.where(pick, NEG_INF, cur)
    masked = jnp.where(sel.reshape(N_GROUPS, 1, tt) > 0, g3, NEG_INF).reshape(ne, tt)
    ioe = lax.broadcasted_iota(jnp.int32, (ne, tt), 0)
    cur = masked
    idxs, ws, picks = [], [], []
    for _ in range(TOP_K):
        _, idx, pick = _argmax_rows(cur, ioe, ne)
        idxs.append(idx)
        picks.append(pick)
        ws.append(jnp.sum(jnp.where(pick, sc, 0.0), axis=0, keepdims=True))
        cur = jnp.where(pick, NEG_INF, cur)
    wk = jnp.concatenate(ws, axis=0)
    eidx_ref[...] = jnp.concatenate(idxs, axis=0)
    w_ref[...] = wk / jnp.sum(wk, axis=0, keepdims=True) * ROUTED_SCALE
    chosen = jnp.where(cur == NEG_INF, jnp.where(masked == NEG_INF, 0.0, 1.0), 0.0)
    r_i = lax.broadcasted_iota(jnp.int32, (tt, tt), 0)
    c_i = lax.broadcasted_iota(jnp.int32, (tt, tt), 1)
    before = jnp.where(r_i < c_i, 1.0, 0.0).astype(BF16)
    pref = _dot(chosen.astype(BF16), before) + carry[...]
    rank_ref[...] = jnp.concatenate(
        [jnp.sum(jnp.where(p, pref, 0.0), axis=0, keepdims=True) for p in picks], axis=0).astype(jnp.int32)
    carry[...] = carry[...] + jnp.sum(chosen, axis=1, keepdims=True)
    cnt_ref[...] = carry[...]


def _route(logits_t, router_bias, tt):
    ne, t = logits_t.shape
    tok = pl.BlockSpec((TOP_K, tt), lambda i: (0, i))
    return pl.pallas_call(
        _route_kernel,
        grid=(t // tt,),
        in_specs=[pl.BlockSpec((ne, tt), lambda i: (0, i)),
                  pl.BlockSpec((ne, 1), lambda i: (0, 0))],
        out_specs=[tok, tok, tok, pl.BlockSpec((ne, 1), lambda i: (0, 0))],
        out_shape=[jax.ShapeDtypeStruct((TOP_K, t), jnp.int32), jax.ShapeDtypeStruct((TOP_K, t), F32),
                   jax.ShapeDtypeStruct((TOP_K, t), jnp.int32), jax.ShapeDtypeStruct((ne, 1), F32)],
        scratch_shapes=[pltpu.VMEM((ne, 1), F32)],
        compiler_params=_cparams(("arbitrary",)),
        name="router_topk",
    )(logits_t, router_bias.reshape(ne, 1))


def _plan_kernel(cnt_ref, eidx_ref, rank_ref, dest_ref, be_ref, nu_ref):
    ne = N_EXPERTS
    tt = eidx_ref.shape[1]
    nblk = be_ref.shape[1]
    cnt = cnt_ref[...].astype(jnp.int32)
    padded = ((cnt + (MOE_BLOCK - 1)) // MOE_BLOCK) * MOE_BLOCK
    r_i = lax.broadcasted_iota(jnp.int32, (ne, ne), 0)
    c_i = lax.broadcasted_iota(jnp.int32, (ne, ne), 1)
    incl = jnp.where(c_i <= r_i, 1.0, 0.0)
    pend = jnp.dot(incl, jnp.broadcast_to(padded.astype(F32), (ne, LANES)),
                   preferred_element_type=F32, precision=lax.Precision.HIGHEST)[:, 0:1]
    pend = pend.astype(jnp.int32)
    pstart = pend - padded
    ioe = lax.broadcasted_iota(jnp.int32, (ne, tt), 0)
    rows = []
    for k in range(TOP_K):
        sel = ioe == eidx_ref[k:k + 1, :]
        rows.append(jnp.sum(jnp.where(sel, pstart, 0), axis=0, keepdims=True))
    dest_ref[...] = jnp.concatenate(rows, axis=0) + rank_ref[...]
    blk0 = lax.broadcasted_iota(jnp.int32, (ne, nblk), 1) * MOE_BLOCK
    be = jnp.sum(jnp.where(pend <= blk0, 1, 0), axis=0, keepdims=True)
    be_ref[...] = jnp.minimum(be, ne - 1)
    nu_ref[...] = jnp.max(pend, axis=0, keepdims=True) // MOE_BLOCK


def _dispatch_plan(cnt, eidx, rank, tt):
    k, t = eidx.shape
    n_blocks = -(-(t * k) // MOE_BLOCK) + N_EXPERTS
    tok = pl.BlockSpec((k, tt), lambda i: (0, i))
    dest, block_e, n_used = pl.pallas_call(
        _plan_kernel,
        grid=(t // tt,),
        in_specs=[pl.BlockSpec(cnt.shape, lambda i: (0, 0)), tok, tok],
        out_specs=[tok, pl.BlockSpec((1, n_blocks), lambda i: (0, 0)), pl.BlockSpec((1, 1), lambda i: (0, 0))],
        out_shape=[jax.ShapeDtypeStruct((k, t), jnp.int32), jax.ShapeDtypeStruct((1, n_blocks), jnp.int32),
                   jax.ShapeDtypeStruct((1, 1), jnp.int32)],
        compiler_params=_cparams(("arbitrary",)),
        name="dispatch_plan",
    )(cnt, eidx, rank)
    return dest, block_e.reshape(n_blocks), n_used.reshape(1), n_blocks


SC_CORES = 2
SC_SUBCORES = 16
SC_LANES = 16
SC_CHUNK = 4096


def _row_tables(dest, wk, n_rows):
    k, t = dest.shape
    a = k * t
    nw = SC_CORES * SC_SUBCORES
    per_w = n_rows // nw
    assert n_rows % (nw * SC_LANES) == 0 and t % SC_CHUNK == 0
    mesh = plsc.VectorSubcoreMesh(core_axis_name="c", subcore_axis_name="s")
    cp = pltpu.CompilerParams()
    if "needs_layout_passes" in pltpu.CompilerParams.__dataclass_fields__:
        cp = dataclasses.replace(cp, needs_layout_passes=False)

    def body(dest_hbm, w_hbm, tok_out, w_out, dbuf, wbuf, tloc, wloc):
        wid = lax.axis_index("s") * SC_CORES + lax.axis_index("c")
        base = wid * per_w

        @pl.loop(0, per_w // SC_LANES)
        def _(i):
            tloc[pl.ds(i * SC_LANES, SC_LANES)] = jnp.zeros((SC_LANES,), jnp.int32)
            wloc[pl.ds(i * SC_LANES, SC_LANES)] = jnp.zeros((SC_LANES,), F32)

        lane = lax.iota(jnp.int32, SC_LANES)

        @pl.loop(0, a // SC_CHUNK)
        def _(c):
            pltpu.sync_copy(dest_hbm.at[pl.ds(c * SC_CHUNK, SC_CHUNK)], dbuf)
            pltpu.sync_copy(w_hbm.at[pl.ds(c * SC_CHUNK, SC_CHUNK)], wbuf)
            tok0 = lax.rem(c * SC_CHUNK, t)

            @pl.loop(0, SC_CHUNK // SC_LANES)
            def _(j):
                loc = dbuf[pl.ds(j * SC_LANES, SC_LANES)] - base
                mine = jnp.logical_and(loc >= 0, loc < per_w)
                loc = jnp.where(mine, loc, 0)
                plsc.store_scatter(tloc, [loc], tok0 + j * SC_LANES + lane, mask=mine)
                plsc.store_scatter(wloc, [loc], wbuf[pl.ds(j * SC_LANES, SC_LANES)], mask=mine)

        pltpu.sync_copy(tloc, tok_out.at[pl.ds(base, per_w)])
        pltpu.sync_copy(wloc, w_out.at[pl.ds(base, per_w)])

    fn = pl.kernel(
        body,
        out_type=(jax.ShapeDtypeStruct((n_rows,), jnp.int32), jax.ShapeDtypeStruct((n_rows,), F32)),
        mesh=mesh,
        scratch_types=[pltpu.VMEM((SC_CHUNK,), jnp.int32), pltpu.VMEM((SC_CHUNK,), F32),
                       pltpu.VMEM((per_w,), jnp.int32), pltpu.VMEM((per_w,), F32)],
        compiler_params=cp,
        name="row_tables",
    )
    return fn(dest.reshape(a), wk.reshape(a))


ROW_TILE = (SUBLANES, LANES)


def _gather_rows(h2p_ref, tok_ref, buf):
    for j in range(MOE_BLOCK):
        buf[pl.ds(H2P_CHUNKS * j, H2P_CHUNKS), :] = h2p_ref[tok_ref[0, j]]


def _ffn_kernel(be_ref, nu_ref, tokc_ref, tokn_ref, w_ref, h2p_ref, wg_ref, wu_ref, wd_ref,
                y_ref, buf_even, buf_odd):
    i = pl.program_id(0)
    n_used = nu_ref[0]
    odd = (i % 2) == 1

    @pl.when(i == 0)
    def _():
        _gather_rows(h2p_ref, tokc_ref, buf_even)

    def step(cur, nxt):
        _gather_rows(h2p_ref, tokn_ref, nxt)
        parts = []
        for cidx in range(H2P_CHUNKS):
            word = cur[pl.ds(cidx, MOE_BLOCK, stride=H2P_CHUNKS), :]
            parts.append(lax.bitcast_convert_type(word << 16, F32))
            parts.append(lax.bitcast_convert_type(word & jnp.int32(-65536), F32))
        x = jnp.concatenate(parts, axis=1)
        g = _dot(x, wg_ref[...])
        u = _dot(x, wu_ref[...])
        hm = (g * jax.nn.sigmoid(g)) * u
        y = _dot(hm, wd_ref[...]) * w_ref[...]
        y_ref[...] = pltpu.einshape("r(cl)->rcl", y, c=SUBLANES)

    used = i < n_used

    @pl.when(jnp.logical_and(used, jnp.logical_not(odd)))
    def _():
        step(buf_even, buf_odd)

    @pl.when(jnp.logical_and(used, odd))
    def _():
        step(buf_odd, buf_even)

    @pl.when(jnp.logical_not(used))
    def _():
        y_ref[...] = jnp.zeros_like(y_ref)


def _routed_experts(h2p, row_token, row_w, block_e, n_used, n_blocks, wg, wu, wd):
    d = wg.shape[1]
    de = wg.shape[2]
    nb = n_blocks
    tok3 = row_token.reshape(nb, 1, MOE_BLOCK)
    w3 = row_w.reshape(nb, MOE_BLOCK, 1)
    grid_spec = pltpu.PrefetchScalarGridSpec(
        num_scalar_prefetch=2,
        grid=(nb,),
        in_specs=[
            pl.BlockSpec((None, 1, MOE_BLOCK), lambda i, be, nu: (i, 0, 0), memory_space=pltpu.SMEM),
            pl.BlockSpec((None, 1, MOE_BLOCK), lambda i, be, nu: (jnp.minimum(i + 1, nb - 1), 0, 0),
                         memory_space=pltpu.SMEM),
            pl.BlockSpec((None, MOE_BLOCK, 1), lambda i, be, nu: (i, 0, 0)),
            pl.BlockSpec(h2p.shape, lambda i, be, nu: (0, 0, 0), pipeline_mode=pl.Buffered(1)),
            pl.BlockSpec((None, d, de), lambda i, be, nu: (be[i], 0, 0)),
            pl.BlockSpec((None, d, de), lambda i, be, nu: (be[i], 0, 0)),
            pl.BlockSpec((None, de, d), lambda i, be, nu: (be[i], 0, 0)),
        ],
        out_specs=pl.BlockSpec((MOE_BLOCK,) + ROW_TILE,
                               lambda i, be, nu: (jnp.minimum(i, nu[0]), 0, 0)),
        scratch_shapes=[pltpu.VMEM((MOE_BLOCK * H2P_CHUNKS, LANES), jnp.int32)] * 2,
    )
    return pl.pallas_call(
        _ffn_kernel,
        grid_spec=grid_spec,
        out_shape=jax.ShapeDtypeStruct(((nb + 1) * MOE_BLOCK,) + ROW_TILE, F32),
        compiler_params=_cparams(("arbitrary",)),
        name="routed_experts",
    )(block_e, n_used, tok3, tok3, w3, h2p, wg, wu, wd)


COMBINE_TOKENS = 128


def _row_copies(src_hbm, idx_ref, buf, sem):
    return [pltpu.make_async_copy(src_hbm.at[idx_ref[k, j]], buf.at[k, j], sem)
            for k in range(TOP_K) for j in range(COMBINE_TOKENS)]


def _final_kernel(dc_ref, dn_ref, x1_ref, h2_ref, g2_ref, wsg_ref, wsu_ref, wsd_ref, lng_ref, lnb_ref,
                  y_hbm, o_ref, ybuf, sems):
    i = pl.program_id(0)
    nsteps = pl.num_programs(0)
    slot = i % 2

    def issue(d_ref, s):
        for n, cp in enumerate(_row_copies(y_hbm, d_ref, ybuf.at[s], sems.at[s])):
            cp.start(priority=n % 2)

    @pl.when(i == 0)
    def _():
        issue(dc_ref, 0)

    @pl.when(i + 1 < nsteps)
    def _():
        issue(dn_ref, 1 - slot)

    hb = h2_ref[...].astype(BF16)
    g = _dot(hb, wsg_ref[...])
    u = _dot(hb, wsu_ref[...])
    shared = _dot(((g * jax.nn.sigmoid(g)) * u).astype(BF16), wsd_ref[...])
    for cp in _row_copies(y_hbm, dc_ref, ybuf.at[slot], sems.at[slot]):
        cp.wait()
    routed = pltpu.einshape("tcl->t(cl)", jnp.sum(ybuf[slot], axis=0))
    x2 = DN_ALPHA * x1_ref[...] + g2_ref[...] * (routed + shared)
    o_ref[...] = _ln_rows(x2) * lng_ref[...] + lnb_ref[...]


def _combine_final(x1, h2, mod4, dest, y_rows, wsg, wsu, wsd, ln_g, ln_b, seq):
    t, d = x1.shape
    tt = COMBINE_TOKENS
    nsteps = t // tt
    spb = seq // tt
    row = pl.BlockSpec((tt, d), lambda i: (i, 0))
    full = lambda a: pl.BlockSpec(a.shape, lambda i: (0,) * a.ndim)
    ln_g2, ln_b2 = ln_g.reshape(1, d), ln_b.reshape(1, d)
    return pl.pallas_call(
        _final_kernel,
        grid=(nsteps,),
        in_specs=[pl.BlockSpec((TOP_K, tt), lambda i: (0, i), memory_space=pltpu.SMEM),
                  pl.BlockSpec((TOP_K, tt), lambda i: (0, jnp.minimum(i + 1, nsteps - 1)),
                               memory_space=pltpu.SMEM),
                  row, row,
                  pl.BlockSpec((None, None, 1, d), lambda i: (i // spb, 5, 0, 0)),
                  full(wsg), full(wsu), full(wsd), full(ln_g2), full(ln_b2),
                  pl.BlockSpec(memory_space=pl.ANY)],
        out_specs=row,
        out_shape=jax.ShapeDtypeStruct((t, d), F32),
        scratch_shapes=[pltpu.VMEM((2, TOP_K, tt) + ROW_TILE, F32),
                        pltpu.SemaphoreType.DMA((2,))],
        compiler_params=_cparams(("arbitrary",)),
        name="shared_combine_ln",
    )(dest, dest, x1, h2, mod4, wsg, wsu, wsd, ln_g2, ln_b2, y_rows)


def kernel(x, c, w_ada, b_ada, w_in, hg_lower_bound, hg_norm_w, rel_bias, w_out, ln1_g, ln1_b, w_router,
           router_bias, w_e_gate, w_e_up, w_e_down, w_sh_gate, w_sh_up, w_sh_down, ln2_g, ln2_b):
    bsz, seq, d = x.shape
    t = bsz * seq
    assert w_ada.shape[0] == DEPTH and seq % (ATT_BRANCHES[-1][0]) == 0
    x2 = x.reshape(t, d)
    bias = _bias_tables(rel_bias)
    for l in range(DEPTH):
        mod4 = _modulation(c, w_ada[l], b_ada[l]).reshape(bsz, 6, 1, d)
        w_in_bf = _cast_bf16(w_in[l], 256)
        hq, hf, hi, hg, *qkv = _in_projection(x2, mod4, w_in_bf, seq, 512)
        nbr = len(ATT_BRANCHES)
        y_hg = _hgrn2(hq, hf, hi, hg, hg_lower_bound, hg_norm_w[l], bsz, seq)
        os_, ls_ = [], []
        for g, (_, dil) in enumerate(ATT_BRANCHES):
            o, lse = _dilated_attention(qkv[g], qkv[nbr + g], qkv[2 * nbr + g], bias[g], bsz, seq, dil)
            os_.append(o)
            ls_.append(lse)
        w_out_bf = _cast_bf16(w_out[l], 256)
        x1, h2, h2p, logits_t = _out_projection(y_hg, os_, ls_, x2, mod4, w_out_bf, ln1_g[l], ln1_b[l],
                                                w_router[l].T, seq, 512)
        eidx, wk, rank, cnt = _route(logits_t, router_bias[l], 256)
        dest, block_e, n_used, n_blocks = _dispatch_plan(cnt, eidx, rank, 512)
        row_token, row_w = _row_tables(dest, wk, n_blocks * MOE_BLOCK)
        y_rows = _routed_experts(h2p, row_token, row_w, block_e, n_used, n_blocks,
                                 w_e_gate[l], w_e_up[l], w_e_down[l])
        x2 = _combine_final(x1, h2, mod4, dest, y_rows,
                            _cast_bf16(w_sh_gate[l], 256), _cast_bf16(w_sh_up[l], 256),
                            _cast_bf16(w_sh_down[l], 256), ln2_g[l], ln2_b[l], seq)
    return x2.reshape(bsz, seq, d)
```

```python
import dataclasses
import math

import jax
import jax.numpy as jnp
import numpy as np
from jax import lax
from jax.experimental import pallas as pl
from jax.experimental.pallas import tpu as pltpu
from jax.experimental.pallas import tpu_sc as plsc

HG_HEADS = 4
HG_DK = 128
HG_WIDTH = HG_HEADS * HG_DK
ATT_BRANCHES = ((128, 1), (512, 4), (2048, 16))
ATT_HEADS_PER_BRANCH = 4
ATT_HEAD_DIM = 64
ATT_BW = ATT_HEADS_PER_BRANCH * ATT_HEAD_DIM
ATT_BLOCK = 128
REL_BUCKETS = 32
REL_MAX_DIST = 2048
N_EXPERTS = 256
TOP_K = 8
N_GROUPS = 8
TOPK_GROUPS = 4
ROUTED_SCALE = 2.5
MOE_BLOCK = 128
DEPTH = 1
DN_ALPHA = (2 * DEPTH) ** 0.25
LN_EPS = 1e-5
RMS_EPS = 1e-6

LANES = 128
SUBLANES = 8
VMEM_LIMIT_BYTES = 56 * 1024 * 1024

F32 = jnp.float32
BF16 = jnp.bfloat16
NEG_INF = float("-inf")


def _cparams(sem):
    return pltpu.CompilerParams(dimension_semantics=sem, vmem_limit_bytes=VMEM_LIMIT_BYTES)


def _ln_rows(x):
    mu = jnp.mean(x, axis=-1, keepdims=True)
    xc = x - mu
    var = jnp.mean(xc * xc, axis=-1, keepdims=True)
    return xc * lax.rsqrt(var + LN_EPS)


def _dot(a, b):
    return jnp.dot(a, b, preferred_element_type=F32)


def _dot_nt(a, b):
    return lax.dot_general(a, b, (((1,), (1,)), ((), ())), preferred_element_type=F32)


def _dot_tn(a, b):
    return lax.dot_general(a, b, (((0,), (0,)), ((), ())), preferred_element_type=F32)


def _cast_kernel(w_ref, o_ref):
    o_ref[...] = w_ref[...].astype(o_ref.dtype)


def _cast_bf16(w, rows_per_step):
    r, c = w.shape
    return pl.pallas_call(
        _cast_kernel,
        grid=(r // rows_per_step,),
        in_specs=[pl.BlockSpec((rows_per_step, c), lambda i: (i, 0))],
        out_specs=pl.BlockSpec((rows_per_step, c), lambda i: (i, 0)),
        out_shape=jax.ShapeDtypeStruct((r, c), BF16),
        compiler_params=_cparams(("parallel",)),
        name="cast_bf16",
    )(w)


def _mod_kernel(c_ref, w_ref, b_ref, o_ref):
    c = c_ref[...]
    cond = c * jax.nn.sigmoid(c)
    o_ref[...] = jnp.dot(cond, w_ref[...], preferred_element_type=F32,
                         precision=lax.Precision.HIGHEST) + b_ref[...]


def _modulation(c, w_ada, b_ada):
    bsz, d = c.shape
    n = w_ada.shape[1]
    rows = -(-bsz // SUBLANES) * SUBLANES
    cpad = jnp.zeros((rows, d), F32).at[:bsz].set(c)
    tn = 1024
    out = pl.pallas_call(
        _mod_kernel,
        grid=(n // tn,),
        in_specs=[pl.BlockSpec((rows, d), lambda j: (0, 0)),
                  pl.BlockSpec((d, tn), lambda j: (0, j)),
                  pl.BlockSpec((1, tn), lambda j: (0, j))],
        out_specs=pl.BlockSpec((rows, tn), lambda j: (0, j)),
        out_shape=jax.ShapeDtypeStruct((rows, n), F32),
        compiler_params=_cparams(("parallel",)),
        name="adaln_modulation",
    )(cpad, w_ada, b_ada.reshape(1, n))
    return out[:bsz]


_IN_HG = 4
_IN_ATT = 3 * len(ATT_BRANCHES)


def _inproj_kernel(x_ref, sc_ref, sh_ref, w_ref, *outs):
    x = x_ref[...]
    h = _ln_rows(x) * (1.0 + sc_ref[...]) + sh_ref[...]
    hb = h.astype(BF16)
    col = 0
    for k, o_ref in enumerate(outs):
        width = o_ref.shape[-1]
        y = _dot(hb, w_ref[:, col:col + width])
        if _IN_HG <= k < _IN_HG + len(ATT_BRANCHES):
            y = y * (ATT_HEAD_DIM ** -0.5)
        o_ref[...] = y.astype(o_ref.dtype)
        col += width


def _in_projection(x2, mod4, w_in_bf, seq, tm):
    t, d = x2.shape
    steps_per_batch = seq // tm
    widths = [HG_WIDTH] * _IN_HG + [ATT_BW] * _IN_ATT
    dtypes = [BF16, F32, BF16, BF16] + [BF16] * _IN_ATT
    mod_spec = lambda row: pl.BlockSpec((None, None, 1, d),
                                        lambda i, row=row: (i // steps_per_batch, row, 0, 0))
    outs = pl.pallas_call(
        _inproj_kernel,
        grid=(t // tm,),
        in_specs=[pl.BlockSpec((tm, d), lambda i: (i, 0)),
                  mod_spec(1), mod_spec(0),
                  pl.BlockSpec(w_in_bf.shape, lambda i: (0, 0))],
        out_specs=[pl.BlockSpec((tm, w), lambda i: (i, 0)) for w in widths],
        out_shape=[jax.ShapeDtypeStruct((t, w), dt) for w, dt in zip(widths, dtypes)],
        compiler_params=_cparams(("parallel",)),
        name="ln_in_projection",
    )(x2, mod4, mod4, w_in_bf)
    return outs


HG_CHUNK = 64
HG_SUB = 8
HG_LEVELS = (64, 32, 16)


def _hgrn_chunk(q, z, iv, lb, st_t):
    c = HG_CHUNK
    f = lb + (1.0 - lb) * jax.nn.sigmoid(z)
    lf = jnp.log(f)
    kk = (1.0 - lb) * jax.nn.sigmoid(-z)
    r_i = lax.broadcasted_iota(jnp.int32, (c, c), 0)
    c_i = lax.broadcasted_iota(jnp.int32, (c, c), 1)
    tril = (c_i <= r_i).astype(F32)
    b = jnp.dot(tril, lf, preferred_element_type=F32, precision=lax.Precision.HIGHEST)

    row = lax.broadcasted_iota(jnp.int32, (c, HG_DK), 0)
    scores = jnp.zeros((c, c), F32)
    for m in HG_LEVELS:
        nb = c // m
        b3 = b.reshape(nb, m, HG_DK)
        piv = jnp.broadcast_to(b3[:, m // 2 - 1:m // 2, :], (nb, m, HG_DK)).reshape(c, HG_DK)
        second = (row % m) >= (m // 2)
        qt = jnp.where(second, q * jnp.exp(b - piv), 0.0)
        kt = jnp.where(second, 0.0, kk * jnp.exp(piv - b))
        s_m = _dot_nt(qt.astype(BF16), kt.astype(BF16))
        if nb > 1:
            s_m = jnp.where((r_i // m) == (c_i // m), s_m, 0.0)
        scores = scores + s_m
    sub = HG_SUB
    t_i = lax.broadcasted_iota(jnp.int32, (sub, 1), 0)
    lane = lax.broadcasted_iota(jnp.int32, (sub, c), 1)
    diag_rows = []
    for j in range(c // sub):
        qb = q[j * sub:(j + 1) * sub]
        kb = kk[j * sub:(j + 1) * sub]
        bb = b[j * sub:(j + 1) * sub]
        a_j = jnp.zeros((sub, c), F32)
        for s in range(sub):
            w = qb * kb[s:s + 1] * jnp.exp(bb - bb[s:s + 1])
            col = jnp.sum(w, axis=-1, keepdims=True)
            col = jnp.where(t_i >= s, col, 0.0)
            a_j = jnp.where(lane == j * sub + s, col, a_j)
        diag_rows.append(a_j)
    scores = scores + jnp.concatenate(diag_rows, axis=0)

    stb = st_t.astype(BF16)
    inter = _dot_nt((q * jnp.exp(b)).astype(BF16), stb)
    ivb = iv.astype(BF16)
    o = inter + _dot(scores.astype(BF16), ivb)
    b_last = b[c - 1:c]
    kdec = (kk * jnp.exp(b_last - b)).astype(BF16)
    st_new = st_t * jnp.exp(b_last) + _dot_tn(ivb, kdec)
    return o, st_new


def _hgrn_kernel(q_ref, f_ref, i_ref, g_ref, lbp_ref, nw_ref, o_ref, st_ref):
    @pl.when(pl.program_id(1) == 0)
    def _():
        st_ref[...] = jnp.zeros_like(st_ref)

    lbp = lbp_ref[...]
    e = jnp.exp(lbp - jnp.max(lbp, axis=0, keepdims=True))
    lb_all = e[0:1] / jnp.sum(e, axis=0, keepdims=True)
    outs = []
    for h in range(HG_HEADS):
        sl = slice(h * HG_DK, (h + 1) * HG_DK)
        o, st_new = _hgrn_chunk(q_ref[:, sl].astype(F32), f_ref[:, sl], i_ref[:, sl].astype(F32),
                                lb_all[:, sl], st_ref[h])
        st_ref[h] = st_new
        o = o * lax.rsqrt(jnp.mean(o * o, axis=-1, keepdims=True) + RMS_EPS)
        outs.append(o)
    o_all = jnp.concatenate(outs, axis=-1)
    g = g_ref[...].astype(F32)
    o_ref[...] = (o_all * nw_ref[...] * (g * jax.nn.sigmoid(g))).astype(o_ref.dtype)


def _hgrn2(hq, hf, hi, hg, lb_param, norm_w, bsz, seq):
    t = hq.shape[0]
    nc = seq // HG_CHUNK
    tok = lambda b, n: (b * nc + n, 0)
    spec = pl.BlockSpec((HG_CHUNK, HG_WIDTH), tok)
    return pl.pallas_call(
        _hgrn_kernel,
        grid=(bsz, nc),
        in_specs=[spec, spec, spec, spec,
                  pl.BlockSpec(lb_param.shape, lambda b, n: (0, 0)),
                  pl.BlockSpec((1, HG_WIDTH), lambda b, n: (0, 0))],
        out_specs=spec,
        out_shape=jax.ShapeDtypeStruct((t, HG_WIDTH), BF16),
        scratch_shapes=[pltpu.VMEM((HG_HEADS, HG_DK, HG_DK), F32)],
        compiler_params=_cparams(("parallel", "arbitrary")),
        name="hgrn2_scan",
    )(hq, hf, hi, hg, lb_param, norm_w.reshape(1, HG_WIDTH))


def _t5_bucket_np(dist):
    max_exact = REL_BUCKETS // 2
    n = np.maximum(dist, 0)
    nf = np.maximum(n, 1).astype(np.float32)
    large = max_exact + (np.log(nf / np.float32(max_exact)) / np.float32(math.log(REL_MAX_DIST / max_exact))
                         * np.float32(REL_BUCKETS - max_exact)).astype(np.int32)
    large = np.minimum(large, REL_BUCKETS - 1)
    return np.where(n < max_exact, n, large).astype(np.int32)


def _band_tables():
    w = ATT_BLOCK
    qi = np.arange(w)[:, None]
    ki = np.arange(2 * w)[None, :]
    m = w + qi - ki
    band = (m >= 0) & (m <= w)
    buckets = np.stack([_t5_bucket_np(m * dil) for _, dil in ATT_BRANCHES])
    return buckets, band


def _bias_kernel(rb_ref, bucket_ref, o_ref):
    g = pl.program_id(0)
    w = ATT_BLOCK
    bucket = bucket_ref[...]
    qi = lax.broadcasted_iota(jnp.int32, (w, 2 * w), 0)
    ki = lax.broadcasted_iota(jnp.int32, (w, 2 * w), 1)
    m = w + qi - ki
    band = (m >= 0) & (m <= w)
    for h in range(ATT_HEADS_PER_BRANCH):
        acc = jnp.zeros((w, 2 * w), F32)
        for c in range(REL_BUCKETS):
            acc = jnp.where(bucket == c, rb_ref[c, g * ATT_HEADS_PER_BRANCH + h], acc)
        full = jnp.where(band, acc, NEG_INF)
        o_ref[1, h] = full
        o_ref[0, h] = jnp.where(ki >= w, full, NEG_INF)


def _bias_tables(rel_bias):
    buckets, _ = _band_tables()
    g = len(ATT_BRANCHES)
    w = ATT_BLOCK
    return pl.pallas_call(
        _bias_kernel,
        grid=(g,),
        in_specs=[pl.BlockSpec(memory_space=pltpu.SMEM),
                  pl.BlockSpec((None, w, 2 * w), lambda i: (i, 0, 0))],
        out_specs=pl.BlockSpec((None, 2, ATT_HEADS_PER_BRANCH, w, 2 * w), lambda i: (i, 0, 0, 0, 0)),
        out_shape=jax.ShapeDtypeStruct((g, 2, ATT_HEADS_PER_BRANCH, w, 2 * w), F32),
        compiler_params=_cparams(("parallel",)),
        name="rel_bias_tables",
    )(rel_bias, jnp.asarray(buckets))


def _attn_kernel(q_ref, kp_ref, kc_ref, vp_ref, vc_ref, bias_ref, o_ref, lse_ref):
    n = pl.program_id(2)
    w = ATT_BLOCK
    hb = ATT_HEADS_PER_BRANCH
    q = q_ref[...]
    lane = lax.broadcasted_iota(jnp.int32, (w, ATT_BW), 1) // ATT_HEAD_DIM
    q4 = jnp.concatenate([jnp.where(lane == h, q, jnp.zeros_like(q)) for h in range(hb)], axis=0)
    kk = jnp.concatenate([kp_ref[...], kc_ref[...]], axis=0)
    vv = jnp.concatenate([vp_ref[...], vc_ref[...]], axis=0)
    s4 = _dot_nt(q4, kk)
    bias = bias_ref[jnp.minimum(n, 1)]
    s4 = s4 + bias.reshape(hb * w, 2 * w)
    mx = jnp.max(s4, axis=-1, keepdims=True)
    p = jnp.exp(s4 - mx)
    l = jnp.sum(p, axis=-1, keepdims=True)
    o4 = _dot((p / l).astype(vv.dtype), vv)
    lse4 = mx + jnp.log(l)
    o = jnp.zeros((w, ATT_BW), F32)
    lse = jnp.zeros((w, ATT_BW), F32)
    for h in range(hb):
        o = jnp.where(lane == h, o4[h * w:(h + 1) * w], o)
        lse = jnp.where(lane == h, lse4[h * w:(h + 1) * w], lse)
    o_ref[...] = o.astype(o_ref.dtype)
    lse_ref[...] = lse


def _dilated_attention(q, k, v, bias_g, bsz, seq, dilation):
    t = q.shape[0]
    w = ATT_BLOCK
    l = seq // dilation
    nb = l // w
    view = lambda a: a.reshape(bsz, l, dilation * ATT_BW)
    cur = pl.BlockSpec((None, w, ATT_BW), lambda b, r, n: (b, n, r))
    prev = pl.BlockSpec((None, w, ATT_BW), lambda b, r, n: (b, jnp.maximum(n - 1, 0), r))
    o, lse = pl.pallas_call(
        _attn_kernel,
        grid=(bsz, dilation, nb),
        in_specs=[cur, prev, cur, prev, cur,
                  pl.BlockSpec(bias_g.shape, lambda b, r, n: (0, 0, 0, 0))],
        out_specs=[cur, cur],
        out_shape=[jax.ShapeDtypeStruct((bsz, l, dilation * ATT_BW), BF16),
                   jax.ShapeDtypeStruct((bsz, l, dilation * ATT_BW), F32)],
        compiler_params=_cparams(("parallel", "parallel", "arbitrary")),
        name=f"dilated_attention_d{dilation}",
    )(view(q), view(k), view(k), view(v), view(v), bias_g)
    return o.reshape(t, ATT_BW), lse.reshape(t, ATT_BW)


def _split_bf16(a):
    hi = a.astype(BF16)
    lo = (a - hi.astype(F32)).astype(BF16)
    return hi, lo


H2P_CHUNKS = 4


def _bf16_bits(a):
    u = lax.bitcast_convert_type(a, jnp.uint32)
    return u + jnp.uint32(0x7FFF) + ((u >> 16) & jnp.uint32(1))


def _outproj_kernel(yhg_ref, o1_ref, o2_ref, o3_ref, l1_ref, l2_ref, l3_ref, x_ref,
                    g1_ref, sc2_ref, sh2_ref, wout_ref, lng_ref, lnb_ref, wrt_ref,
                    x1_ref, h2_ref, h2p_ref, lgt_ref):
    l1, l2, l3 = l1_ref[...], l2_ref[...], l3_ref[...]
    mx = jnp.maximum(jnp.maximum(l1, l2), l3)
    e1, e2, e3 = jnp.exp(l1 - mx), jnp.exp(l2 - mx), jnp.exp(l3 - mx)
    den = e1 + e2 + e3
    att = (e1 / den) * o1_ref[...].astype(F32) + (e2 / den) * o2_ref[...].astype(F32) \
        + (e3 / den) * o3_ref[...].astype(F32)
    mix = _dot(yhg_ref[...], wout_ref[:HG_WIDTH, :]) + _dot(att.astype(BF16), wout_ref[HG_WIDTH:, :])
    x1 = _ln_rows(DN_ALPHA * x_ref[...] + g1_ref[...] * mix) * lng_ref[...] + lnb_ref[...]
    x1_ref[...] = x1
    h2 = _ln_rows(x1) * (1.0 + sc2_ref[...]) + sh2_ref[...]
    h2_ref[...] = h2
    for cidx in range(H2P_CHUNKS):
        lo = _bf16_bits(h2[:, 2 * LANES * cidx:2 * LANES * cidx + LANES])
        hi = _bf16_bits(h2[:, 2 * LANES * cidx + LANES:2 * LANES * (cidx + 1)])
        word = (lo >> 16) | (hi & jnp.uint32(0xFFFF0000))
        h2p_ref[:, cidx, :] = lax.bitcast_convert_type(word, jnp.int32)
    h_hi, h_lo = _split_bf16(h2)
    w_hi, w_lo = _split_bf16(wrt_ref[...])
    lgt_ref[...] = _dot_nt(w_hi, h_hi) + (_dot_nt(w_hi, h_lo) + _dot_nt(w_lo, h_hi))


def _out_projection(yhg, os_, ls_, x2, mod4, w_out_bf, ln_g, ln_b, w_router_t, seq, tm):
    t, d = x2.shape
    spb = seq // tm
    ne = w_router_t.shape[0]
    row = lambda w: pl.BlockSpec((tm, w), lambda i: (i, 0))
    mod_spec = lambda r: pl.BlockSpec((None, None, 1, d), lambda i, r=r: (i // spb, r, 0, 0))
    full = lambda a: pl.BlockSpec(a.shape, lambda i: (0,) * a.ndim)
    ln_g2, ln_b2 = ln_g.reshape(1, d), ln_b.reshape(1, d)
    return pl.pallas_call(
        _outproj_kernel,
        grid=(t // tm,),
        in_specs=[row(HG_WIDTH)] + [row(ATT_BW)] * 6 + [row(d),
                  mod_spec(2), mod_spec(4), mod_spec(3),
                  full(w_out_bf), full(ln_g2), full(ln_b2), full(w_router_t)],
        out_specs=[row(d), row(d), pl.BlockSpec((tm, H2P_CHUNKS, LANES), lambda i: (i, 0, 0)),
                   pl.BlockSpec((ne, tm), lambda i: (0, i))],
        out_shape=[jax.ShapeDtypeStruct((t, d), F32), jax.ShapeDtypeStruct((t, d), F32),
                   jax.ShapeDtypeStruct((t, H2P_CHUNKS, LANES), jnp.int32),
                   jax.ShapeDtypeStruct((ne, t), F32)],
        compiler_params=_cparams(("parallel",)),
        name="merge_outproj_ln",
    )(yhg, *os_, *ls_, x2, mod4, mod4, mod4, w_out_bf, ln_g2, ln_b2, w_router_t)


def _argmax_rows(cur, iota, nrows):
    m = jnp.max(cur, axis=0, keepdims=True)
    idx = jnp.min(jnp.where(cur == m, iota, nrows), axis=0, keepdims=True)
    return m, idx, iota == idx


def _route_kernel(lgt_ref, rb_ref, eidx_ref, w_ref, rank_ref, cnt_ref, carry):
    ne = N_EXPERTS
    gsz = ne // N_GROUPS
    tt = lgt_ref.shape[1]

    @pl.when(pl.program_id(0) == 0)
    def _():
        carry[...] = jnp.zeros_like(carry)

    sc = jax.nn.sigmoid(lgt_ref[...])
    biased = sc + rb_ref[...]
    g3 = biased.reshape(N_GROUPS, gsz, tt)
    io3 = lax.broadcasted_iota(jnp.int32, (N_GROUPS, gsz, tt), 1)
    m1 = jnp.max(g3, axis=1, keepdims=True)
    first = jnp.min(jnp.where(g3 == m1, io3, gsz), axis=1, keepdims=True)
    m2 = jnp.max(jnp.where(io3 == first, NEG_INF, g3), axis=1, keepdims=True)
    gs = (m1 + m2).reshape(N_GROUPS, tt)
    io8 = lax.broadcasted_iota(jnp.int32, (N_GROUPS, tt), 0)
    sel = jnp.zeros((N_GROUPS, tt), jnp.int32)
    cur = gs
    for _ in range(TOPK_GROUPS):
        _, _, pick = _argmax_rows(cur, io8, N_GROUPS)
        sel = jnp.where(pick, 1, sel)
        cur = jnp.where(pick, NEG_INF, cur)
    masked = jnp.where(sel.reshape(N_GROUPS, 1, tt) > 0, g3, NEG_INF).reshape(ne, tt)
    ioe = lax.broadcasted_iota(jnp.int32, (ne, tt), 0)
    cur = masked
    idxs, ws, picks = [], [], []
    for _ in range(TOP_K):
        _, idx, pick = _argmax_rows(cur, ioe, ne)
        idxs.append(idx)
        picks.append(pick)
        ws.append(jnp.sum(jnp.where(pick, sc, 0.0), axis=0, keepdims=True))
        cur = jnp.where(pick, NEG_INF, cur)
    wk = jnp.concatenate(ws, axis=0)
    eidx_ref[...] = jnp.concatenate(idxs, axis=0)
    w_ref[...] = wk / jnp.sum(wk, axis=0, keepdims=True) * ROUTED_SCALE
    chosen = jnp.where(cur == NEG_INF, jnp.where(masked == NEG_INF, 0.0, 1.0), 0.0)
    r_i = lax.broadcasted_iota(jnp.int32, (tt, tt), 0)
    c_i = lax.broadcasted_iota(jnp.int32, (tt, tt), 1)
    before = jnp.where(r_i < c_i, 1.0, 0.0).astype(BF16)
    pref = _dot(chosen.astype(BF16), before) + carry[...]
    rank_ref[...] = jnp.concatenate(
        [jnp.sum(jnp.where(p, pref, 0.0), axis=0, keepdims=True) for p in picks], axis=0).astype(jnp.int32)
    carry[...] = carry[...] + jnp.sum(chosen, axis=1, keepdims=True)
    cnt_ref[...] = carry[...]


def _route(logits_t, router_bias, tt):
    ne, t = logits_t.shape
    tok = pl.BlockSpec((TOP_K, tt), lambda i: (0, i))
    return pl.pallas_call(
        _route_kernel,
        grid=(t // tt,),
        in_specs=[pl.BlockSpec((ne, tt), lambda i: (0, i)),
                  pl.BlockSpec((ne, 1), lambda i: (0, 0))],
        out_specs=[tok, tok, tok, pl.BlockSpec((ne, 1), lambda i: (0, 0))],
        out_shape=[jax.ShapeDtypeStruct((TOP_K, t), jnp.int32), jax.ShapeDtypeStruct((TOP_K, t), F32),
                   jax.ShapeDtypeStruct((TOP_K, t), jnp.int32), jax.ShapeDtypeStruct((ne, 1), F32)],
        scratch_shapes=[pltpu.VMEM((ne, 1), F32)],
        compiler_params=_cparams(("arbitrary",)),
        name="router_topk",
    )(logits_t, router_bias.reshape(ne, 1))


def _plan_kernel(cnt_ref, eidx_ref, rank_ref, dest_ref, seq_ref, dexp_ref, meta_ref):
    ne = N_EXPERTS
    tt = eidx_ref.shape[1]
    nblk = seq_ref.shape[1]
    cnt = cnt_ref[...].astype(jnp.int32)
    padded = ((cnt + (MOE_BLOCK - 1)) // MOE_BLOCK) * MOE_BLOCK
    r_i = lax.broadcasted_iota(jnp.int32, (ne, ne), 0)
    c_i = lax.broadcasted_iota(jnp.int32, (ne, ne), 1)
    incl = jnp.where(c_i <= r_i, 1.0, 0.0)
    pend = jnp.dot(incl, jnp.broadcast_to(padded.astype(F32), (ne, LANES)),
                   preferred_element_type=F32, precision=lax.Precision.HIGHEST)[:, 0:1]
    pend = pend.astype(jnp.int32)
    pstart = pend - padded
    ioe = lax.broadcasted_iota(jnp.int32, (ne, tt), 0)
    rows = []
    for k in range(TOP_K):
        sel = ioe == eidx_ref[k:k + 1, :]
        rows.append(jnp.sum(jnp.where(sel, pstart, 0), axis=0, keepdims=True))
    dest_ref[...] = jnp.concatenate(rows, axis=0) + rank_ref[...]
    blk0 = lax.broadcasted_iota(jnp.int32, (ne, nblk), 1) * MOE_BLOCK
    be = jnp.minimum(jnp.sum(jnp.where(pend <= blk0, 1, 0), axis=0, keepdims=True), ne - 1)
    present = cnt > 0
    strict = jnp.where(c_i < r_i, 1.0, 0.0).astype(BF16)
    sidx = _dot(strict, jnp.broadcast_to(jnp.where(present, 1.0, 0.0), (ne, LANES)).astype(BF16))[:, 0:1]
    sidx = sidx.astype(jnp.int32)
    dexp_ref[...] = jnp.sum(jnp.where(jnp.logical_and(present, sidx == c_i), r_i, 0), axis=0, keepdims=True)
    ioeb = lax.broadcasted_iota(jnp.int32, (ne, nblk), 0)
    seq_ref[...] = jnp.sum(jnp.where(ioeb == be, sidx, 0), axis=0, keepdims=True)
    nu = jnp.max(pend, axis=0, keepdims=True) // MOE_BLOCK
    nd = jnp.sum(jnp.where(present, 1, 0), axis=0, keepdims=True)
    lane = lax.broadcasted_iota(jnp.int32, (1, LANES), 1)
    meta_ref[...] = jnp.where(lane == 0, nu, jnp.where(lane == 1, nd, 0))


def _dispatch_plan(cnt, eidx, rank, tt):
    k, t = eidx.shape
    n_blocks = -(-(t * k) // MOE_BLOCK) + N_EXPERTS
    tok = pl.BlockSpec((k, tt), lambda i: (0, i))
    one = lambda n: pl.BlockSpec((1, n), lambda i: (0, 0))
    dest, seq, dexp, meta = pl.pallas_call(
        _plan_kernel,
        grid=(t // tt,),
        in_specs=[pl.BlockSpec(cnt.shape, lambda i: (0, 0)), tok, tok],
        out_specs=[tok, one(n_blocks), one(N_EXPERTS), one(LANES)],
        out_shape=[jax.ShapeDtypeStruct((k, t), jnp.int32), jax.ShapeDtypeStruct((1, n_blocks), jnp.int32),
                   jax.ShapeDtypeStruct((1, N_EXPERTS), jnp.int32), jax.ShapeDtypeStruct((1, LANES), jnp.int32)],
        compiler_params=_cparams(("arbitrary",)),
        name="dispatch_plan",
    )(cnt, eidx, rank)
    return dest, seq.reshape(n_blocks), dexp.reshape(N_EXPERTS), meta.reshape(LANES), n_blocks


SC_CORES = 2
SC_SUBCORES = 16
SC_LANES = 16
SC_CHUNK = 16384
SC_UNROLL = 4


def _row_tables(dest, n_rows):
    k, t = dest.shape
    a = k * t
    nw = SC_CORES * SC_SUBCORES
    per_w = n_rows // nw
    assert n_rows % (nw * SC_LANES) == 0 and t % SC_CHUNK == 0
    mesh = plsc.VectorSubcoreMesh(core_axis_name="c", subcore_axis_name="s")
    cp = pltpu.CompilerParams()
    if "needs_layout_passes" in pltpu.CompilerParams.__dataclass_fields__:
        cp = dataclasses.replace(cp, needs_layout_passes=False)

    def body(dest_hbm, tok_out, dbuf, tloc):
        wid = lax.axis_index("s") * SC_CORES + lax.axis_index("c")
        base = wid * per_w

        @pl.loop(0, per_w // SC_LANES)
        def _(i):
            tloc[pl.ds(i * SC_LANES, SC_LANES)] = jnp.zeros((SC_LANES,), jnp.int32)

        lane = lax.iota(jnp.int32, SC_LANES)

        @pl.loop(0, a // SC_CHUNK)
        def _(c):
            pltpu.sync_copy(dest_hbm.at[pl.ds(c * SC_CHUNK, SC_CHUNK)], dbuf)
            tok0 = lax.rem(c * SC_CHUNK, t)

            @pl.loop(0, SC_CHUNK // (SC_LANES * SC_UNROLL))
            def _(j):
                for u in range(SC_UNROLL):
                    off = (j * SC_UNROLL + u) * SC_LANES
                    loc = dbuf[pl.ds(off, SC_LANES)] - base
                    mine = jnp.logical_and(loc >= 0, loc < per_w)
                    loc = jnp.where(mine, loc, 0)
                    plsc.store_scatter(tloc, [loc], tok0 + off + lane, mask=mine)

        pltpu.sync_copy(tloc, tok_out.at[pl.ds(base, per_w)])

    fn = pl.kernel(
        body,
        out_type=jax.ShapeDtypeStruct((n_rows,), jnp.int32),
        mesh=mesh,
        scratch_types=[pltpu.VMEM((SC_CHUNK,), jnp.int32), pltpu.VMEM((per_w,), jnp.int32)],
        compiler_params=cp,
        name="row_tables",
    )
    return fn(dest.reshape(a))


ROW_TILE = (SUBLANES, LANES)


FFN_GROUP = 4
GATHER_BATCH = 32


def _gather_rows(h2p_ref, tok_ref, row, buf):
    for j0 in range(0, MOE_BLOCK, GATHER_BATCH):
        vals = [h2p_ref[tok_ref[row, j]] for j in range(j0, j0 + GATHER_BATCH)]
        for j, v in zip(range(j0, j0 + GATHER_BATCH), vals):
            buf[pl.ds(H2P_CHUNKS * j, H2P_CHUNKS), :] = v


def _ffn_kernel(seq_ref, dexp_ref, meta_ref, tokc_ref, tokn_ref, h2p_ref, wg_hbm, wu_hbm, wd_hbm,
                y_ref, buf_even, buf_odd, yraw_even, yraw_odd,
                wg_a, wu_a, wd_a, wg_b, wu_b, wd_b, sems):
    i = pl.program_id(0)
    n_used = meta_ref[0]
    n_exp = meta_ref[1]
    nblk = seq_ref.shape[0]
    wbufs = ((wg_a, wu_a, wd_a), (wg_b, wu_b, wd_b))

    def emit_rows(j):
        yraw = yraw_even if j % 2 == 0 else yraw_odd
        y_ref[pl.ds(j * MOE_BLOCK, MOE_BLOCK)] = pltpu.einshape("r(cl)->rcl", yraw[...], c=SUBLANES)

    def weight_copies(s, par):
        e = dexp_ref[s]
        return [pltpu.make_async_copy(src.at[e], dst, sems.at[par, n])
                for n, (src, dst) in enumerate(zip((wg_hbm, wu_hbm, wd_hbm), wbufs[par]))]

    @pl.when(i == 0)
    def _():
        _gather_rows(h2p_ref, tokc_ref, 0, buf_even)
        for cp in weight_copies(0, 0):
            cp.start()

    for j in range(FFN_GROUP):
        b = i * FFN_GROUP + j
        cur, nxt = (buf_even, buf_odd) if j % 2 == 0 else (buf_odd, buf_even)
        rows = pl.ds(j * MOE_BLOCK, MOE_BLOCK)
        used = b < n_used
        s = seq_ref[jnp.minimum(b, nblk - 1)]
        first_of_expert = jnp.logical_or(b == 0, s != seq_ref[jnp.clip(b - 1, 0, nblk - 1)])

        for par in range(2):
            wg_c, wu_c, wd_c = wbufs[par]

            @pl.when(jnp.logical_and(used, s % 2 == par))
            def _():
                @pl.when(first_of_expert)
                def _():
                    for cp in weight_copies(s, par):
                        cp.wait()

                    @pl.when(s + 1 < n_exp)
                    def _():
                        for cp in weight_copies(s + 1, 1 - par):
                            cp.start()

                if j + 1 < FFN_GROUP:
                    _gather_rows(h2p_ref, tokc_ref, j + 1, nxt)
                else:
                    _gather_rows(h2p_ref, tokn_ref, 0, nxt)
                if j >= 1:
                    emit_rows(j - 1)
                parts = []
                for cidx in range(H2P_CHUNKS):
                    word = cur[pl.ds(cidx, MOE_BLOCK, stride=H2P_CHUNKS), :]
                    parts.append(lax.bitcast_convert_type(word << 16, F32))
                    parts.append(lax.bitcast_convert_type(word & jnp.int32(-65536), F32))
                x = jnp.concatenate(parts, axis=1)
                g = _dot(x, wg_c[...])
                u = _dot(x, wu_c[...])
                hm = (g * jax.nn.sigmoid(g)) * u
                (yraw_even if j % 2 == 0 else yraw_odd)[...] = _dot(hm, wd_c[...])

        @pl.when(jnp.logical_not(used))
        def _():
            if j >= 1:
                @pl.when(b - 1 < n_used)
                def _():
                    emit_rows(j - 1)
            y_ref[rows] = jnp.zeros((MOE_BLOCK,) + ROW_TILE, F32)

    @pl.when(i * FFN_GROUP + (FFN_GROUP - 1) < n_used)
    def _():
        emit_rows(FFN_GROUP - 1)


def _routed_experts(h2p, row_token, seq, dexp, meta, n_blocks, wg, wu, wd):
    d = wg.shape[1]
    de = wg.shape[2]
    ng = n_blocks // FFN_GROUP
    assert n_blocks % FFN_GROUP == 0 and FFN_GROUP % 2 == 0
    tok3 = row_token.reshape(ng, FFN_GROUP, MOE_BLOCK)
    last_step = lambda m: (m[0] + (FFN_GROUP - 1)) // FFN_GROUP
    smem = lambda imap: pl.BlockSpec((None, FFN_GROUP, MOE_BLOCK), imap, memory_space=pltpu.SMEM)
    grid_spec = pltpu.PrefetchScalarGridSpec(
        num_scalar_prefetch=3,
        grid=(ng,),
        in_specs=[
            smem(lambda i, sq, dx, m: (i, 0, 0)),
            smem(lambda i, sq, dx, m: (jnp.minimum(i + 1, ng - 1), 0, 0)),
            pl.BlockSpec(h2p.shape, lambda i, sq, dx, m: (0, 0, 0), pipeline_mode=pl.Buffered(1)),
            pl.BlockSpec(memory_space=pl.ANY),
            pl.BlockSpec(memory_space=pl.ANY),
            pl.BlockSpec(memory_space=pl.ANY),
        ],
        out_specs=pl.BlockSpec((FFN_GROUP * MOE_BLOCK,) + ROW_TILE,
                               lambda i, sq, dx, m: (jnp.minimum(i, last_step(m)), 0, 0)),
        scratch_shapes=[pltpu.VMEM((MOE_BLOCK * H2P_CHUNKS, LANES), jnp.int32)] * 2 + [
            pltpu.VMEM((MOE_BLOCK, d), F32)] * 2 + [
            pltpu.VMEM((d, de), F32), pltpu.VMEM((d, de), F32), pltpu.VMEM((de, d), F32)] * 2 + [
            pltpu.SemaphoreType.DMA((2, 3))],
    )
    return pl.pallas_call(
        _ffn_kernel,
        grid_spec=grid_spec,
        out_shape=jax.ShapeDtypeStruct(((ng + 1) * FFN_GROUP * MOE_BLOCK,) + ROW_TILE, F32),
        compiler_params=_cparams(("arbitrary",)),
        name="routed_experts",
    )(seq, dexp, meta, tok3, tok3, h2p, wg, wu, wd)


COMBINE_TOKENS = 128


def _row_copies(src_hbm, idx_ref, buf, sem):
    return [pltpu.make_async_copy(src_hbm.at[idx_ref[k, j]], buf.at[k, j], sem)
            for k in range(TOP_K) for j in range(COMBINE_TOKENS)]


def _final_kernel(dc_ref, dn_ref, wk_ref, x1_ref, h2_ref, g2_ref, wsg_ref, wsu_ref, wsd_ref, lng_ref, lnb_ref,
                  y_hbm, o_ref, ybuf, sems):
    i = pl.program_id(0)
    nsteps = pl.num_programs(0)
    slot = i % 2

    def issue(d_ref, s):
        for n, cp in enumerate(_row_copies(y_hbm, d_ref, ybuf.at[s], sems.at[s])):
            cp.start(priority=n % 2)

    @pl.when(i == 0)
    def _():
        issue(dc_ref, 0)

    @pl.when(i + 1 < nsteps)
    def _():
        issue(dn_ref, 1 - slot)

    hb = h2_ref[...].astype(BF16)
    g = _dot(hb, wsg_ref[...])
    u = _dot(hb, wsu_ref[...])
    shared = _dot(((g * jax.nn.sigmoid(g)) * u).astype(BF16), wsd_ref[...])
    for cp in _row_copies(y_hbm, dc_ref, ybuf.at[slot], sems.at[slot]):
        cp.wait()
    rows = []
    for j in range(COMBINE_TOKENS):
        acc = ybuf[slot, 0, j] * wk_ref[0, j]
        for k in range(1, TOP_K):
            acc = acc + ybuf[slot, k, j] * wk_ref[k, j]
        rows.append(acc)
    routed = pltpu.einshape("tcl->t(cl)", jnp.stack(rows, axis=0))
    x2 = DN_ALPHA * x1_ref[...] + g2_ref[...] * (routed + shared)
    o_ref[...] = _ln_rows(x2) * lng_ref[...] + lnb_ref[...]


def _combine_final(x1, h2, mod4, dest, wk, y_rows, wsg, wsu, wsd, ln_g, ln_b, seq):
    t, d = x1.shape
    tt = COMBINE_TOKENS
    nsteps = t // tt
    spb = seq // tt
    row = pl.BlockSpec((tt, d), lambda i: (i, 0))
    full = lambda a: pl.BlockSpec(a.shape, lambda i: (0,) * a.ndim)
    ln_g2, ln_b2 = ln_g.reshape(1, d), ln_b.reshape(1, d)
    return pl.pallas_call(
        _final_kernel,
        grid=(nsteps,),
        in_specs=[pl.BlockSpec((TOP_K, tt), lambda i: (0, i), memory_space=pltpu.SMEM),
                  pl.BlockSpec((TOP_K, tt), lambda i: (0, jnp.minimum(i + 1, nsteps - 1)),
                               memory_space=pltpu.SMEM),
                  pl.BlockSpec((TOP_K, tt), lambda i: (0, i), memory_space=pltpu.SMEM),
                  row, row,
                  pl.BlockSpec((None, None, 1, d), lambda i: (i // spb, 5, 0, 0)),
                  full(wsg), full(wsu), full(wsd), full(ln_g2), full(ln_b2),
                  pl.BlockSpec(memory_space=pl.ANY)],
        out_specs=row,
        out_shape=jax.ShapeDtypeStruct((t, d), F32),
        scratch_shapes=[pltpu.VMEM((2, TOP_K, tt) + ROW_TILE, F32),
                        pltpu.SemaphoreType.DMA((2,))],
        compiler_params=_cparams(("arbitrary",)),
        name="shared_combine_ln",
    )(dest, dest, wk, x1, h2, mod4, wsg, wsu, wsd, ln_g2, ln_b2, y_rows)


def kernel(x, c, w_ada, b_ada, w_in, hg_lower_bound, hg_norm_w, rel_bias, w_out, ln1_g, ln1_b, w_router,
           router_bias, w_e_gate, w_e_up, w_e_down, w_sh_gate, w_sh_up, w_sh_down, ln2_g, ln2_b):
    bsz, seq, d = x.shape
    t = bsz * seq
    assert w_ada.shape[0] == DEPTH and seq % (ATT_BRANCHES[-1][0]) == 0
    x2 = x.reshape(t, d)
    bias = _bias_tables(rel_bias)
    for l in range(DEPTH):
        mod4 = _modulation(c, w_ada[l], b_ada[l]).reshape(bsz, 6, 1, d)
        w_in_bf = _cast_bf16(w_in[l], 256)
        hq, hf, hi, hg, *qkv = _in_projection(x2, mod4, w_in_bf, seq, 512)
        nbr = len(ATT_BRANCHES)
        y_hg = _hgrn2(hq, hf, hi, hg, hg_lower_bound, hg_norm_w[l], bsz, seq)
        os_, ls_ = [], []
        for g, (_, dil) in enumerate(ATT_BRANCHES):
            o, lse = _dilated_attention(qkv[g], qkv[nbr + g], qkv[2 * nbr + g], bias[g], bsz, seq, dil)
            os_.append(o)
            ls_.append(lse)
        w_out_bf = _cast_bf16(w_out[l], 256)
        x1, h2, h2p, logits_t = _out_projection(y_hg, os_, ls_, x2, mod4, w_out_bf, ln1_g[l], ln1_b[l],
                                                w_router[l].T, seq, 512)
        eidx, wk, rank, cnt = _route(logits_t, router_bias[l], 256)
        dest, blk_seq, dexp, meta, n_blocks = _dispatch_plan(cnt, eidx, rank, 512)
        row_token = _row_tables(dest, n_blocks * MOE_BLOCK)
        y_rows = _routed_experts(h2p, row_token, blk_seq, dexp, meta, n_blocks,
                                 w_e_gate[l], w_e_up[l], w_e_down[l])
        x2 = _combine_final(x1, h2, mod4, dest, wk, y_rows,
                            _cast_bf16(w_sh_gate[l], 256), _cast_bf16(w_sh_up[l], 256),
                            _cast_bf16(w_sh_down[l], 256), ln2_g[l], ln2_b[l], seq)
    return x2.reshape(bsz, seq, d)
```

```python
import dataclasses
import math

import jax
import jax.numpy as jnp
import numpy as np
from jax import lax
from jax.experimental import pallas as pl
from jax.experimental.pallas import tpu as pltpu
from jax.experimental.pallas import tpu_sc as plsc

HG_HEADS = 4
HG_DK = 128
HG_WIDTH = HG_HEADS * HG_DK
ATT_BRANCHES = ((128, 1), (512, 4), (2048, 16))
ATT_HEADS_PER_BRANCH = 4
ATT_HEAD_DIM = 64
ATT_BW = ATT_HEADS_PER_BRANCH * ATT_HEAD_DIM
ATT_BLOCK = 128
REL_BUCKETS = 32
REL_MAX_DIST = 2048
N_EXPERTS = 256
TOP_K = 8
N_GROUPS = 8
TOPK_GROUPS = 4
ROUTED_SCALE = 2.5
MOE_BLOCK = 128
DEPTH = 1
DN_ALPHA = (2 * DEPTH) ** 0.25
LN_EPS = 1e-5
RMS_EPS = 1e-6

LANES = 128
SUBLANES = 8
VMEM_LIMIT_BYTES = 56 * 1024 * 1024

F32 = jnp.float32
BF16 = jnp.bfloat16
NEG_INF = float("-inf")


def _cparams(sem):
    return pltpu.CompilerParams(dimension_semantics=sem, vmem_limit_bytes=VMEM_LIMIT_BYTES)


def _ln_rows(x):
    mu = jnp.mean(x, axis=-1, keepdims=True)
    xc = x - mu
    var = jnp.mean(xc * xc, axis=-1, keepdims=True)
    return xc * lax.rsqrt(var + LN_EPS)


def _dot(a, b):
    return jnp.dot(a, b, preferred_element_type=F32)


def _dot_nt(a, b):
    return lax.dot_general(a, b, (((1,), (1,)), ((), ())), preferred_element_type=F32)


def _dot_tn(a, b):
    return lax.dot_general(a, b, (((0,), (0,)), ((), ())), preferred_element_type=F32)


def _cast_kernel(w_ref, o_ref):
    o_ref[...] = w_ref[...].astype(o_ref.dtype)


def _cast_bf16(w, rows_per_step):
    r, c = w.shape
    return pl.pallas_call(
        _cast_kernel,
        grid=(r // rows_per_step,),
        in_specs=[pl.BlockSpec((rows_per_step, c), lambda i: (i, 0))],
        out_specs=pl.BlockSpec((rows_per_step, c), lambda i: (i, 0)),
        out_shape=jax.ShapeDtypeStruct((r, c), BF16),
        compiler_params=_cparams(("parallel",)),
        name="cast_bf16",
    )(w)


def _mod_kernel(c_ref, w_ref, b_ref, o_ref):
    c = c_ref[...]
    cond = c * jax.nn.sigmoid(c)
    o_ref[...] = jnp.dot(cond, w_ref[...], preferred_element_type=F32,
                         precision=lax.Precision.HIGHEST) + b_ref[...]


def _modulation(c, w_ada, b_ada):
    bsz, d = c.shape
    n = w_ada.shape[1]
    rows = -(-bsz // SUBLANES) * SUBLANES
    cpad = jnp.zeros((rows, d), F32).at[:bsz].set(c)
    tn = 1024
    out = pl.pallas_call(
        _mod_kernel,
        grid=(n // tn,),
        in_specs=[pl.BlockSpec((rows, d), lambda j: (0, 0)),
                  pl.BlockSpec((d, tn), lambda j: (0, j)),
                  pl.BlockSpec((1, tn), lambda j: (0, j))],
        out_specs=pl.BlockSpec((rows, tn), lambda j: (0, j)),
        out_shape=jax.ShapeDtypeStruct((rows, n), F32),
        compiler_params=_cparams(("parallel",)),
        name="adaln_modulation",
    )(cpad, w_ada, b_ada.reshape(1, n))
    return out[:bsz]


_IN_HG = 4
_IN_ATT = 3 * len(ATT_BRANCHES)


def _inproj_kernel(x_ref, sc_ref, sh_ref, w_ref, *outs):
    x = x_ref[...]
    h = _ln_rows(x) * (1.0 + sc_ref[...]) + sh_ref[...]
    hb = h.astype(BF16)
    col = 0
    for k, o_ref in enumerate(outs):
        width = o_ref.shape[-1]
        y = _dot(hb, w_ref[:, col:col + width])
        if _IN_HG <= k < _IN_HG + len(ATT_BRANCHES):
            y = y * (ATT_HEAD_DIM ** -0.5)
        o_ref[...] = y.astype(o_ref.dtype)
        col += width


def _in_projection(x2, mod4, w_in_bf, seq, tm):
    t, d = x2.shape
    steps_per_batch = seq // tm
    widths = [HG_WIDTH] * _IN_HG + [ATT_BW] * _IN_ATT
    dtypes = [BF16, F32, BF16, BF16] + [BF16] * _IN_ATT
    mod_spec = lambda row: pl.BlockSpec((None, None, 1, d),
                                        lambda i, row=row: (i // steps_per_batch, row, 0, 0))
    outs = pl.pallas_call(
        _inproj_kernel,
        grid=(t // tm,),
        in_specs=[pl.BlockSpec((tm, d), lambda i: (i, 0)),
                  mod_spec(1), mod_spec(0),
                  pl.BlockSpec(w_in_bf.shape, lambda i: (0, 0))],
        out_specs=[pl.BlockSpec((tm, w), lambda i: (i, 0)) for w in widths],
        out_shape=[jax.ShapeDtypeStruct((t, w), dt) for w, dt in zip(widths, dtypes)],
        compiler_params=_cparams(("parallel",)),
        name="ln_in_projection",
    )(x2, mod4, mod4, w_in_bf)
    return outs


HG_CHUNK = 64
HG_CHUNKS_PER_STEP = 4
HG_SUB = 8
HG_LEVELS = (64, 32, 16)


def _hgrn_chunk(q, z, iv, lb):
    c = HG_CHUNK
    f = lb + (1.0 - lb) * jax.nn.sigmoid(z)
    lf = jnp.log(f)
    kk = (1.0 - lb) * jax.nn.sigmoid(-z)
    r_i = lax.broadcasted_iota(jnp.int32, (c, c), 0)
    c_i = lax.broadcasted_iota(jnp.int32, (c, c), 1)
    tril = (c_i <= r_i).astype(F32)
    b = jnp.dot(tril, lf, preferred_element_type=F32, precision=lax.Precision.HIGHEST)

    row = lax.broadcasted_iota(jnp.int32, (c, HG_DK), 0)
    scores = jnp.zeros((c, c), F32)
    for m in HG_LEVELS:
        nb = c // m
        b3 = b.reshape(nb, m, HG_DK)
        piv = jnp.broadcast_to(b3[:, m // 2 - 1:m // 2, :], (nb, m, HG_DK)).reshape(c, HG_DK)
        second = (row % m) >= (m // 2)
        qt = jnp.where(second, q * jnp.exp(b - piv), 0.0)
        kt = jnp.where(second, 0.0, kk * jnp.exp(piv - b))
        s_m = _dot_nt(qt.astype(BF16), kt.astype(BF16))
        if nb > 1:
            s_m = jnp.where((r_i // m) == (c_i // m), s_m, 0.0)
        scores = scores + s_m
    sub = HG_SUB
    t_i = lax.broadcasted_iota(jnp.int32, (sub, 1), 0)
    lane = lax.broadcasted_iota(jnp.int32, (sub, c), 1)
    diag_rows = []
    for j in range(c // sub):
        qb = q[j * sub:(j + 1) * sub]
        kb = kk[j * sub:(j + 1) * sub]
        bb = b[j * sub:(j + 1) * sub]
        a_j = jnp.zeros((sub, c), F32)
        for s in range(sub):
            w = qb * kb[s:s + 1] * jnp.exp(bb - bb[s:s + 1])
            col = jnp.sum(w, axis=-1, keepdims=True)
            col = jnp.where(t_i >= s, col, 0.0)
            a_j = jnp.where(lane == j * sub + s, col, a_j)
        diag_rows.append(a_j)
    scores = scores + jnp.concatenate(diag_rows, axis=0)

    ivb = iv.astype(BF16)
    intra = _dot(scores.astype(BF16), ivb)
    b_last = b[c - 1:c]
    kdec = (kk * jnp.exp(b_last - b)).astype(BF16)
    return (q * jnp.exp(b)).astype(BF16), intra, jnp.exp(b_last), _dot_tn(ivb, kdec)


def _hgrn_kernel(q_ref, f_ref, i_ref, g_ref, lbp_ref, nw_ref, o_ref, st_ref):
    @pl.when(pl.program_id(1) == 0)
    def _():
        st_ref[...] = jnp.zeros_like(st_ref)

    lbp = lbp_ref[...]
    e = jnp.exp(lbp - jnp.max(lbp, axis=0, keepdims=True))
    lb_all = e[0:1] / jnp.sum(e, axis=0, keepdims=True)
    heads = []
    for h in range(HG_HEADS):
        sl = slice(h * HG_DK, (h + 1) * HG_DK)
        st_t = st_ref[h]
        outs = []
        for n in range(HG_CHUNKS_PER_STEP):
            rows = slice(n * HG_CHUNK, (n + 1) * HG_CHUNK)
            qdec, intra, dec_last, kv = _hgrn_chunk(q_ref[rows, sl].astype(F32), f_ref[rows, sl],
                                                    i_ref[rows, sl].astype(F32), lb_all[:, sl])
            o = _dot_nt(qdec, st_t.astype(BF16)) + intra
            st_t = st_t * dec_last + kv
            outs.append(o * lax.rsqrt(jnp.mean(o * o, axis=-1, keepdims=True) + RMS_EPS))
        st_ref[h] = st_t
        heads.append(jnp.concatenate(outs, axis=0))
    o_all = jnp.concatenate(heads, axis=-1)
    g = g_ref[...].astype(F32)
    o_ref[...] = (o_all * nw_ref[...] * (g * jax.nn.sigmoid(g))).astype(o_ref.dtype)


def _hgrn2(hq, hf, hi, hg, lb_param, norm_w, bsz, seq):
    t = hq.shape[0]
    rows = HG_CHUNK * HG_CHUNKS_PER_STEP
    nc = seq // rows
    tok = lambda b, n: (b * nc + n, 0)
    spec = pl.BlockSpec((rows, HG_WIDTH), tok)
    return pl.pallas_call(
        _hgrn_kernel,
        grid=(bsz, nc),
        in_specs=[spec, spec, spec, spec,
                  pl.BlockSpec(lb_param.shape, lambda b, n: (0, 0)),
                  pl.BlockSpec((1, HG_WIDTH), lambda b, n: (0, 0))],
        out_specs=spec,
        out_shape=jax.ShapeDtypeStruct((t, HG_WIDTH), BF16),
        scratch_shapes=[pltpu.VMEM((HG_HEADS, HG_DK, HG_DK), F32)],
        compiler_params=_cparams(("parallel", "arbitrary")),
        name="hgrn2_scan",
    )(hq, hf, hi, hg, lb_param, norm_w.reshape(1, HG_WIDTH))


def _t5_bucket_np(dist):
    max_exact = REL_BUCKETS // 2
    n = np.maximum(dist, 0)
    nf = np.maximum(n, 1).astype(np.float32)
    large = max_exact + (np.log(nf / np.float32(max_exact)) / np.float32(math.log(REL_MAX_DIST / max_exact))
                         * np.float32(REL_BUCKETS - max_exact)).astype(np.int32)
    large = np.minimum(large, REL_BUCKETS - 1)
    return np.where(n < max_exact, n, large).astype(np.int32)


def _band_tables():
    w = ATT_BLOCK
    qi = np.arange(w)[:, None]
    ki = np.arange(2 * w)[None, :]
    m = w + qi - ki
    band = (m >= 0) & (m <= w)
    buckets = np.stack([_t5_bucket_np(m * dil) for _, dil in ATT_BRANCHES])
    return buckets, band


def _bias_kernel(rb_ref, bucket_ref, o_ref):
    g = pl.program_id(0)
    w = ATT_BLOCK
    bucket = bucket_ref[...]
    qi = lax.broadcasted_iota(jnp.int32, (w, 2 * w), 0)
    ki = lax.broadcasted_iota(jnp.int32, (w, 2 * w), 1)
    m = w + qi - ki
    band = (m >= 0) & (m <= w)
    for h in range(ATT_HEADS_PER_BRANCH):
        acc = jnp.zeros((w, 2 * w), F32)
        for c in range(REL_BUCKETS):
            acc = jnp.where(bucket == c, rb_ref[c, g * ATT_HEADS_PER_BRANCH + h], acc)
        full = jnp.where(band, acc, NEG_INF)
        o_ref[1, h] = full
        o_ref[0, h] = jnp.where(ki >= w, full, NEG_INF)


def _bias_tables(rel_bias):
    buckets, _ = _band_tables()
    g = len(ATT_BRANCHES)
    w = ATT_BLOCK
    return pl.pallas_call(
        _bias_kernel,
        grid=(g,),
        in_specs=[pl.BlockSpec(memory_space=pltpu.SMEM),
                  pl.BlockSpec((None, w, 2 * w), lambda i: (i, 0, 0))],
        out_specs=pl.BlockSpec((None, 2, ATT_HEADS_PER_BRANCH, w, 2 * w), lambda i: (i, 0, 0, 0, 0)),
        out_shape=jax.ShapeDtypeStruct((g, 2, ATT_HEADS_PER_BRANCH, w, 2 * w), F32),
        compiler_params=_cparams(("parallel",)),
        name="rel_bias_tables",
    )(rel_bias, jnp.asarray(buckets))


ATT_BLOCKS_PER_STEP = 4


def _attn_kernel(q_ref, kp_ref, kc_ref, vp_ref, vc_ref, bias_ref, o_ref, lse_ref):
    m = pl.program_id(2)
    w = ATT_BLOCK
    hb = ATT_HEADS_PER_BRANCH
    lane = lax.broadcasted_iota(jnp.int32, (w, ATT_BW), 1) // ATT_HEAD_DIM
    kall = jnp.concatenate([kp_ref[...], kc_ref[...]], axis=0)
    vall = jnp.concatenate([vp_ref[...], vc_ref[...]], axis=0)
    for blk in range(q_ref.shape[0] // w):
        q = q_ref[blk * w:(blk + 1) * w]
        q4 = jnp.concatenate([jnp.where(lane == h, q, jnp.zeros_like(q)) for h in range(hb)], axis=0)
        kk = kall[blk * w:(blk + 2) * w]
        vv = vall[blk * w:(blk + 2) * w]
        s4 = _dot_nt(q4, kk)
        bias = bias_ref[jnp.minimum(m, 1)] if blk == 0 else bias_ref[1]
        s4 = s4 + bias.reshape(hb * w, 2 * w)
        mx = jnp.max(s4, axis=-1, keepdims=True)
        p = jnp.exp(s4 - mx)
        l = jnp.sum(p, axis=-1, keepdims=True)
        o4 = _dot((p / l).astype(vv.dtype), vv)
        lse4 = mx + jnp.log(l)
        o = jnp.zeros((w, ATT_BW), F32)
        lse = jnp.zeros((w, ATT_BW), F32)
        for h in range(hb):
            o = jnp.where(lane == h, o4[h * w:(h + 1) * w], o)
            lse = jnp.where(lane == h, lse4[h * w:(h + 1) * w], lse)
        o_ref[blk * w:(blk + 1) * w] = o.astype(o_ref.dtype)
        lse_ref[blk * w:(blk + 1) * w] = lse


def _dilated_attention(q, k, v, bias_g, bsz, seq, dilation):
    t = q.shape[0]
    w = ATT_BLOCK
    l = seq // dilation
    pstep = math.gcd(ATT_BLOCKS_PER_STEP, l // w)
    nb = l // (w * pstep)
    view = lambda a: a.reshape(bsz, l, dilation * ATT_BW)
    cur = pl.BlockSpec((None, pstep * w, ATT_BW), lambda b, r, n: (b, n, r))
    prev = pl.BlockSpec((None, w, ATT_BW), lambda b, r, n: (b, jnp.maximum(pstep * n - 1, 0), r))
    o, lse = pl.pallas_call(
        _attn_kernel,
        grid=(bsz, dilation, nb),
        in_specs=[cur, prev, cur, prev, cur,
                  pl.BlockSpec(bias_g.shape, lambda b, r, n: (0, 0, 0, 0))],
        out_specs=[cur, cur],
        out_shape=[jax.ShapeDtypeStruct((bsz, l, dilation * ATT_BW), BF16),
                   jax.ShapeDtypeStruct((bsz, l, dilation * ATT_BW), F32)],
        compiler_params=_cparams(("parallel", "parallel", "arbitrary")),
        name=f"dilated_attention_d{dilation}",
    )(view(q), view(k), view(k), view(v), view(v), bias_g)
    return o.reshape(t, ATT_BW), lse.reshape(t, ATT_BW)


def _split_bf16(a):
    hi = a.astype(BF16)
    lo = (a - hi.astype(F32)).astype(BF16)
    return hi, lo


H2P_CHUNKS = 4


def _bf16_bits(a):
    u = lax.bitcast_convert_type(a, jnp.uint32)
    return u + jnp.uint32(0x7FFF) + ((u >> 16) & jnp.uint32(1))


def _outproj_kernel(yhg_ref, o1_ref, o2_ref, o3_ref, l1_ref, l2_ref, l3_ref, x_ref,
                    g1_ref, sc2_ref, sh2_ref, wout_ref, lng_ref, lnb_ref, wrt_ref,
                    x1_ref, h2_ref, h2p_ref, lgt_ref):
    l1, l2, l3 = l1_ref[...], l2_ref[...], l3_ref[...]
    mx = jnp.maximum(jnp.maximum(l1, l2), l3)
    e1, e2, e3 = jnp.exp(l1 - mx), jnp.exp(l2 - mx), jnp.exp(l3 - mx)
    den = e1 + e2 + e3
    att = (e1 / den) * o1_ref[...].astype(F32) + (e2 / den) * o2_ref[...].astype(F32) \
        + (e3 / den) * o3_ref[...].astype(F32)
    mix = _dot(yhg_ref[...], wout_ref[:HG_WIDTH, :]) + _dot(att.astype(BF16), wout_ref[HG_WIDTH:, :])
    x1 = _ln_rows(DN_ALPHA * x_ref[...] + g1_ref[...] * mix) * lng_ref[...] + lnb_ref[...]
    x1_ref[...] = x1
    h2 = _ln_rows(x1) * (1.0 + sc2_ref[...]) + sh2_ref[...]
    h2_ref[...] = h2
    for cidx in range(H2P_CHUNKS):
        lo = _bf16_bits(h2[:, 2 * LANES * cidx:2 * LANES * cidx + LANES])
        hi = _bf16_bits(h2[:, 2 * LANES * cidx + LANES:2 * LANES * (cidx + 1)])
        word = (lo >> 16) | (hi & jnp.uint32(0xFFFF0000))
        h2p_ref[:, cidx, :] = lax.bitcast_convert_type(word, jnp.int32)
    h_hi, h_lo = _split_bf16(h2)
    w_hi, w_lo = _split_bf16(wrt_ref[...])
    lgt_ref[...] = _dot_nt(w_hi, h_hi) + (_dot_nt(w_hi, h_lo) + _dot_nt(w_lo, h_hi))


def _out_projection(yhg, os_, ls_, x2, mod4, w_out_bf, ln_g, ln_b, w_router_t, seq, tm):
    t, d = x2.shape
    spb = seq // tm
    ne = w_router_t.shape[0]
    row = lambda w: pl.BlockSpec((tm, w), lambda i: (i, 0))
    mod_spec = lambda r: pl.BlockSpec((None, None, 1, d), lambda i, r=r: (i // spb, r, 0, 0))
    full = lambda a: pl.BlockSpec(a.shape, lambda i: (0,) * a.ndim)
    ln_g2, ln_b2 = ln_g.reshape(1, d), ln_b.reshape(1, d)
    return pl.pallas_call(
        _outproj_kernel,
        grid=(t // tm,),
        in_specs=[row(HG_WIDTH)] + [row(ATT_BW)] * 6 + [row(d),
                  mod_spec(2), mod_spec(4), mod_spec(3),
                  full(w_out_bf), full(ln_g2), full(ln_b2), full(w_router_t)],
        out_specs=[row(d), row(d), pl.BlockSpec((tm, H2P_CHUNKS, LANES), lambda i: (i, 0, 0)),
                   pl.BlockSpec((ne, tm), lambda i: (0, i))],
        out_shape=[jax.ShapeDtypeStruct((t, d), F32), jax.ShapeDtypeStruct((t, d), F32),
                   jax.ShapeDtypeStruct((t, H2P_CHUNKS, LANES), jnp.int32),
                   jax.ShapeDtypeStruct((ne, t), F32)],
        compiler_params=_cparams(("parallel",)),
        name="merge_outproj_ln",
    )(yhg, *os_, *ls_, x2, mod4, mod4, mod4, w_out_bf, ln_g2, ln_b2, w_router_t)


def _argmax_rows(cur, iota, nrows):
    m = jnp.max(cur, axis=0, keepdims=True)
    idx = jnp.min(jnp.where(cur == m, iota, nrows), axis=0, keepdims=True)
    return m, idx, iota == idx


def _route_kernel(lgt_ref, rb_ref, eidx_ref, w_ref, rank_ref, cnt_ref, carry):
    ne = N_EXPERTS
    gsz = ne // N_GROUPS
    tt = lgt_ref.shape[1]

    @pl.when(pl.program_id(0) == 0)
    def _():
        carry[...] = jnp.zeros_like(carry)

    sc = jax.nn.sigmoid(lgt_ref[...])
    biased = sc + rb_ref[...]
    g3 = biased.reshape(N_GROUPS, gsz, tt)
    io3 = lax.broadcasted_iota(jnp.int32, (N_GROUPS, gsz, tt), 1)
    m1 = jnp.max(g3, axis=1, keepdims=True)
    first = jnp.min(jnp.where(g3 == m1, io3, gsz), axis=1, keepdims=True)
    m2 = jnp.max(jnp.where(io3 == first, NEG_INF, g3), axis=1, keepdims=True)
    gs = (m1 + m2).reshape(N_GROUPS, tt)
    io8 = lax.broadcasted_iota(jnp.int32, (N_GROUPS, tt), 0)
    sel = jnp.zeros((N_GROUPS, tt), jnp.int32)
    cur = gs
    for _ in range(TOPK_GROUPS):
        _, _, pick = _argmax_rows(cur, io8, N_GROUPS)
        sel = jnp.where(pick, 1, sel)
        cur = jnp.where(pick, NEG_INF, cur)
    masked = jnp.where(sel.reshape(N_GROUPS, 1, tt) > 0, g3, NEG_INF).reshape(ne, tt)
    ioe = lax.broadcasted_iota(jnp.int32, (ne, tt), 0)
    cur = masked
    idxs, ws, picks = [], [], []
    for _ in range(TOP_K):
        _, idx, pick = _argmax_rows(cur, ioe, ne)
        idxs.append(idx)
        picks.append(pick)
        ws.append(jnp.sum(jnp.where(pick, sc, 0.0), axis=0, keepdims=True))
        cur = jnp.where(pick, NEG_INF, cur)
    wk = jnp.concatenate(ws, axis=0)
    eidx_ref[...] = jnp.concatenate(idxs, axis=0)
    w_ref[...] = wk / jnp.sum(wk, axis=0, keepdims=True) * ROUTED_SCALE
    chosen = jnp.where(cur == NEG_INF, jnp.where(masked == NEG_INF, 0.0, 1.0), 0.0)
    r_i = lax.broadcasted_iota(jnp.int32, (tt, tt), 0)
    c_i = lax.broadcasted_iota(jnp.int32, (tt, tt), 1)
    before = jnp.where(r_i < c_i, 1.0, 0.0).astype(BF16)
    pref = _dot(chosen.astype(BF16), before) + carry[...]
    rank_ref[...] = jnp.concatenate(
        [jnp.sum(jnp.where(p, pref, 0.0), axis=0, keepdims=True) for p in picks], axis=0).astype(jnp.int32)
    carry[...] = carry[...] + jnp.sum(chosen, axis=1, keepdims=True)
    cnt_ref[...] = carry[...]


def _route(logits_t, router_bias, tt):
    ne, t = logits_t.shape
    tok = pl.BlockSpec((TOP_K, tt), lambda i: (0, i))
    return pl.pallas_call(
        _route_kernel,
        grid=(t // tt,),
        in_specs=[pl.BlockSpec((ne, tt), lambda i: (0, i)),
                  pl.BlockSpec((ne, 1), lambda i: (0, 0))],
        out_specs=[tok, tok, tok, pl.BlockSpec((ne, 1), lambda i: (0, 0))],
        out_shape=[jax.ShapeDtypeStruct((TOP_K, t), jnp.int32), jax.ShapeDtypeStruct((TOP_K, t), F32),
                   jax.ShapeDtypeStruct((TOP_K, t), jnp.int32), jax.ShapeDtypeStruct((ne, 1), F32)],
        scratch_shapes=[pltpu.VMEM((ne, 1), F32)],
        compiler_params=_cparams(("arbitrary",)),
        name="router_topk",
    )(logits_t, router_bias.reshape(ne, 1))


def _plan_kernel(cnt_ref, eidx_ref, rank_ref, dest_ref, seq_ref, dexp_ref, meta_ref):
    ne = N_EXPERTS
    tt = eidx_ref.shape[1]
    nblk = seq_ref.shape[1]
    cnt = cnt_ref[...].astype(jnp.int32)
    padded = ((cnt + (MOE_BLOCK - 1)) // MOE_BLOCK) * MOE_BLOCK
    r_i = lax.broadcasted_iota(jnp.int32, (ne, ne), 0)
    c_i = lax.broadcasted_iota(jnp.int32, (ne, ne), 1)
    incl = jnp.where(c_i <= r_i, 1.0, 0.0)
    pend = jnp.dot(incl, jnp.broadcast_to(padded.astype(F32), (ne, LANES)),
                   preferred_element_type=F32, precision=lax.Precision.HIGHEST)[:, 0:1]
    pend = pend.astype(jnp.int32)
    pstart = pend - padded
    ioe = lax.broadcasted_iota(jnp.int32, (ne, tt), 0)
    rows = []
    for k in range(TOP_K):
        sel = ioe == eidx_ref[k:k + 1, :]
        rows.append(jnp.sum(jnp.where(sel, pstart, 0), axis=0, keepdims=True))
    dest_ref[...] = jnp.concatenate(rows, axis=0) + rank_ref[...]
    blk0 = lax.broadcasted_iota(jnp.int32, (ne, nblk), 1) * MOE_BLOCK
    be = jnp.minimum(jnp.sum(jnp.where(pend <= blk0, 1, 0), axis=0, keepdims=True), ne - 1)
    present = cnt > 0
    strict = jnp.where(c_i < r_i, 1.0, 0.0).astype(BF16)
    sidx = _dot(strict, jnp.broadcast_to(jnp.where(present, 1.0, 0.0), (ne, LANES)).astype(BF16))[:, 0:1]
    sidx = sidx.astype(jnp.int32)
    dexp_ref[...] = jnp.sum(jnp.where(jnp.logical_and(present, sidx == c_i), r_i, 0), axis=0, keepdims=True)
    ioeb = lax.broadcasted_iota(jnp.int32, (ne, nblk), 0)
    seq_ref[...] = jnp.sum(jnp.where(ioeb == be, sidx, 0), axis=0, keepdims=True)
    nu = jnp.max(pend, axis=0, keepdims=True) // MOE_BLOCK
    nd = jnp.sum(jnp.where(present, 1, 0), axis=0, keepdims=True)
    lane = lax.broadcasted_iota(jnp.int32, (1, LANES), 1)
    meta_ref[...] = jnp.where(lane == 0, nu, jnp.where(lane == 1, nd, 0))


def _dispatch_plan(cnt, eidx, rank, tt):
    k, t = eidx.shape
    n_blocks = -(-(t * k) // MOE_BLOCK) + N_EXPERTS
    tok = pl.BlockSpec((k, tt), lambda i: (0, i))
    one = lambda n: pl.BlockSpec((1, n), lambda i: (0, 0))
    dest, seq, dexp, meta = pl.pallas_call(
        _plan_kernel,
        grid=(t // tt,),
        in_specs=[pl.BlockSpec(cnt.shape, lambda i: (0, 0)), tok, tok],
        out_specs=[tok, one(n_blocks), one(N_EXPERTS), one(LANES)],
        out_shape=[jax.ShapeDtypeStruct((k, t), jnp.int32), jax.ShapeDtypeStruct((1, n_blocks), jnp.int32),
                   jax.ShapeDtypeStruct((1, N_EXPERTS), jnp.int32), jax.ShapeDtypeStruct((1, LANES), jnp.int32)],
        compiler_params=_cparams(("arbitrary",)),
        name="dispatch_plan",
    )(cnt, eidx, rank)
    return dest, seq.reshape(n_blocks), dexp.reshape(N_EXPERTS), meta.reshape(LANES), n_blocks


SC_CORES = 2
SC_SUBCORES = 16
SC_LANES = 16
SC_CHUNK = 16384
SC_UNROLL = 4


def _row_tables(dest, n_rows):
    k, t = dest.shape
    a = k * t
    nw = SC_CORES * SC_SUBCORES
    per_w = n_rows // nw
    assert n_rows % (nw * SC_LANES) == 0 and t % SC_CHUNK == 0
    mesh = plsc.VectorSubcoreMesh(core_axis_name="c", subcore_axis_name="s")
    cp = pltpu.CompilerParams()
    if "needs_layout_passes" in pltpu.CompilerParams.__dataclass_fields__:
        cp = dataclasses.replace(cp, needs_layout_passes=False)

    def body(dest_hbm, tok_out, dbuf, tloc):
        wid = lax.axis_index("s") * SC_CORES + lax.axis_index("c")
        base = wid * per_w

        @pl.loop(0, per_w // SC_LANES)
        def _(i):
            tloc[pl.ds(i * SC_LANES, SC_LANES)] = jnp.zeros((SC_LANES,), jnp.int32)

        lane = lax.iota(jnp.int32, SC_LANES)

        @pl.loop(0, a // SC_CHUNK)
        def _(c):
            pltpu.sync_copy(dest_hbm.at[pl.ds(c * SC_CHUNK, SC_CHUNK)], dbuf)
            tok0 = lax.rem(c * SC_CHUNK, t)

            @pl.loop(0, SC_CHUNK // (SC_LANES * SC_UNROLL))
            def _(j):
                for u in range(SC_UNROLL):
                    off = (j * SC_UNROLL + u) * SC_LANES
                    loc = dbuf[pl.ds(off, SC_LANES)] - base
                    mine = jnp.logical_and(loc >= 0, loc < per_w)
                    loc = jnp.where(mine, loc, 0)
                    plsc.store_scatter(tloc, [loc], tok0 + off + lane, mask=mine)

        pltpu.sync_copy(tloc, tok_out.at[pl.ds(base, per_w)])

    fn = pl.kernel(
        body,
        out_type=jax.ShapeDtypeStruct((n_rows,), jnp.int32),
        mesh=mesh,
        scratch_types=[pltpu.VMEM((SC_CHUNK,), jnp.int32), pltpu.VMEM((per_w,), jnp.int32)],
        compiler_params=cp,
        name="row_tables",
    )
    return fn(dest.reshape(a))


ROW_TILE = (SUBLANES, LANES)


FFN_GROUP = 4
GATHER_BATCH = 32


def _gather_rows(h2p_ref, tok_ref, row, buf):
    for j0 in range(0, MOE_BLOCK, GATHER_BATCH):
        vals = [h2p_ref[tok_ref[row, j]] for j in range(j0, j0 + GATHER_BATCH)]
        for j, v in zip(range(j0, j0 + GATHER_BATCH), vals):
            buf[pl.ds(H2P_CHUNKS * j, H2P_CHUNKS), :] = v


def _ffn_kernel(seq_ref, dexp_ref, meta_ref, tokc_ref, tokn_ref, h2p_ref, wg_hbm, wu_hbm, wd_hbm,
                y_ref, buf_even, buf_odd, yraw_even, yraw_odd,
                wg_a, wu_a, wd_a, wg_b, wu_b, wd_b, sems):
    i = pl.program_id(0)
    n_used = meta_ref[0]
    n_exp = meta_ref[1]
    nblk = seq_ref.shape[0]
    wbufs = ((wg_a, wu_a, wd_a), (wg_b, wu_b, wd_b))

    def emit_rows(j):
        yraw = yraw_even if j % 2 == 0 else yraw_odd
        y_ref[pl.ds(j * MOE_BLOCK, MOE_BLOCK)] = pltpu.einshape("r(cl)->rcl", yraw[...], c=SUBLANES)

    def weight_copies(s, par):
        e = dexp_ref[s]
        return [pltpu.make_async_copy(src.at[e], dst, sems.at[par, n])
                for n, (src, dst) in enumerate(zip((wg_hbm, wu_hbm, wd_hbm), wbufs[par]))]

    @pl.when(i == 0)
    def _():
        _gather_rows(h2p_ref, tokc_ref, 0, buf_even)
        for cp in weight_copies(0, 0):
            cp.start()

    for j in range(FFN_GROUP):
        b = i * FFN_GROUP + j
        cur, nxt = (buf_even, buf_odd) if j % 2 == 0 else (buf_odd, buf_even)
        rows = pl.ds(j * MOE_BLOCK, MOE_BLOCK)
        used = b < n_used
        s = seq_ref[jnp.minimum(b, nblk - 1)]
        first_of_expert = jnp.logical_or(b == 0, s != seq_ref[jnp.clip(b - 1, 0, nblk - 1)])

        for par in range(2):
            wg_c, wu_c, wd_c = wbufs[par]

            @pl.when(jnp.logical_and(used, s % 2 == par))
            def _():
                @pl.when(first_of_expert)
                def _():
                    for cp in weight_copies(s, par):
                        cp.wait()

                    @pl.when(s + 1 < n_exp)
                    def _():
                        for cp in weight_copies(s + 1, 1 - par):
                            cp.start()

                if j + 1 < FFN_GROUP:
                    _gather_rows(h2p_ref, tokc_ref, j + 1, nxt)
                else:
                    _gather_rows(h2p_ref, tokn_ref, 0, nxt)
                if j >= 1:
                    emit_rows(j - 1)
                parts = []
                for cidx in range(H2P_CHUNKS):
                    word = cur[pl.ds(cidx, MOE_BLOCK, stride=H2P_CHUNKS), :]
                    parts.append(lax.bitcast_convert_type(word << 16, F32))
                    parts.append(lax.bitcast_convert_type(word & jnp.int32(-65536), F32))
                x = jnp.concatenate(parts, axis=1)
                g = _dot(x, wg_c[...])
                u = _dot(x, wu_c[...])
                hm = (g * jax.nn.sigmoid(g)) * u
                (yraw_even if j % 2 == 0 else yraw_odd)[...] = _dot(hm, wd_c[...])

        @pl.when(jnp.logical_not(used))
        def _():
            if j >= 1:
                @pl.when(b - 1 < n_used)
                def _():
                    emit_rows(j - 1)
            y_ref[rows] = jnp.zeros((MOE_BLOCK,) + ROW_TILE, F32)

    @pl.when(i * FFN_GROUP + (FFN_GROUP - 1) < n_used)
    def _():
        emit_rows(FFN_GROUP - 1)


def _routed_experts(h2p, row_token, seq, dexp, meta, n_blocks, wg, wu, wd):
    d = wg.shape[1]
    de = wg.shape[2]
    ng = n_blocks // FFN_GROUP
    assert n_blocks % FFN_GROUP == 0 and FFN_GROUP % 2 == 0
    tok3 = row_token.reshape(ng, FFN_GROUP, MOE_BLOCK)
    last_step = lambda m: (m[0] + (FFN_GROUP - 1)) // FFN_GROUP
    smem = lambda imap: pl.BlockSpec((None, FFN_GROUP, MOE_BLOCK), imap, memory_space=pltpu.SMEM)
    grid_spec = pltpu.PrefetchScalarGridSpec(
        num_scalar_prefetch=3,
        grid=(ng,),
        in_specs=[
            smem(lambda i, sq, dx, m: (i, 0, 0)),
            smem(lambda i, sq, dx, m: (jnp.minimum(i + 1, ng - 1), 0, 0)),
            pl.BlockSpec(h2p.shape, lambda i, sq, dx, m: (0, 0, 0), pipeline_mode=pl.Buffered(1)),
            pl.BlockSpec(memory_space=pl.ANY),
            pl.BlockSpec(memory_space=pl.ANY),
            pl.BlockSpec(memory_space=pl.ANY),
        ],
        out_specs=pl.BlockSpec((FFN_GROUP * MOE_BLOCK,) + ROW_TILE,
                               lambda i, sq, dx, m: (jnp.minimum(i, last_step(m)), 0, 0)),
        scratch_shapes=[pltpu.VMEM((MOE_BLOCK * H2P_CHUNKS, LANES), jnp.int32)] * 2 + [
            pltpu.VMEM((MOE_BLOCK, d), F32)] * 2 + [
            pltpu.VMEM((d, de), F32), pltpu.VMEM((d, de), F32), pltpu.VMEM((de, d), F32)] * 2 + [
            pltpu.SemaphoreType.DMA((2, 3))],
    )
    return pl.pallas_call(
        _ffn_kernel,
        grid_spec=grid_spec,
        out_shape=jax.ShapeDtypeStruct(((ng + 1) * FFN_GROUP * MOE_BLOCK,) + ROW_TILE, F32),
        compiler_params=_cparams(("arbitrary",)),
        name="routed_experts",
    )(seq, dexp, meta, tok3, tok3, h2p, wg, wu, wd)


COMBINE_TOKENS = 128


def _row_copies(src_hbm, idx_ref, buf, sem):
    return [pltpu.make_async_copy(src_hbm.at[idx_ref[k, j]], buf.at[k, j], sem)
            for k in range(TOP_K) for j in range(COMBINE_TOKENS)]


def _final_kernel(dc_ref, dn_ref, wk_ref, x1_ref, h2_ref, g2_ref, wsg_ref, wsu_ref, wsd_ref, lng_ref, lnb_ref,
                  y_hbm, o_ref, ybuf, sems):
    i = pl.program_id(0)
    nsteps = pl.num_programs(0)
    slot = i % 2

    def issue(d_ref, s):
        for n, cp in enumerate(_row_copies(y_hbm, d_ref, ybuf.at[s], sems.at[s])):
            cp.start(priority=n % 2)

    @pl.when(i == 0)
    def _():
        issue(dc_ref, 0)

    @pl.when(i + 1 < nsteps)
    def _():
        issue(dn_ref, 1 - slot)

    hb = h2_ref[...].astype(BF16)
    g = _dot(hb, wsg_ref[...])
    u = _dot(hb, wsu_ref[...])
    shared = _dot(((g * jax.nn.sigmoid(g)) * u).astype(BF16), wsd_ref[...])
    for cp in _row_copies(y_hbm, dc_ref, ybuf.at[slot], sems.at[slot]):
        cp.wait()
    rows = []
    for j in range(COMBINE_TOKENS):
        acc = ybuf[slot, 0, j] * wk_ref[0, j]
        for k in range(1, TOP_K):
            acc = acc + ybuf[slot, k, j] * wk_ref[k, j]
        rows.append(acc)
    routed = pltpu.einshape("tcl->t(cl)", jnp.stack(rows, axis=0))
    x2 = DN_ALPHA * x1_ref[...] + g2_ref[...] * (routed + shared)
    o_ref[...] = _ln_rows(x2) * lng_ref[...] + lnb_ref[...]


def _combine_final(x1, h2, mod4, dest, wk, y_rows, wsg, wsu, wsd, ln_g, ln_b, seq):
    t, d = x1.shape
    tt = COMBINE_TOKENS
    nsteps = t // tt
    spb = seq // tt
    row = pl.BlockSpec((tt, d), lambda i: (i, 0))
    full = lambda a: pl.BlockSpec(a.shape, lambda i: (0,) * a.ndim)
    ln_g2, ln_b2 = ln_g.reshape(1, d), ln_b.reshape(1, d)
    return pl.pallas_call(
        _final_kernel,
        grid=(nsteps,),
        in_specs=[pl.BlockSpec((TOP_K, tt), lambda i: (0, i), memory_space=pltpu.SMEM),
                  pl.BlockSpec((TOP_K, tt), lambda i: (0, jnp.minimum(i + 1, nsteps - 1)),
                               memory_space=pltpu.SMEM),
                  pl.BlockSpec((TOP_K, tt), lambda i: (0, i), memory_space=pltpu.SMEM),
                  row, row,
                  pl.BlockSpec((None, None, 1, d), lambda i: (i // spb, 5, 0, 0)),
                  full(wsg), full(wsu), full(wsd), full(ln_g2), full(ln_b2),
                  pl.BlockSpec(memory_space=pl.ANY)],
        out_specs=row,
        out_shape=jax.ShapeDtypeStruct((t, d), F32),
        scratch_shapes=[pltpu.VMEM((2, TOP_K, tt) + ROW_TILE, F32),
                        pltpu.SemaphoreType.DMA((2,))],
        compiler_params=_cparams(("arbitrary",)),
        name="shared_combine_ln",
    )(dest, dest, wk, x1, h2, mod4, wsg, wsu, wsd, ln_g2, ln_b2, y_rows)


def kernel(x, c, w_ada, b_ada, w_in, hg_lower_bound, hg_norm_w, rel_bias, w_out, ln1_g, ln1_b, w_router,
           router_bias, w_e_gate, w_e_up, w_e_down, w_sh_gate, w_sh_up, w_sh_down, ln2_g, ln2_b):
    bsz, seq, d = x.shape
    t = bsz * seq
    assert w_ada.shape[0] == DEPTH and seq % (ATT_BRANCHES[-1][0]) == 0
    x2 = x.reshape(t, d)
    bias = _bias_tables(rel_bias)
    for l in range(DEPTH):
        mod4 = _modulation(c, w_ada[l], b_ada[l]).reshape(bsz, 6, 1, d)
        w_in_bf = _cast_bf16(w_in[l], 256)
        hq, hf, hi, hg, *qkv = _in_projection(x2, mod4, w_in_bf, seq, 512)
        nbr = len(ATT_BRANCHES)
        y_hg = _hgrn2(hq, hf, hi, hg, hg_lower_bound, hg_norm_w[l], bsz, seq)
        os_, ls_ = [], []
        for g, (_, dil) in enumerate(ATT_BRANCHES):
            o, lse = _dilated_attention(qkv[g], qkv[nbr + g], qkv[2 * nbr + g], bias[g], bsz, seq, dil)
            os_.append(o)
            ls_.append(lse)
        w_out_bf = _cast_bf16(w_out[l], 256)
        x1, h2, h2p, logits_t = _out_projection(y_hg, os_, ls_, x2, mod4, w_out_bf, ln1_g[l], ln1_b[l],
                                                w_router[l].T, seq, 512)
        eidx, wk, rank, cnt = _route(logits_t, router_bias[l], 256)
        dest, blk_seq, dexp, meta, n_blocks = _dispatch_plan(cnt, eidx, rank, 512)
        row_token = _row_tables(dest, n_blocks * MOE_BLOCK)
        y_rows = _routed_experts(h2p, row_token, blk_seq, dexp, meta, n_blocks,
                                 w_e_gate[l], w_e_up[l], w_e_down[l])
        x2 = _combine_final(x1, h2, mod4, dest, wk, y_rows,
                            _cast_bf16(w_sh_gate[l], 256), _cast_bf16(w_sh_up[l], 256),
                            _cast_bf16(w_sh_down[l], 256), ln2_g[l], ln2_b[l], seq)
    return x2.reshape(bsz, seq, d)
```

```python
import dataclasses
import math

import jax
import jax.numpy as jnp
import numpy as np
from jax import lax
from jax.experimental import pallas as pl
from jax.experimental.pallas import tpu as pltpu
from jax.experimental.pallas import tpu_sc as plsc

HG_HEADS = 4
HG_DK = 128
HG_WIDTH = HG_HEADS * HG_DK
ATT_BRANCHES = ((128, 1), (512, 4), (2048, 16))
ATT_HEADS_PER_BRANCH = 4
ATT_HEAD_DIM = 64
ATT_BW = ATT_HEADS_PER_BRANCH * ATT_HEAD_DIM
ATT_BLOCK = 128
REL_BUCKETS = 32
REL_MAX_DIST = 2048
N_EXPERTS = 256
TOP_K = 8
N_GROUPS = 8
TOPK_GROUPS = 4
ROUTED_SCALE = 2.5
MOE_BLOCK = 128
DEPTH = 1
DN_ALPHA = (2 * DEPTH) ** 0.25
LN_EPS = 1e-5
RMS_EPS = 1e-6

LANES = 128
SUBLANES = 8
VMEM_LIMIT_BYTES = 56 * 1024 * 1024

F32 = jnp.float32
BF16 = jnp.bfloat16
NEG_INF = float("-inf")


def _cparams(sem):
    return pltpu.CompilerParams(dimension_semantics=sem, vmem_limit_bytes=VMEM_LIMIT_BYTES)


def _ln_rows(x):
    mu = jnp.mean(x, axis=-1, keepdims=True)
    xc = x - mu
    var = jnp.mean(xc * xc, axis=-1, keepdims=True)
    return xc * lax.rsqrt(var + LN_EPS)


def _dot(a, b):
    return jnp.dot(a, b, preferred_element_type=F32)


def _dot_nt(a, b):
    return lax.dot_general(a, b, (((1,), (1,)), ((), ())), preferred_element_type=F32)


def _dot_tn(a, b):
    return lax.dot_general(a, b, (((0,), (0,)), ((), ())), preferred_element_type=F32)


def _cast_kernel(w_ref, o_ref):
    o_ref[...] = w_ref[...].astype(o_ref.dtype)


def _cast_bf16(w, rows_per_step):
    r, c = w.shape
    return pl.pallas_call(
        _cast_kernel,
        grid=(r // rows_per_step,),
        in_specs=[pl.BlockSpec((rows_per_step, c), lambda i: (i, 0))],
        out_specs=pl.BlockSpec((rows_per_step, c), lambda i: (i, 0)),
        out_shape=jax.ShapeDtypeStruct((r, c), BF16),
        compiler_params=_cparams(("parallel",)),
        name="cast_bf16",
    )(w)


def _mod_kernel(c_ref, w_ref, b_ref, o_ref):
    c = c_ref[...]
    cond = c * jax.nn.sigmoid(c)
    o_ref[...] = jnp.dot(cond, w_ref[...], preferred_element_type=F32,
                         precision=lax.Precision.HIGHEST) + b_ref[...]


def _modulation(c, w_ada, b_ada):
    bsz, d = c.shape
    n = w_ada.shape[1]
    rows = -(-bsz // SUBLANES) * SUBLANES
    cpad = jnp.zeros((rows, d), F32).at[:bsz].set(c)
    tn = 1024
    out = pl.pallas_call(
        _mod_kernel,
        grid=(n // tn,),
        in_specs=[pl.BlockSpec((rows, d), lambda j: (0, 0)),
                  pl.BlockSpec((d, tn), lambda j: (0, j)),
                  pl.BlockSpec((1, tn), lambda j: (0, j))],
        out_specs=pl.BlockSpec((rows, tn), lambda j: (0, j)),
        out_shape=jax.ShapeDtypeStruct((rows, n), F32),
        compiler_params=_cparams(("parallel",)),
        name="adaln_modulation",
    )(cpad, w_ada, b_ada.reshape(1, n))
    return out[:bsz]


_IN_HG = 4
_IN_ATT = 3 * len(ATT_BRANCHES)


def _inproj_kernel(x_ref, sc_ref, sh_ref, w_ref, *outs):
    x = x_ref[...]
    h = _ln_rows(x) * (1.0 + sc_ref[...]) + sh_ref[...]
    hb = h.astype(BF16)
    col = 0
    for k, o_ref in enumerate(outs):
        width = o_ref.shape[-1]
        y = _dot(hb, w_ref[:, col:col + width])
        if _IN_HG <= k < _IN_HG + len(ATT_BRANCHES):
            y = y * (ATT_HEAD_DIM ** -0.5)
        o_ref[...] = y.astype(o_ref.dtype)
        col += width


def _in_projection(x2, mod4, w_in_bf, seq, tm):
    t, d = x2.shape
    steps_per_batch = seq // tm
    widths = [HG_WIDTH] * _IN_HG + [ATT_BW] * _IN_ATT
    dtypes = [BF16, F32, BF16, BF16] + [BF16] * _IN_ATT
    mod_spec = lambda row: pl.BlockSpec((None, None, 1, d),
                                        lambda i, row=row: (i // steps_per_batch, row, 0, 0))
    outs = pl.pallas_call(
        _inproj_kernel,
        grid=(t // tm,),
        in_specs=[pl.BlockSpec((tm, d), lambda i: (i, 0)),
                  mod_spec(1), mod_spec(0),
                  pl.BlockSpec(w_in_bf.shape, lambda i: (0, 0))],
        out_specs=[pl.BlockSpec((tm, w), lambda i: (i, 0)) for w in widths],
        out_shape=[jax.ShapeDtypeStruct((t, w), dt) for w, dt in zip(widths, dtypes)],
        compiler_params=_cparams(("parallel",)),
        name="ln_in_projection",
    )(x2, mod4, mod4, w_in_bf)
    return outs


HG_CHUNK = 64
HG_CHUNKS_PER_STEP = 4
HG_SUB = 8
HG_LEVELS = (64, 32, 16)


def _hgrn_chunk(q, z, iv, lb):
    c = HG_CHUNK
    f = lb + (1.0 - lb) * jax.nn.sigmoid(z)
    lf = jnp.log(f)
    kk = (1.0 - lb) * jax.nn.sigmoid(-z)
    r_i = lax.broadcasted_iota(jnp.int32, (c, c), 0)
    c_i = lax.broadcasted_iota(jnp.int32, (c, c), 1)
    tril = (c_i <= r_i).astype(F32)
    b = jnp.dot(tril, lf, preferred_element_type=F32, precision=lax.Precision.HIGHEST)

    row = lax.broadcasted_iota(jnp.int32, (c, HG_DK), 0)
    scores = jnp.zeros((c, c), F32)
    for m in HG_LEVELS:
        nb = c // m
        b3 = b.reshape(nb, m, HG_DK)
        piv = jnp.broadcast_to(b3[:, m // 2 - 1:m // 2, :], (nb, m, HG_DK)).reshape(c, HG_DK)
        second = (row % m) >= (m // 2)
        qt = jnp.where(second, q * jnp.exp(b - piv), 0.0)
        kt = jnp.where(second, 0.0, kk * jnp.exp(piv - b))
        s_m = _dot_nt(qt.astype(BF16), kt.astype(BF16))
        if nb > 1:
            s_m = jnp.where((r_i // m) == (c_i // m), s_m, 0.0)
        scores = scores + s_m
    sub = HG_SUB
    t_i = lax.broadcasted_iota(jnp.int32, (sub, 1), 0)
    lane = lax.broadcasted_iota(jnp.int32, (sub, c), 1)
    diag_rows = []
    for j in range(c // sub):
        qb = q[j * sub:(j + 1) * sub]
        kb = kk[j * sub:(j + 1) * sub]
        bb = b[j * sub:(j + 1) * sub]
        a_j = jnp.zeros((sub, c), F32)
        for s in range(sub):
            w = qb * kb[s:s + 1] * jnp.exp(bb - bb[s:s + 1])
            col = jnp.sum(w, axis=-1, keepdims=True)
            col = jnp.where(t_i >= s, col, 0.0)
            a_j = jnp.where(lane == j * sub + s, col, a_j)
        diag_rows.append(a_j)
    scores = scores + jnp.concatenate(diag_rows, axis=0)

    ivb = iv.astype(BF16)
    intra = _dot(scores.astype(BF16), ivb)
    b_last = b[c - 1:c]
    kdec = (kk * jnp.exp(b_last - b)).astype(BF16)
    return (q * jnp.exp(b)).astype(BF16), intra, jnp.exp(b_last), _dot_tn(ivb, kdec)


def _hgrn_kernel(q_ref, f_ref, i_ref, g_ref, lbp_ref, nw_ref, o_ref, st_ref):
    @pl.when(pl.program_id(1) == 0)
    def _():
        st_ref[...] = jnp.zeros_like(st_ref)

    lbp = lbp_ref[...]
    e = jnp.exp(lbp - jnp.max(lbp, axis=0, keepdims=True))
    lb_all = e[0:1] / jnp.sum(e, axis=0, keepdims=True)
    heads = []
    for h in range(HG_HEADS):
        sl = slice(h * HG_DK, (h + 1) * HG_DK)
        st_t = st_ref[h]
        outs = []
        for n in range(HG_CHUNKS_PER_STEP):
            rows = slice(n * HG_CHUNK, (n + 1) * HG_CHUNK)
            qdec, intra, dec_last, kv = _hgrn_chunk(q_ref[rows, sl].astype(F32), f_ref[rows, sl],
                                                    i_ref[rows, sl].astype(F32), lb_all[:, sl])
            o = _dot_nt(qdec, st_t.astype(BF16)) + intra
            st_t = st_t * dec_last + kv
            outs.append(o * lax.rsqrt(jnp.mean(o * o, axis=-1, keepdims=True) + RMS_EPS))
        st_ref[h] = st_t
        heads.append(jnp.concatenate(outs, axis=0))
    o_all = jnp.concatenate(heads, axis=-1)
    g = g_ref[...].astype(F32)
    o_ref[...] = (o_all * nw_ref[...] * (g * jax.nn.sigmoid(g))).astype(o_ref.dtype)


def _hgrn2(hq, hf, hi, hg, lb_param, norm_w, bsz, seq):
    t = hq.shape[0]
    rows = HG_CHUNK * HG_CHUNKS_PER_STEP
    nc = seq // rows
    tok = lambda b, n: (b * nc + n, 0)
    spec = pl.BlockSpec((rows, HG_WIDTH), tok)
    return pl.pallas_call(
        _hgrn_kernel,
        grid=(bsz, nc),
        in_specs=[spec, spec, spec, spec,
                  pl.BlockSpec(lb_param.shape, lambda b, n: (0, 0)),
                  pl.BlockSpec((1, HG_WIDTH), lambda b, n: (0, 0))],
        out_specs=spec,
        out_shape=jax.ShapeDtypeStruct((t, HG_WIDTH), BF16),
        scratch_shapes=[pltpu.VMEM((HG_HEADS, HG_DK, HG_DK), F32)],
        compiler_params=_cparams(("parallel", "arbitrary")),
        name="hgrn2_scan",
    )(hq, hf, hi, hg, lb_param, norm_w.reshape(1, HG_WIDTH))


def _t5_bucket_np(dist):
    max_exact = REL_BUCKETS // 2
    n = np.maximum(dist, 0)
    nf = np.maximum(n, 1).astype(np.float32)
    large = max_exact + (np.log(nf / np.float32(max_exact)) / np.float32(math.log(REL_MAX_DIST / max_exact))
                         * np.float32(REL_BUCKETS - max_exact)).astype(np.int32)
    large = np.minimum(large, REL_BUCKETS - 1)
    return np.where(n < max_exact, n, large).astype(np.int32)


def _band_tables():
    w = ATT_BLOCK
    qi = np.arange(w)[:, None]
    ki = np.arange(2 * w)[None, :]
    m = w + qi - ki
    band = (m >= 0) & (m <= w)
    buckets = np.stack([_t5_bucket_np(m * dil) for _, dil in ATT_BRANCHES])
    return buckets, band


def _bias_kernel(rb_ref, bucket_ref, o_ref):
    g = pl.program_id(0)
    w = ATT_BLOCK
    bucket = bucket_ref[...]
    qi = lax.broadcasted_iota(jnp.int32, (w, 2 * w), 0)
    ki = lax.broadcasted_iota(jnp.int32, (w, 2 * w), 1)
    m = w + qi - ki
    band = (m >= 0) & (m <= w)
    for h in range(ATT_HEADS_PER_BRANCH):
        acc = jnp.zeros((w, 2 * w), F32)
        for c in range(REL_BUCKETS):
            acc = jnp.where(bucket == c, rb_ref[c, g * ATT_HEADS_PER_BRANCH + h], acc)
        full = jnp.where(band, acc, NEG_INF)
        o_ref[1, h] = full
        o_ref[0, h] = jnp.where(ki >= w, full, NEG_INF)


def _bias_tables(rel_bias):
    buckets, _ = _band_tables()
    g = len(ATT_BRANCHES)
    w = ATT_BLOCK
    return pl.pallas_call(
        _bias_kernel,
        grid=(g,),
        in_specs=[pl.BlockSpec(memory_space=pltpu.SMEM),
                  pl.BlockSpec((None, w, 2 * w), lambda i: (i, 0, 0))],
        out_specs=pl.BlockSpec((None, 2, ATT_HEADS_PER_BRANCH, w, 2 * w), lambda i: (i, 0, 0, 0, 0)),
        out_shape=jax.ShapeDtypeStruct((g, 2, ATT_HEADS_PER_BRANCH, w, 2 * w), F32),
        compiler_params=_cparams(("parallel",)),
        name="rel_bias_tables",
    )(rel_bias, jnp.asarray(buckets))


ATT_BLOCKS_PER_STEP = 4


def _attn_kernel(q_ref, kp_ref, kc_ref, vp_ref, vc_ref, bias_ref, o_ref, lse_ref):
    m = pl.program_id(2)
    w = ATT_BLOCK
    hb = ATT_HEADS_PER_BRANCH
    lane = lax.broadcasted_iota(jnp.int32, (w, ATT_BW), 1) // ATT_HEAD_DIM
    kall = jnp.concatenate([kp_ref[...], kc_ref[...]], axis=0)
    vall = jnp.concatenate([vp_ref[...], vc_ref[...]], axis=0)
    for blk in range(q_ref.shape[0] // w):
        q = q_ref[blk * w:(blk + 1) * w]
        q4 = jnp.concatenate([jnp.where(lane == h, q, jnp.zeros_like(q)) for h in range(hb)], axis=0)
        kk = kall[blk * w:(blk + 2) * w]
        vv = vall[blk * w:(blk + 2) * w]
        s4 = _dot_nt(q4, kk)
        bias = bias_ref[jnp.minimum(m, 1)] if blk == 0 else bias_ref[1]
        s4 = s4 + bias.reshape(hb * w, 2 * w)
        mx = jnp.max(s4, axis=-1, keepdims=True)
        p = jnp.exp(s4 - mx)
        l = jnp.sum(p, axis=-1, keepdims=True)
        o4 = _dot((p / l).astype(vv.dtype), vv)
        lse4 = mx + jnp.log(l)
        o = jnp.zeros((w, ATT_BW), F32)
        lse = jnp.zeros((w, ATT_BW), F32)
        for h in range(hb):
            o = jnp.where(lane == h, o4[h * w:(h + 1) * w], o)
            lse = jnp.where(lane == h, lse4[h * w:(h + 1) * w], lse)
        o_ref[blk * w:(blk + 1) * w] = o.astype(o_ref.dtype)
        lse_ref[blk * w:(blk + 1) * w] = lse


def _dilated_attention(q, k, v, bias_g, bsz, seq, dilation):
    t = q.shape[0]
    w = ATT_BLOCK
    l = seq // dilation
    pstep = math.gcd(ATT_BLOCKS_PER_STEP, l // w)
    nb = l // (w * pstep)
    view = lambda a: a.reshape(bsz, l, dilation * ATT_BW)
    cur = pl.BlockSpec((None, pstep * w, ATT_BW), lambda b, r, n: (b, n, r))
    prev = pl.BlockSpec((None, w, ATT_BW), lambda b, r, n: (b, jnp.maximum(pstep * n - 1, 0), r))
    o, lse = pl.pallas_call(
        _attn_kernel,
        grid=(bsz, dilation, nb),
        in_specs=[cur, prev, cur, prev, cur,
                  pl.BlockSpec(bias_g.shape, lambda b, r, n: (0, 0, 0, 0))],
        out_specs=[cur, cur],
        out_shape=[jax.ShapeDtypeStruct((bsz, l, dilation * ATT_BW), BF16),
                   jax.ShapeDtypeStruct((bsz, l, dilation * ATT_BW), F32)],
        compiler_params=_cparams(("parallel", "parallel", "arbitrary")),
        name=f"dilated_attention_d{dilation}",
    )(view(q), view(k), view(k), view(v), view(v), bias_g)
    return o.reshape(t, ATT_BW), lse.reshape(t, ATT_BW)


def _split_bf16(a):
    hi = a.astype(BF16)
    lo = (a - hi.astype(F32)).astype(BF16)
    return hi, lo


H2P_CHUNKS = 4


def _bf16_bits(a):
    u = lax.bitcast_convert_type(a, jnp.uint32)
    return u + jnp.uint32(0x7FFF) + ((u >> 16) & jnp.uint32(1))


def _outproj_kernel(yhg_ref, o1_ref, o2_ref, o3_ref, l1_ref, l2_ref, l3_ref, x_ref,
                    g1_ref, sc2_ref, sh2_ref, wout_ref, lng_ref, lnb_ref, wrt_ref,
                    x1_ref, h2_ref, h2p_ref, lgt_ref):
    l1, l2, l3 = l1_ref[...], l2_ref[...], l3_ref[...]
    mx = jnp.maximum(jnp.maximum(l1, l2), l3)
    e1, e2, e3 = jnp.exp(l1 - mx), jnp.exp(l2 - mx), jnp.exp(l3 - mx)
    den = e1 + e2 + e3
    att = (e1 / den) * o1_ref[...].astype(F32) + (e2 / den) * o2_ref[...].astype(F32) \
        + (e3 / den) * o3_ref[...].astype(F32)
    mix = _dot(yhg_ref[...], wout_ref[:HG_WIDTH, :]) + _dot(att.astype(BF16), wout_ref[HG_WIDTH:, :])
    x1 = _ln_rows(DN_ALPHA * x_ref[...] + g1_ref[...] * mix) * lng_ref[...] + lnb_ref[...]
    x1_ref[...] = x1
    h2 = _ln_rows(x1) * (1.0 + sc2_ref[...]) + sh2_ref[...]
    h2_ref[...] = h2
    for cidx in range(H2P_CHUNKS):
        lo = _bf16_bits(h2[:, 2 * LANES * cidx:2 * LANES * cidx + LANES])
        hi = _bf16_bits(h2[:, 2 * LANES * cidx + LANES:2 * LANES * (cidx + 1)])
        word = (lo >> 16) | (hi & jnp.uint32(0xFFFF0000))
        h2p_ref[:, cidx, :] = lax.bitcast_convert_type(word, jnp.int32)
    h_hi, h_lo = _split_bf16(h2)
    w_hi, w_lo = _split_bf16(wrt_ref[...])
    lgt_ref[...] = _dot_nt(w_hi, h_hi) + (_dot_nt(w_hi, h_lo) + _dot_nt(w_lo, h_hi))


def _out_projection(yhg, os_, ls_, x2, mod4, w_out_bf, ln_g, ln_b, w_router_t, seq, tm):
    t, d = x2.shape
    spb = seq // tm
    ne = w_router_t.shape[0]
    row = lambda w: pl.BlockSpec((tm, w), lambda i: (i, 0))
    mod_spec = lambda r: pl.BlockSpec((None, None, 1, d), lambda i, r=r: (i // spb, r, 0, 0))
    full = lambda a: pl.BlockSpec(a.shape, lambda i: (0,) * a.ndim)
    ln_g2, ln_b2 = ln_g.reshape(1, d), ln_b.reshape(1, d)
    return pl.pallas_call(
        _outproj_kernel,
        grid=(t // tm,),
        in_specs=[row(HG_WIDTH)] + [row(ATT_BW)] * 6 + [row(d),
                  mod_spec(2), mod_spec(4), mod_spec(3),
                  full(w_out_bf), full(ln_g2), full(ln_b2), full(w_router_t)],
        out_specs=[row(d), row(d), pl.BlockSpec((tm, H2P_CHUNKS, LANES), lambda i: (i, 0, 0)),
                   pl.BlockSpec((ne, tm), lambda i: (0, i))],
        out_shape=[jax.ShapeDtypeStruct((t, d), F32), jax.ShapeDtypeStruct((t, d), F32),
                   jax.ShapeDtypeStruct((t, H2P_CHUNKS, LANES), jnp.int32),
                   jax.ShapeDtypeStruct((ne, t), F32)],
        compiler_params=_cparams(("parallel",)),
        name="merge_outproj_ln",
    )(yhg, *os_, *ls_, x2, mod4, mod4, mod4, w_out_bf, ln_g2, ln_b2, w_router_t)


def _argmax_rows(cur, iota, nrows):
    m = jnp.max(cur, axis=0, keepdims=True)
    idx = jnp.min(jnp.where(cur == m, iota, nrows), axis=0, keepdims=True)
    return m, idx, iota == idx


def _route_kernel(lgt_ref, rb_ref, eidx_ref, w_ref, rank_ref, cnt_ref, carry):
    ne = N_EXPERTS
    gsz = ne // N_GROUPS
    tt = lgt_ref.shape[1]

    @pl.when(pl.program_id(0) == 0)
    def _():
        carry[...] = jnp.zeros_like(carry)

    sc = jax.nn.sigmoid(lgt_ref[...])
    biased = sc + rb_ref[...]
    g3 = biased.reshape(N_GROUPS, gsz, tt)
    io3 = lax.broadcasted_iota(jnp.int32, (N_GROUPS, gsz, tt), 1)
    m1 = jnp.max(g3, axis=1, keepdims=True)
    first = jnp.min(jnp.where(g3 == m1, io3, gsz), axis=1, keepdims=True)
    m2 = jnp.max(jnp.where(io3 == first, NEG_INF, g3), axis=1, keepdims=True)
    gs = (m1 + m2).reshape(N_GROUPS, tt)
    io8 = lax.broadcasted_iota(jnp.int32, (N_GROUPS, tt), 0)
    sel = jnp.zeros((N_GROUPS, tt), jnp.int32)
    cur = gs
    for _ in range(TOPK_GROUPS):
        _, _, pick = _argmax_rows(cur, io8, N_GROUPS)
        sel = jnp.where(pick, 1, sel)
        cur = jnp.where(pick, NEG_INF, cur)
    masked = jnp.where(sel.reshape(N_GROUPS, 1, tt) > 0, g3, NEG_INF).reshape(ne, tt)
    ioe = lax.broadcasted_iota(jnp.int32, (ne, tt), 0)
    cur = masked
    idxs, ws, picks = [], [], []
    for _ in range(TOP_K):
        _, idx, pick = _argmax_rows(cur, ioe, ne)
        idxs.append(idx)
        picks.append(pick)
        ws.append(jnp.sum(jnp.where(pick, sc, 0.0), axis=0, keepdims=True))
        cur = jnp.where(pick, NEG_INF, cur)
    wk = jnp.concatenate(ws, axis=0)
    eidx_ref[...] = jnp.concatenate(idxs, axis=0)
    w_ref[...] = wk / jnp.sum(wk, axis=0, keepdims=True) * ROUTED_SCALE
    chosen = jnp.where(cur == NEG_INF, jnp.where(masked == NEG_INF, 0.0, 1.0), 0.0)
    r_i = lax.broadcasted_iota(jnp.int32, (tt, tt), 0)
    c_i = lax.broadcasted_iota(jnp.int32, (tt, tt), 1)
    before = jnp.where(r_i < c_i, 1.0, 0.0).astype(BF16)
    pref = _dot(chosen.astype(BF16), before) + carry[...]
    rank_ref[...] = jnp.concatenate(
        [jnp.sum(jnp.where(p, pref, 0.0), axis=0, keepdims=True) for p in picks], axis=0).astype(jnp.int32)
    carry[...] = carry[...] + jnp.sum(chosen, axis=1, keepdims=True)
    cnt_ref[...] = carry[...]


def _route(logits_t, router_bias, tt):
    ne, t = logits_t.shape
    tok = pl.BlockSpec((TOP_K, tt), lambda i: (0, i))
    return pl.pallas_call(
        _route_kernel,
        grid=(t // tt,),
        in_specs=[pl.BlockSpec((ne, tt), lambda i: (0, i)),
                  pl.BlockSpec((ne, 1), lambda i: (0, 0))],
        out_specs=[tok, tok, tok, pl.BlockSpec((ne, 1), lambda i: (0, 0))],
        out_shape=[jax.ShapeDtypeStruct((TOP_K, t), jnp.int32), jax.ShapeDtypeStruct((TOP_K, t), F32),
                   jax.ShapeDtypeStruct((TOP_K, t), jnp.int32), jax.ShapeDtypeStruct((ne, 1), F32)],
        scratch_shapes=[pltpu.VMEM((ne, 1), F32)],
        compiler_params=_cparams(("arbitrary",)),
        name="router_topk",
    )(logits_t, router_bias.reshape(ne, 1))


def _plan_kernel(cnt_ref, eidx_ref, rank_ref, dest_ref, seq_ref, dexp_ref, meta_ref):
    ne = N_EXPERTS
    tt = eidx_ref.shape[1]
    nblk = seq_ref.shape[1]
    cnt = cnt_ref[...].astype(jnp.int32)
    padded = ((cnt + (MOE_BLOCK - 1)) // MOE_BLOCK) * MOE_BLOCK
    r_i = lax.broadcasted_iota(jnp.int32, (ne, ne), 0)
    c_i = lax.broadcasted_iota(jnp.int32, (ne, ne), 1)
    incl = jnp.where(c_i <= r_i, 1.0, 0.0)
    pend = jnp.dot(incl, jnp.broadcast_to(padded.astype(F32), (ne, LANES)),
                   preferred_element_type=F32, precision=lax.Precision.HIGHEST)[:, 0:1]
    pend = pend.astype(jnp.int32)
    pstart = pend - padded
    ioe = lax.broadcasted_iota(jnp.int32, (ne, tt), 0)
    rows = []
    for k in range(TOP_K):
        sel = ioe == eidx_ref[k:k + 1, :]
        rows.append(jnp.sum(jnp.where(sel, pstart, 0), axis=0, keepdims=True))
    dest_ref[...] = jnp.concatenate(rows, axis=0) + rank_ref[...]
    blk0 = lax.broadcasted_iota(jnp.int32, (ne, nblk), 1) * MOE_BLOCK
    be = jnp.minimum(jnp.sum(jnp.where(pend <= blk0, 1, 0), axis=0, keepdims=True), ne - 1)
    present = cnt > 0
    strict = jnp.where(c_i < r_i, 1.0, 0.0).astype(BF16)
    sidx = _dot(strict, jnp.broadcast_to(jnp.where(present, 1.0, 0.0), (ne, LANES)).astype(BF16))[:, 0:1]
    sidx = sidx.astype(jnp.int32)
    dexp_ref[...] = jnp.sum(jnp.where(jnp.logical_and(present, sidx == c_i), r_i, 0), axis=0, keepdims=True)
    ioeb = lax.broadcasted_iota(jnp.int32, (ne, nblk), 0)
    nu = jnp.max(pend, axis=0, keepdims=True) // MOE_BLOCK
    nd = jnp.sum(jnp.where(present, 1, 0), axis=0, keepdims=True)
    seq_ref[...] = jnp.minimum(jnp.sum(jnp.where(ioeb == be, sidx, 0), axis=0, keepdims=True), nd - 1)
    lane = lax.broadcasted_iota(jnp.int32, (1, LANES), 1)
    meta_ref[...] = jnp.where(lane == 0, nu, jnp.where(lane == 1, nd, 0))


def _dispatch_plan(cnt, eidx, rank, tt):
    k, t = eidx.shape
    n_blocks = -(-(t * k) // MOE_BLOCK) + N_EXPERTS
    tok = pl.BlockSpec((k, tt), lambda i: (0, i))
    one = lambda n: pl.BlockSpec((1, n), lambda i: (0, 0))
    dest, seq, dexp, meta = pl.pallas_call(
        _plan_kernel,
        grid=(t // tt,),
        in_specs=[pl.BlockSpec(cnt.shape, lambda i: (0, 0)), tok, tok],
        out_specs=[tok, one(n_blocks), one(N_EXPERTS), one(LANES)],
        out_shape=[jax.ShapeDtypeStruct((k, t), jnp.int32), jax.ShapeDtypeStruct((1, n_blocks), jnp.int32),
                   jax.ShapeDtypeStruct((1, N_EXPERTS), jnp.int32), jax.ShapeDtypeStruct((1, LANES), jnp.int32)],
        compiler_params=_cparams(("arbitrary",)),
        name="dispatch_plan",
    )(cnt, eidx, rank)
    return dest, seq.reshape(n_blocks), dexp.reshape(N_EXPERTS), meta.reshape(LANES), n_blocks


SC_CORES = 2
SC_SUBCORES = 16
SC_LANES = 16
SC_CHUNK = 16384
SC_UNROLL = 4


def _row_tables(dest, n_rows):
    k, t = dest.shape
    a = k * t
    nw = SC_CORES * SC_SUBCORES
    per_w = n_rows // nw
    assert n_rows % (nw * SC_LANES) == 0 and t % SC_CHUNK == 0
    mesh = plsc.VectorSubcoreMesh(core_axis_name="c", subcore_axis_name="s")
    cp = pltpu.CompilerParams()
    if "needs_layout_passes" in pltpu.CompilerParams.__dataclass_fields__:
        cp = dataclasses.replace(cp, needs_layout_passes=False)

    def body(dest_hbm, tok_out, dbuf, tloc):
        wid = lax.axis_index("s") * SC_CORES + lax.axis_index("c")
        base = wid * per_w

        @pl.loop(0, per_w // SC_LANES)
        def _(i):
            tloc[pl.ds(i * SC_LANES, SC_LANES)] = jnp.zeros((SC_LANES,), jnp.int32)

        lane = lax.iota(jnp.int32, SC_LANES)

        @pl.loop(0, a // SC_CHUNK)
        def _(c):
            pltpu.sync_copy(dest_hbm.at[pl.ds(c * SC_CHUNK, SC_CHUNK)], dbuf)
            tok0 = lax.rem(c * SC_CHUNK, t)

            @pl.loop(0, SC_CHUNK // (SC_LANES * SC_UNROLL))
            def _(j):
                for u in range(SC_UNROLL):
                    off = (j * SC_UNROLL + u) * SC_LANES
                    loc = dbuf[pl.ds(off, SC_LANES)] - base
                    mine = jnp.logical_and(loc >= 0, loc < per_w)
                    loc = jnp.where(mine, loc, 0)
                    plsc.store_scatter(tloc, [loc], tok0 + off + lane, mask=mine)

        pltpu.sync_copy(tloc, tok_out.at[pl.ds(base, per_w)])

    fn = pl.kernel(
        body,
        out_type=jax.ShapeDtypeStruct((n_rows,), jnp.int32),
        mesh=mesh,
        scratch_types=[pltpu.VMEM((SC_CHUNK,), jnp.int32), pltpu.VMEM((per_w,), jnp.int32)],
        compiler_params=cp,
        name="row_tables",
    )
    return fn(dest.reshape(a))


ROW_TILE = (SUBLANES, LANES)


FFN_GROUP = 4
PAIR_ROWS = 2 * MOE_BLOCK
GATHER_BATCH = 32


def _gather_rows(h2p_ref, tok_ref, row, buf, base):
    for j0 in range(0, MOE_BLOCK, GATHER_BATCH):
        vals = [h2p_ref[tok_ref[row, j]] for j in range(j0, j0 + GATHER_BATCH)]
        for j, v in zip(range(j0, j0 + GATHER_BATCH), vals):
            buf[pl.ds(H2P_CHUNKS * (base + j), H2P_CHUNKS), :] = v


def _expert_rows(buf, row0, nrows, wset, out, out_row0):
    wg_c, wu_c, wd_c = wset
    parts = []
    for cidx in range(H2P_CHUNKS):
        word = buf[pl.ds(H2P_CHUNKS * row0 + cidx, nrows, stride=H2P_CHUNKS), :]
        parts.append(lax.bitcast_convert_type(word << 16, F32))
        parts.append(lax.bitcast_convert_type(word & jnp.int32(-65536), F32))
    x = jnp.concatenate(parts, axis=1)
    g = _dot(x, wg_c[...])
    u = _dot(x, wu_c[...])
    hm = (g * jax.nn.sigmoid(g)) * u
    out[pl.ds(out_row0, nrows), :] = _dot(hm, wd_c[...])


def _ffn_kernel(seq_ref, dexp_ref, meta_ref, tokc_ref, tokn_ref, h2p_ref, wg_hbm, wu_hbm, wd_hbm,
                y_ref, buf_0, buf_1, yraw_0, yraw_1, wg_a, wu_a, wd_a, wg_b, wu_b, wd_b, started_ref, sems):
    i = pl.program_id(0)
    n_used = meta_ref[0]
    n_exp = meta_ref[1]
    nblk = seq_ref.shape[0]
    wsets = ((wg_a, wu_a, wd_a), (wg_b, wu_b, wd_b))
    bufs = (buf_0, buf_1)
    yraws = (yraw_0, yraw_1)

    def weight_copies(s, par):
        e = dexp_ref[s]
        return [pltpu.make_async_copy(src.at[e], dst, sems.at[par, n])
                for n, (src, dst) in enumerate(zip((wg_hbm, wu_hbm, wd_hbm), wsets[par]))]

    def start_expert(s):
        for par in range(2):
            @pl.when(s % 2 == par)
            def _():
                for cp in weight_copies(s, par):
                    cp.start()

    @pl.when(i == 0)
    def _():
        _gather_rows(h2p_ref, tokc_ref, 0, buf_0, 0)
        _gather_rows(h2p_ref, tokc_ref, 1, buf_0, MOE_BLOCK)
        yraw_1[...] = jnp.zeros_like(yraw_1)
        start_expert(0)
        started_ref[0] = 0

    for p in range(FFN_GROUP // 2):
        b_a = i * FFN_GROUP + 2 * p
        used = b_a < n_used
        s_a = seq_ref[jnp.minimum(b_a, nblk - 1)]
        s_b = seq_ref[jnp.minimum(b_a + 1, nblk - 1)]
        first_a = jnp.logical_or(b_a == 0, s_a != seq_ref[jnp.clip(b_a - 1, 0, nblk - 1)])
        same = s_a == s_b
        cur, nxt = bufs[p], bufs[1 - p]
        out_rows = pl.ds(p * PAIR_ROWS, PAIR_ROWS)

        def emit_previous():
            y_ref[out_rows] = pltpu.einshape("r(cl)->rcl", yraws[1 - p][...], c=SUBLANES)

        def gather_next():
            if p == 0:
                _gather_rows(h2p_ref, tokc_ref, 2, nxt, 0)
                _gather_rows(h2p_ref, tokc_ref, 3, nxt, MOE_BLOCK)
            else:
                _gather_rows(h2p_ref, tokn_ref, 0, nxt, 0)
                _gather_rows(h2p_ref, tokn_ref, 1, nxt, MOE_BLOCK)

        @pl.when(used)
        def _():
            started = started_ref[0]

            @pl.when(s_a > started)
            def _():
                start_expert(s_a)
            started = jnp.maximum(started, s_a)

            @pl.when(s_b > started)
            def _():
                start_expert(s_b)
            started = jnp.maximum(started, s_b)
            ahead = jnp.logical_and(same, jnp.logical_and(s_b + 1 < n_exp, s_b + 1 > started))

            @pl.when(ahead)
            def _():
                start_expert(s_b + 1)
            started_ref[0] = jnp.where(ahead, s_b + 1, started)

        for par in range(2):
            @pl.when(jnp.logical_and(used, jnp.logical_and(same, s_a % 2 == par)))
            def _():
                @pl.when(first_a)
                def _():
                    for cp in weight_copies(s_a, par):
                        cp.wait()
                gather_next()
                emit_previous()
                _expert_rows(cur, 0, PAIR_ROWS, wsets[par], yraws[p], 0)

            @pl.when(jnp.logical_and(used, jnp.logical_and(jnp.logical_not(same), s_a % 2 == par)))
            def _():
                @pl.when(first_a)
                def _():
                    for cp in weight_copies(s_a, par):
                        cp.wait()
                for cp in weight_copies(s_b, 1 - par):
                    cp.wait()
                gather_next()
                emit_previous()
                _expert_rows(cur, 0, MOE_BLOCK, wsets[par], yraws[p], 0)
                _expert_rows(cur, MOE_BLOCK, MOE_BLOCK, wsets[1 - par], yraws[p], MOE_BLOCK)

        @pl.when(jnp.logical_not(used))
        def _():
            prev_used = jnp.logical_and(b_a >= 2, b_a - 2 < n_used)

            @pl.when(prev_used)
            def _():
                emit_previous()

            @pl.when(jnp.logical_not(prev_used))
            def _():
                y_ref[out_rows] = jnp.zeros((PAIR_ROWS,) + ROW_TILE, F32)


def _routed_experts(h2p, row_token, seq, dexp, meta, n_blocks, wg, wu, wd):
    d = wg.shape[1]
    de = wg.shape[2]
    ng = n_blocks // FFN_GROUP
    assert n_blocks % FFN_GROUP == 0 and FFN_GROUP == 4
    tok3 = row_token.reshape(ng, FFN_GROUP, MOE_BLOCK)
    idle_step = lambda m: ((m[0] + 1) // 2) // 2 + 1
    smem = lambda imap: pl.BlockSpec((None, FFN_GROUP, MOE_BLOCK), imap, memory_space=pltpu.SMEM)
    grid_spec = pltpu.PrefetchScalarGridSpec(
        num_scalar_prefetch=3,
        grid=(ng + 1,),
        in_specs=[
            smem(lambda i, sq, dx, m: (jnp.minimum(i, ng - 1), 0, 0)),
            smem(lambda i, sq, dx, m: (jnp.minimum(i + 1, ng - 1), 0, 0)),
            pl.BlockSpec(h2p.shape, lambda i, sq, dx, m: (0, 0, 0), pipeline_mode=pl.Buffered(1)),
            pl.BlockSpec(memory_space=pl.ANY),
            pl.BlockSpec(memory_space=pl.ANY),
            pl.BlockSpec(memory_space=pl.ANY),
        ],
        out_specs=pl.BlockSpec((FFN_GROUP * MOE_BLOCK,) + ROW_TILE,
                               lambda i, sq, dx, m: (jnp.minimum(i, idle_step(m)), 0, 0)),
        scratch_shapes=[pltpu.VMEM((PAIR_ROWS * H2P_CHUNKS, LANES), jnp.int32)] * 2 + [
            pltpu.VMEM((PAIR_ROWS, d), F32)] * 2 + [
            pltpu.VMEM((d, de), F32), pltpu.VMEM((d, de), F32), pltpu.VMEM((de, d), F32)] * 2 + [
            pltpu.SMEM((1,), jnp.int32), pltpu.SemaphoreType.DMA((2, 3))],
    )
    return pl.pallas_call(
        _ffn_kernel,
        grid_spec=grid_spec,
        out_shape=jax.ShapeDtypeStruct(((ng + 2) * FFN_GROUP * MOE_BLOCK,) + ROW_TILE, F32),
        compiler_params=_cparams(("arbitrary",)),
        name="routed_experts",
    )(seq, dexp, meta, tok3, tok3, h2p, wg, wu, wd)


COMBINE_TOKENS = 128


def _row_copies(src_hbm, idx_ref, buf, sem):
    return [pltpu.make_async_copy(src_hbm.at[idx_ref[k, j] + PAIR_ROWS], buf.at[k, j], sem)
            for k in range(TOP_K) for j in range(COMBINE_TOKENS)]


def _final_kernel(dc_ref, dn_ref, wk_ref, x1_ref, h2_ref, g2_ref, wsg_ref, wsu_ref, wsd_ref, lng_ref, lnb_ref,
                  y_hbm, o_ref, ybuf, sems):
    i = pl.program_id(0)
    nsteps = pl.num_programs(0)
    slot = i % 2

    def issue(d_ref, s):
        for n, cp in enumerate(_row_copies(y_hbm, d_ref, ybuf.at[s], sems.at[s])):
            cp.start(priority=n % 2)

    @pl.when(i == 0)
    def _():
        issue(dc_ref, 0)

    @pl.when(i + 1 < nsteps)
    def _():
        issue(dn_ref, 1 - slot)

    hb = h2_ref[...].astype(BF16)
    g = _dot(hb, wsg_ref[...])
    u = _dot(hb, wsu_ref[...])
    shared = _dot(((g * jax.nn.sigmoid(g)) * u).astype(BF16), wsd_ref[...])
    for cp in _row_copies(y_hbm, dc_ref, ybuf.at[slot], sems.at[slot]):
        cp.wait()
    rows = []
    for j in range(COMBINE_TOKENS):
        acc = ybuf[slot, 0, j] * wk_ref[0, j]
        for k in range(1, TOP_K):
            acc = acc + ybuf[slot, k, j] * wk_ref[k, j]
        rows.append(acc)
    routed = pltpu.einshape("tcl->t(cl)", jnp.stack(rows, axis=0))
    x2 = DN_ALPHA * x1_ref[...] + g2_ref[...] * (routed + shared)
    o_ref[...] = _ln_rows(x2) * lng_ref[...] + lnb_ref[...]


def _combine_final(x1, h2, mod4, dest, wk, y_rows, wsg, wsu, wsd, ln_g, ln_b, seq):
    t, d = x1.shape
    tt = COMBINE_TOKENS
    nsteps = t // tt
    spb = seq // tt
    row = pl.BlockSpec((tt, d), lambda i: (i, 0))
    full = lambda a: pl.BlockSpec(a.shape, lambda i: (0,) * a.ndim)
    ln_g2, ln_b2 = ln_g.reshape(1, d), ln_b.reshape(1, d)
    return pl.pallas_call(
        _final_kernel,
        grid=(nsteps,),
        in_specs=[pl.BlockSpec((TOP_K, tt), lambda i: (0, i), memory_space=pltpu.SMEM),
                  pl.BlockSpec((TOP_K, tt), lambda i: (0, jnp.minimum(i + 1, nsteps - 1)),
                               memory_space=pltpu.SMEM),
                  pl.BlockSpec((TOP_K, tt), lambda i: (0, i), memory_space=pltpu.SMEM),
                  row, row,
                  pl.BlockSpec((None, None, 1, d), lambda i: (i // spb, 5, 0, 0)),
                  full(wsg), full(wsu), full(wsd), full(ln_g2), full(ln_b2),
                  pl.BlockSpec(memory_space=pl.ANY)],
        out_specs=row,
        out_shape=jax.ShapeDtypeStruct((t, d), F32),
        scratch_shapes=[pltpu.VMEM((2, TOP_K, tt) + ROW_TILE, F32),
                        pltpu.SemaphoreType.DMA((2,))],
        compiler_params=_cparams(("arbitrary",)),
        name="shared_combine_ln",
    )(dest, dest, wk, x1, h2, mod4, wsg, wsu, wsd, ln_g2, ln_b2, y_rows)


def kernel(x, c, w_ada, b_ada, w_in, hg_lower_bound, hg_norm_w, rel_bias, w_out, ln1_g, ln1_b, w_router,
           router_bias, w_e_gate, w_e_up, w_e_down, w_sh_gate, w_sh_up, w_sh_down, ln2_g, ln2_b):
    bsz, seq, d = x.shape
    t = bsz * seq
    assert w_ada.shape[0] == DEPTH and seq % (ATT_BRANCHES[-1][0]) == 0
    x2 = x.reshape(t, d)
    bias = _bias_tables(rel_bias)
    for l in range(DEPTH):
        mod4 = _modulation(c, w_ada[l], b_ada[l]).reshape(bsz, 6, 1, d)
        w_in_bf = _cast_bf16(w_in[l], 256)
        hq, hf, hi, hg, *qkv = _in_projection(x2, mod4, w_in_bf, seq, 512)
        nbr = len(ATT_BRANCHES)
        y_hg = _hgrn2(hq, hf, hi, hg, hg_lower_bound, hg_norm_w[l], bsz, seq)
        os_, ls_ = [], []
        for g, (_, dil) in enumerate(ATT_BRANCHES):
            o, lse = _dilated_attention(qkv[g], qkv[nbr + g], qkv[2 * nbr + g], bias[g], bsz, seq, dil)
            os_.append(o)
            ls_.append(lse)
        w_out_bf = _cast_bf16(w_out[l], 256)
        x1, h2, h2p, logits_t = _out_projection(y_hg, os_, ls_, x2, mod4, w_out_bf, ln1_g[l], ln1_b[l],
                                                w_router[l].T, seq, 512)
        eidx, wk, rank, cnt = _route(logits_t, router_bias[l], 256)
        dest, blk_seq, dexp, meta, n_blocks = _dispatch_plan(cnt, eidx, rank, 512)
        row_token = _row_tables(dest, n_blocks * MOE_BLOCK)
        y_rows = _routed_experts(h2p, row_token, blk_seq, dexp, meta, n_blocks,
                                 w_e_gate[l], w_e_up[l], w_e_down[l])
        x2 = _combine_final(x1, h2, mod4, dest, wk, y_rows,
                            _cast_bf16(w_sh_gate[l], 256), _cast_bf16(w_sh_up[l], 256),
                            _cast_bf16(w_sh_down[l], 256), ln2_g[l], ln2_b[l], seq)
    return x2.reshape(bsz, seq, d)
```

```python
import dataclasses
import math

import jax
import jax.numpy as jnp
import numpy as np
from jax import lax
from jax.experimental import pallas as pl
from jax.experimental.pallas import tpu as pltpu
from jax.experimental.pallas import tpu_sc as plsc

HG_HEADS = 4
HG_DK = 128
HG_WIDTH = HG_HEADS * HG_DK
ATT_BRANCHES = ((128, 1), (512, 4), (2048, 16))
ATT_HEADS_PER_BRANCH = 4
ATT_HEAD_DIM = 64
ATT_BW = ATT_HEADS_PER_BRANCH * ATT_HEAD_DIM
ATT_BLOCK = 128
REL_BUCKETS = 32
REL_MAX_DIST = 2048
N_EXPERTS = 256
TOP_K = 8
N_GROUPS = 8
TOPK_GROUPS = 4
ROUTED_SCALE = 2.5
MOE_BLOCK = 128
DEPTH = 1
DN_ALPHA = (2 * DEPTH) ** 0.25
LN_EPS = 1e-5
RMS_EPS = 1e-6

LANES = 128
SUBLANES = 8
VMEM_LIMIT_BYTES = 56 * 1024 * 1024

F32 = jnp.float32
BF16 = jnp.bfloat16
NEG_INF = float("-inf")


def _cparams(sem):
    return pltpu.CompilerParams(dimension_semantics=sem, vmem_limit_bytes=VMEM_LIMIT_BYTES)


def _ln_rows(x):
    mu = jnp.mean(x, axis=-1, keepdims=True)
    xc = x - mu
    var = jnp.mean(xc * xc, axis=-1, keepdims=True)
    return xc * lax.rsqrt(var + LN_EPS)


def _dot(a, b):
    return jnp.dot(a, b, preferred_element_type=F32)


def _dot_nt(a, b):
    return lax.dot_general(a, b, (((1,), (1,)), ((), ())), preferred_element_type=F32)


def _dot_tn(a, b):
    return lax.dot_general(a, b, (((0,), (0,)), ((), ())), preferred_element_type=F32)


def _cast_kernel(w_ref, o_ref):
    o_ref[...] = w_ref[...].astype(o_ref.dtype)


def _cast_bf16(w, rows_per_step):
    r, c = w.shape
    return pl.pallas_call(
        _cast_kernel,
        grid=(r // rows_per_step,),
        in_specs=[pl.BlockSpec((rows_per_step, c), lambda i: (i, 0))],
        out_specs=pl.BlockSpec((rows_per_step, c), lambda i: (i, 0)),
        out_shape=jax.ShapeDtypeStruct((r, c), BF16),
        compiler_params=_cparams(("parallel",)),
        name="cast_bf16",
    )(w)


def _mod_kernel(c_ref, w_ref, b_ref, o_ref):
    c = c_ref[...]
    cond = c * jax.nn.sigmoid(c)
    o_ref[...] = jnp.dot(cond, w_ref[...], preferred_element_type=F32,
                         precision=lax.Precision.HIGHEST) + b_ref[...]


def _modulation(c, w_ada, b_ada):
    bsz, d = c.shape
    n = w_ada.shape[1]
    rows = -(-bsz // SUBLANES) * SUBLANES
    cpad = jnp.zeros((rows, d), F32).at[:bsz].set(c)
    tn = 1024
    out = pl.pallas_call(
        _mod_kernel,
        grid=(n // tn,),
        in_specs=[pl.BlockSpec((rows, d), lambda j: (0, 0)),
                  pl.BlockSpec((d, tn), lambda j: (0, j)),
                  pl.BlockSpec((1, tn), lambda j: (0, j))],
        out_specs=pl.BlockSpec((rows, tn), lambda j: (0, j)),
        out_shape=jax.ShapeDtypeStruct((rows, n), F32),
        compiler_params=_cparams(("parallel",)),
        name="adaln_modulation",
    )(cpad, w_ada, b_ada.reshape(1, n))
    return out[:bsz]


_IN_HG = 4
_IN_ATT = 3 * len(ATT_BRANCHES)


def _inproj_kernel(x_ref, sc_ref, sh_ref, w_ref, *refs):
    outs = refs[:_IN_HG + _IN_ATT]
    scratch = refs[_IN_HG + _IN_ATT:]
    x = x_ref[...]
    h = _ln_rows(x) * (1.0 + sc_ref[...]) + sh_ref[...]
    hb = h.astype(BF16)
    tm = x.shape[0]
    col = 0
    n_scr = 0
    for k, o_ref in enumerate(outs):
        width = HG_WIDTH if k < _IN_HG else ATT_BW
        y = _dot(hb, w_ref[:, col:col + width])
        col += width
        if k < _IN_HG:
            o_ref[...] = y.astype(o_ref.dtype)
            continue
        if k < _IN_HG + len(ATT_BRANCHES):
            y = y * (ATT_HEAD_DIM ** -0.5)
        dil = ATT_BRANCHES[(k - _IN_HG) % len(ATT_BRANCHES)][1]
        if dil == 1:
            o_ref[...] = y.astype(o_ref.dtype)
            continue
        scr = scratch[n_scr]
        n_scr += 1
        for half in range(ATT_BW // LANES):
            scr[half] = y[:, half * LANES:(half + 1) * LANES]
        for r in range(dil):
            for half in range(ATT_BW // LANES):
                c0 = r * ATT_BW + half * LANES
                o_ref[:, c0:c0 + LANES] = scr[half, pl.ds(r, tm // dil, stride=dil), :].astype(o_ref.dtype)


def _in_projection(x2, mod4, w_in_bf, seq, tm):
    t, d = x2.shape
    steps_per_batch = seq // tm
    dils = [dil for _, dil in ATT_BRANCHES] * 3
    shapes = [(t, HG_WIDTH)] * _IN_HG + [(t // dil, dil * ATT_BW) for dil in dils]
    blocks = [(tm, HG_WIDTH)] * _IN_HG + [(tm // dil, dil * ATT_BW) for dil in dils]
    dtypes = [BF16, F32, BF16, BF16] + [BF16] * _IN_ATT
    mod_spec = lambda row: pl.BlockSpec((None, None, 1, d),
                                        lambda i, row=row: (i // steps_per_batch, row, 0, 0))
    outs = pl.pallas_call(
        _inproj_kernel,
        grid=(t // tm,),
        in_specs=[pl.BlockSpec((tm, d), lambda i: (i, 0)),
                  mod_spec(1), mod_spec(0),
                  pl.BlockSpec(w_in_bf.shape, lambda i: (0, 0))],
        out_specs=[pl.BlockSpec(b, lambda i: (i, 0)) for b in blocks],
        out_shape=[jax.ShapeDtypeStruct(s, dt) for s, dt in zip(shapes, dtypes)],
        scratch_shapes=[pltpu.VMEM((ATT_BW // LANES, tm, LANES), F32) for dil in dils if dil > 1],
        compiler_params=_cparams(("parallel",)),
        name="ln_in_projection",
    )(x2, mod4, mod4, w_in_bf)
    return outs


HG_CHUNK = 64
HG_CHUNKS_PER_STEP = 4
HG_SUB = 8
HG_LEVELS = (64, 32, 16)


def _hgrn_chunk(q, z, iv, lb):
    c = HG_CHUNK
    f = lb + (1.0 - lb) * jax.nn.sigmoid(z)
    lf = jnp.log(f)
    kk = (1.0 - lb) * jax.nn.sigmoid(-z)
    r_i = lax.broadcasted_iota(jnp.int32, (c, c), 0)
    c_i = lax.broadcasted_iota(jnp.int32, (c, c), 1)
    tril = (c_i <= r_i).astype(F32)
    b = jnp.dot(tril, lf, preferred_element_type=F32, precision=lax.Precision.HIGHEST)

    row = lax.broadcasted_iota(jnp.int32, (c, HG_DK), 0)
    scores = jnp.zeros((c, c), F32)
    for m in HG_LEVELS:
        nb = c // m
        b3 = b.reshape(nb, m, HG_DK)
        piv = jnp.broadcast_to(b3[:, m // 2 - 1:m // 2, :], (nb, m, HG_DK)).reshape(c, HG_DK)
        second = (row % m) >= (m // 2)
        qt = jnp.where(second, q * jnp.exp(b - piv), 0.0)
        kt = jnp.where(second, 0.0, kk * jnp.exp(piv - b))
        s_m = _dot_nt(qt.astype(BF16), kt.astype(BF16))
        if nb > 1:
            s_m = jnp.where((r_i // m) == (c_i // m), s_m, 0.0)
        scores = scores + s_m
    sub = HG_SUB
    t_i = lax.broadcasted_iota(jnp.int32, (sub, 1), 0)
    lane = lax.broadcasted_iota(jnp.int32, (sub, c), 1)
    diag_rows = []
    for j in range(c // sub):
        qb = q[j * sub:(j + 1) * sub]
        kb = kk[j * sub:(j + 1) * sub]
        bb = b[j * sub:(j + 1) * sub]
        a_j = jnp.zeros((sub, c), F32)
        for s in range(sub):
            w = qb * kb[s:s + 1] * jnp.exp(bb - bb[s:s + 1])
            col = jnp.sum(w, axis=-1, keepdims=True)
            col = jnp.where(t_i >= s, col, 0.0)
            a_j = jnp.where(lane == j * sub + s, col, a_j)
        diag_rows.append(a_j)
    scores = scores + jnp.concatenate(diag_rows, axis=0)

    ivb = iv.astype(BF16)
    intra = _dot(scores.astype(BF16), ivb)
    b_last = b[c - 1:c]
    kdec = (kk * jnp.exp(b_last - b)).astype(BF16)
    return (q * jnp.exp(b)).astype(BF16), intra, jnp.exp(b_last), _dot_tn(ivb, kdec)


def _hgrn_kernel(q_ref, f_ref, i_ref, g_ref, lbp_ref, nw_ref, o_ref, st_ref):
    @pl.when(pl.program_id(1) == 0)
    def _():
        st_ref[...] = jnp.zeros_like(st_ref)

    lbp = lbp_ref[...]
    e = jnp.exp(lbp - jnp.max(lbp, axis=0, keepdims=True))
    lb_all = e[0:1] / jnp.sum(e, axis=0, keepdims=True)
    heads = []
    for h in range(HG_HEADS):
        sl = slice(h * HG_DK, (h + 1) * HG_DK)
        st_t = st_ref[h]
        outs = []
        for n in range(HG_CHUNKS_PER_STEP):
            rows = slice(n * HG_CHUNK, (n + 1) * HG_CHUNK)
            qdec, intra, dec_last, kv = _hgrn_chunk(q_ref[rows, sl].astype(F32), f_ref[rows, sl],
                                                    i_ref[rows, sl].astype(F32), lb_all[:, sl])
            o = _dot_nt(qdec, st_t.astype(BF16)) + intra
            st_t = st_t * dec_last + kv
            outs.append(o * lax.rsqrt(jnp.mean(o * o, axis=-1, keepdims=True) + RMS_EPS))
        st_ref[h] = st_t
        heads.append(jnp.concatenate(outs, axis=0))
    o_all = jnp.concatenate(heads, axis=-1)
    g = g_ref[...].astype(F32)
    o_ref[...] = (o_all * nw_ref[...] * (g * jax.nn.sigmoid(g))).astype(o_ref.dtype)


def _hgrn2(hq, hf, hi, hg, lb_param, norm_w, bsz, seq):
    t = hq.shape[0]
    rows = HG_CHUNK * HG_CHUNKS_PER_STEP
    nc = seq // rows
    tok = lambda b, n: (b * nc + n, 0)
    spec = pl.BlockSpec((rows, HG_WIDTH), tok)
    return pl.pallas_call(
        _hgrn_kernel,
        grid=(bsz, nc),
        in_specs=[spec, spec, spec, spec,
                  pl.BlockSpec(lb_param.shape, lambda b, n: (0, 0)),
                  pl.BlockSpec((1, HG_WIDTH), lambda b, n: (0, 0))],
        out_specs=spec,
        out_shape=jax.ShapeDtypeStruct((t, HG_WIDTH), BF16),
        scratch_shapes=[pltpu.VMEM((HG_HEADS, HG_DK, HG_DK), F32)],
        compiler_params=_cparams(("parallel", "arbitrary")),
        name="hgrn2_scan",
    )(hq, hf, hi, hg, lb_param, norm_w.reshape(1, HG_WIDTH))


def _t5_bucket_np(dist):
    max_exact = REL_BUCKETS // 2
    n = np.maximum(dist, 0)
    nf = np.maximum(n, 1).astype(np.float32)
    large = max_exact + (np.log(nf / np.float32(max_exact)) / np.float32(math.log(REL_MAX_DIST / max_exact))
                         * np.float32(REL_BUCKETS - max_exact)).astype(np.int32)
    large = np.minimum(large, REL_BUCKETS - 1)
    return np.where(n < max_exact, n, large).astype(np.int32)


def _band_tables():
    w = ATT_BLOCK
    qi = np.arange(w)[:, None]
    ki = np.arange(2 * w)[None, :]
    m = w + qi - ki
    band = (m >= 0) & (m <= w)
    buckets = np.stack([_t5_bucket_np(m * dil) for _, dil in ATT_BRANCHES])
    return buckets, band


def _bias_kernel(rb_ref, bucket_ref, o_ref):
    g = pl.program_id(0)
    w = ATT_BLOCK
    bucket = bucket_ref[...]
    qi = lax.broadcasted_iota(jnp.int32, (w, 2 * w), 0)
    ki = lax.broadcasted_iota(jnp.int32, (w, 2 * w), 1)
    m = w + qi - ki
    band = (m >= 0) & (m <= w)
    for h in range(ATT_HEADS_PER_BRANCH):
        acc = jnp.zeros((w, 2 * w), F32)
        for c in range(REL_BUCKETS):
            acc = jnp.where(bucket == c, rb_ref[c, g * ATT_HEADS_PER_BRANCH + h], acc)
        full = jnp.where(band, acc, NEG_INF)
        o_ref[1, h] = full
        o_ref[0, h] = jnp.where(ki >= w, full, NEG_INF)


def _bias_tables(rel_bias):
    buckets, _ = _band_tables()
    g = len(ATT_BRANCHES)
    w = ATT_BLOCK
    return pl.pallas_call(
        _bias_kernel,
        grid=(g,),
        in_specs=[pl.BlockSpec(memory_space=pltpu.SMEM),
                  pl.BlockSpec((None, w, 2 * w), lambda i: (i, 0, 0))],
        out_specs=pl.BlockSpec((None, 2, ATT_HEADS_PER_BRANCH, w, 2 * w), lambda i: (i, 0, 0, 0, 0)),
        out_shape=jax.ShapeDtypeStruct((g, 2, ATT_HEADS_PER_BRANCH, w, 2 * w), F32),
        compiler_params=_cparams(("parallel",)),
        name="rel_bias_tables",
    )(rel_bias, jnp.asarray(buckets))


ATT_BLOCKS_PER_STEP = 4


def _attn_kernel(q_ref, kp_ref, kc_ref, vp_ref, vc_ref, bias_ref, o_ref, lse_ref):
    m = pl.program_id(2)
    w = ATT_BLOCK
    hb = ATT_HEADS_PER_BRANCH
    lane = lax.broadcasted_iota(jnp.int32, (w, ATT_BW), 1) // ATT_HEAD_DIM
    kall = jnp.concatenate([kp_ref[...], kc_ref[...]], axis=0)
    vall = jnp.concatenate([vp_ref[...], vc_ref[...]], axis=0)
    for blk in range(q_ref.shape[0] // w):
        q = q_ref[blk * w:(blk + 1) * w]
        q4 = jnp.concatenate([jnp.where(lane == h, q, jnp.zeros_like(q)) for h in range(hb)], axis=0)
        kk = kall[blk * w:(blk + 2) * w]
        vv = vall[blk * w:(blk + 2) * w]
        s4 = _dot_nt(q4, kk)
        bias = bias_ref[jnp.minimum(m, 1)] if blk == 0 else bias_ref[1]
        s4 = s4 + bias.reshape(hb * w, 2 * w)
        mx = jnp.max(s4, axis=-1, keepdims=True)
        p = jnp.exp(s4 - mx)
        l = jnp.sum(p, axis=-1, keepdims=True)
        o4 = _dot((p / l).astype(vv.dtype), vv)
        lse4 = mx + jnp.log(l)
        o = jnp.zeros((w, ATT_BW), F32)
        lse = jnp.zeros((w, ATT_BW), F32)
        for h in range(hb):
            o = jnp.where(lane == h, o4[h * w:(h + 1) * w], o)
            lse = jnp.where(lane == h, lse4[h * w:(h + 1) * w], lse)
        o_ref[blk * w:(blk + 1) * w] = o.astype(o_ref.dtype)
        lse_ref[blk * w:(blk + 1) * w] = lse


def _dilated_attention(q, k, v, bias_g, bsz, seq, dilation):
    w = ATT_BLOCK
    l = seq // dilation
    pstep = math.gcd(ATT_BLOCKS_PER_STEP, l // w)
    nb = l // (w * pstep)
    view = lambda a: a.reshape(bsz, l, dilation * ATT_BW)
    cur = pl.BlockSpec((None, pstep * w, ATT_BW), lambda b, r, n: (b, n, r))
    prev = pl.BlockSpec((None, w, ATT_BW), lambda b, r, n: (b, jnp.maximum(pstep * n - 1, 0), r))
    o, lse = pl.pallas_call(
        _attn_kernel,
        grid=(bsz, dilation, nb),
        in_specs=[cur, prev, cur, prev, cur,
                  pl.BlockSpec(bias_g.shape, lambda b, r, n: (0, 0, 0, 0))],
        out_specs=[cur, cur],
        out_shape=[jax.ShapeDtypeStruct((bsz, l, dilation * ATT_BW), BF16),
                   jax.ShapeDtypeStruct((bsz, l, dilation * ATT_BW), F32)],
        compiler_params=_cparams(("parallel", "parallel", "arbitrary")),
        name=f"dilated_attention_d{dilation}",
    )(view(q), view(k), view(k), view(v), view(v), bias_g)
    return o.reshape(bsz * l, dilation * ATT_BW), lse.reshape(bsz * l, dilation * ATT_BW)


def _split_bf16(a):
    hi = a.astype(BF16)
    lo = (a - hi.astype(F32)).astype(BF16)
    return hi, lo


H2P_CHUNKS = 4


def _bf16_bits(a):
    u = lax.bitcast_convert_type(a, jnp.uint32)
    return u + jnp.uint32(0x7FFF) + ((u >> 16) & jnp.uint32(1))


def _token_order(ref, scr, dil):
    if dil == 1:
        return ref[...].astype(F32)
    n = ref.shape[0]
    halves = ATT_BW // LANES
    for r in range(dil):
        for half in range(halves):
            c0 = r * ATT_BW + half * LANES
            scr[half, pl.ds(r, n, stride=dil), :] = ref[:, c0:c0 + LANES].astype(F32)
    return jnp.concatenate([scr[half] for half in range(halves)], axis=1)


def _outproj_kernel(yhg_ref, o1_ref, o2_ref, o3_ref, l1_ref, l2_ref, l3_ref, x_ref,
                    g1_ref, sc2_ref, sh2_ref, wout_ref, lng_ref, lnb_ref, wrt_ref,
                    x1_ref, h2_ref, h2p_ref, lgt_ref, *scratch):
    dils = [dil for _, dil in ATT_BRANCHES]
    scr = iter(scratch)
    o1, o2, o3 = [_token_order(r, None if dil == 1 else next(scr), dil)
                  for r, dil in zip((o1_ref, o2_ref, o3_ref), dils)]
    l1, l2, l3 = [_token_order(r, None if dil == 1 else next(scr), dil)
                  for r, dil in zip((l1_ref, l2_ref, l3_ref), dils)]
    mx = jnp.maximum(jnp.maximum(l1, l2), l3)
    e1, e2, e3 = jnp.exp(l1 - mx), jnp.exp(l2 - mx), jnp.exp(l3 - mx)
    den = e1 + e2 + e3
    att = (e1 / den) * o1 + (e2 / den) * o2 + (e3 / den) * o3
    mix = _dot(yhg_ref[...], wout_ref[:HG_WIDTH, :]) + _dot(att.astype(BF16), wout_ref[HG_WIDTH:, :])
    x1 = _ln_rows(DN_ALPHA * x_ref[...] + g1_ref[...] * mix) * lng_ref[...] + lnb_ref[...]
    x1_ref[...] = x1
    h2 = _ln_rows(x1) * (1.0 + sc2_ref[...]) + sh2_ref[...]
    h2_ref[...] = h2
    for cidx in range(H2P_CHUNKS):
        lo = _bf16_bits(h2[:, 2 * LANES * cidx:2 * LANES * cidx + LANES])
        hi = _bf16_bits(h2[:, 2 * LANES * cidx + LANES:2 * LANES * (cidx + 1)])
        word = (lo >> 16) | (hi & jnp.uint32(0xFFFF0000))
        h2p_ref[:, cidx, :] = lax.bitcast_convert_type(word, jnp.int32)
    h_hi, h_lo = _split_bf16(h2)
    w_hi, w_lo = _split_bf16(wrt_ref[...])
    lgt_ref[...] = _dot_nt(w_hi, h_hi) + (_dot_nt(w_hi, h_lo) + _dot_nt(w_lo, h_hi))


def _out_projection(yhg, os_, ls_, x2, mod4, w_out_bf, ln_g, ln_b, w_router_t, seq, tm):
    t, d = x2.shape
    spb = seq // tm
    ne = w_router_t.shape[0]
    row = lambda w: pl.BlockSpec((tm, w), lambda i: (i, 0))
    mod_spec = lambda r: pl.BlockSpec((None, None, 1, d), lambda i, r=r: (i // spb, r, 0, 0))
    full = lambda a: pl.BlockSpec(a.shape, lambda i: (0,) * a.ndim)
    ln_g2, ln_b2 = ln_g.reshape(1, d), ln_b.reshape(1, d)
    dils = [dil for _, dil in ATT_BRANCHES]
    branch = [pl.BlockSpec((tm // dil, dil * ATT_BW), lambda i: (i, 0)) for dil in dils]
    return pl.pallas_call(
        _outproj_kernel,
        grid=(t // tm,),
        in_specs=[row(HG_WIDTH)] + branch + branch + [row(d),
                  mod_spec(2), mod_spec(4), mod_spec(3),
                  full(w_out_bf), full(ln_g2), full(ln_b2), full(w_router_t)],
        out_specs=[row(d), row(d), pl.BlockSpec((tm, H2P_CHUNKS, LANES), lambda i: (i, 0, 0)),
                   pl.BlockSpec((ne, tm), lambda i: (0, i))],
        out_shape=[jax.ShapeDtypeStruct((t, d), F32), jax.ShapeDtypeStruct((t, d), F32),
                   jax.ShapeDtypeStruct((t, H2P_CHUNKS, LANES), jnp.int32),
                   jax.ShapeDtypeStruct((ne, t), F32)],
        scratch_shapes=[pltpu.VMEM((ATT_BW // LANES, tm, LANES), F32) for dil in dils + dils if dil > 1],
        compiler_params=_cparams(("parallel",)),
        name="merge_outproj_ln",
    )(yhg, *os_, *ls_, x2, mod4, mod4, mod4, w_out_bf, ln_g2, ln_b2, w_router_t)


def _argmax_rows(cur, iota, nrows):
    m = jnp.max(cur, axis=0, keepdims=True)
    idx = jnp.min(jnp.where(cur == m, iota, nrows), axis=0, keepdims=True)
    return m, idx, iota == idx


def _route_kernel(lgt_ref, rb_ref, eidx_ref, w_ref, rank_ref, cnt_ref, carry):
    ne = N_EXPERTS
    gsz = ne // N_GROUPS
    tt = lgt_ref.shape[1]

    @pl.when(pl.program_id(0) == 0)
    def _():
        carry[...] = jnp.zeros_like(carry)

    sc = jax.nn.sigmoid(lgt_ref[...])
    biased = sc + rb_ref[...]
    g3 = biased.reshape(N_GROUPS, gsz, tt)
    io3 = lax.broadcasted_iota(jnp.int32, (N_GROUPS, gsz, tt), 1)
    m1 = jnp.max(g3, axis=1, keepdims=True)
    first = jnp.min(jnp.where(g3 == m1, io3, gsz), axis=1, keepdims=True)
    m2 = jnp.max(jnp.where(io3 == first, NEG_INF, g3), axis=1, keepdims=True)
    gs = (m1 + m2).reshape(N_GROUPS, tt)
    io8 = lax.broadcasted_iota(jnp.int32, (N_GROUPS, tt), 0)
    sel = jnp.zeros((N_GROUPS, tt), jnp.int32)
    cur = gs
    for _ in range(TOPK_GROUPS):
        _, _, pick = _argmax_rows(cur, io8, N_GROUPS)
        sel = jnp.where(pick, 1, sel)
        cur = jnp.where(pick, NEG_INF, cur)
    masked = jnp.where(sel.reshape(N_GROUPS, 1, tt) > 0, g3, NEG_INF).reshape(ne, tt)
    ioe = lax.broadcasted_iota(jnp.int32, (ne, tt), 0)
    cur = masked
    idxs, ws, picks = [], [], []
    for _ in range(TOP_K):
        _, idx, pick = _argmax_rows(cur, ioe, ne)
        idxs.append(idx)
        picks.append(pick)
        ws.append(jnp.sum(jnp.where(pick, sc, 0.0), axis=0, keepdims=True))
        cur = jnp.where(pick, NEG_INF, cur)
    wk = jnp.concatenate(ws, axis=0)
    eidx_ref[...] = jnp.concatenate(idxs, axis=0)
    w_ref[...] = wk / jnp.sum(wk, axis=0, keepdims=True) * ROUTED_SCALE
    chosen = jnp.where(cur == NEG_INF, jnp.where(masked == NEG_INF, 0.0, 1.0), 0.0)
    r_i = lax.broadcasted_iota(jnp.int32, (tt, tt), 0)
    c_i = lax.broadcasted_iota(jnp.int32, (tt, tt), 1)
    before = jnp.where(r_i < c_i, 1.0, 0.0).astype(BF16)
    pref = _dot(chosen.astype(BF16), before) + carry[...]
    rank_ref[...] = jnp.concatenate(
        [jnp.sum(jnp.where(p, pref, 0.0), axis=0, keepdims=True) for p in picks], axis=0).astype(jnp.int32)
    carry[...] = carry[...] + jnp.sum(chosen, axis=1, keepdims=True)
    cnt_ref[...] = carry[...]


def _route(logits_t, router_bias, tt):
    ne, t = logits_t.shape
    tok = pl.BlockSpec((TOP_K, tt), lambda i: (0, i))
    return pl.pallas_call(
        _route_kernel,
        grid=(t // tt,),
        in_specs=[pl.BlockSpec((ne, tt), lambda i: (0, i)),
                  pl.BlockSpec((ne, 1), lambda i: (0, 0))],
        out_specs=[tok, tok, tok, pl.BlockSpec((ne, 1), lambda i: (0, 0))],
        out_shape=[jax.ShapeDtypeStruct((TOP_K, t), jnp.int32), jax.ShapeDtypeStruct((TOP_K, t), F32),
                   jax.ShapeDtypeStruct((TOP_K, t), jnp.int32), jax.ShapeDtypeStruct((ne, 1), F32)],
        scratch_shapes=[pltpu.VMEM((ne, 1), F32)],
        compiler_params=_cparams(("arbitrary",)),
        name="router_topk",
    )(logits_t, router_bias.reshape(ne, 1))


def _plan_kernel(cnt_ref, eidx_ref, rank_ref, dest_ref, seq_ref, dexp_ref, meta_ref):
    ne = N_EXPERTS
    tt = eidx_ref.shape[1]
    nblk = seq_ref.shape[1]
    cnt = cnt_ref[...].astype(jnp.int32)
    padded = ((cnt + (MOE_BLOCK - 1)) // MOE_BLOCK) * MOE_BLOCK
    r_i = lax.broadcasted_iota(jnp.int32, (ne, ne), 0)
    c_i = lax.broadcasted_iota(jnp.int32, (ne, ne), 1)
    incl = jnp.where(c_i <= r_i, 1.0, 0.0)
    pend = jnp.dot(incl, jnp.broadcast_to(padded.astype(F32), (ne, LANES)),
                   preferred_element_type=F32, precision=lax.Precision.HIGHEST)[:, 0:1]
    pend = pend.astype(jnp.int32)
    pstart = pend - padded
    ioe = lax.broadcasted_iota(jnp.int32, (ne, tt), 0)
    rows = []
    for k in range(TOP_K):
        sel = ioe == eidx_ref[k:k + 1, :]
        rows.append(jnp.sum(jnp.where(sel, pstart, 0), axis=0, keepdims=True))
    dest_ref[...] = jnp.concatenate(rows, axis=0) + rank_ref[...] + PAIR_ROWS
    blk0 = lax.broadcasted_iota(jnp.int32, (ne, nblk), 1) * MOE_BLOCK
    be = jnp.minimum(jnp.sum(jnp.where(pend <= blk0, 1, 0), axis=0, keepdims=True), ne - 1)
    present = cnt > 0
    strict = jnp.where(c_i < r_i, 1.0, 0.0).astype(BF16)
    sidx = _dot(strict, jnp.broadcast_to(jnp.where(present, 1.0, 0.0), (ne, LANES)).astype(BF16))[:, 0:1]
    sidx = sidx.astype(jnp.int32)
    dexp_ref[...] = jnp.sum(jnp.where(jnp.logical_and(present, sidx == c_i), r_i, 0), axis=0, keepdims=True)
    ioeb = lax.broadcasted_iota(jnp.int32, (ne, nblk), 0)
    nu = jnp.max(pend, axis=0, keepdims=True) // MOE_BLOCK
    nd = jnp.sum(jnp.where(present, 1, 0), axis=0, keepdims=True)
    seq_ref[...] = jnp.minimum(jnp.sum(jnp.where(ioeb == be, sidx, 0), axis=0, keepdims=True), nd - 1)
    lane = lax.broadcasted_iota(jnp.int32, (1, LANES), 1)
    meta_ref[...] = jnp.where(lane == 0, nu, jnp.where(lane == 1, nd, 0))


def _dispatch_plan(cnt, eidx, rank, tt):
    k, t = eidx.shape
    n_blocks = -(-(t * k) // MOE_BLOCK) + N_EXPERTS
    tok = pl.BlockSpec((k, tt), lambda i: (0, i))
    one = lambda n: pl.BlockSpec((1, n), lambda i: (0, 0))
    dest, seq, dexp, meta = pl.pallas_call(
        _plan_kernel,
        grid=(t // tt,),
        in_specs=[pl.BlockSpec(cnt.shape, lambda i: (0, 0)), tok, tok],
        out_specs=[tok, one(n_blocks), one(N_EXPERTS), one(LANES)],
        out_shape=[jax.ShapeDtypeStruct((k, t), jnp.int32), jax.ShapeDtypeStruct((1, n_blocks), jnp.int32),
                   jax.ShapeDtypeStruct((1, N_EXPERTS), jnp.int32), jax.ShapeDtypeStruct((1, LANES), jnp.int32)],
        compiler_params=_cparams(("arbitrary",)),
        name="dispatch_plan",
    )(cnt, eidx, rank)
    return dest, seq.reshape(n_blocks), dexp.reshape(N_EXPERTS), meta.reshape(LANES), n_blocks


SC_CORES = 2
SC_SUBCORES = 16
SC_LANES = 16
SC_CHUNK = 16384
SC_UNROLL = 4


def _row_tables(dest, n_rows):
    k, t = dest.shape
    a = k * t
    nw = SC_CORES * SC_SUBCORES
    per_w = n_rows // nw
    assert n_rows % (nw * SC_LANES) == 0 and t % SC_CHUNK == 0
    mesh = plsc.VectorSubcoreMesh(core_axis_name="c", subcore_axis_name="s")
    cp = pltpu.CompilerParams()
    if "needs_layout_passes" in pltpu.CompilerParams.__dataclass_fields__:
        cp = dataclasses.replace(cp, needs_layout_passes=False)

    def body(dest_hbm, tok_out, dbuf, tloc):
        wid = lax.axis_index("s") * SC_CORES + lax.axis_index("c")
        base = wid * per_w
        shifted = base + PAIR_ROWS

        @pl.loop(0, per_w // SC_LANES)
        def _(i):
            tloc[pl.ds(i * SC_LANES, SC_LANES)] = jnp.zeros((SC_LANES,), jnp.int32)

        lane = lax.iota(jnp.int32, SC_LANES)

        @pl.loop(0, a // SC_CHUNK)
        def _(c):
            pltpu.sync_copy(dest_hbm.at[pl.ds(c * SC_CHUNK, SC_CHUNK)], dbuf)
            tok0 = lax.rem(c * SC_CHUNK, t)

            @pl.loop(0, SC_CHUNK // (SC_LANES * SC_UNROLL))
            def _(j):
                for u in range(SC_UNROLL):
                    off = (j * SC_UNROLL + u) * SC_LANES
                    loc = dbuf[pl.ds(off, SC_LANES)] - shifted
                    mine = jnp.logical_and(loc >= 0, loc < per_w)
                    loc = jnp.where(mine, loc, 0)
                    plsc.store_scatter(tloc, [loc], tok0 + off + lane, mask=mine)

        pltpu.sync_copy(tloc, tok_out.at[pl.ds(base, per_w)])

    fn = pl.kernel(
        body,
        out_type=jax.ShapeDtypeStruct((n_rows,), jnp.int32),
        mesh=mesh,
        scratch_types=[pltpu.VMEM((SC_CHUNK,), jnp.int32), pltpu.VMEM((per_w,), jnp.int32)],
        compiler_params=cp,
        name="row_tables",
    )
    return fn(dest.reshape(a))


ROW_TILE = (SUBLANES, LANES)


FFN_GROUP = 4
PAIR_ROWS = 2 * MOE_BLOCK
GATHER_BATCH = 32


def _gather_rows(h2p_ref, tok_ref, row, buf, base):
    for j0 in range(0, MOE_BLOCK, GATHER_BATCH):
        vals = [h2p_ref[tok_ref[row, j]] for j in range(j0, j0 + GATHER_BATCH)]
        for j, v in zip(range(j0, j0 + GATHER_BATCH), vals):
            buf[pl.ds(H2P_CHUNKS * (base + j), H2P_CHUNKS), :] = v


def _expert_rows(buf, row0, nrows, wset, out, out_row0):
    wg_c, wu_c, wd_c = wset
    parts = []
    for cidx in range(H2P_CHUNKS):
        word = buf[pl.ds(H2P_CHUNKS * row0 + cidx, nrows, stride=H2P_CHUNKS), :]
        parts.append(lax.bitcast_convert_type(word << 16, F32))
        parts.append(lax.bitcast_convert_type(word & jnp.int32(-65536), F32))
    x = jnp.concatenate(parts, axis=1)
    g = _dot(x, wg_c[...])
    u = _dot(x, wu_c[...])
    hm = (g * jax.nn.sigmoid(g)) * u
    out[pl.ds(out_row0, nrows), :] = _dot(hm, wd_c[...])


def _ffn_kernel(seq_ref, dexp_ref, meta_ref, tokc_ref, tokn_ref, h2p_ref, wg_hbm, wu_hbm, wd_hbm,
                y_ref, buf_0, buf_1, yraw_0, yraw_1, wg_a, wu_a, wd_a, wg_b, wu_b, wd_b, started_ref, sems):
    i = pl.program_id(0)
    n_used = meta_ref[0]
    n_exp = meta_ref[1]
    nblk = seq_ref.shape[0]
    wsets = ((wg_a, wu_a, wd_a), (wg_b, wu_b, wd_b))
    bufs = (buf_0, buf_1)
    yraws = (yraw_0, yraw_1)

    def weight_copies(s, par):
        e = dexp_ref[s]
        return [pltpu.make_async_copy(src.at[e], dst, sems.at[par, n])
                for n, (src, dst) in enumerate(zip((wg_hbm, wu_hbm, wd_hbm), wsets[par]))]

    def start_expert(s):
        for par in range(2):
            @pl.when(s % 2 == par)
            def _():
                for cp in weight_copies(s, par):
                    cp.start()

    @pl.when(i == 0)
    def _():
        _gather_rows(h2p_ref, tokc_ref, 0, buf_0, 0)
        _gather_rows(h2p_ref, tokc_ref, 1, buf_0, MOE_BLOCK)
        yraw_1[...] = jnp.zeros_like(yraw_1)
        start_expert(0)
        started_ref[0] = 0

    for p in range(FFN_GROUP // 2):
        b_a = i * FFN_GROUP + 2 * p
        used = b_a < n_used
        s_a = seq_ref[jnp.minimum(b_a, nblk - 1)]
        s_b = seq_ref[jnp.minimum(b_a + 1, nblk - 1)]
        first_a = jnp.logical_or(b_a == 0, s_a != seq_ref[jnp.clip(b_a - 1, 0, nblk - 1)])
        same = s_a == s_b
        cur, nxt = bufs[p], bufs[1 - p]
        out_rows = pl.ds(p * PAIR_ROWS, PAIR_ROWS)

        def emit_previous():
            y_ref[out_rows] = pltpu.einshape("r(cl)->rcl", yraws[1 - p][...], c=SUBLANES)

        def gather_next():
            if p == 0:
                _gather_rows(h2p_ref, tokc_ref, 2, nxt, 0)
                _gather_rows(h2p_ref, tokc_ref, 3, nxt, MOE_BLOCK)
            else:
                _gather_rows(h2p_ref, tokn_ref, 0, nxt, 0)
                _gather_rows(h2p_ref, tokn_ref, 1, nxt, MOE_BLOCK)

        @pl.when(used)
        def _():
            started = started_ref[0]

            @pl.when(s_a > started)
            def _():
                start_expert(s_a)
            started = jnp.maximum(started, s_a)

            @pl.when(s_b > started)
            def _():
                start_expert(s_b)
            started = jnp.maximum(started, s_b)
            ahead = jnp.logical_and(same, jnp.logical_and(s_b + 1 < n_exp, s_b + 1 > started))

            @pl.when(ahead)
            def _():
                start_expert(s_b + 1)
            started_ref[0] = jnp.where(ahead, s_b + 1, started)

        for par in range(2):
            @pl.when(jnp.logical_and(used, jnp.logical_and(same, s_a % 2 == par)))
            def _():
                @pl.when(first_a)
                def _():
                    for cp in weight_copies(s_a, par):
                        cp.wait()
                gather_next()
                emit_previous()
                _expert_rows(cur, 0, PAIR_ROWS, wsets[par], yraws[p], 0)

            @pl.when(jnp.logical_and(used, jnp.logical_and(jnp.logical_not(same), s_a % 2 == par)))
            def _():
                @pl.when(first_a)
                def _():
                    for cp in weight_copies(s_a, par):
                        cp.wait()
                for cp in weight_copies(s_b, 1 - par):
                    cp.wait()
                gather_next()
                emit_previous()
                _expert_rows(cur, 0, MOE_BLOCK, wsets[par], yraws[p], 0)
                _expert_rows(cur, MOE_BLOCK, MOE_BLOCK, wsets[1 - par], yraws[p], MOE_BLOCK)

        @pl.when(jnp.logical_not(used))
        def _():
            prev_used = jnp.logical_and(b_a >= 2, b_a - 2 < n_used)

            @pl.when(prev_used)
            def _():
                emit_previous()

            @pl.when(jnp.logical_not(prev_used))
            def _():
                y_ref[out_rows] = jnp.zeros((PAIR_ROWS,) + ROW_TILE, F32)


def _routed_experts(h2p, row_token, seq, dexp, meta, n_blocks, wg, wu, wd):
    d = wg.shape[1]
    de = wg.shape[2]
    ng = n_blocks // FFN_GROUP
    assert n_blocks % FFN_GROUP == 0 and FFN_GROUP == 4
    tok3 = row_token.reshape(ng, FFN_GROUP, MOE_BLOCK)
    idle_step = lambda m: ((m[0] + 1) // 2) // 2 + 1
    smem = lambda imap: pl.BlockSpec((None, FFN_GROUP, MOE_BLOCK), imap, memory_space=pltpu.SMEM)
    grid_spec = pltpu.PrefetchScalarGridSpec(
        num_scalar_prefetch=3,
        grid=(ng + 1,),
        in_specs=[
            smem(lambda i, sq, dx, m: (jnp.minimum(i, ng - 1), 0, 0)),
            smem(lambda i, sq, dx, m: (jnp.minimum(i + 1, ng - 1), 0, 0)),
            pl.BlockSpec(h2p.shape, lambda i, sq, dx, m: (0, 0, 0), pipeline_mode=pl.Buffered(1)),
            pl.BlockSpec(memory_space=pl.ANY),
            pl.BlockSpec(memory_space=pl.ANY),
            pl.BlockSpec(memory_space=pl.ANY),
        ],
        out_specs=pl.BlockSpec((FFN_GROUP * MOE_BLOCK,) + ROW_TILE,
                               lambda i, sq, dx, m: (jnp.minimum(i, idle_step(m)), 0, 0)),
        scratch_shapes=[pltpu.VMEM((PAIR_ROWS * H2P_CHUNKS, LANES), jnp.int32)] * 2 + [
            pltpu.VMEM((PAIR_ROWS, d), F32)] * 2 + [
            pltpu.VMEM((d, de), F32), pltpu.VMEM((d, de), F32), pltpu.VMEM((de, d), F32)] * 2 + [
            pltpu.SMEM((1,), jnp.int32), pltpu.SemaphoreType.DMA((2, 3))],
    )
    return pl.pallas_call(
        _ffn_kernel,
        grid_spec=grid_spec,
        out_shape=jax.ShapeDtypeStruct(((ng + 2) * FFN_GROUP * MOE_BLOCK,) + ROW_TILE, F32),
        compiler_params=_cparams(("arbitrary",)),
        name="routed_experts",
    )(seq, dexp, meta, tok3, tok3, h2p, wg, wu, wd)


COMBINE_TOKENS = 128


def _row_copies(src_hbm, idx_ref, buf, sem):
    return [pltpu.make_async_copy(src_hbm.at[idx_ref[k, j]], buf.at[k, j], sem)
            for k in range(TOP_K) for j in range(COMBINE_TOKENS)]


def _final_kernel(dc_ref, dn_ref, wk_ref, x1_ref, h2_ref, g2_ref, wsg_ref, wsu_ref, wsd_ref, lng_ref, lnb_ref,
                  y_hbm, o_ref, ybuf, sems):
    i = pl.program_id(0)
    nsteps = pl.num_programs(0)
    slot = i % 2

    def issue(d_ref, s):
        for n, cp in enumerate(_row_copies(y_hbm, d_ref, ybuf.at[s], sems.at[s])):
            cp.start(priority=n % 2)

    @pl.when(i == 0)
    def _():
        issue(dc_ref, 0)

    @pl.when(i + 1 < nsteps)
    def _():
        issue(dn_ref, 1 - slot)

    hb = h2_ref[...].astype(BF16)
    g = _dot(hb, wsg_ref[...])
    u = _dot(hb, wsu_ref[...])
    shared = _dot(((g * jax.nn.sigmoid(g)) * u).astype(BF16), wsd_ref[...])
    for cp in _row_copies(y_hbm, dc_ref, ybuf.at[slot], sems.at[slot]):
        cp.wait()
    rows = []
    for j in range(COMBINE_TOKENS):
        acc = ybuf[slot, 0, j] * wk_ref[0, j]
        for k in range(1, TOP_K):
            acc = acc + ybuf[slot, k, j] * wk_ref[k, j]
        rows.append(acc)
    routed = pltpu.einshape("tcl->t(cl)", jnp.stack(rows, axis=0))
    x2 = DN_ALPHA * x1_ref[...] + g2_ref[...] * (routed + shared)
    o_ref[...] = _ln_rows(x2) * lng_ref[...] + lnb_ref[...]


def _combine_final(x1, h2, mod4, dest, wk, y_rows, wsg, wsu, wsd, ln_g, ln_b, seq):
    t, d = x1.shape
    tt = COMBINE_TOKENS
    nsteps = t // tt
    spb = seq // tt
    row = pl.BlockSpec((tt, d), lambda i: (i, 0))
    full = lambda a: pl.BlockSpec(a.shape, lambda i: (0,) * a.ndim)
    ln_g2, ln_b2 = ln_g.reshape(1, d), ln_b.reshape(1, d)
    return pl.pallas_call(
        _final_kernel,
        grid=(nsteps,),
        in_specs=[pl.BlockSpec((TOP_K, tt), lambda i: (0, i), memory_space=pltpu.SMEM),
                  pl.BlockSpec((TOP_K, tt), lambda i: (0, jnp.minimum(i + 1, nsteps - 1)),
                               memory_space=pltpu.SMEM),
                  pl.BlockSpec((TOP_K, tt), lambda i: (0, i), memory_space=pltpu.SMEM),
                  row, row,
                  pl.BlockSpec((None, None, 1, d), lambda i: (i // spb, 5, 0, 0)),
                  full(wsg), full(wsu), full(wsd), full(ln_g2), full(ln_b2),
                  pl.BlockSpec(memory_space=pl.ANY)],
        out_specs=row,
        out_shape=jax.ShapeDtypeStruct((t, d), F32),
        scratch_shapes=[pltpu.VMEM((2, TOP_K, tt) + ROW_TILE, F32),
                        pltpu.SemaphoreType.DMA((2,))],
        compiler_params=_cparams(("arbitrary",)),
        name="shared_combine_ln",
    )(dest, dest, wk, x1, h2, mod4, wsg, wsu, wsd, ln_g2, ln_b2, y_rows)


def kernel(x, c, w_ada, b_ada, w_in, hg_lower_bound, hg_norm_w, rel_bias, w_out, ln1_g, ln1_b, w_router,
           router_bias, w_e_gate, w_e_up, w_e_down, w_sh_gate, w_sh_up, w_sh_down, ln2_g, ln2_b):
    bsz, seq, d = x.shape
    t = bsz * seq
    assert w_ada.shape[0] == DEPTH and seq % (ATT_BRANCHES[-1][0]) == 0
    x2 = x.reshape(t, d)
    bias = _bias_tables(rel_bias)
    for l in range(DEPTH):
        mod4 = _modulation(c, w_ada[l], b_ada[l]).reshape(bsz, 6, 1, d)
        w_in_bf = _cast_bf16(w_in[l], 256)
        hq, hf, hi, hg, *qkv = _in_projection(x2, mod4, w_in_bf, seq, 512)
        nbr = len(ATT_BRANCHES)
        y_hg = _hgrn2(hq, hf, hi, hg, hg_lower_bound, hg_norm_w[l], bsz, seq)
        os_, ls_ = [], []
        for g, (_, dil) in enumerate(ATT_BRANCHES):
            o, lse = _dilated_attention(qkv[g], qkv[nbr + g], qkv[2 * nbr + g], bias[g], bsz, seq, dil)
            os_.append(o)
            ls_.append(lse)
        w_out_bf = _cast_bf16(w_out[l], 256)
        x1, h2, h2p, logits_t = _out_projection(y_hg, os_, ls_, x2, mod4, w_out_bf, ln1_g[l], ln1_b[l],
                                                w_router[l].T, seq, 512)
        eidx, wk, rank, cnt = _route(logits_t, router_bias[l], 256)
        dest, blk_seq, dexp, meta, n_blocks = _dispatch_plan(cnt, eidx, rank, 512)
        row_token = _row_tables(dest, n_blocks * MOE_BLOCK)
        y_rows = _routed_experts(h2p, row_token, blk_seq, dexp, meta, n_blocks,
                                 w_e_gate[l], w_e_up[l], w_e_down[l])
        x2 = _combine_final(x1, h2, mod4, dest, wk, y_rows,
                            _cast_bf16(w_sh_gate[l], 256), _cast_bf16(w_sh_up[l], 256),
                            _cast_bf16(w_sh_down[l], 256), ln2_g[l], ln2_b[l], seq)
    return x2.reshape(bsz, seq, d)
```

```python
import dataclasses
import math

import jax
import jax.numpy as jnp
import numpy as np
from jax import lax
from jax.experimental import pallas as pl
from jax.experimental.pallas import tpu as pltpu
from jax.experimental.pallas import tpu_sc as plsc

HG_HEADS = 4
HG_DK = 128
HG_WIDTH = HG_HEADS * HG_DK
ATT_BRANCHES = ((128, 1), (512, 4), (2048, 16))
ATT_HEADS_PER_BRANCH = 4
ATT_HEAD_DIM = 64
ATT_BW = ATT_HEADS_PER_BRANCH * ATT_HEAD_DIM
ATT_BLOCK = 128
REL_BUCKETS = 32
REL_MAX_DIST = 2048
N_EXPERTS = 256
TOP_K = 8
N_GROUPS = 8
TOPK_GROUPS = 4
ROUTED_SCALE = 2.5
MOE_BLOCK = 128
DEPTH = 1
DN_ALPHA = (2 * DEPTH) ** 0.25
LN_EPS = 1e-5
RMS_EPS = 1e-6

LANES = 128
SUBLANES = 8
VMEM_LIMIT_BYTES = 56 * 1024 * 1024

F32 = jnp.float32
BF16 = jnp.bfloat16
NEG_INF = float("-inf")


def _cparams(sem):
    return pltpu.CompilerParams(dimension_semantics=sem, vmem_limit_bytes=VMEM_LIMIT_BYTES)


def _ln_rows(x):
    mu = jnp.mean(x, axis=-1, keepdims=True)
    xc = x - mu
    var = jnp.mean(xc * xc, axis=-1, keepdims=True)
    return xc * lax.rsqrt(var + LN_EPS)


def _dot(a, b):
    return jnp.dot(a, b, preferred_element_type=F32)


def _dot_nt(a, b):
    return lax.dot_general(a, b, (((1,), (1,)), ((), ())), preferred_element_type=F32)


def _dot_tn(a, b):
    return lax.dot_general(a, b, (((0,), (0,)), ((), ())), preferred_element_type=F32)


def _cast_kernel(w_ref, o_ref):
    o_ref[...] = w_ref[...].astype(o_ref.dtype)


def _cast_bf16(w, rows_per_step):
    r, c = w.shape
    return pl.pallas_call(
        _cast_kernel,
        grid=(r // rows_per_step,),
        in_specs=[pl.BlockSpec((rows_per_step, c), lambda i: (i, 0))],
        out_specs=pl.BlockSpec((rows_per_step, c), lambda i: (i, 0)),
        out_shape=jax.ShapeDtypeStruct((r, c), BF16),
        compiler_params=_cparams(("parallel",)),
        name="cast_bf16",
    )(w)


def _mod_kernel(c_ref, w_ref, b_ref, o_ref):
    c = c_ref[...]
    cond = c * jax.nn.sigmoid(c)
    o_ref[...] = jnp.dot(cond, w_ref[...], preferred_element_type=F32,
                         precision=lax.Precision.HIGHEST) + b_ref[...]


def _modulation(c, w_ada, b_ada):
    bsz, d = c.shape
    n = w_ada.shape[1]
    rows = -(-bsz // SUBLANES) * SUBLANES
    cpad = jnp.zeros((rows, d), F32).at[:bsz].set(c)
    tn = 1024
    out = pl.pallas_call(
        _mod_kernel,
        grid=(n // tn,),
        in_specs=[pl.BlockSpec((rows, d), lambda j: (0, 0)),
                  pl.BlockSpec((d, tn), lambda j: (0, j)),
                  pl.BlockSpec((1, tn), lambda j: (0, j))],
        out_specs=pl.BlockSpec((rows, tn), lambda j: (0, j)),
        out_shape=jax.ShapeDtypeStruct((rows, n), F32),
        compiler_params=_cparams(("parallel",)),
        name="adaln_modulation",
    )(cpad, w_ada, b_ada.reshape(1, n))
    return out[:bsz]


_IN_HG = 4
_IN_ATT = 3 * len(ATT_BRANCHES)


def _inproj_kernel(x_ref, sc_ref, sh_ref, w_ref, *refs):
    outs = refs[:_IN_HG + _IN_ATT]
    scratch = refs[_IN_HG + _IN_ATT:]
    x = x_ref[...]
    h = _ln_rows(x) * (1.0 + sc_ref[...]) + sh_ref[...]
    hb = h.astype(BF16)
    tm = x.shape[0]
    col = 0
    n_scr = 0
    for k, o_ref in enumerate(outs):
        width = HG_WIDTH if k < _IN_HG else ATT_BW
        y = _dot(hb, w_ref[:, col:col + width])
        col += width
        if k < _IN_HG:
            o_ref[...] = y.astype(o_ref.dtype)
            continue
        if k < _IN_HG + len(ATT_BRANCHES):
            y = y * (ATT_HEAD_DIM ** -0.5)
        dil = ATT_BRANCHES[(k - _IN_HG) % len(ATT_BRANCHES)][1]
        if dil == 1:
            o_ref[...] = y.astype(o_ref.dtype)
            continue
        scr = scratch[n_scr]
        n_scr += 1
        for half in range(ATT_BW // LANES):
            scr[half] = y[:, half * LANES:(half + 1) * LANES]
        for r in range(dil):
            for half in range(ATT_BW // LANES):
                c0 = r * ATT_BW + half * LANES
                o_ref[:, c0:c0 + LANES] = scr[half, pl.ds(r, tm // dil, stride=dil), :].astype(o_ref.dtype)


def _in_projection(x2, mod4, w_in_bf, seq, tm):
    t, d = x2.shape
    steps_per_batch = seq // tm
    dils = [dil for _, dil in ATT_BRANCHES] * 3
    shapes = [(t, HG_WIDTH)] * _IN_HG + [(t // dil, dil * ATT_BW) for dil in dils]
    blocks = [(tm, HG_WIDTH)] * _IN_HG + [(tm // dil, dil * ATT_BW) for dil in dils]
    dtypes = [BF16, F32, BF16, BF16] + [BF16] * _IN_ATT
    mod_spec = lambda row: pl.BlockSpec((None, None, 1, d),
                                        lambda i, row=row: (i // steps_per_batch, row, 0, 0))
    outs = pl.pallas_call(
        _inproj_kernel,
        grid=(t // tm,),
        in_specs=[pl.BlockSpec((tm, d), lambda i: (i, 0)),
                  mod_spec(1), mod_spec(0),
                  pl.BlockSpec(w_in_bf.shape, lambda i: (0, 0))],
        out_specs=[pl.BlockSpec(b, lambda i: (i, 0)) for b in blocks],
        out_shape=[jax.ShapeDtypeStruct(s, dt) for s, dt in zip(shapes, dtypes)],
        scratch_shapes=[pltpu.VMEM((ATT_BW // LANES, tm, LANES), F32) for dil in dils if dil > 1],
        compiler_params=_cparams(("parallel",)),
        name="ln_in_projection",
    )(x2, mod4, mod4, w_in_bf)
    return outs


HG_CHUNK = 64
HG_CHUNKS_PER_STEP = 4
HG_SUB = 8
HG_LEVELS = (64, 32, 16)


def _hgrn_chunk(q, z, iv, lb):
    c = HG_CHUNK
    f = lb + (1.0 - lb) * jax.nn.sigmoid(z)
    lf = jnp.log(f)
    kk = (1.0 - lb) * jax.nn.sigmoid(-z)
    r_i = lax.broadcasted_iota(jnp.int32, (c, c), 0)
    c_i = lax.broadcasted_iota(jnp.int32, (c, c), 1)
    tril = (c_i <= r_i).astype(F32)
    b = jnp.dot(tril, lf, preferred_element_type=F32, precision=lax.Precision.HIGHEST)

    row = lax.broadcasted_iota(jnp.int32, (c, HG_DK), 0)
    scores = jnp.zeros((c, c), F32)
    for m in HG_LEVELS:
        nb = c // m
        b3 = b.reshape(nb, m, HG_DK)
        piv = jnp.broadcast_to(b3[:, m // 2 - 1:m // 2, :], (nb, m, HG_DK)).reshape(c, HG_DK)
        second = (row % m) >= (m // 2)
        qt = jnp.where(second, q * jnp.exp(b - piv), 0.0)
        kt = jnp.where(second, 0.0, kk * jnp.exp(piv - b))
        s_m = _dot_nt(qt.astype(BF16), kt.astype(BF16))
        if nb > 1:
            s_m = jnp.where((r_i // m) == (c_i // m), s_m, 0.0)
        scores = scores + s_m
    sub = HG_SUB
    t_i = lax.broadcasted_iota(jnp.int32, (sub, 1), 0)
    lane = lax.broadcasted_iota(jnp.int32, (sub, c), 1)
    diag_rows = []
    for j in range(c // sub):
        qb = q[j * sub:(j + 1) * sub]
        kb = kk[j * sub:(j + 1) * sub]
        bb = b[j * sub:(j + 1) * sub]
        a_j = jnp.zeros((sub, c), F32)
        for s in range(sub):
            w = qb * kb[s:s + 1] * jnp.exp(bb - bb[s:s + 1])
            col = jnp.sum(w, axis=-1, keepdims=True)
            col = jnp.where(t_i >= s, col, 0.0)
            a_j = jnp.where(lane == j * sub + s, col, a_j)
        diag_rows.append(a_j)
    scores = scores + jnp.concatenate(diag_rows, axis=0)

    ivb = iv.astype(BF16)
    intra = _dot(scores.astype(BF16), ivb)
    b_last = b[c - 1:c]
    kdec = (kk * jnp.exp(b_last - b)).astype(BF16)
    return (q * jnp.exp(b)).astype(BF16), intra, jnp.exp(b_last), _dot_tn(ivb, kdec)


def _hgrn_kernel(q_ref, f_ref, i_ref, g_ref, lbp_ref, nw_ref, o_ref, st_ref):
    @pl.when(pl.program_id(1) == 0)
    def _():
        st_ref[...] = jnp.zeros_like(st_ref)

    lbp = lbp_ref[...]
    e = jnp.exp(lbp - jnp.max(lbp, axis=0, keepdims=True))
    lb_all = e[0:1] / jnp.sum(e, axis=0, keepdims=True)
    heads = []
    for h in range(HG_HEADS):
        sl = slice(h * HG_DK, (h + 1) * HG_DK)
        st_t = st_ref[h]
        outs = []
        for n in range(HG_CHUNKS_PER_STEP):
            rows = slice(n * HG_CHUNK, (n + 1) * HG_CHUNK)
            qdec, intra, dec_last, kv = _hgrn_chunk(q_ref[rows, sl].astype(F32), f_ref[rows, sl],
                                                    i_ref[rows, sl].astype(F32), lb_all[:, sl])
            o = _dot_nt(qdec, st_t.astype(BF16)) + intra
            st_t = st_t * dec_last + kv
            outs.append(o * lax.rsqrt(jnp.mean(o * o, axis=-1, keepdims=True) + RMS_EPS))
        st_ref[h] = st_t
        heads.append(jnp.concatenate(outs, axis=0))
    o_all = jnp.concatenate(heads, axis=-1)
    g = g_ref[...].astype(F32)
    o_ref[...] = (o_all * nw_ref[...] * (g * jax.nn.sigmoid(g))).astype(o_ref.dtype)


def _hgrn2(hq, hf, hi, hg, lb_param, norm_w, bsz, seq):
    t = hq.shape[0]
    rows = HG_CHUNK * HG_CHUNKS_PER_STEP
    nc = seq // rows
    tok = lambda b, n: (b * nc + n, 0)
    spec = pl.BlockSpec((rows, HG_WIDTH), tok)
    return pl.pallas_call(
        _hgrn_kernel,
        grid=(bsz, nc),
        in_specs=[spec, spec, spec, spec,
                  pl.BlockSpec(lb_param.shape, lambda b, n: (0, 0)),
                  pl.BlockSpec((1, HG_WIDTH), lambda b, n: (0, 0))],
        out_specs=spec,
        out_shape=jax.ShapeDtypeStruct((t, HG_WIDTH), BF16),
        scratch_shapes=[pltpu.VMEM((HG_HEADS, HG_DK, HG_DK), F32)],
        compiler_params=_cparams(("parallel", "arbitrary")),
        name="hgrn2_scan",
    )(hq, hf, hi, hg, lb_param, norm_w.reshape(1, HG_WIDTH))


def _t5_bucket_np(dist):
    max_exact = REL_BUCKETS // 2
    n = np.maximum(dist, 0)
    nf = np.maximum(n, 1).astype(np.float32)
    large = max_exact + (np.log(nf / np.float32(max_exact)) / np.float32(math.log(REL_MAX_DIST / max_exact))
                         * np.float32(REL_BUCKETS - max_exact)).astype(np.int32)
    large = np.minimum(large, REL_BUCKETS - 1)
    return np.where(n < max_exact, n, large).astype(np.int32)


def _band_tables():
    w = ATT_BLOCK
    qi = np.arange(w)[:, None]
    ki = np.arange(2 * w)[None, :]
    m = w + qi - ki
    band = (m >= 0) & (m <= w)
    buckets = np.stack([_t5_bucket_np(m * dil) for _, dil in ATT_BRANCHES])
    return buckets, band


def _bias_kernel(rb_ref, bucket_ref, o_ref):
    g = pl.program_id(0)
    w = ATT_BLOCK
    bucket = bucket_ref[...]
    qi = lax.broadcasted_iota(jnp.int32, (w, 2 * w), 0)
    ki = lax.broadcasted_iota(jnp.int32, (w, 2 * w), 1)
    m = w + qi - ki
    band = (m >= 0) & (m <= w)
    for h in range(ATT_HEADS_PER_BRANCH):
        acc = jnp.zeros((w, 2 * w), F32)
        for c in range(REL_BUCKETS):
            acc = jnp.where(bucket == c, rb_ref[c, g * ATT_HEADS_PER_BRANCH + h], acc)
        full = jnp.where(band, acc, NEG_INF)
        o_ref[1, h] = full
        o_ref[0, h] = jnp.where(ki >= w, full, NEG_INF)


def _bias_tables(rel_bias):
    buckets, _ = _band_tables()
    g = len(ATT_BRANCHES)
    w = ATT_BLOCK
    return pl.pallas_call(
        _bias_kernel,
        grid=(g,),
        in_specs=[pl.BlockSpec(memory_space=pltpu.SMEM),
                  pl.BlockSpec((None, w, 2 * w), lambda i: (i, 0, 0))],
        out_specs=pl.BlockSpec((None, 2, ATT_HEADS_PER_BRANCH, w, 2 * w), lambda i: (i, 0, 0, 0, 0)),
        out_shape=jax.ShapeDtypeStruct((g, 2, ATT_HEADS_PER_BRANCH, w, 2 * w), F32),
        compiler_params=_cparams(("parallel",)),
        name="rel_bias_tables",
    )(rel_bias, jnp.asarray(buckets))


ATT_BLOCKS_PER_STEP = 4


def _attn_kernel(q_ref, kp_ref, kc_ref, vp_ref, vc_ref, bias_ref, o_ref, lse_ref):
    m = pl.program_id(2)
    w = ATT_BLOCK
    hb = ATT_HEADS_PER_BRANCH
    lane = lax.broadcasted_iota(jnp.int32, (w, ATT_BW), 1) // ATT_HEAD_DIM
    kall = jnp.concatenate([kp_ref[...], kc_ref[...]], axis=0)
    vall = jnp.concatenate([vp_ref[...], vc_ref[...]], axis=0)
    for blk in range(q_ref.shape[0] // w):
        q = q_ref[blk * w:(blk + 1) * w]
        q4 = jnp.concatenate([jnp.where(lane == h, q, jnp.zeros_like(q)) for h in range(hb)], axis=0)
        kk = kall[blk * w:(blk + 2) * w]
        vv = vall[blk * w:(blk + 2) * w]
        s4 = _dot_nt(q4, kk)
        bias = bias_ref[jnp.minimum(m, 1)] if blk == 0 else bias_ref[1]
        s4 = s4 + bias.reshape(hb * w, 2 * w)
        mx = jnp.max(s4, axis=-1, keepdims=True)
        p = jnp.exp(s4 - mx)
        l = jnp.sum(p, axis=-1, keepdims=True)
        o4 = _dot((p / l).astype(vv.dtype), vv)
        lse4 = mx + jnp.log(l)
        o = jnp.zeros((w, ATT_BW), F32)
        lse = jnp.zeros((w, ATT_BW), F32)
        for h in range(hb):
            o = jnp.where(lane == h, o4[h * w:(h + 1) * w], o)
            lse = jnp.where(lane == h, lse4[h * w:(h + 1) * w], lse)
        o_ref[blk * w:(blk + 1) * w] = o.astype(o_ref.dtype)
        lse_ref[blk * w:(blk + 1) * w] = lse


def _dilated_attention(q, k, v, bias_g, bsz, seq, dilation):
    w = ATT_BLOCK
    l = seq // dilation
    pstep = math.gcd(ATT_BLOCKS_PER_STEP, l // w)
    nb = l // (w * pstep)
    view = lambda a: a.reshape(bsz, l, dilation * ATT_BW)
    cur = pl.BlockSpec((None, pstep * w, ATT_BW), lambda b, r, n: (b, n, r))
    prev = pl.BlockSpec((None, w, ATT_BW), lambda b, r, n: (b, jnp.maximum(pstep * n - 1, 0), r))
    o, lse = pl.pallas_call(
        _attn_kernel,
        grid=(bsz, dilation, nb),
        in_specs=[cur, prev, cur, prev, cur,
                  pl.BlockSpec(bias_g.shape, lambda b, r, n: (0, 0, 0, 0))],
        out_specs=[cur, cur],
        out_shape=[jax.ShapeDtypeStruct((bsz, l, dilation * ATT_BW), BF16),
                   jax.ShapeDtypeStruct((bsz, l, dilation * ATT_BW), F32)],
        compiler_params=_cparams(("parallel", "parallel", "arbitrary")),
        name=f"dilated_attention_d{dilation}",
    )(view(q), view(k), view(k), view(v), view(v), bias_g)
    return o.reshape(bsz * l, dilation * ATT_BW), lse.reshape(bsz * l, dilation * ATT_BW)


def _split_bf16(a):
    hi = a.astype(BF16)
    lo = (a - hi.astype(F32)).astype(BF16)
    return hi, lo


H2P_CHUNKS = 4


def _bf16_bits(a):
    u = lax.bitcast_convert_type(a, jnp.uint32)
    return u + jnp.uint32(0x7FFF) + ((u >> 16) & jnp.uint32(1))


def _token_order(ref, scr, dil):
    if dil == 1:
        return ref[...].astype(F32)
    n = ref.shape[0]
    halves = ATT_BW // LANES
    for r in range(dil):
        for half in range(halves):
            c0 = r * ATT_BW + half * LANES
            scr[half, pl.ds(r, n, stride=dil), :] = ref[:, c0:c0 + LANES].astype(F32)
    return jnp.concatenate([scr[half] for half in range(halves)], axis=1)


def _outproj_kernel(yhg_ref, o1_ref, o2_ref, o3_ref, l1_ref, l2_ref, l3_ref, x_ref,
                    g1_ref, sc2_ref, sh2_ref, wout_ref, lng_ref, lnb_ref, wrt_ref,
                    x1_ref, h2_ref, h2p_ref, lgt_ref, *scratch):
    dils = [dil for _, dil in ATT_BRANCHES]
    scr = iter(scratch)
    o1, o2, o3 = [_token_order(r, None if dil == 1 else next(scr), dil)
                  for r, dil in zip((o1_ref, o2_ref, o3_ref), dils)]
    l1, l2, l3 = [_token_order(r, None if dil == 1 else next(scr), dil)
                  for r, dil in zip((l1_ref, l2_ref, l3_ref), dils)]
    mx = jnp.maximum(jnp.maximum(l1, l2), l3)
    e1, e2, e3 = jnp.exp(l1 - mx), jnp.exp(l2 - mx), jnp.exp(l3 - mx)
    den = e1 + e2 + e3
    att = (e1 / den) * o1 + (e2 / den) * o2 + (e3 / den) * o3
    mix = _dot(yhg_ref[...], wout_ref[:HG_WIDTH, :]) + _dot(att.astype(BF16), wout_ref[HG_WIDTH:, :])
    x1 = _ln_rows(DN_ALPHA * x_ref[...] + g1_ref[...] * mix) * lng_ref[...] + lnb_ref[...]
    x1_ref[...] = x1
    h2 = _ln_rows(x1) * (1.0 + sc2_ref[...]) + sh2_ref[...]
    h2_ref[...] = h2
    for cidx in range(H2P_CHUNKS):
        lo = _bf16_bits(h2[:, 2 * LANES * cidx:2 * LANES * cidx + LANES])
        hi = _bf16_bits(h2[:, 2 * LANES * cidx + LANES:2 * LANES * (cidx + 1)])
        word = (lo >> 16) | (hi & jnp.uint32(0xFFFF0000))
        h2p_ref[:, cidx, :] = lax.bitcast_convert_type(word, jnp.int32)
    h_hi, h_lo = _split_bf16(h2)
    w_hi, w_lo = _split_bf16(wrt_ref[...])
    lgt_ref[...] = _dot_nt(w_hi, h_hi) + (_dot_nt(w_hi, h_lo) + _dot_nt(w_lo, h_hi))


def _out_projection(yhg, os_, ls_, x2, mod4, w_out_bf, ln_g, ln_b, w_router_t, seq, tm):
    t, d = x2.shape
    spb = seq // tm
    ne = w_router_t.shape[0]
    row = lambda w: pl.BlockSpec((tm, w), lambda i: (i, 0))
    mod_spec = lambda r: pl.BlockSpec((None, None, 1, d), lambda i, r=r: (i // spb, r, 0, 0))
    full = lambda a: pl.BlockSpec(a.shape, lambda i: (0,) * a.ndim)
    ln_g2, ln_b2 = ln_g.reshape(1, d), ln_b.reshape(1, d)
    dils = [dil for _, dil in ATT_BRANCHES]
    branch = [pl.BlockSpec((tm // dil, dil * ATT_BW), lambda i: (i, 0)) for dil in dils]
    return pl.pallas_call(
        _outproj_kernel,
        grid=(t // tm,),
        in_specs=[row(HG_WIDTH)] + branch + branch + [row(d),
                  mod_spec(2), mod_spec(4), mod_spec(3),
                  full(w_out_bf), full(ln_g2), full(ln_b2), full(w_router_t)],
        out_specs=[row(d), row(d), pl.BlockSpec((tm, H2P_CHUNKS, LANES), lambda i: (i, 0, 0)),
                   pl.BlockSpec((ne, tm), lambda i: (0, i))],
        out_shape=[jax.ShapeDtypeStruct((t, d), F32), jax.ShapeDtypeStruct((t, d), F32),
                   jax.ShapeDtypeStruct((t, H2P_CHUNKS, LANES), jnp.int32),
                   jax.ShapeDtypeStruct((ne, t), F32)],
        scratch_shapes=[pltpu.VMEM((ATT_BW // LANES, tm, LANES), F32) for dil in dils + dils if dil > 1],
        compiler_params=_cparams(("parallel",)),
        name="merge_outproj_ln",
    )(yhg, *os_, *ls_, x2, mod4, mod4, mod4, w_out_bf, ln_g2, ln_b2, w_router_t)


def _argmax_rows(cur, iota, nrows):
    m = jnp.max(cur, axis=0, keepdims=True)
    idx = jnp.min(jnp.where(cur == m, iota, nrows), axis=0, keepdims=True)
    return m, idx, iota == idx


def _route_kernel(lgt_ref, rb_ref, eidx_ref, w_ref, rank_ref, cnt_ref, carry):
    ne = N_EXPERTS
    gsz = ne // N_GROUPS
    tt = lgt_ref.shape[1]

    @pl.when(pl.program_id(0) == 0)
    def _():
        carry[...] = jnp.zeros_like(carry)

    sc = jax.nn.sigmoid(lgt_ref[...])
    biased = sc + rb_ref[...]
    g3 = biased.reshape(N_GROUPS, gsz, tt)
    io3 = lax.broadcasted_iota(jnp.int32, (N_GROUPS, gsz, tt), 1)
    m1 = jnp.max(g3, axis=1, keepdims=True)
    first = jnp.min(jnp.where(g3 == m1, io3, gsz), axis=1, keepdims=True)
    m2 = jnp.max(jnp.where(io3 == first, NEG_INF, g3), axis=1, keepdims=True)
    gs = (m1 + m2).reshape(N_GROUPS, tt)
    io8 = lax.broadcasted_iota(jnp.int32, (N_GROUPS, tt), 0)
    sel = jnp.zeros((N_GROUPS, tt), jnp.int32)
    cur = gs
    for _ in range(TOPK_GROUPS):
        _, _, pick = _argmax_rows(cur, io8, N_GROUPS)
        sel = jnp.where(pick, 1, sel)
        cur = jnp.where(pick, NEG_INF, cur)
    masked = jnp.where(sel.reshape(N_GROUPS, 1, tt) > 0, g3, NEG_INF).reshape(ne, tt)
    ioe = lax.broadcasted_iota(jnp.int32, (ne, tt), 0)
    cur = masked
    idxs, ws, picks = [], [], []
    for _ in range(TOP_K):
        _, idx, pick = _argmax_rows(cur, ioe, ne)
        idxs.append(idx)
        picks.append(pick)
        ws.append(jnp.sum(jnp.where(pick, sc, 0.0), axis=0, keepdims=True))
        cur = jnp.where(pick, NEG_INF, cur)
    wk = jnp.concatenate(ws, axis=0)
    eidx_ref[...] = jnp.concatenate(idxs, axis=0)
    w_ref[...] = wk / jnp.sum(wk, axis=0, keepdims=True) * ROUTED_SCALE
    chosen = jnp.where(cur == NEG_INF, jnp.where(masked == NEG_INF, 0.0, 1.0), 0.0)
    r_i = lax.broadcasted_iota(jnp.int32, (tt, tt), 0)
    c_i = lax.broadcasted_iota(jnp.int32, (tt, tt), 1)
    before = jnp.where(r_i < c_i, 1.0, 0.0).astype(BF16)
    pref = _dot(chosen.astype(BF16), before) + carry[...]
    rank_ref[...] = jnp.concatenate(
        [jnp.sum(jnp.where(p, pref, 0.0), axis=0, keepdims=True) for p in picks], axis=0).astype(jnp.int32)
    carry[...] = carry[...] + jnp.sum(chosen, axis=1, keepdims=True)
    cnt_ref[...] = carry[...]


def _route(logits_t, router_bias, tt):
    ne, t = logits_t.shape
    tok = pl.BlockSpec((TOP_K, tt), lambda i: (0, i))
    return pl.pallas_call(
        _route_kernel,
        grid=(t // tt,),
        in_specs=[pl.BlockSpec((ne, tt), lambda i: (0, i)),
                  pl.BlockSpec((ne, 1), lambda i: (0, 0))],
        out_specs=[tok, tok, tok, pl.BlockSpec((ne, 1), lambda i: (0, 0))],
        out_shape=[jax.ShapeDtypeStruct((TOP_K, t), jnp.int32), jax.ShapeDtypeStruct((TOP_K, t), F32),
                   jax.ShapeDtypeStruct((TOP_K, t), jnp.int32), jax.ShapeDtypeStruct((ne, 1), F32)],
        scratch_shapes=[pltpu.VMEM((ne, 1), F32)],
        compiler_params=_cparams(("arbitrary",)),
        name="router_topk",
    )(logits_t, router_bias.reshape(ne, 1))


def _plan_kernel(cnt_ref, eidx_ref, rank_ref, dest_ref, seq_ref, dexp_ref, meta_ref):
    ne = N_EXPERTS
    tt = eidx_ref.shape[1]
    nblk = seq_ref.shape[1]
    cnt = cnt_ref[...].astype(jnp.int32)
    padded = ((cnt + (MOE_BLOCK - 1)) // MOE_BLOCK) * MOE_BLOCK
    r_i = lax.broadcasted_iota(jnp.int32, (ne, ne), 0)
    c_i = lax.broadcasted_iota(jnp.int32, (ne, ne), 1)
    incl = jnp.where(c_i <= r_i, 1.0, 0.0)
    pend = jnp.dot(incl, jnp.broadcast_to(padded.astype(F32), (ne, LANES)),
                   preferred_element_type=F32, precision=lax.Precision.HIGHEST)[:, 0:1]
    pend = pend.astype(jnp.int32)
    pstart = pend - padded
    ioe = lax.broadcasted_iota(jnp.int32, (ne, tt), 0)
    rows = []
    for k in range(TOP_K):
        sel = ioe == eidx_ref[k:k + 1, :]
        rows.append(jnp.sum(jnp.where(sel, pstart, 0), axis=0, keepdims=True))
    dest_ref[...] = jnp.concatenate(rows, axis=0) + rank_ref[...] + PAIR_ROWS
    blk0 = lax.broadcasted_iota(jnp.int32, (ne, nblk), 1) * MOE_BLOCK
    be = jnp.minimum(jnp.sum(jnp.where(pend <= blk0, 1, 0), axis=0, keepdims=True), ne - 1)
    present = cnt > 0
    strict = jnp.where(c_i < r_i, 1.0, 0.0).astype(BF16)
    sidx = _dot(strict, jnp.broadcast_to(jnp.where(present, 1.0, 0.0), (ne, LANES)).astype(BF16))[:, 0:1]
    sidx = sidx.astype(jnp.int32)
    dexp_ref[...] = jnp.sum(jnp.where(jnp.logical_and(present, sidx == c_i), r_i, 0), axis=0, keepdims=True)
    ioeb = lax.broadcasted_iota(jnp.int32, (ne, nblk), 0)
    nu = jnp.max(pend, axis=0, keepdims=True) // MOE_BLOCK
    nd = jnp.sum(jnp.where(present, 1, 0), axis=0, keepdims=True)
    seq_ref[...] = jnp.minimum(jnp.sum(jnp.where(ioeb == be, sidx, 0), axis=0, keepdims=True), nd - 1)
    lane = lax.broadcasted_iota(jnp.int32, (1, LANES), 1)
    meta_ref[...] = jnp.where(lane == 0, nu, jnp.where(lane == 1, nd, 0))


def _dispatch_plan(cnt, eidx, rank, tt):
    k, t = eidx.shape
    n_blocks = -(-(t * k) // MOE_BLOCK) + N_EXPERTS
    tok = pl.BlockSpec((k, tt), lambda i: (0, i))
    one = lambda n: pl.BlockSpec((1, n), lambda i: (0, 0))
    dest, seq, dexp, meta = pl.pallas_call(
        _plan_kernel,
        grid=(t // tt,),
        in_specs=[pl.BlockSpec(cnt.shape, lambda i: (0, 0)), tok, tok],
        out_specs=[tok, one(n_blocks), one(N_EXPERTS), one(LANES)],
        out_shape=[jax.ShapeDtypeStruct((k, t), jnp.int32), jax.ShapeDtypeStruct((1, n_blocks), jnp.int32),
                   jax.ShapeDtypeStruct((1, N_EXPERTS), jnp.int32), jax.ShapeDtypeStruct((1, LANES), jnp.int32)],
        compiler_params=_cparams(("arbitrary",)),
        name="dispatch_plan",
    )(cnt, eidx, rank)
    return dest, seq.reshape(n_blocks), dexp.reshape(N_EXPERTS), meta.reshape(LANES), n_blocks


SC_CORES = 2
SC_SUBCORES = 16
SC_LANES = 16
SC_CHUNK = 16384
SC_UNROLL = 4


def _row_tables(dest, n_rows):
    k, t = dest.shape
    a = k * t
    nw = SC_CORES * SC_SUBCORES
    per_w = n_rows // nw
    assert n_rows % (nw * SC_LANES) == 0 and t % SC_CHUNK == 0
    mesh = plsc.VectorSubcoreMesh(core_axis_name="c", subcore_axis_name="s")
    cp = pltpu.CompilerParams()
    if "needs_layout_passes" in pltpu.CompilerParams.__dataclass_fields__:
        cp = dataclasses.replace(cp, needs_layout_passes=False)

    def body(dest_hbm, tok_out, dbuf, tloc):
        wid = lax.axis_index("s") * SC_CORES + lax.axis_index("c")
        base = wid * per_w
        shifted = base + PAIR_ROWS

        @pl.loop(0, per_w // SC_LANES)
        def _(i):
            tloc[pl.ds(i * SC_LANES, SC_LANES)] = jnp.zeros((SC_LANES,), jnp.int32)

        lane = lax.iota(jnp.int32, SC_LANES)

        @pl.loop(0, a // SC_CHUNK)
        def _(c):
            pltpu.sync_copy(dest_hbm.at[pl.ds(c * SC_CHUNK, SC_CHUNK)], dbuf)
            tok0 = lax.rem(c * SC_CHUNK, t)

            @pl.loop(0, SC_CHUNK // (SC_LANES * SC_UNROLL))
            def _(j):
                for u in range(SC_UNROLL):
                    off = (j * SC_UNROLL + u) * SC_LANES
                    loc = dbuf[pl.ds(off, SC_LANES)] - shifted
                    mine = jnp.logical_and(loc >= 0, loc < per_w)
                    loc = jnp.where(mine, loc, 0)
                    plsc.store_scatter(tloc, [loc], tok0 + off + lane, mask=mine)

        pltpu.sync_copy(tloc, tok_out.at[pl.ds(base, per_w)])

    fn = pl.kernel(
        body,
        out_type=jax.ShapeDtypeStruct((n_rows,), jnp.int32),
        mesh=mesh,
        scratch_types=[pltpu.VMEM((SC_CHUNK,), jnp.int32), pltpu.VMEM((per_w,), jnp.int32)],
        compiler_params=cp,
        name="row_tables",
    )
    return fn(dest.reshape(a))


ROW_TILE = (SUBLANES, LANES)


FFN_GROUP = 4
PAIR_ROWS = 2 * MOE_BLOCK
W_SETS = 3
GATHER_BATCH = 32


def _gather_rows(h2p_ref, tok_ref, row, buf, base):
    for j0 in range(0, MOE_BLOCK, GATHER_BATCH):
        vals = [h2p_ref[tok_ref[row, j]] for j in range(j0, j0 + GATHER_BATCH)]
        for j, v in zip(range(j0, j0 + GATHER_BATCH), vals):
            buf[pl.ds(H2P_CHUNKS * (base + j), H2P_CHUNKS), :] = v


def _expert_rows(buf, row0, nrows, wset, out, out_row0):
    wg_c, wu_c, wd_c = wset
    parts = []
    for cidx in range(H2P_CHUNKS):
        word = buf[pl.ds(H2P_CHUNKS * row0 + cidx, nrows, stride=H2P_CHUNKS), :]
        parts.append(lax.bitcast_convert_type(word << 16, F32))
        parts.append(lax.bitcast_convert_type(word & jnp.int32(-65536), F32))
    x = jnp.concatenate(parts, axis=1)
    g = _dot(x, wg_c[...])
    u = _dot(x, wu_c[...])
    hm = (g * jax.nn.sigmoid(g)) * u
    out[pl.ds(out_row0, nrows), :] = _dot(hm, wd_c[...])


def _ffn_kernel(seq_ref, dexp_ref, meta_ref, tokc_ref, tokn_ref, h2p_ref, wg_hbm, wu_hbm, wd_hbm,
                y_ref, buf_0, buf_1, yraw_0, yraw_1, *rest):
    i = pl.program_id(0)
    n_used = meta_ref[0]
    n_exp = meta_ref[1]
    nblk = seq_ref.shape[0]
    wsets = tuple(tuple(rest[3 * n:3 * n + 3]) for n in range(W_SETS))
    started_ref, sems = rest[3 * W_SETS:]
    bufs = (buf_0, buf_1)
    yraws = (yraw_0, yraw_1)

    def weight_copies(s, par):
        e = dexp_ref[s]
        return [pltpu.make_async_copy(src.at[e], dst, sems.at[par, n])
                for n, (src, dst) in enumerate(zip((wg_hbm, wu_hbm, wd_hbm), wsets[par]))]

    def start_expert(s):
        for par in range(W_SETS):
            @pl.when(s % W_SETS == par)
            def _():
                for cp in weight_copies(s, par):
                    cp.start()

    @pl.when(i == 0)
    def _():
        _gather_rows(h2p_ref, tokc_ref, 0, buf_0, 0)
        _gather_rows(h2p_ref, tokc_ref, 1, buf_0, MOE_BLOCK)
        yraw_1[...] = jnp.zeros_like(yraw_1)
        start_expert(0)
        started_ref[0] = 0

    for p in range(FFN_GROUP // 2):
        b_a = i * FFN_GROUP + 2 * p
        used = b_a < n_used
        s_a = seq_ref[jnp.minimum(b_a, nblk - 1)]
        s_b = seq_ref[jnp.minimum(b_a + 1, nblk - 1)]
        first_a = jnp.logical_or(b_a == 0, s_a != seq_ref[jnp.clip(b_a - 1, 0, nblk - 1)])
        same = s_a == s_b
        cur, nxt = bufs[p], bufs[1 - p]
        out_rows = pl.ds(p * PAIR_ROWS, PAIR_ROWS)

        def emit_previous():
            y_ref[out_rows] = pltpu.einshape("r(cl)->rcl", yraws[1 - p][...], c=SUBLANES)

        def gather_next():
            if p == 0:
                _gather_rows(h2p_ref, tokc_ref, 2, nxt, 0)
                _gather_rows(h2p_ref, tokc_ref, 3, nxt, MOE_BLOCK)
            else:
                _gather_rows(h2p_ref, tokn_ref, 0, nxt, 0)
                _gather_rows(h2p_ref, tokn_ref, 1, nxt, MOE_BLOCK)

        @pl.when(used)
        def _():
            started = started_ref[0]
            limit = jnp.minimum(s_a + (W_SETS - 1), n_exp - 1)
            for _unused in range(W_SETS - 1):
                go = started < limit

                @pl.when(go)
                def _():
                    start_expert(started + 1)
                started = jnp.where(go, started + 1, started)
            started_ref[0] = started

        for par in range(W_SETS):
            @pl.when(jnp.logical_and(used, jnp.logical_and(same, s_a % W_SETS == par)))
            def _():
                @pl.when(first_a)
                def _():
                    for cp in weight_copies(s_a, par):
                        cp.wait()
                gather_next()
                emit_previous()
                _expert_rows(cur, 0, PAIR_ROWS, wsets[par], yraws[p], 0)

            @pl.when(jnp.logical_and(used, jnp.logical_and(jnp.logical_not(same), s_a % W_SETS == par)))
            def _():
                @pl.when(first_a)
                def _():
                    for cp in weight_copies(s_a, par):
                        cp.wait()
                for cp in weight_copies(s_b, (par + 1) % W_SETS):
                    cp.wait()
                gather_next()
                emit_previous()
                _expert_rows(cur, 0, MOE_BLOCK, wsets[par], yraws[p], 0)
                _expert_rows(cur, MOE_BLOCK, MOE_BLOCK, wsets[(par + 1) % W_SETS], yraws[p], MOE_BLOCK)

        @pl.when(jnp.logical_not(used))
        def _():
            prev_used = jnp.logical_and(b_a >= 2, b_a - 2 < n_used)

            @pl.when(prev_used)
            def _():
                emit_previous()

            @pl.when(jnp.logical_not(prev_used))
            def _():
                y_ref[out_rows] = jnp.zeros((PAIR_ROWS,) + ROW_TILE, F32)


def _routed_experts(h2p, row_token, seq, dexp, meta, n_blocks, wg, wu, wd):
    d = wg.shape[1]
    de = wg.shape[2]
    ng = n_blocks // FFN_GROUP
    assert n_blocks % FFN_GROUP == 0 and FFN_GROUP == 4
    tok3 = row_token.reshape(ng, FFN_GROUP, MOE_BLOCK)
    idle_step = lambda m: ((m[0] + 1) // 2) // 2 + 1
    smem = lambda imap: pl.BlockSpec((None, FFN_GROUP, MOE_BLOCK), imap, memory_space=pltpu.SMEM)
    grid_spec = pltpu.PrefetchScalarGridSpec(
        num_scalar_prefetch=3,
        grid=(ng + 1,),
        in_specs=[
            smem(lambda i, sq, dx, m: (jnp.minimum(i, ng - 1), 0, 0)),
            smem(lambda i, sq, dx, m: (jnp.minimum(i + 1, ng - 1), 0, 0)),
            pl.BlockSpec(h2p.shape, lambda i, sq, dx, m: (0, 0, 0), pipeline_mode=pl.Buffered(1)),
            pl.BlockSpec(memory_space=pl.ANY),
            pl.BlockSpec(memory_space=pl.ANY),
            pl.BlockSpec(memory_space=pl.ANY),
        ],
        out_specs=pl.BlockSpec((FFN_GROUP * MOE_BLOCK,) + ROW_TILE,
                               lambda i, sq, dx, m: (jnp.minimum(i, idle_step(m)), 0, 0)),
        scratch_shapes=[pltpu.VMEM((PAIR_ROWS * H2P_CHUNKS, LANES), jnp.int32)] * 2 + [
            pltpu.VMEM((PAIR_ROWS, d), F32)] * 2 + [
            pltpu.VMEM((d, de), F32), pltpu.VMEM((d, de), F32), pltpu.VMEM((de, d), F32)] * W_SETS + [
            pltpu.SMEM((1,), jnp.int32), pltpu.SemaphoreType.DMA((W_SETS, 3))],
    )
    return pl.pallas_call(
        _ffn_kernel,
        grid_spec=grid_spec,
        out_shape=jax.ShapeDtypeStruct(((ng + 2) * FFN_GROUP * MOE_BLOCK,) + ROW_TILE, F32),
        compiler_params=_cparams(("arbitrary",)),
        name="routed_experts",
    )(seq, dexp, meta, tok3, tok3, h2p, wg, wu, wd)


COMBINE_TOKENS = 128


def _row_copies(src_hbm, idx_ref, buf, sem):
    return [pltpu.make_async_copy(src_hbm.at[idx_ref[k, j]], buf.at[k, j], sem)
            for k in range(TOP_K) for j in range(COMBINE_TOKENS)]


def _final_kernel(dc_ref, dn_ref, wk_ref, x1_ref, h2_ref, g2_ref, wsg_ref, wsu_ref, wsd_ref, lng_ref, lnb_ref,
                  y_hbm, o_ref, ybuf, sems):
    i = pl.program_id(0)
    nsteps = pl.num_programs(0)
    slot = i % 2

    def issue(d_ref, s):
        for n, cp in enumerate(_row_copies(y_hbm, d_ref, ybuf.at[s], sems.at[s])):
            cp.start(priority=n % 2)

    @pl.when(i == 0)
    def _():
        issue(dc_ref, 0)

    @pl.when(i + 1 < nsteps)
    def _():
        issue(dn_ref, 1 - slot)

    hb = h2_ref[...].astype(BF16)
    g = _dot(hb, wsg_ref[...])
    u = _dot(hb, wsu_ref[...])
    shared = _dot(((g * jax.nn.sigmoid(g)) * u).astype(BF16), wsd_ref[...])
    for cp in _row_copies(y_hbm, dc_ref, ybuf.at[slot], sems.at[slot]):
        cp.wait()
    rows = []
    for j in range(COMBINE_TOKENS):
        acc = ybuf[slot, 0, j] * wk_ref[0, j]
        for k in range(1, TOP_K):
            acc = acc + ybuf[slot, k, j] * wk_ref[k, j]
        rows.append(acc)
    routed = pltpu.einshape("tcl->t(cl)", jnp.stack(rows, axis=0))
    x2 = DN_ALPHA * x1_ref[...] + g2_ref[...] * (routed + shared)
    o_ref[...] = _ln_rows(x2) * lng_ref[...] + lnb_ref[...]


def _combine_final(x1, h2, mod4, dest, wk, y_rows, wsg, wsu, wsd, ln_g, ln_b, seq):
    t, d = x1.shape
    tt = COMBINE_TOKENS
    nsteps = t // tt
    spb = seq // tt
    row = pl.BlockSpec((tt, d), lambda i: (i, 0))
    full = lambda a: pl.BlockSpec(a.shape, lambda i: (0,) * a.ndim)
    ln_g2, ln_b2 = ln_g.reshape(1, d), ln_b.reshape(1, d)
    return pl.pallas_call(
        _final_kernel,
        grid=(nsteps,),
        in_specs=[pl.BlockSpec((TOP_K, tt), lambda i: (0, i), memory_space=pltpu.SMEM),
                  pl.BlockSpec((TOP_K, tt), lambda i: (0, jnp.minimum(i + 1, nsteps - 1)),
                               memory_space=pltpu.SMEM),
                  pl.BlockSpec((TOP_K, tt), lambda i: (0, i), memory_space=pltpu.SMEM),
                  row, row,
                  pl.BlockSpec((None, None, 1, d), lambda i: (i // spb, 5, 0, 0)),
                  full(wsg), full(wsu), full(wsd), full(ln_g2), full(ln_b2),
                  pl.BlockSpec(memory_space=pl.ANY)],
        out_specs=row,
        out_shape=jax.ShapeDtypeStruct((t, d), F32),
        scratch_shapes=[pltpu.VMEM((2, TOP_K, tt) + ROW_TILE, F32),
                        pltpu.SemaphoreType.DMA((2,))],
        compiler_params=_cparams(("arbitrary",)),
        name="shared_combine_ln",
    )(dest, dest, wk, x1, h2, mod4, wsg, wsu, wsd, ln_g2, ln_b2, y_rows)


def kernel(x, c, w_ada, b_ada, w_in, hg_lower_bound, hg_norm_w, rel_bias, w_out, ln1_g, ln1_b, w_router,
           router_bias, w_e_gate, w_e_up, w_e_down, w_sh_gate, w_sh_up, w_sh_down, ln2_g, ln2_b):
    bsz, seq, d = x.shape
    t = bsz * seq
    assert w_ada.shape[0] == DEPTH and seq % (ATT_BRANCHES[-1][0]) == 0
    x2 = x.reshape(t, d)
    bias = _bias_tables(rel_bias)
    for l in range(DEPTH):
        mod4 = _modulation(c, w_ada[l], b_ada[l]).reshape(bsz, 6, 1, d)
        w_in_bf = _cast_bf16(w_in[l], 256)
        hq, hf, hi, hg, *qkv = _in_projection(x2, mod4, w_in_bf, seq, 512)
        nbr = len(ATT_BRANCHES)
        y_hg = _hgrn2(hq, hf, hi, hg, hg_lower_bound, hg_norm_w[l], bsz, seq)
        os_, ls_ = [], []
        for g, (_, dil) in enumerate(ATT_BRANCHES):
            o, lse = _dilated_attention(qkv[g], qkv[nbr + g], qkv[2 * nbr + g], bias[g], bsz, seq, dil)
            os_.append(o)
            ls_.append(lse)
        w_out_bf = _cast_bf16(w_out[l], 256)
        x1, h2, h2p, logits_t = _out_projection(y_hg, os_, ls_, x2, mod4, w_out_bf, ln1_g[l], ln1_b[l],
                                                w_router[l].T, seq, 512)
        eidx, wk, rank, cnt = _route(logits_t, router_bias[l], 256)
        dest, blk_seq, dexp, meta, n_blocks = _dispatch_plan(cnt, eidx, rank, 512)
        row_token = _row_tables(dest, n_blocks * MOE_BLOCK)
        y_rows = _routed_experts(h2p, row_token, blk_seq, dexp, meta, n_blocks,
                                 w_e_gate[l], w_e_up[l], w_e_down[l])
        x2 = _combine_final(x1, h2, mod4, dest, wk, y_rows,
                            _cast_bf16(w_sh_gate[l], 256), _cast_bf16(w_sh_up[l], 256),
                            _cast_bf16(w_sh_down[l], 256), ln2_g[l], ln2_b[l], seq)
    return x2.reshape(bsz, seq, d)
```

```python
import dataclasses
import math

import jax
import jax.numpy as jnp
import numpy as np
from jax import lax
from jax.experimental import pallas as pl
from jax.experimental.pallas import tpu as pltpu
from jax.experimental.pallas import tpu_sc as plsc

HG_HEADS = 4
HG_DK = 128
HG_WIDTH = HG_HEADS * HG_DK
ATT_BRANCHES = ((128, 1), (512, 4), (2048, 16))
ATT_HEADS_PER_BRANCH = 4
ATT_HEAD_DIM = 64
ATT_BW = ATT_HEADS_PER_BRANCH * ATT_HEAD_DIM
ATT_BLOCK = 128
REL_BUCKETS = 32
REL_MAX_DIST = 2048
N_EXPERTS = 256
TOP_K = 8
N_GROUPS = 8
TOPK_GROUPS = 4
ROUTED_SCALE = 2.5
MOE_BLOCK = 128
DEPTH = 1
DN_ALPHA = (2 * DEPTH) ** 0.25
LN_EPS = 1e-5
RMS_EPS = 1e-6

LANES = 128
SUBLANES = 8
VMEM_LIMIT_BYTES = 56 * 1024 * 1024

F32 = jnp.float32
BF16 = jnp.bfloat16
NEG_INF = float("-inf")


def _cparams(sem):
    return pltpu.CompilerParams(dimension_semantics=sem, vmem_limit_bytes=VMEM_LIMIT_BYTES)


def _ln_rows(x):
    mu = jnp.mean(x, axis=-1, keepdims=True)
    xc = x - mu
    var = jnp.mean(xc * xc, axis=-1, keepdims=True)
    return xc * lax.rsqrt(var + LN_EPS)


def _dot(a, b):
    return jnp.dot(a, b, preferred_element_type=F32)


def _dot_nt(a, b):
    return lax.dot_general(a, b, (((1,), (1,)), ((), ())), preferred_element_type=F32)


def _dot_tn(a, b):
    return lax.dot_general(a, b, (((0,), (0,)), ((), ())), preferred_element_type=F32)


def _cast_kernel(w_ref, o_ref):
    o_ref[...] = w_ref[...].astype(o_ref.dtype)


def _cast_bf16(w, rows_per_step):
    r, c = w.shape
    return pl.pallas_call(
        _cast_kernel,
        grid=(r // rows_per_step,),
        in_specs=[pl.BlockSpec((rows_per_step, c), lambda i: (i, 0))],
        out_specs=pl.BlockSpec((rows_per_step, c), lambda i: (i, 0)),
        out_shape=jax.ShapeDtypeStruct((r, c), BF16),
        compiler_params=_cparams(("parallel",)),
        name="cast_bf16",
    )(w)


def _mod_kernel(c_ref, w_ref, b_ref, o_ref):
    c = c_ref[...]
    cond = c * jax.nn.sigmoid(c)
    o_ref[...] = jnp.dot(cond, w_ref[...], preferred_element_type=F32,
                         precision=lax.Precision.HIGHEST) + b_ref[...]


def _modulation(c, w_ada, b_ada):
    bsz, d = c.shape
    n = w_ada.shape[1]
    rows = -(-bsz // SUBLANES) * SUBLANES
    cpad = jnp.zeros((rows, d), F32).at[:bsz].set(c)
    tn = 1024
    out = pl.pallas_call(
        _mod_kernel,
        grid=(n // tn,),
        in_specs=[pl.BlockSpec((rows, d), lambda j: (0, 0)),
                  pl.BlockSpec((d, tn), lambda j: (0, j)),
                  pl.BlockSpec((1, tn), lambda j: (0, j))],
        out_specs=pl.BlockSpec((rows, tn), lambda j: (0, j)),
        out_shape=jax.ShapeDtypeStruct((rows, n), F32),
        compiler_params=_cparams(("parallel",)),
        name="adaln_modulation",
    )(cpad, w_ada, b_ada.reshape(1, n))
    return out[:bsz]


_IN_HG = 4
_IN_ATT = 3 * len(ATT_BRANCHES)


def _inproj_kernel(x_ref, sc_ref, sh_ref, w_ref, *refs):
    outs = refs[:_IN_HG + _IN_ATT]
    scratch = refs[_IN_HG + _IN_ATT:]
    x = x_ref[...]
    h = _ln_rows(x) * (1.0 + sc_ref[...]) + sh_ref[...]
    hb = h.astype(BF16)
    tm = x.shape[0]
    col = 0
    n_scr = 0
    for k, o_ref in enumerate(outs):
        width = HG_WIDTH if k < _IN_HG else ATT_BW
        y = _dot(hb, w_ref[:, col:col + width])
        col += width
        if k < _IN_HG:
            o_ref[...] = y.astype(o_ref.dtype)
            continue
        if k < _IN_HG + len(ATT_BRANCHES):
            y = y * (ATT_HEAD_DIM ** -0.5)
        dil = ATT_BRANCHES[(k - _IN_HG) % len(ATT_BRANCHES)][1]
        if dil == 1:
            o_ref[...] = y.astype(o_ref.dtype)
            continue
        scr = scratch[n_scr]
        n_scr += 1
        for half in range(ATT_BW // LANES):
            scr[half] = y[:, half * LANES:(half + 1) * LANES]
        for r in range(dil):
            for half in range(ATT_BW // LANES):
                c0 = r * ATT_BW + half * LANES
                o_ref[:, c0:c0 + LANES] = scr[half, pl.ds(r, tm // dil, stride=dil), :].astype(o_ref.dtype)


def _in_projection(x2, mod4, w_in_bf, seq, tm):
    t, d = x2.shape
    steps_per_batch = seq // tm
    dils = [dil for _, dil in ATT_BRANCHES] * 3
    shapes = [(t, HG_WIDTH)] * _IN_HG + [(t // dil, dil * ATT_BW) for dil in dils]
    blocks = [(tm, HG_WIDTH)] * _IN_HG + [(tm // dil, dil * ATT_BW) for dil in dils]
    dtypes = [BF16, F32, BF16, BF16] + [BF16] * _IN_ATT
    mod_spec = lambda row: pl.BlockSpec((None, None, 1, d),
                                        lambda i, row=row: (i // steps_per_batch, row, 0, 0))
    outs = pl.pallas_call(
        _inproj_kernel,
        grid=(t // tm,),
        in_specs=[pl.BlockSpec((tm, d), lambda i: (i, 0)),
                  mod_spec(1), mod_spec(0),
                  pl.BlockSpec(w_in_bf.shape, lambda i: (0, 0))],
        out_specs=[pl.BlockSpec(b, lambda i: (i, 0)) for b in blocks],
        out_shape=[jax.ShapeDtypeStruct(s, dt) for s, dt in zip(shapes, dtypes)],
        scratch_shapes=[pltpu.VMEM((ATT_BW // LANES, tm, LANES), F32) for dil in dils if dil > 1],
        compiler_params=_cparams(("parallel",)),
        name="ln_in_projection",
    )(x2, mod4, mod4, w_in_bf)
    return outs


HG_CHUNK = 64
HG_CHUNKS_PER_STEP = 4
HG_SUB = 8
HG_LEVELS = (64, 32, 16)


def _hgrn_chunk(q, z, iv, lb):
    c = HG_CHUNK
    f = lb + (1.0 - lb) * jax.nn.sigmoid(z)
    lf = jnp.log(f)
    kk = (1.0 - lb) * jax.nn.sigmoid(-z)
    r_i = lax.broadcasted_iota(jnp.int32, (c, c), 0)
    c_i = lax.broadcasted_iota(jnp.int32, (c, c), 1)
    tril = (c_i <= r_i).astype(F32)
    b = jnp.dot(tril, lf, preferred_element_type=F32, precision=lax.Precision.HIGHEST)

    row = lax.broadcasted_iota(jnp.int32, (c, HG_DK), 0)
    scores = jnp.zeros((c, c), F32)
    for m in HG_LEVELS:
        nb = c // m
        b3 = b.reshape(nb, m, HG_DK)
        piv = jnp.broadcast_to(b3[:, m // 2 - 1:m // 2, :], (nb, m, HG_DK)).reshape(c, HG_DK)
        second = (row % m) >= (m // 2)
        qt = jnp.where(second, q * jnp.exp(b - piv), 0.0)
        kt = jnp.where(second, 0.0, kk * jnp.exp(piv - b))
        s_m = _dot_nt(qt.astype(BF16), kt.astype(BF16))
        if nb > 1:
            s_m = jnp.where((r_i // m) == (c_i // m), s_m, 0.0)
        scores = scores + s_m
    sub = HG_SUB
    t_i = lax.broadcasted_iota(jnp.int32, (sub, 1), 0)
    lane = lax.broadcasted_iota(jnp.int32, (sub, c), 1)
    diag_rows = []
    for j in range(c // sub):
        qb = q[j * sub:(j + 1) * sub]
        kb = kk[j * sub:(j + 1) * sub]
        bb = b[j * sub:(j + 1) * sub]
        a_j = jnp.zeros((sub, c), F32)
        for s in range(sub):
            w = qb * kb[s:s + 1] * jnp.exp(bb - bb[s:s + 1])
            col = jnp.sum(w, axis=-1, keepdims=True)
            col = jnp.where(t_i >= s, col, 0.0)
            a_j = jnp.where(lane == j * sub + s, col, a_j)
        diag_rows.append(a_j)
    scores = scores + jnp.concatenate(diag_rows, axis=0)

    ivb = iv.astype(BF16)
    intra = _dot(scores.astype(BF16), ivb)
    b_last = b[c - 1:c]
    kdec = (kk * jnp.exp(b_last - b)).astype(BF16)
    return (q * jnp.exp(b)).astype(BF16), intra, jnp.exp(b_last), _dot_tn(ivb, kdec)


def _hgrn_kernel(q_ref, f_ref, i_ref, g_ref, lbp_ref, nw_ref, o_ref, st_ref):
    @pl.when(pl.program_id(1) == 0)
    def _():
        st_ref[...] = jnp.zeros_like(st_ref)

    lbp = lbp_ref[...]
    e = jnp.exp(lbp - jnp.max(lbp, axis=0, keepdims=True))
    lb_all = e[0:1] / jnp.sum(e, axis=0, keepdims=True)
    heads = []
    for h in range(HG_HEADS):
        sl = slice(h * HG_DK, (h + 1) * HG_DK)
        st_t = st_ref[h]
        outs = []
        for n in range(HG_CHUNKS_PER_STEP):
            rows = slice(n * HG_CHUNK, (n + 1) * HG_CHUNK)
            qdec, intra, dec_last, kv = _hgrn_chunk(q_ref[rows, sl].astype(F32), f_ref[rows, sl],
                                                    i_ref[rows, sl].astype(F32), lb_all[:, sl])
            o = _dot_nt(qdec, st_t.astype(BF16)) + intra
            st_t = st_t * dec_last + kv
            outs.append(o * lax.rsqrt(jnp.mean(o * o, axis=-1, keepdims=True) + RMS_EPS))
        st_ref[h] = st_t
        heads.append(jnp.concatenate(outs, axis=0))
    o_all = jnp.concatenate(heads, axis=-1)
    g = g_ref[...].astype(F32)
    o_ref[...] = (o_all * nw_ref[...] * (g * jax.nn.sigmoid(g))).astype(o_ref.dtype)


def _hgrn2(hq, hf, hi, hg, lb_param, norm_w, bsz, seq):
    t = hq.shape[0]
    rows = HG_CHUNK * HG_CHUNKS_PER_STEP
    nc = seq // rows
    tok = lambda b, n: (b * nc + n, 0)
    spec = pl.BlockSpec((rows, HG_WIDTH), tok)
    return pl.pallas_call(
        _hgrn_kernel,
        grid=(bsz, nc),
        in_specs=[spec, spec, spec, spec,
                  pl.BlockSpec(lb_param.shape, lambda b, n: (0, 0)),
                  pl.BlockSpec((1, HG_WIDTH), lambda b, n: (0, 0))],
        out_specs=spec,
        out_shape=jax.ShapeDtypeStruct((t, HG_WIDTH), BF16),
        scratch_shapes=[pltpu.VMEM((HG_HEADS, HG_DK, HG_DK), F32)],
        compiler_params=_cparams(("parallel", "arbitrary")),
        name="hgrn2_scan",
    )(hq, hf, hi, hg, lb_param, norm_w.reshape(1, HG_WIDTH))


def _t5_bucket_np(dist):
    max_exact = REL_BUCKETS // 2
    n = np.maximum(dist, 0)
    nf = np.maximum(n, 1).astype(np.float32)
    large = max_exact + (np.log(nf / np.float32(max_exact)) / np.float32(math.log(REL_MAX_DIST / max_exact))
                         * np.float32(REL_BUCKETS - max_exact)).astype(np.int32)
    large = np.minimum(large, REL_BUCKETS - 1)
    return np.where(n < max_exact, n, large).astype(np.int32)


def _band_tables():
    w = ATT_BLOCK
    qi = np.arange(w)[:, None]
    ki = np.arange(2 * w)[None, :]
    m = w + qi - ki
    band = (m >= 0) & (m <= w)
    buckets = np.stack([_t5_bucket_np(m * dil) for _, dil in ATT_BRANCHES])
    return buckets, band


def _bias_kernel(rb_ref, bucket_ref, o_ref):
    g = pl.program_id(0)
    w = ATT_BLOCK
    bucket = bucket_ref[...]
    qi = lax.broadcasted_iota(jnp.int32, (w, 2 * w), 0)
    ki = lax.broadcasted_iota(jnp.int32, (w, 2 * w), 1)
    m = w + qi - ki
    band = (m >= 0) & (m <= w)
    for h in range(ATT_HEADS_PER_BRANCH):
        acc = jnp.zeros((w, 2 * w), F32)
        for c in range(REL_BUCKETS):
            acc = jnp.where(bucket == c, rb_ref[c, g * ATT_HEADS_PER_BRANCH + h], acc)
        full = jnp.where(band, acc, NEG_INF)
        o_ref[1, h] = full
        o_ref[0, h] = jnp.where(ki >= w, full, NEG_INF)


def _bias_tables(rel_bias):
    buckets, _ = _band_tables()
    g = len(ATT_BRANCHES)
    w = ATT_BLOCK
    return pl.pallas_call(
        _bias_kernel,
        grid=(g,),
        in_specs=[pl.BlockSpec(memory_space=pltpu.SMEM),
                  pl.BlockSpec((None, w, 2 * w), lambda i: (i, 0, 0))],
        out_specs=pl.BlockSpec((None, 2, ATT_HEADS_PER_BRANCH, w, 2 * w), lambda i: (i, 0, 0, 0, 0)),
        out_shape=jax.ShapeDtypeStruct((g, 2, ATT_HEADS_PER_BRANCH, w, 2 * w), F32),
        compiler_params=_cparams(("parallel",)),
        name="rel_bias_tables",
    )(rel_bias, jnp.asarray(buckets))


ATT_BLOCKS_PER_STEP = 4


def _attn_kernel(q_ref, kp_ref, kc_ref, vp_ref, vc_ref, bias_ref, o_ref, lse_ref):
    m = pl.program_id(2)
    w = ATT_BLOCK
    hb = ATT_HEADS_PER_BRANCH
    lane = lax.broadcasted_iota(jnp.int32, (w, ATT_BW), 1) // ATT_HEAD_DIM
    kall = jnp.concatenate([kp_ref[...], kc_ref[...]], axis=0)
    vall = jnp.concatenate([vp_ref[...], vc_ref[...]], axis=0)
    for blk in range(q_ref.shape[0] // w):
        q = q_ref[blk * w:(blk + 1) * w]
        q4 = jnp.concatenate([jnp.where(lane == h, q, jnp.zeros_like(q)) for h in range(hb)], axis=0)
        kk = kall[blk * w:(blk + 2) * w]
        vv = vall[blk * w:(blk + 2) * w]
        s4 = _dot_nt(q4, kk)
        bias = bias_ref[jnp.minimum(m, 1)] if blk == 0 else bias_ref[1]
        s4 = s4 + bias.reshape(hb * w, 2 * w)
        mx = jnp.max(s4, axis=-1, keepdims=True)
        p = jnp.exp(s4 - mx)
        l = jnp.sum(p, axis=-1, keepdims=True)
        o4 = _dot((p / l).astype(vv.dtype), vv)
        lse4 = mx + jnp.log(l)
        o = jnp.zeros((w, ATT_BW), F32)
        lse = jnp.zeros((w, ATT_BW), F32)
        for h in range(hb):
            o = jnp.where(lane == h, o4[h * w:(h + 1) * w], o)
            lse = jnp.where(lane == h, lse4[h * w:(h + 1) * w], lse)
        o_ref[blk * w:(blk + 1) * w] = o.astype(o_ref.dtype)
        lse_ref[blk * w:(blk + 1) * w] = lse


def _dilated_attention(q, k, v, bias_g, bsz, seq, dilation):
    w = ATT_BLOCK
    l = seq // dilation
    pstep = math.gcd(ATT_BLOCKS_PER_STEP, l // w)
    nb = l // (w * pstep)
    view = lambda a: a.reshape(bsz, l, dilation * ATT_BW)
    cur = pl.BlockSpec((None, pstep * w, ATT_BW), lambda b, r, n: (b, n, r))
    prev = pl.BlockSpec((None, w, ATT_BW), lambda b, r, n: (b, jnp.maximum(pstep * n - 1, 0), r))
    o, lse = pl.pallas_call(
        _attn_kernel,
        grid=(bsz, dilation, nb),
        in_specs=[cur, prev, cur, prev, cur,
                  pl.BlockSpec(bias_g.shape, lambda b, r, n: (0, 0, 0, 0))],
        out_specs=[cur, cur],
        out_shape=[jax.ShapeDtypeStruct((bsz, l, dilation * ATT_BW), BF16),
                   jax.ShapeDtypeStruct((bsz, l, dilation * ATT_BW), F32)],
        compiler_params=_cparams(("parallel", "parallel", "arbitrary")),
        name=f"dilated_attention_d{dilation}",
    )(view(q), view(k), view(k), view(v), view(v), bias_g)
    return o.reshape(bsz * l, dilation * ATT_BW), lse.reshape(bsz * l, dilation * ATT_BW)


def _split_bf16(a):
    hi = a.astype(BF16)
    lo = (a - hi.astype(F32)).astype(BF16)
    return hi, lo


H2P_CHUNKS = 4


def _bf16_bits(a):
    u = lax.bitcast_convert_type(a, jnp.uint32)
    return u + jnp.uint32(0x7FFF) + ((u >> 16) & jnp.uint32(1))


def _token_order(ref, scr, dil):
    if dil == 1:
        return ref[...].astype(F32)
    n = ref.shape[0]
    halves = ATT_BW // LANES
    for r in range(dil):
        for half in range(halves):
            c0 = r * ATT_BW + half * LANES
            scr[half, pl.ds(r, n, stride=dil), :] = ref[:, c0:c0 + LANES].astype(F32)
    return jnp.concatenate([scr[half] for half in range(halves)], axis=1)


def _outproj_kernel(yhg_ref, o1_ref, o2_ref, o3_ref, l1_ref, l2_ref, l3_ref, x_ref,
                    g1_ref, sc2_ref, sh2_ref, wout_ref, lng_ref, lnb_ref, wrt_ref,
                    x1_ref, h2_ref, h2p_ref, lgt_ref, *scratch):
    dils = [dil for _, dil in ATT_BRANCHES]
    scr = iter(scratch)
    o1, o2, o3 = [_token_order(r, None if dil == 1 else next(scr), dil)
                  for r, dil in zip((o1_ref, o2_ref, o3_ref), dils)]
    l1, l2, l3 = [_token_order(r, None if dil == 1 else next(scr), dil)
                  for r, dil in zip((l1_ref, l2_ref, l3_ref), dils)]
    mx = jnp.maximum(jnp.maximum(l1, l2), l3)
    e1, e2, e3 = jnp.exp(l1 - mx), jnp.exp(l2 - mx), jnp.exp(l3 - mx)
    den = e1 + e2 + e3
    att = (e1 / den) * o1 + (e2 / den) * o2 + (e3 / den) * o3
    mix = _dot(yhg_ref[...], wout_ref[:HG_WIDTH, :]) + _dot(att.astype(BF16), wout_ref[HG_WIDTH:, :])
    x1 = _ln_rows(DN_ALPHA * x_ref[...] + g1_ref[...] * mix) * lng_ref[...] + lnb_ref[...]
    x1_ref[...] = x1
    h2 = _ln_rows(x1) * (1.0 + sc2_ref[...]) + sh2_ref[...]
    h2_ref[...] = h2
    for cidx in range(H2P_CHUNKS):
        lo = _bf16_bits(h2[:, 2 * LANES * cidx:2 * LANES * cidx + LANES])
        hi = _bf16_bits(h2[:, 2 * LANES * cidx + LANES:2 * LANES * (cidx + 1)])
        word = (lo >> 16) | (hi & jnp.uint32(0xFFFF0000))
        h2p_ref[:, cidx, :] = lax.bitcast_convert_type(word, jnp.int32)
    h_hi, h_lo = _split_bf16(h2)
    w_hi, w_lo = _split_bf16(wrt_ref[...])
    lgt_ref[...] = _dot_nt(w_hi, h_hi) + (_dot_nt(w_hi, h_lo) + _dot_nt(w_lo, h_hi))


def _out_projection(yhg, os_, ls_, x2, mod4, w_out_bf, ln_g, ln_b, w_router_t, seq, tm):
    t, d = x2.shape
    spb = seq // tm
    ne = w_router_t.shape[0]
    row = lambda w: pl.BlockSpec((tm, w), lambda i: (i, 0))
    mod_spec = lambda r: pl.BlockSpec((None, None, 1, d), lambda i, r=r: (i // spb, r, 0, 0))
    full = lambda a: pl.BlockSpec(a.shape, lambda i: (0,) * a.ndim)
    ln_g2, ln_b2 = ln_g.reshape(1, d), ln_b.reshape(1, d)
    dils = [dil for _, dil in ATT_BRANCHES]
    branch = [pl.BlockSpec((tm // dil, dil * ATT_BW), lambda i: (i, 0)) for dil in dils]
    return pl.pallas_call(
        _outproj_kernel,
        grid=(t // tm,),
        in_specs=[row(HG_WIDTH)] + branch + branch + [row(d),
                  mod_spec(2), mod_spec(4), mod_spec(3),
                  full(w_out_bf), full(ln_g2), full(ln_b2), full(w_router_t)],
        out_specs=[row(d), row(d), pl.BlockSpec((tm, H2P_CHUNKS, LANES), lambda i: (i, 0, 0)),
                   pl.BlockSpec((ne, tm), lambda i: (0, i))],
        out_shape=[jax.ShapeDtypeStruct((t, d), F32), jax.ShapeDtypeStruct((t, d), F32),
                   jax.ShapeDtypeStruct((t, H2P_CHUNKS, LANES), jnp.int32),
                   jax.ShapeDtypeStruct((ne, t), F32)],
        scratch_shapes=[pltpu.VMEM((ATT_BW // LANES, tm, LANES), F32) for dil in dils + dils if dil > 1],
        compiler_params=_cparams(("parallel",)),
        name="merge_outproj_ln",
    )(yhg, *os_, *ls_, x2, mod4, mod4, mod4, w_out_bf, ln_g2, ln_b2, w_router_t)


def _argmax_rows(cur, iota, nrows):
    m = jnp.max(cur, axis=0, keepdims=True)
    idx = jnp.min(jnp.where(cur == m, iota, nrows), axis=0, keepdims=True)
    return m, idx, iota == idx


def _route_kernel(lgt_ref, rb_ref, eidx_ref, w_ref, rank_ref, cnt_ref, carry):
    ne = N_EXPERTS
    gsz = ne // N_GROUPS
    tt = lgt_ref.shape[1]

    @pl.when(pl.program_id(0) == 0)
    def _():
        carry[...] = jnp.zeros_like(carry)

    sc = jax.nn.sigmoid(lgt_ref[...])
    biased = sc + rb_ref[...]
    g3 = biased.reshape(N_GROUPS, gsz, tt)
    io3 = lax.broadcasted_iota(jnp.int32, (N_GROUPS, gsz, tt), 1)
    m1 = jnp.max(g3, axis=1, keepdims=True)
    first = jnp.min(jnp.where(g3 == m1, io3, gsz), axis=1, keepdims=True)
    m2 = jnp.max(jnp.where(io3 == first, NEG_INF, g3), axis=1, keepdims=True)
    gs = (m1 + m2).reshape(N_GROUPS, tt)
    io8 = lax.broadcasted_iota(jnp.int32, (N_GROUPS, tt), 0)
    sel = jnp.zeros((N_GROUPS, tt), jnp.int32)
    cur = gs
    for _ in range(TOPK_GROUPS):
        _, _, pick = _argmax_rows(cur, io8, N_GROUPS)
        sel = jnp.where(pick, 1, sel)
        cur = jnp.where(pick, NEG_INF, cur)
    masked = jnp.where(sel.reshape(N_GROUPS, 1, tt) > 0, g3, NEG_INF).reshape(ne, tt)
    ioe = lax.broadcasted_iota(jnp.int32, (ne, tt), 0)
    cur = masked
    idxs, ws, picks = [], [], []
    for _ in range(TOP_K):
        _, idx, pick = _argmax_rows(cur, ioe, ne)
        idxs.append(idx)
        picks.append(pick)
        ws.append(jnp.sum(jnp.where(pick, sc, 0.0), axis=0, keepdims=True))
        cur = jnp.where(pick, NEG_INF, cur)
    wk = jnp.concatenate(ws, axis=0)
    eidx_ref[...] = jnp.concatenate(idxs, axis=0)
    w_ref[...] = wk / jnp.sum(wk, axis=0, keepdims=True) * ROUTED_SCALE
    chosen = jnp.where(cur == NEG_INF, jnp.where(masked == NEG_INF, 0.0, 1.0), 0.0)
    r_i = lax.broadcasted_iota(jnp.int32, (tt, tt), 0)
    c_i = lax.broadcasted_iota(jnp.int32, (tt, tt), 1)
    before = jnp.where(r_i < c_i, 1.0, 0.0).astype(BF16)
    pref = _dot(chosen.astype(BF16), before) + carry[...]
    rank_ref[...] = jnp.concatenate(
        [jnp.sum(jnp.where(p, pref, 0.0), axis=0, keepdims=True) for p in picks], axis=0).astype(jnp.int32)
    carry[...] = carry[...] + jnp.sum(chosen, axis=1, keepdims=True)
    cnt_ref[...] = carry[...]


def _route(logits_t, router_bias, tt):
    ne, t = logits_t.shape
    tok = pl.BlockSpec((TOP_K, tt), lambda i: (0, i))
    return pl.pallas_call(
        _route_kernel,
        grid=(t // tt,),
        in_specs=[pl.BlockSpec((ne, tt), lambda i: (0, i)),
                  pl.BlockSpec((ne, 1), lambda i: (0, 0))],
        out_specs=[tok, tok, tok, pl.BlockSpec((ne, 1), lambda i: (0, 0))],
        out_shape=[jax.ShapeDtypeStruct((TOP_K, t), jnp.int32), jax.ShapeDtypeStruct((TOP_K, t), F32),
                   jax.ShapeDtypeStruct((TOP_K, t), jnp.int32), jax.ShapeDtypeStruct((ne, 1), F32)],
        scratch_shapes=[pltpu.VMEM((ne, 1), F32)],
        compiler_params=_cparams(("arbitrary",)),
        name="router_topk",
    )(logits_t, router_bias.reshape(ne, 1))


def _plan_kernel(cnt_ref, eidx_ref, rank_ref, dest_ref, seq_ref, dexp_ref, meta_ref):
    ne = N_EXPERTS
    tt = eidx_ref.shape[1]
    nblk = seq_ref.shape[1]
    cnt = cnt_ref[...].astype(jnp.int32)
    padded = ((cnt + (MOE_BLOCK - 1)) // MOE_BLOCK) * MOE_BLOCK
    r_i = lax.broadcasted_iota(jnp.int32, (ne, ne), 0)
    c_i = lax.broadcasted_iota(jnp.int32, (ne, ne), 1)
    incl = jnp.where(c_i <= r_i, 1.0, 0.0)
    pend = jnp.dot(incl, jnp.broadcast_to(padded.astype(F32), (ne, LANES)),
                   preferred_element_type=F32, precision=lax.Precision.HIGHEST)[:, 0:1]
    pend = pend.astype(jnp.int32)
    pstart = pend - padded
    ioe = lax.broadcasted_iota(jnp.int32, (ne, tt), 0)
    rows = []
    for k in range(TOP_K):
        sel = ioe == eidx_ref[k:k + 1, :]
        rows.append(jnp.sum(jnp.where(sel, pstart, 0), axis=0, keepdims=True))
    dest_ref[...] = jnp.concatenate(rows, axis=0) + rank_ref[...] + PAIR_ROWS
    blk0 = lax.broadcasted_iota(jnp.int32, (ne, nblk), 1) * MOE_BLOCK
    be = jnp.minimum(jnp.sum(jnp.where(pend <= blk0, 1, 0), axis=0, keepdims=True), ne - 1)
    present = cnt > 0
    strict = jnp.where(c_i < r_i, 1.0, 0.0).astype(BF16)
    sidx = _dot(strict, jnp.broadcast_to(jnp.where(present, 1.0, 0.0), (ne, LANES)).astype(BF16))[:, 0:1]
    sidx = sidx.astype(jnp.int32)
    dexp_ref[...] = jnp.sum(jnp.where(jnp.logical_and(present, sidx == c_i), r_i, 0), axis=0, keepdims=True)
    ioeb = lax.broadcasted_iota(jnp.int32, (ne, nblk), 0)
    nu = jnp.max(pend, axis=0, keepdims=True) // MOE_BLOCK
    nd = jnp.sum(jnp.where(present, 1, 0), axis=0, keepdims=True)
    seq_ref[...] = jnp.minimum(jnp.sum(jnp.where(ioeb == be, sidx, 0), axis=0, keepdims=True), nd - 1)
    lane = lax.broadcasted_iota(jnp.int32, (1, LANES), 1)
    meta_ref[...] = jnp.where(lane == 0, nu, jnp.where(lane == 1, nd, 0))


def _dispatch_plan(cnt, eidx, rank, tt):
    k, t = eidx.shape
    n_blocks = -(-(t * k) // MOE_BLOCK) + N_EXPERTS
    tok = pl.BlockSpec((k, tt), lambda i: (0, i))
    one = lambda n: pl.BlockSpec((1, n), lambda i: (0, 0))
    dest, seq, dexp, meta = pl.pallas_call(
        _plan_kernel,
        grid=(t // tt,),
        in_specs=[pl.BlockSpec(cnt.shape, lambda i: (0, 0)), tok, tok],
        out_specs=[tok, one(n_blocks), one(N_EXPERTS), one(LANES)],
        out_shape=[jax.ShapeDtypeStruct((k, t), jnp.int32), jax.ShapeDtypeStruct((1, n_blocks), jnp.int32),
                   jax.ShapeDtypeStruct((1, N_EXPERTS), jnp.int32), jax.ShapeDtypeStruct((1, LANES), jnp.int32)],
        compiler_params=_cparams(("arbitrary",)),
        name="dispatch_plan",
    )(cnt, eidx, rank)
    return dest, seq.reshape(n_blocks), dexp.reshape(N_EXPERTS), meta.reshape(LANES), n_blocks


SC_CORES = 2
SC_SUBCORES = 16
SC_LANES = 16
SC_CHUNK = 16384
SC_UNROLL = 4


def _row_tables(dest, n_rows):
    k, t = dest.shape
    a = k * t
    nw = SC_CORES * SC_SUBCORES
    per_w = n_rows // nw
    assert n_rows % (nw * SC_LANES) == 0 and t % SC_CHUNK == 0
    mesh = plsc.VectorSubcoreMesh(core_axis_name="c", subcore_axis_name="s")
    cp = pltpu.CompilerParams()
    if "needs_layout_passes" in pltpu.CompilerParams.__dataclass_fields__:
        cp = dataclasses.replace(cp, needs_layout_passes=False)

    def body(dest_hbm, tok_out, dbuf, tloc):
        wid = lax.axis_index("s") * SC_CORES + lax.axis_index("c")
        base = wid * per_w
        shifted = base + PAIR_ROWS

        @pl.loop(0, per_w // SC_LANES)
        def _(i):
            tloc[pl.ds(i * SC_LANES, SC_LANES)] = jnp.zeros((SC_LANES,), jnp.int32)

        lane = lax.iota(jnp.int32, SC_LANES)

        @pl.loop(0, a // SC_CHUNK)
        def _(c):
            pltpu.sync_copy(dest_hbm.at[pl.ds(c * SC_CHUNK, SC_CHUNK)], dbuf)
            tok0 = lax.rem(c * SC_CHUNK, t)

            @pl.loop(0, SC_CHUNK // (SC_LANES * SC_UNROLL))
            def _(j):
                for u in range(SC_UNROLL):
                    off = (j * SC_UNROLL + u) * SC_LANES
                    loc = dbuf[pl.ds(off, SC_LANES)] - shifted
                    mine = jnp.logical_and(loc >= 0, loc < per_w)
                    loc = jnp.where(mine, loc, 0)
                    plsc.store_scatter(tloc, [loc], tok0 + off + lane, mask=mine)

        pltpu.sync_copy(tloc, tok_out.at[pl.ds(base, per_w)])

    fn = pl.kernel(
        body,
        out_type=jax.ShapeDtypeStruct((n_rows,), jnp.int32),
        mesh=mesh,
        scratch_types=[pltpu.VMEM((SC_CHUNK,), jnp.int32), pltpu.VMEM((per_w,), jnp.int32)],
        compiler_params=cp,
        name="row_tables",
    )
    return fn(dest.reshape(a))


ROW_TILE = (SUBLANES, LANES)


FFN_GROUP = 4
PAIR_ROWS = 2 * MOE_BLOCK
W_SETS = 3
GATHER_BATCH = 32


def _gather_rows(h2p_ref, tok_ref, row, buf, base):
    for j0 in range(0, MOE_BLOCK, GATHER_BATCH):
        vals = [h2p_ref[tok_ref[row, j]] for j in range(j0, j0 + GATHER_BATCH)]
        for j, v in zip(range(j0, j0 + GATHER_BATCH), vals):
            buf[pl.ds(H2P_CHUNKS * (base + j), H2P_CHUNKS), :] = v


def _expert_rows(buf, row0, nrows, wset, out, out_row0):
    wg_c, wu_c, wd_c = wset
    parts = []
    for cidx in range(H2P_CHUNKS):
        word = buf[pl.ds(H2P_CHUNKS * row0 + cidx, nrows, stride=H2P_CHUNKS), :]
        parts.append(lax.bitcast_convert_type(word << 16, F32))
        parts.append(lax.bitcast_convert_type(word & jnp.int32(-65536), F32))
    x = jnp.concatenate(parts, axis=1)
    g = _dot(x, wg_c[...])
    u = _dot(x, wu_c[...])
    hm = (g * jax.nn.sigmoid(g)) * u
    out[pl.ds(out_row0, nrows), :] = _dot(hm, wd_c[...])


def _ffn_kernel(seq_ref, dexp_ref, meta_ref, tokc_ref, tokn_ref, h2p_ref, wg_hbm, wu_hbm, wd_hbm,
                y_ref, buf_0, buf_1, yraw_0, yraw_1, *rest):
    i = pl.program_id(0)
    n_used = meta_ref[0]
    n_exp = meta_ref[1]
    nblk = seq_ref.shape[0]
    wsets = tuple(tuple(rest[3 * n:3 * n + 3]) for n in range(W_SETS))
    started_ref, sems = rest[3 * W_SETS:]
    bufs = (buf_0, buf_1)
    yraws = (yraw_0, yraw_1)

    def weight_copies(s, par):
        e = dexp_ref[s]
        return [pltpu.make_async_copy(src.at[e], dst, sems.at[par, n])
                for n, (src, dst) in enumerate(zip((wg_hbm, wu_hbm, wd_hbm), wsets[par]))]

    def start_expert(s):
        for par in range(W_SETS):
            @pl.when(s % W_SETS == par)
            def _():
                for cp in weight_copies(s, par):
                    cp.start()

    @pl.when(i == 0)
    def _():
        _gather_rows(h2p_ref, tokc_ref, 0, buf_0, 0)
        _gather_rows(h2p_ref, tokc_ref, 1, buf_0, MOE_BLOCK)
        yraw_1[...] = jnp.zeros_like(yraw_1)
        start_expert(0)
        started_ref[0] = 0

    for p in range(FFN_GROUP // 2):
        b_a = i * FFN_GROUP + 2 * p
        used = b_a < n_used
        s_a = seq_ref[jnp.minimum(b_a, nblk - 1)]
        s_b = seq_ref[jnp.minimum(b_a + 1, nblk - 1)]
        first_a = jnp.logical_or(b_a == 0, s_a != seq_ref[jnp.clip(b_a - 1, 0, nblk - 1)])
        same = s_a == s_b
        cur, nxt = bufs[p], bufs[1 - p]
        out_rows = pl.ds(p * PAIR_ROWS, PAIR_ROWS)

        def emit_previous():
            y_ref[out_rows] = pltpu.einshape("r(cl)->rcl", yraws[1 - p][...], c=SUBLANES)

        def gather_next():
            if p == 0:
                _gather_rows(h2p_ref, tokc_ref, 2, nxt, 0)
                _gather_rows(h2p_ref, tokc_ref, 3, nxt, MOE_BLOCK)
            else:
                _gather_rows(h2p_ref, tokn_ref, 0, nxt, 0)
                _gather_rows(h2p_ref, tokn_ref, 1, nxt, MOE_BLOCK)

        @pl.when(used)
        def _():
            started = started_ref[0]
            limit = jnp.minimum(s_a + (W_SETS - 1), n_exp - 1)
            for _unused in range(W_SETS - 1):
                go = started < limit

                @pl.when(go)
                def _():
                    start_expert(started + 1)
                started = jnp.where(go, started + 1, started)
            started_ref[0] = started

        for par in range(W_SETS):
            @pl.when(jnp.logical_and(used, jnp.logical_and(same, s_a % W_SETS == par)))
            def _():
                @pl.when(first_a)
                def _():
                    for cp in weight_copies(s_a, par):
                        cp.wait()
                gather_next()
                emit_previous()
                _expert_rows(cur, 0, PAIR_ROWS, wsets[par], yraws[p], 0)

            @pl.when(jnp.logical_and(used, jnp.logical_and(jnp.logical_not(same), s_a % W_SETS == par)))
            def _():
                @pl.when(first_a)
                def _():
                    for cp in weight_copies(s_a, par):
                        cp.wait()
                for cp in weight_copies(s_b, (par + 1) % W_SETS):
                    cp.wait()
                gather_next()
                emit_previous()
                _expert_rows(cur, 0, MOE_BLOCK, wsets[par], yraws[p], 0)
                _expert_rows(cur, MOE_BLOCK, MOE_BLOCK, wsets[(par + 1) % W_SETS], yraws[p], MOE_BLOCK)

        @pl.when(jnp.logical_not(used))
        def _():
            prev_used = jnp.logical_and(b_a >= 2, b_a - 2 < n_used)

            @pl.when(prev_used)
            def _():
                emit_previous()

            @pl.when(jnp.logical_not(prev_used))
            def _():
                y_ref[out_rows] = jnp.zeros((PAIR_ROWS,) + ROW_TILE, F32)


def _routed_experts(h2p, row_token, seq, dexp, meta, n_blocks, wg, wu, wd):
    d = wg.shape[1]
    de = wg.shape[2]
    ng = n_blocks // FFN_GROUP
    assert n_blocks % FFN_GROUP == 0 and FFN_GROUP == 4
    tok3 = row_token.reshape(ng, FFN_GROUP, MOE_BLOCK)
    idle_step = lambda m: ((m[0] + 1) // 2) // 2 + 1
    smem = lambda imap: pl.BlockSpec((None, FFN_GROUP, MOE_BLOCK), imap, memory_space=pltpu.SMEM)
    grid_spec = pltpu.PrefetchScalarGridSpec(
        num_scalar_prefetch=3,
        grid=(ng + 1,),
        in_specs=[
            smem(lambda i, sq, dx, m: (jnp.minimum(i, ng - 1), 0, 0)),
            smem(lambda i, sq, dx, m: (jnp.minimum(i + 1, ng - 1), 0, 0)),
            pl.BlockSpec(h2p.shape, lambda i, sq, dx, m: (0, 0, 0), pipeline_mode=pl.Buffered(1)),
            pl.BlockSpec(memory_space=pl.ANY),
            pl.BlockSpec(memory_space=pl.ANY),
            pl.BlockSpec(memory_space=pl.ANY),
        ],
        out_specs=pl.BlockSpec((FFN_GROUP * MOE_BLOCK,) + ROW_TILE,
                               lambda i, sq, dx, m: (jnp.minimum(i, idle_step(m)), 0, 0)),
        scratch_shapes=[pltpu.VMEM((PAIR_ROWS * H2P_CHUNKS, LANES), jnp.int32)] * 2 + [
            pltpu.VMEM((PAIR_ROWS, d), F32)] * 2 + [
            pltpu.VMEM((d, de), F32), pltpu.VMEM((d, de), F32), pltpu.VMEM((de, d), F32)] * W_SETS + [
            pltpu.SMEM((1,), jnp.int32), pltpu.SemaphoreType.DMA((W_SETS, 3))],
    )
    return pl.pallas_call(
        _ffn_kernel,
        grid_spec=grid_spec,
        out_shape=jax.ShapeDtypeStruct(((ng + 2) * FFN_GROUP * MOE_BLOCK,) + ROW_TILE, F32),
        compiler_params=_cparams(("arbitrary",)),
        name="routed_experts",
    )(seq, dexp, meta, tok3, tok3, h2p, wg, wu, wd)


COMBINE_TOKENS = 128


def _row_copies(src_hbm, idx_ref, buf, sem):
    return [pltpu.make_async_copy(src_hbm.at[idx_ref[k, j]], buf.at[k, j], sem)
            for k in range(TOP_K) for j in range(COMBINE_TOKENS)]


def _final_kernel(dc_ref, dn_ref, wk_ref, x1_ref, h2_ref, g2_ref, wsg_ref, wsu_ref, wsd_ref, lng_ref, lnb_ref,
                  y_hbm, o_ref, ybuf_0, ybuf_1, sems):
    i = pl.program_id(0)
    nsteps = pl.num_programs(0)

    def issue(d_ref, buf, sem):
        for n, cp in enumerate(_row_copies(y_hbm, d_ref, buf, sem)):
            cp.start(priority=n % 2)

    @pl.when(i == 0)
    def _():
        issue(dc_ref, ybuf_0, sems.at[0])

    def step(cur, cur_sem, nxt, nxt_sem):
        for cp in _row_copies(y_hbm, dc_ref, cur, cur_sem):
            cp.wait()
        issue(dn_ref, nxt, nxt_sem)
        hb = h2_ref[...].astype(BF16)
        g = _dot(hb, wsg_ref[...])
        u = _dot(hb, wsu_ref[...])
        shared = _dot(((g * jax.nn.sigmoid(g)) * u).astype(BF16), wsd_ref[...])
        rows = []
        for j in range(COMBINE_TOKENS):
            acc = cur[0, j] * wk_ref[0, j]
            for k in range(1, TOP_K):
                acc = acc + cur[k, j] * wk_ref[k, j]
            rows.append(acc)
        routed = pltpu.einshape("tcl->t(cl)", jnp.stack(rows, axis=0))
        x2 = DN_ALPHA * x1_ref[...] + g2_ref[...] * (routed + shared)
        o_ref[...] = _ln_rows(x2) * lng_ref[...] + lnb_ref[...]

        @pl.when(i == nsteps - 1)
        def _():
            for cp in _row_copies(y_hbm, dn_ref, nxt, nxt_sem):
                cp.wait()

    @pl.when(i % 2 == 0)
    def _():
        step(ybuf_0, sems.at[0], ybuf_1, sems.at[1])

    @pl.when(i % 2 == 1)
    def _():
        step(ybuf_1, sems.at[1], ybuf_0, sems.at[0])


def _combine_final(x1, h2, mod4, dest, wk, y_rows, wsg, wsu, wsd, ln_g, ln_b, seq):
    t, d = x1.shape
    tt = COMBINE_TOKENS
    nsteps = t // tt
    spb = seq // tt
    row = pl.BlockSpec((tt, d), lambda i: (i, 0))
    full = lambda a: pl.BlockSpec(a.shape, lambda i: (0,) * a.ndim)
    ln_g2, ln_b2 = ln_g.reshape(1, d), ln_b.reshape(1, d)
    return pl.pallas_call(
        _final_kernel,
        grid=(nsteps,),
        in_specs=[pl.BlockSpec((TOP_K, tt), lambda i: (0, i), memory_space=pltpu.SMEM),
                  pl.BlockSpec((TOP_K, tt), lambda i: (0, jnp.minimum(i + 1, nsteps - 1)),
                               memory_space=pltpu.SMEM),
                  pl.BlockSpec((TOP_K, tt), lambda i: (0, i), memory_space=pltpu.SMEM),
                  row, row,
                  pl.BlockSpec((None, None, 1, d), lambda i: (i // spb, 5, 0, 0)),
                  full(wsg), full(wsu), full(wsd), full(ln_g2), full(ln_b2),
                  pl.BlockSpec(memory_space=pl.ANY)],
        out_specs=row,
        out_shape=jax.ShapeDtypeStruct((t, d), F32),
        scratch_shapes=[pltpu.VMEM((TOP_K, tt) + ROW_TILE, F32), pltpu.VMEM((TOP_K, tt) + ROW_TILE, F32),
                        pltpu.SemaphoreType.DMA((2,))],
        compiler_params=_cparams(("arbitrary",)),
        name="shared_combine_ln",
    )(dest, dest, wk, x1, h2, mod4, wsg, wsu, wsd, ln_g2, ln_b2, y_rows)


def kernel(x, c, w_ada, b_ada, w_in, hg_lower_bound, hg_norm_w, rel_bias, w_out, ln1_g, ln1_b, w_router,
           router_bias, w_e_gate, w_e_up, w_e_down, w_sh_gate, w_sh_up, w_sh_down, ln2_g, ln2_b):
    bsz, seq, d = x.shape
    t = bsz * seq
    assert w_ada.shape[0] == DEPTH and seq % (ATT_BRANCHES[-1][0]) == 0
    x2 = x.reshape(t, d)
    bias = _bias_tables(rel_bias)
    for l in range(DEPTH):
        mod4 = _modulation(c, w_ada[l], b_ada[l]).reshape(bsz, 6, 1, d)
        w_in_bf = _cast_bf16(w_in[l], 256)
        hq, hf, hi, hg, *qkv = _in_projection(x2, mod4, w_in_bf, seq, 512)
        nbr = len(ATT_BRANCHES)
        y_hg = _hgrn2(hq, hf, hi, hg, hg_lower_bound, hg_norm_w[l], bsz, seq)
        os_, ls_ = [], []
        for g, (_, dil) in enumerate(ATT_BRANCHES):
            o, lse = _dilated_attention(qkv[g], qkv[nbr + g], qkv[2 * nbr + g], bias[g], bsz, seq, dil)
            os_.append(o)
            ls_.append(lse)
        w_out_bf = _cast_bf16(w_out[l], 256)
        x1, h2, h2p, logits_t = _out_projection(y_hg, os_, ls_, x2, mod4, w_out_bf, ln1_g[l], ln1_b[l],
                                                w_router[l].T, seq, 512)
        eidx, wk, rank, cnt = _route(logits_t, router_bias[l], 256)
        dest, blk_seq, dexp, meta, n_blocks = _dispatch_plan(cnt, eidx, rank, 512)
        row_token = _row_tables(dest, n_blocks * MOE_BLOCK)
        y_rows = _routed_experts(h2p, row_token, blk_seq, dexp, meta, n_blocks,
                                 w_e_gate[l], w_e_up[l], w_e_down[l])
        x2 = _combine_final(x1, h2, mod4, dest, wk, y_rows,
                            _cast_bf16(w_sh_gate[l], 256), _cast_bf16(w_sh_up[l], 256),
                            _cast_bf16(w_sh_down[l], 256), ln2_g[l], ln2_b[l], seq)
    return x2.reshape(bsz, seq, d)
```

```python
import dataclasses
import math

import jax
import jax.numpy as jnp
import numpy as np
from jax import lax
from jax.experimental import pallas as pl
from jax.experimental.pallas import tpu as pltpu
from jax.experimental.pallas import tpu_sc as plsc

HG_HEADS = 4
HG_DK = 128
HG_WIDTH = HG_HEADS * HG_DK
ATT_BRANCHES = ((128, 1), (512, 4), (2048, 16))
ATT_HEADS_PER_BRANCH = 4
ATT_HEAD_DIM = 64
ATT_BW = ATT_HEADS_PER_BRANCH * ATT_HEAD_DIM
ATT_BLOCK = 128
REL_BUCKETS = 32
REL_MAX_DIST = 2048
N_EXPERTS = 256
TOP_K = 8
N_GROUPS = 8
TOPK_GROUPS = 4
ROUTED_SCALE = 2.5
MOE_BLOCK = 128
DEPTH = 1
DN_ALPHA = (2 * DEPTH) ** 0.25
LN_EPS = 1e-5
RMS_EPS = 1e-6

LANES = 128
SUBLANES = 8
VMEM_LIMIT_BYTES = 56 * 1024 * 1024

F32 = jnp.float32
BF16 = jnp.bfloat16
NEG_INF = float("-inf")


def _cparams(sem):
    return pltpu.CompilerParams(dimension_semantics=sem, vmem_limit_bytes=VMEM_LIMIT_BYTES)


def _ln_rows(x):
    mu = jnp.mean(x, axis=-1, keepdims=True)
    xc = x - mu
    var = jnp.mean(xc * xc, axis=-1, keepdims=True)
    return xc * lax.rsqrt(var + LN_EPS)


def _dot(a, b):
    return jnp.dot(a, b, preferred_element_type=F32)


def _dot_nt(a, b):
    return lax.dot_general(a, b, (((1,), (1,)), ((), ())), preferred_element_type=F32)


def _dot_tn(a, b):
    return lax.dot_general(a, b, (((0,), (0,)), ((), ())), preferred_element_type=F32)


def _cast_kernel(w_ref, o_ref):
    o_ref[...] = w_ref[...].astype(o_ref.dtype)


def _cast_bf16(w, rows_per_step):
    r, c = w.shape
    return pl.pallas_call(
        _cast_kernel,
        grid=(r // rows_per_step,),
        in_specs=[pl.BlockSpec((rows_per_step, c), lambda i: (i, 0))],
        out_specs=pl.BlockSpec((rows_per_step, c), lambda i: (i, 0)),
        out_shape=jax.ShapeDtypeStruct((r, c), BF16),
        compiler_params=_cparams(("parallel",)),
        name="cast_bf16",
    )(w)


def _mod_kernel(c_ref, w_ref, b_ref, o_ref):
    c = c_ref[...]
    cond = c * jax.nn.sigmoid(c)
    o_ref[...] = jnp.dot(cond, w_ref[...], preferred_element_type=F32,
                         precision=lax.Precision.HIGHEST) + b_ref[...]


def _modulation(c, w_ada, b_ada):
    bsz, d = c.shape
    n = w_ada.shape[1]
    rows = -(-bsz // SUBLANES) * SUBLANES
    cpad = jnp.zeros((rows, d), F32).at[:bsz].set(c)
    tn = 1024
    out = pl.pallas_call(
        _mod_kernel,
        grid=(n // tn,),
        in_specs=[pl.BlockSpec((rows, d), lambda j: (0, 0)),
                  pl.BlockSpec((d, tn), lambda j: (0, j)),
                  pl.BlockSpec((1, tn), lambda j: (0, j))],
        out_specs=pl.BlockSpec((rows, tn), lambda j: (0, j)),
        out_shape=jax.ShapeDtypeStruct((rows, n), F32),
        compiler_params=_cparams(("parallel",)),
        name="adaln_modulation",
    )(cpad, w_ada, b_ada.reshape(1, n))
    return out[:bsz]


_IN_HG = 4
_IN_ATT = 3 * len(ATT_BRANCHES)


def _inproj_kernel(x_ref, sc_ref, sh_ref, w_ref, *refs):
    outs = refs[:_IN_HG + _IN_ATT]
    scratch = refs[_IN_HG + _IN_ATT:]
    x = x_ref[...]
    h = _ln_rows(x) * (1.0 + sc_ref[...]) + sh_ref[...]
    hb = h.astype(BF16)
    tm = x.shape[0]
    col = 0
    n_scr = 0
    for k, o_ref in enumerate(outs):
        width = HG_WIDTH if k < _IN_HG else ATT_BW
        y = _dot(hb, w_ref[:, col:col + width])
        col += width
        if k < _IN_HG:
            o_ref[...] = y.astype(o_ref.dtype)
            continue
        if k < _IN_HG + len(ATT_BRANCHES):
            y = y * (ATT_HEAD_DIM ** -0.5)
        dil = ATT_BRANCHES[(k - _IN_HG) % len(ATT_BRANCHES)][1]
        if dil == 1:
            o_ref[...] = y.astype(o_ref.dtype)
            continue
        scr = scratch[n_scr]
        n_scr += 1
        for half in range(ATT_BW // LANES):
            scr[half] = y[:, half * LANES:(half + 1) * LANES]
        for r in range(dil):
            for half in range(ATT_BW // LANES):
                c0 = r * ATT_BW + half * LANES
                o_ref[:, c0:c0 + LANES] = scr[half, pl.ds(r, tm // dil, stride=dil), :].astype(o_ref.dtype)


def _in_projection(x2, mod4, w_in_bf, seq, tm):
    t, d = x2.shape
    steps_per_batch = seq // tm
    dils = [dil for _, dil in ATT_BRANCHES] * 3
    shapes = [(t, HG_WIDTH)] * _IN_HG + [(t // dil, dil * ATT_BW) for dil in dils]
    blocks = [(tm, HG_WIDTH)] * _IN_HG + [(tm // dil, dil * ATT_BW) for dil in dils]
    dtypes = [BF16, F32, BF16, BF16] + [BF16] * _IN_ATT
    mod_spec = lambda row: pl.BlockSpec((None, None, 1, d),
                                        lambda i, row=row: (i // steps_per_batch, row, 0, 0))
    outs = pl.pallas_call(
        _inproj_kernel,
        grid=(t // tm,),
        in_specs=[pl.BlockSpec((tm, d), lambda i: (i, 0)),
                  mod_spec(1), mod_spec(0),
                  pl.BlockSpec(w_in_bf.shape, lambda i: (0, 0))],
        out_specs=[pl.BlockSpec(b, lambda i: (i, 0)) for b in blocks],
        out_shape=[jax.ShapeDtypeStruct(s, dt) for s, dt in zip(shapes, dtypes)],
        scratch_shapes=[pltpu.VMEM((ATT_BW // LANES, tm, LANES), F32) for dil in dils if dil > 1],
        compiler_params=_cparams(("parallel",)),
        name="ln_in_projection",
    )(x2, mod4, mod4, w_in_bf)
    return outs


HG_CHUNK = 64
HG_CHUNKS_PER_STEP = 4
HG_SUB = 8
HG_LEVELS = (64, 32, 16)
LOG2E = 1.4426950408889634


def _hgrn_chunk(q, z, iv, lb):
    c = HG_CHUNK
    f = lb + (1.0 - lb) * jax.nn.sigmoid(z)
    lf = jnp.log(f)
    kk = (1.0 - lb) * jax.nn.sigmoid(-z)
    r_i = lax.broadcasted_iota(jnp.int32, (c, c), 0)
    c_i = lax.broadcasted_iota(jnp.int32, (c, c), 1)
    tril = (c_i <= r_i).astype(F32)
    b = jnp.dot(tril, lf, preferred_element_type=F32, precision=lax.Precision.HIGHEST)
    bl = b * LOG2E

    row = lax.broadcasted_iota(jnp.int32, (c, HG_DK), 0)
    scores = jnp.zeros((c, c), F32)
    for m in HG_LEVELS:
        nb = c // m
        b3 = bl.reshape(nb, m, HG_DK)
        piv = jnp.broadcast_to(b3[:, m // 2 - 1:m // 2, :], (nb, m, HG_DK)).reshape(c, HG_DK)
        second = (row % m) >= (m // 2)
        qt = jnp.where(second, q * jnp.exp2(bl - piv), 0.0)
        kt = jnp.where(second, 0.0, kk * jnp.exp2(piv - bl))
        s_m = _dot_nt(qt.astype(BF16), kt.astype(BF16))
        if nb > 1:
            s_m = jnp.where((r_i // m) == (c_i // m), s_m, 0.0)
        scores = scores + s_m
    sub = HG_SUB
    t_i = lax.broadcasted_iota(jnp.int32, (sub, 1), 0)
    lane = lax.broadcasted_iota(jnp.int32, (sub, c), 1)
    diag_rows = []
    for j in range(c // sub):
        qb = q[j * sub:(j + 1) * sub]
        kb = kk[j * sub:(j + 1) * sub]
        bb = bl[j * sub:(j + 1) * sub]
        a_j = jnp.zeros((sub, c), F32)
        for s in range(sub):
            w = qb * kb[s:s + 1] * jnp.exp2(bb - bb[s:s + 1])
            a_j = jnp.where(lane == j * sub + s, jnp.sum(w, axis=-1, keepdims=True), a_j)
        diag_rows.append(jnp.where(lane - j * sub <= t_i, a_j, 0.0))
    scores = scores + jnp.concatenate(diag_rows, axis=0)

    ivb = iv.astype(BF16)
    intra = _dot(scores.astype(BF16), ivb)
    b_last = bl[c - 1:c]
    kdec = (kk * jnp.exp2(b_last - bl)).astype(BF16)
    return (q * jnp.exp2(bl)).astype(BF16), intra, jnp.exp2(b_last), _dot_tn(ivb, kdec)


def _hgrn_kernel(q_ref, f_ref, i_ref, g_ref, lbp_ref, nw_ref, o_ref, st_ref):
    @pl.when(pl.program_id(1) == 0)
    def _():
        st_ref[...] = jnp.zeros_like(st_ref)

    lbp = lbp_ref[...]
    e = jnp.exp(lbp - jnp.max(lbp, axis=0, keepdims=True))
    lb_all = e[0:1] / jnp.sum(e, axis=0, keepdims=True)
    heads = []
    for h in range(HG_HEADS):
        sl = slice(h * HG_DK, (h + 1) * HG_DK)
        st_t = st_ref[h]
        outs = []
        for n in range(HG_CHUNKS_PER_STEP):
            rows = slice(n * HG_CHUNK, (n + 1) * HG_CHUNK)
            qdec, intra, dec_last, kv = _hgrn_chunk(q_ref[rows, sl].astype(F32), f_ref[rows, sl],
                                                    i_ref[rows, sl].astype(F32), lb_all[:, sl])
            o = _dot_nt(qdec, st_t.astype(BF16)) + intra
            st_t = st_t * dec_last + kv
            outs.append(o * lax.rsqrt(jnp.mean(o * o, axis=-1, keepdims=True) + RMS_EPS))
        st_ref[h] = st_t
        heads.append(jnp.concatenate(outs, axis=0))
    o_all = jnp.concatenate(heads, axis=-1)
    g = g_ref[...].astype(F32)
    o_ref[...] = (o_all * nw_ref[...] * (g * jax.nn.sigmoid(g))).astype(o_ref.dtype)


def _hgrn2(hq, hf, hi, hg, lb_param, norm_w, bsz, seq):
    t = hq.shape[0]
    rows = HG_CHUNK * HG_CHUNKS_PER_STEP
    nc = seq // rows
    tok = lambda b, n: (b * nc + n, 0)
    spec = pl.BlockSpec((rows, HG_WIDTH), tok)
    return pl.pallas_call(
        _hgrn_kernel,
        grid=(bsz, nc),
        in_specs=[spec, spec, spec, spec,
                  pl.BlockSpec(lb_param.shape, lambda b, n: (0, 0)),
                  pl.BlockSpec((1, HG_WIDTH), lambda b, n: (0, 0))],
        out_specs=spec,
        out_shape=jax.ShapeDtypeStruct((t, HG_WIDTH), BF16),
        scratch_shapes=[pltpu.VMEM((HG_HEADS, HG_DK, HG_DK), F32)],
        compiler_params=_cparams(("parallel", "arbitrary")),
        name="hgrn2_scan",
    )(hq, hf, hi, hg, lb_param, norm_w.reshape(1, HG_WIDTH))


def _t5_bucket_np(dist):
    max_exact = REL_BUCKETS // 2
    n = np.maximum(dist, 0)
    nf = np.maximum(n, 1).astype(np.float32)
    large = max_exact + (np.log(nf / np.float32(max_exact)) / np.float32(math.log(REL_MAX_DIST / max_exact))
                         * np.float32(REL_BUCKETS - max_exact)).astype(np.int32)
    large = np.minimum(large, REL_BUCKETS - 1)
    return np.where(n < max_exact, n, large).astype(np.int32)


def _band_tables():
    w = ATT_BLOCK
    qi = np.arange(w)[:, None]
    ki = np.arange(2 * w)[None, :]
    m = w + qi - ki
    band = (m >= 0) & (m <= w)
    buckets = np.stack([_t5_bucket_np(m * dil) for _, dil in ATT_BRANCHES])
    return buckets, band


def _bias_kernel(rb_ref, bucket_ref, o_ref):
    g = pl.program_id(0)
    w = ATT_BLOCK
    bucket = bucket_ref[...]
    qi = lax.broadcasted_iota(jnp.int32, (w, 2 * w), 0)
    ki = lax.broadcasted_iota(jnp.int32, (w, 2 * w), 1)
    m = w + qi - ki
    band = (m >= 0) & (m <= w)
    for h in range(ATT_HEADS_PER_BRANCH):
        acc = jnp.zeros((w, 2 * w), F32)
        for c in range(REL_BUCKETS):
            acc = jnp.where(bucket == c, rb_ref[c, g * ATT_HEADS_PER_BRANCH + h], acc)
        full = jnp.where(band, acc, NEG_INF)
        o_ref[1, h] = full
        o_ref[0, h] = jnp.where(ki >= w, full, NEG_INF)


def _bias_tables(rel_bias):
    buckets, _ = _band_tables()
    g = len(ATT_BRANCHES)
    w = ATT_BLOCK
    return pl.pallas_call(
        _bias_kernel,
        grid=(g,),
        in_specs=[pl.BlockSpec(memory_space=pltpu.SMEM),
                  pl.BlockSpec((None, w, 2 * w), lambda i: (i, 0, 0))],
        out_specs=pl.BlockSpec((None, 2, ATT_HEADS_PER_BRANCH, w, 2 * w), lambda i: (i, 0, 0, 0, 0)),
        out_shape=jax.ShapeDtypeStruct((g, 2, ATT_HEADS_PER_BRANCH, w, 2 * w), F32),
        compiler_params=_cparams(("parallel",)),
        name="rel_bias_tables",
    )(rel_bias, jnp.asarray(buckets))


ATT_BLOCKS_PER_STEP = 4


def _attn_kernel(q_ref, kp_ref, kc_ref, vp_ref, vc_ref, bias_ref, o_ref, lse_ref):
    m = pl.program_id(2)
    w = ATT_BLOCK
    hb = ATT_HEADS_PER_BRANCH
    lane = lax.broadcasted_iota(jnp.int32, (w, ATT_BW), 1) // ATT_HEAD_DIM
    kall = jnp.concatenate([kp_ref[...], kc_ref[...]], axis=0)
    vall = jnp.concatenate([vp_ref[...], vc_ref[...]], axis=0)
    for blk in range(q_ref.shape[0] // w):
        q = q_ref[blk * w:(blk + 1) * w]
        q4 = jnp.concatenate([jnp.where(lane == h, q, jnp.zeros_like(q)) for h in range(hb)], axis=0)
        kk = kall[blk * w:(blk + 2) * w]
        vv = vall[blk * w:(blk + 2) * w]
        s4 = _dot_nt(q4, kk)
        bias = bias_ref[jnp.minimum(m, 1)] if blk == 0 else bias_ref[1]
        s4 = s4 + bias.reshape(hb * w, 2 * w)
        mx = jnp.max(s4, axis=-1, keepdims=True)
        p = jnp.exp(s4 - mx)
        l = jnp.sum(p, axis=-1, keepdims=True)
        o4 = _dot((p / l).astype(vv.dtype), vv)
        lse4 = mx + jnp.log(l)
        o = jnp.zeros((w, ATT_BW), F32)
        lse = jnp.zeros((w, ATT_BW), F32)
        for h in range(hb):
            o = jnp.where(lane == h, o4[h * w:(h + 1) * w], o)
            lse = jnp.where(lane == h, lse4[h * w:(h + 1) * w], lse)
        o_ref[blk * w:(blk + 1) * w] = o.astype(o_ref.dtype)
        lse_ref[blk * w:(blk + 1) * w] = lse


def _dilated_attention(q, k, v, bias_g, bsz, seq, dilation):
    w = ATT_BLOCK
    l = seq // dilation
    pstep = math.gcd(ATT_BLOCKS_PER_STEP, l // w)
    nb = l // (w * pstep)
    view = lambda a: a.reshape(bsz, l, dilation * ATT_BW)
    cur = pl.BlockSpec((None, pstep * w, ATT_BW), lambda b, r, n: (b, n, r))
    prev = pl.BlockSpec((None, w, ATT_BW), lambda b, r, n: (b, jnp.maximum(pstep * n - 1, 0), r))
    o, lse = pl.pallas_call(
        _attn_kernel,
        grid=(bsz, dilation, nb),
        in_specs=[cur, prev, cur, prev, cur,
                  pl.BlockSpec(bias_g.shape, lambda b, r, n: (0, 0, 0, 0))],
        out_specs=[cur, cur],
        out_shape=[jax.ShapeDtypeStruct((bsz, l, dilation * ATT_BW), BF16),
                   jax.ShapeDtypeStruct((bsz, l, dilation * ATT_BW), F32)],
        compiler_params=_cparams(("parallel", "parallel", "arbitrary")),
        name=f"dilated_attention_d{dilation}",
    )(view(q), view(k), view(k), view(v), view(v), bias_g)
    return o.reshape(bsz * l, dilation * ATT_BW), lse.reshape(bsz * l, dilation * ATT_BW)


def _split_bf16(a):
    hi = a.astype(BF16)
    lo = (a - hi.astype(F32)).astype(BF16)
    return hi, lo


H2P_CHUNKS = 4


def _bf16_bits(a):
    u = lax.bitcast_convert_type(a, jnp.uint32)
    return u + jnp.uint32(0x7FFF) + ((u >> 16) & jnp.uint32(1))


def _token_order(ref, scr, dil):
    if dil == 1:
        return ref[...].astype(F32)
    n = ref.shape[0]
    halves = ATT_BW // LANES
    for r in range(dil):
        for half in range(halves):
            c0 = r * ATT_BW + half * LANES
            scr[half, pl.ds(r, n, stride=dil), :] = ref[:, c0:c0 + LANES].astype(F32)
    return jnp.concatenate([scr[half] for half in range(halves)], axis=1)


def _outproj_kernel(yhg_ref, o1_ref, o2_ref, o3_ref, l1_ref, l2_ref, l3_ref, x_ref,
                    g1_ref, sc2_ref, sh2_ref, wout_ref, lng_ref, lnb_ref, wrt_ref,
                    x1_ref, h2_ref, h2p_ref, lgt_ref, *scratch):
    dils = [dil for _, dil in ATT_BRANCHES]
    scr = iter(scratch)
    o1, o2, o3 = [_token_order(r, None if dil == 1 else next(scr), dil)
                  for r, dil in zip((o1_ref, o2_ref, o3_ref), dils)]
    l1, l2, l3 = [_token_order(r, None if dil == 1 else next(scr), dil)
                  for r, dil in zip((l1_ref, l2_ref, l3_ref), dils)]
    mx = jnp.maximum(jnp.maximum(l1, l2), l3)
    e1, e2, e3 = jnp.exp(l1 - mx), jnp.exp(l2 - mx), jnp.exp(l3 - mx)
    den = e1 + e2 + e3
    att = (e1 / den) * o1 + (e2 / den) * o2 + (e3 / den) * o3
    mix = _dot(yhg_ref[...], wout_ref[:HG_WIDTH, :]) + _dot(att.astype(BF16), wout_ref[HG_WIDTH:, :])
    x1 = _ln_rows(DN_ALPHA * x_ref[...] + g1_ref[...] * mix) * lng_ref[...] + lnb_ref[...]
    x1_ref[...] = x1
    h2 = _ln_rows(x1) * (1.0 + sc2_ref[...]) + sh2_ref[...]
    h2_ref[...] = h2
    for cidx in range(H2P_CHUNKS):
        lo = _bf16_bits(h2[:, 2 * LANES * cidx:2 * LANES * cidx + LANES])
        hi = _bf16_bits(h2[:, 2 * LANES * cidx + LANES:2 * LANES * (cidx + 1)])
        word = (lo >> 16) | (hi & jnp.uint32(0xFFFF0000))
        h2p_ref[:, cidx, :] = lax.bitcast_convert_type(word, jnp.int32)
    h_hi, h_lo = _split_bf16(h2)
    w_hi, w_lo = _split_bf16(wrt_ref[...])
    lgt_ref[...] = _dot_nt(w_hi, h_hi) + (_dot_nt(w_hi, h_lo) + _dot_nt(w_lo, h_hi))


def _out_projection(yhg, os_, ls_, x2, mod4, w_out_bf, ln_g, ln_b, w_router_t, seq, tm):
    t, d = x2.shape
    spb = seq // tm
    ne = w_router_t.shape[0]
    row = lambda w: pl.BlockSpec((tm, w), lambda i: (i, 0))
    mod_spec = lambda r: pl.BlockSpec((None, None, 1, d), lambda i, r=r: (i // spb, r, 0, 0))
    full = lambda a: pl.BlockSpec(a.shape, lambda i: (0,) * a.ndim)
    ln_g2, ln_b2 = ln_g.reshape(1, d), ln_b.reshape(1, d)
    dils = [dil for _, dil in ATT_BRANCHES]
    branch = [pl.BlockSpec((tm // dil, dil * ATT_BW), lambda i: (i, 0)) for dil in dils]
    return pl.pallas_call(
        _outproj_kernel,
        grid=(t // tm,),
        in_specs=[row(HG_WIDTH)] + branch + branch + [row(d),
                  mod_spec(2), mod_spec(4), mod_spec(3),
                  full(w_out_bf), full(ln_g2), full(ln_b2), full(w_router_t)],
        out_specs=[row(d), row(d), pl.BlockSpec((tm, H2P_CHUNKS, LANES), lambda i: (i, 0, 0)),
                   pl.BlockSpec((ne, tm), lambda i: (0, i))],
        out_shape=[jax.ShapeDtypeStruct((t, d), F32), jax.ShapeDtypeStruct((t, d), F32),
                   jax.ShapeDtypeStruct((t, H2P_CHUNKS, LANES), jnp.int32),
                   jax.ShapeDtypeStruct((ne, t), F32)],
        scratch_shapes=[pltpu.VMEM((ATT_BW // LANES, tm, LANES), F32) for dil in dils + dils if dil > 1],
        compiler_params=_cparams(("parallel",)),
        name="merge_outproj_ln",
    )(yhg, *os_, *ls_, x2, mod4, mod4, mod4, w_out_bf, ln_g2, ln_b2, w_router_t)


def _argmax_rows(cur, iota, nrows):
    m = jnp.max(cur, axis=0, keepdims=True)
    idx = jnp.min(jnp.where(cur == m, iota, nrows), axis=0, keepdims=True)
    return m, idx, iota == idx


def _route_kernel(lgt_ref, rb_ref, eidx_ref, w_ref, rank_ref, cnt_ref, carry):
    ne = N_EXPERTS
    gsz = ne // N_GROUPS
    tt = lgt_ref.shape[1]

    @pl.when(pl.program_id(0) == 0)
    def _():
        carry[...] = jnp.zeros_like(carry)

    sc = jax.nn.sigmoid(lgt_ref[...])
    biased = sc + rb_ref[...]
    g3 = biased.reshape(N_GROUPS, gsz, tt)
    io3 = lax.broadcasted_iota(jnp.int32, (N_GROUPS, gsz, tt), 1)
    m1 = jnp.max(g3, axis=1, keepdims=True)
    first = jnp.min(jnp.where(g3 == m1, io3, gsz), axis=1, keepdims=True)
    m2 = jnp.max(jnp.where(io3 == first, NEG_INF, g3), axis=1, keepdims=True)
    gs = (m1 + m2).reshape(N_GROUPS, tt)
    io8 = lax.broadcasted_iota(jnp.int32, (N_GROUPS, tt), 0)
    sel = jnp.zeros((N_GROUPS, tt), jnp.int32)
    cur = gs
    for _ in range(TOPK_GROUPS):
        _, _, pick = _argmax_rows(cur, io8, N_GROUPS)
        sel = jnp.where(pick, 1, sel)
        cur = jnp.where(pick, NEG_INF, cur)
    masked = jnp.where(sel.reshape(N_GROUPS, 1, tt) > 0, g3, NEG_INF).reshape(ne, tt)
    ioe = lax.broadcasted_iota(jnp.int32, (ne, tt), 0)
    cur = masked
    idxs, ws, picks = [], [], []
    for _ in range(TOP_K):
        _, idx, pick = _argmax_rows(cur, ioe, ne)
        idxs.append(idx)
        picks.append(pick)
        ws.append(jnp.sum(jnp.where(pick, sc, 0.0), axis=0, keepdims=True))
        cur = jnp.where(pick, NEG_INF, cur)
    wk = jnp.concatenate(ws, axis=0)
    eidx_ref[...] = jnp.concatenate(idxs, axis=0)
    w_ref[...] = wk / jnp.sum(wk, axis=0, keepdims=True) * ROUTED_SCALE
    chosen = jnp.where(cur == NEG_INF, jnp.where(masked == NEG_INF, 0.0, 1.0), 0.0)
    r_i = lax.broadcasted_iota(jnp.int32, (tt, tt), 0)
    c_i = lax.broadcasted_iota(jnp.int32, (tt, tt), 1)
    before = jnp.where(r_i < c_i, 1.0, 0.0).astype(BF16)
    pref = _dot(chosen.astype(BF16), before) + carry[...]
    rank_ref[...] = jnp.concatenate(
        [jnp.sum(jnp.where(p, pref, 0.0), axis=0, keepdims=True) for p in picks], axis=0).astype(jnp.int32)
    carry[...] = carry[...] + jnp.sum(chosen, axis=1, keepdims=True)
    cnt_ref[...] = carry[...]


def _route(logits_t, router_bias, tt):
    ne, t = logits_t.shape
    tok = pl.BlockSpec((TOP_K, tt), lambda i: (0, i))
    return pl.pallas_call(
        _route_kernel,
        grid=(t // tt,),
        in_specs=[pl.BlockSpec((ne, tt), lambda i: (0, i)),
                  pl.BlockSpec((ne, 1), lambda i: (0, 0))],
        out_specs=[tok, tok, tok, pl.BlockSpec((ne, 1), lambda i: (0, 0))],
        out_shape=[jax.ShapeDtypeStruct((TOP_K, t), jnp.int32), jax.ShapeDtypeStruct((TOP_K, t), F32),
                   jax.ShapeDtypeStruct((TOP_K, t), jnp.int32), jax.ShapeDtypeStruct((ne, 1), F32)],
        scratch_shapes=[pltpu.VMEM((ne, 1), F32)],
        compiler_params=_cparams(("arbitrary",)),
        name="router_topk",
    )(logits_t, router_bias.reshape(ne, 1))


def _plan_kernel(cnt_ref, eidx_ref, rank_ref, dest_ref, seq_ref, dexp_ref, meta_ref):
    ne = N_EXPERTS
    tt = eidx_ref.shape[1]
    nblk = seq_ref.shape[1]
    cnt = cnt_ref[...].astype(jnp.int32)
    padded = ((cnt + (MOE_BLOCK - 1)) // MOE_BLOCK) * MOE_BLOCK
    r_i = lax.broadcasted_iota(jnp.int32, (ne, ne), 0)
    c_i = lax.broadcasted_iota(jnp.int32, (ne, ne), 1)
    incl = jnp.where(c_i <= r_i, 1.0, 0.0)
    pend = jnp.dot(incl, jnp.broadcast_to(padded.astype(F32), (ne, LANES)),
                   preferred_element_type=F32, precision=lax.Precision.HIGHEST)[:, 0:1]
    pend = pend.astype(jnp.int32)
    pstart = pend - padded
    ioe = lax.broadcasted_iota(jnp.int32, (ne, tt), 0)
    rows = []
    for k in range(TOP_K):
        sel = ioe == eidx_ref[k:k + 1, :]
        rows.append(jnp.sum(jnp.where(sel, pstart, 0), axis=0, keepdims=True))
    dest_ref[...] = jnp.concatenate(rows, axis=0) + rank_ref[...] + PAIR_ROWS
    blk0 = lax.broadcasted_iota(jnp.int32, (ne, nblk), 1) * MOE_BLOCK
    be = jnp.minimum(jnp.sum(jnp.where(pend <= blk0, 1, 0), axis=0, keepdims=True), ne - 1)
    present = cnt > 0
    strict = jnp.where(c_i < r_i, 1.0, 0.0).astype(BF16)
    sidx = _dot(strict, jnp.broadcast_to(jnp.where(present, 1.0, 0.0), (ne, LANES)).astype(BF16))[:, 0:1]
    sidx = sidx.astype(jnp.int32)
    dexp_ref[...] = jnp.sum(jnp.where(jnp.logical_and(present, sidx == c_i), r_i, 0), axis=0, keepdims=True)
    ioeb = lax.broadcasted_iota(jnp.int32, (ne, nblk), 0)
    nu = jnp.max(pend, axis=0, keepdims=True) // MOE_BLOCK
    nd = jnp.sum(jnp.where(present, 1, 0), axis=0, keepdims=True)
    seq_ref[...] = jnp.minimum(jnp.sum(jnp.where(ioeb == be, sidx, 0), axis=0, keepdims=True), nd - 1)
    lane = lax.broadcasted_iota(jnp.int32, (1, LANES), 1)
    meta_ref[...] = jnp.where(lane == 0, nu, jnp.where(lane == 1, nd, 0))


def _dispatch_plan(cnt, eidx, rank, tt):
    k, t = eidx.shape
    n_blocks = -(-(t * k) // MOE_BLOCK) + N_EXPERTS
    tok = pl.BlockSpec((k, tt), lambda i: (0, i))
    one = lambda n: pl.BlockSpec((1, n), lambda i: (0, 0))
    dest, seq, dexp, meta = pl.pallas_call(
        _plan_kernel,
        grid=(t // tt,),
        in_specs=[pl.BlockSpec(cnt.shape, lambda i: (0, 0)), tok, tok],
        out_specs=[tok, one(n_blocks), one(N_EXPERTS), one(LANES)],
        out_shape=[jax.ShapeDtypeStruct((k, t), jnp.int32), jax.ShapeDtypeStruct((1, n_blocks), jnp.int32),
                   jax.ShapeDtypeStruct((1, N_EXPERTS), jnp.int32), jax.ShapeDtypeStruct((1, LANES), jnp.int32)],
        compiler_params=_cparams(("arbitrary",)),
        name="dispatch_plan",
    )(cnt, eidx, rank)
    return dest, seq.reshape(n_blocks), dexp.reshape(N_EXPERTS), meta.reshape(LANES), n_blocks


SC_CORES = 2
SC_SUBCORES = 16
SC_LANES = 16
SC_CHUNK = 16384
SC_UNROLL = 4


def _row_tables(dest, n_rows):
    k, t = dest.shape
    a = k * t
    nw = SC_CORES * SC_SUBCORES
    per_w = n_rows // nw
    assert n_rows % (nw * SC_LANES) == 0 and t % SC_CHUNK == 0
    mesh = plsc.VectorSubcoreMesh(core_axis_name="c", subcore_axis_name="s")
    cp = pltpu.CompilerParams()
    if "needs_layout_passes" in pltpu.CompilerParams.__dataclass_fields__:
        cp = dataclasses.replace(cp, needs_layout_passes=False)

    def body(dest_hbm, tok_out, dbuf, tloc):
        wid = lax.axis_index("s") * SC_CORES + lax.axis_index("c")
        base = wid * per_w
        shifted = base + PAIR_ROWS

        @pl.loop(0, per_w // SC_LANES)
        def _(i):
            tloc[pl.ds(i * SC_LANES, SC_LANES)] = jnp.zeros((SC_LANES,), jnp.int32)

        lane = lax.iota(jnp.int32, SC_LANES)

        @pl.loop(0, a // SC_CHUNK)
        def _(c):
            pltpu.sync_copy(dest_hbm.at[pl.ds(c * SC_CHUNK, SC_CHUNK)], dbuf)
            tok0 = lax.rem(c * SC_CHUNK, t)

            @pl.loop(0, SC_CHUNK // (SC_LANES * SC_UNROLL))
            def _(j):
                for u in range(SC_UNROLL):
                    off = (j * SC_UNROLL + u) * SC_LANES
                    loc = dbuf[pl.ds(off, SC_LANES)] - shifted
                    mine = jnp.logical_and(loc >= 0, loc < per_w)
                    loc = jnp.where(mine, loc, 0)
                    plsc.store_scatter(tloc, [loc], tok0 + off + lane, mask=mine)

        pltpu.sync_copy(tloc, tok_out.at[pl.ds(base, per_w)])

    fn = pl.kernel(
        body,
        out_type=jax.ShapeDtypeStruct((n_rows,), jnp.int32),
        mesh=mesh,
        scratch_types=[pltpu.VMEM((SC_CHUNK,), jnp.int32), pltpu.VMEM((per_w,), jnp.int32)],
        compiler_params=cp,
        name="row_tables",
    )
    return fn(dest.reshape(a))


ROW_TILE = (SUBLANES, LANES)


FFN_GROUP = 4
PAIR_ROWS = 2 * MOE_BLOCK
W_SETS = 3
GATHER_BATCH = 32


def _gather_rows(h2p_ref, tok_ref, row, buf, base):
    for j0 in range(0, MOE_BLOCK, GATHER_BATCH):
        vals = [h2p_ref[tok_ref[row, j]] for j in range(j0, j0 + GATHER_BATCH)]
        for j, v in zip(range(j0, j0 + GATHER_BATCH), vals):
            buf[pl.ds(H2P_CHUNKS * (base + j), H2P_CHUNKS), :] = v


def _expert_rows(buf, row0, nrows, wset, out, out_row0):
    wg_c, wu_c, wd_c = wset
    parts = []
    for cidx in range(H2P_CHUNKS):
        word = buf[pl.ds(H2P_CHUNKS * row0 + cidx, nrows, stride=H2P_CHUNKS), :]
        parts.append(lax.bitcast_convert_type(word << 16, F32))
        parts.append(lax.bitcast_convert_type(word & jnp.int32(-65536), F32))
    x = jnp.concatenate(parts, axis=1)
    g = _dot(x, wg_c[...])
    u = _dot(x, wu_c[...])
    hm = (g * jax.nn.sigmoid(g)) * u
    out[pl.ds(out_row0, nrows), :] = _dot(hm, wd_c[...])


def _ffn_kernel(seq_ref, dexp_ref, meta_ref, tokc_ref, tokn_ref, h2p_ref, wg_hbm, wu_hbm, wd_hbm,
                y_ref, buf_0, buf_1, yraw_0, yraw_1, *rest):
    i = pl.program_id(0)
    n_used = meta_ref[0]
    n_exp = meta_ref[1]
    nblk = seq_ref.shape[0]
    wsets = tuple(tuple(rest[3 * n:3 * n + 3]) for n in range(W_SETS))
    started_ref, sems = rest[3 * W_SETS:]
    bufs = (buf_0, buf_1)
    yraws = (yraw_0, yraw_1)

    def weight_copies(s, par):
        e = dexp_ref[s]
        return [pltpu.make_async_copy(src.at[e], dst, sems.at[par, n])
                for n, (src, dst) in enumerate(zip((wg_hbm, wu_hbm, wd_hbm), wsets[par]))]

    def start_expert(s):
        for par in range(W_SETS):
            @pl.when(s % W_SETS == par)
            def _():
                for cp in weight_copies(s, par):
                    cp.start()

    @pl.when(i == 0)
    def _():
        _gather_rows(h2p_ref, tokc_ref, 0, buf_0, 0)
        _gather_rows(h2p_ref, tokc_ref, 1, buf_0, MOE_BLOCK)
        yraw_1[...] = jnp.zeros_like(yraw_1)
        start_expert(0)
        started_ref[0] = 0

    for p in range(FFN_GROUP // 2):
        b_a = i * FFN_GROUP + 2 * p
        used = b_a < n_used
        s_a = seq_ref[jnp.minimum(b_a, nblk - 1)]
        s_b = seq_ref[jnp.minimum(b_a + 1, nblk - 1)]
        first_a = jnp.logical_or(b_a == 0, s_a != seq_ref[jnp.clip(b_a - 1, 0, nblk - 1)])
        same = s_a == s_b
        cur, nxt = bufs[p], bufs[1 - p]
        out_rows = pl.ds(p * PAIR_ROWS, PAIR_ROWS)

        def emit_previous():
            y_ref[out_rows] = pltpu.einshape("r(cl)->rcl", yraws[1 - p][...], c=SUBLANES).astype(y_ref.dtype)

        def gather_next():
            if p == 0:
                _gather_rows(h2p_ref, tokc_ref, 2, nxt, 0)
                _gather_rows(h2p_ref, tokc_ref, 3, nxt, MOE_BLOCK)
            else:
                _gather_rows(h2p_ref, tokn_ref, 0, nxt, 0)
                _gather_rows(h2p_ref, tokn_ref, 1, nxt, MOE_BLOCK)

        @pl.when(used)
        def _():
            started = started_ref[0]
            limit = jnp.minimum(s_a + (W_SETS - 1), n_exp - 1)
            for _unused in range(W_SETS - 1):
                go = started < limit

                @pl.when(go)
                def _():
                    start_expert(started + 1)
                started = jnp.where(go, started + 1, started)
            started_ref[0] = started

        for par in range(W_SETS):
            @pl.when(jnp.logical_and(used, jnp.logical_and(same, s_a % W_SETS == par)))
            def _():
                @pl.when(first_a)
                def _():
                    for cp in weight_copies(s_a, par):
                        cp.wait()
                gather_next()
                emit_previous()
                _expert_rows(cur, 0, PAIR_ROWS, wsets[par], yraws[p], 0)

            @pl.when(jnp.logical_and(used, jnp.logical_and(jnp.logical_not(same), s_a % W_SETS == par)))
            def _():
                @pl.when(first_a)
                def _():
                    for cp in weight_copies(s_a, par):
                        cp.wait()
                for cp in weight_copies(s_b, (par + 1) % W_SETS):
                    cp.wait()
                gather_next()
                emit_previous()
                _expert_rows(cur, 0, MOE_BLOCK, wsets[par], yraws[p], 0)
                _expert_rows(cur, MOE_BLOCK, MOE_BLOCK, wsets[(par + 1) % W_SETS], yraws[p], MOE_BLOCK)

        @pl.when(jnp.logical_not(used))
        def _():
            prev_used = jnp.logical_and(b_a >= 2, b_a - 2 < n_used)

            @pl.when(prev_used)
            def _():
                emit_previous()

            @pl.when(jnp.logical_not(prev_used))
            def _():
                y_ref[out_rows] = jnp.zeros((PAIR_ROWS,) + ROW_TILE, y_ref.dtype)


def _routed_experts(h2p, row_token, seq, dexp, meta, n_blocks, wg, wu, wd):
    d = wg.shape[1]
    de = wg.shape[2]
    ng = n_blocks // FFN_GROUP
    assert n_blocks % FFN_GROUP == 0 and FFN_GROUP == 4
    tok3 = row_token.reshape(ng, FFN_GROUP, MOE_BLOCK)
    idle_step = lambda m: ((m[0] + 1) // 2) // 2 + 1
    smem = lambda imap: pl.BlockSpec((None, FFN_GROUP, MOE_BLOCK), imap, memory_space=pltpu.SMEM)
    grid_spec = pltpu.PrefetchScalarGridSpec(
        num_scalar_prefetch=3,
        grid=(ng + 1,),
        in_specs=[
            smem(lambda i, sq, dx, m: (jnp.minimum(i, ng - 1), 0, 0)),
            smem(lambda i, sq, dx, m: (jnp.minimum(i + 1, ng - 1), 0, 0)),
            pl.BlockSpec(h2p.shape, lambda i, sq, dx, m: (0, 0, 0), pipeline_mode=pl.Buffered(1)),
            pl.BlockSpec(memory_space=pl.ANY),
            pl.BlockSpec(memory_space=pl.ANY),
            pl.BlockSpec(memory_space=pl.ANY),
        ],
        out_specs=pl.BlockSpec((FFN_GROUP * MOE_BLOCK,) + ROW_TILE,
                               lambda i, sq, dx, m: (jnp.minimum(i, idle_step(m)), 0, 0)),
        scratch_shapes=[pltpu.VMEM((PAIR_ROWS * H2P_CHUNKS, LANES), jnp.int32)] * 2 + [
            pltpu.VMEM((PAIR_ROWS, d), F32)] * 2 + [
            pltpu.VMEM((d, de), F32), pltpu.VMEM((d, de), F32), pltpu.VMEM((de, d), F32)] * W_SETS + [
            pltpu.SMEM((1,), jnp.int32), pltpu.SemaphoreType.DMA((W_SETS, 3))],
    )
    return pl.pallas_call(
        _ffn_kernel,
        grid_spec=grid_spec,
        out_shape=jax.ShapeDtypeStruct(((ng + 2) * FFN_GROUP * MOE_BLOCK,) + ROW_TILE, BF16),
        compiler_params=_cparams(("arbitrary",)),
        name="routed_experts",
    )(seq, dexp, meta, tok3, tok3, h2p, wg, wu, wd)


COMBINE_TOKENS = 128


def _row_copies(src_hbm, idx_ref, buf, sem):
    return [pltpu.make_async_copy(src_hbm.at[idx_ref[k, j]], buf.at[k, j], sem)
            for k in range(TOP_K) for j in range(COMBINE_TOKENS)]


def _final_kernel(dc_ref, dn_ref, wk_ref, x1_ref, h2_ref, g2_ref, wsg_ref, wsu_ref, wsd_ref, lng_ref, lnb_ref,
                  y_hbm, o_ref, ybuf, sems):
    i = pl.program_id(0)
    nsteps = pl.num_programs(0)
    slot = i % 2

    def issue(d_ref, s):
        for n, cp in enumerate(_row_copies(y_hbm, d_ref, ybuf.at[s], sems.at[s])):
            cp.start(priority=n % 2)

    @pl.when(i == 0)
    def _():
        issue(dc_ref, 0)

    @pl.when(i + 1 < nsteps)
    def _():
        issue(dn_ref, 1 - slot)

    hb = h2_ref[...].astype(BF16)
    g = _dot(hb, wsg_ref[...])
    u = _dot(hb, wsu_ref[...])
    shared = _dot(((g * jax.nn.sigmoid(g)) * u).astype(BF16), wsd_ref[...])
    for cp in _row_copies(y_hbm, dc_ref, ybuf.at[slot], sems.at[slot]):
        cp.wait()
    rows = []
    for j in range(COMBINE_TOKENS):
        acc = ybuf[slot, 0, j].astype(F32) * wk_ref[0, j]
        for k in range(1, TOP_K):
            acc = acc + ybuf[slot, k, j].astype(F32) * wk_ref[k, j]
        rows.append(acc)
    routed = pltpu.einshape("tcl->t(cl)", jnp.stack(rows, axis=0))
    x2 = DN_ALPHA * x1_ref[...] + g2_ref[...] * (routed + shared)
    o_ref[...] = _ln_rows(x2) * lng_ref[...] + lnb_ref[...]


def _combine_final(x1, h2, mod4, dest, wk, y_rows, wsg, wsu, wsd, ln_g, ln_b, seq):
    t, d = x1.shape
    tt = COMBINE_TOKENS
    nsteps = t // tt
    spb = seq // tt
    row = pl.BlockSpec((tt, d), lambda i: (i, 0))
    full = lambda a: pl.BlockSpec(a.shape, lambda i: (0,) * a.ndim)
    ln_g2, ln_b2 = ln_g.reshape(1, d), ln_b.reshape(1, d)
    return pl.pallas_call(
        _final_kernel,
        grid=(nsteps,),
        in_specs=[pl.BlockSpec((TOP_K, tt), lambda i: (0, i), memory_space=pltpu.SMEM),
                  pl.BlockSpec((TOP_K, tt), lambda i: (0, jnp.minimum(i + 1, nsteps - 1)),
                               memory_space=pltpu.SMEM),
                  pl.BlockSpec((TOP_K, tt), lambda i: (0, i), memory_space=pltpu.SMEM),
                  row, row,
                  pl.BlockSpec((None, None, 1, d), lambda i: (i // spb, 5, 0, 0)),
                  full(wsg), full(wsu), full(wsd), full(ln_g2), full(ln_b2),
                  pl.BlockSpec(memory_space=pl.ANY)],
        out_specs=row,
        out_shape=jax.ShapeDtypeStruct((t, d), F32),
        scratch_shapes=[pltpu.VMEM((2, TOP_K, tt) + ROW_TILE, y_rows.dtype),
                        pltpu.SemaphoreType.DMA((2,))],
        compiler_params=_cparams(("arbitrary",)),
        name="shared_combine_ln",
    )(dest, dest, wk, x1, h2, mod4, wsg, wsu, wsd, ln_g2, ln_b2, y_rows)


def kernel(x, c, w_ada, b_ada, w_in, hg_lower_bound, hg_norm_w, rel_bias, w_out, ln1_g, ln1_b, w_router,
           router_bias, w_e_gate, w_e_up, w_e_down, w_sh_gate, w_sh_up, w_sh_down, ln2_g, ln2_b):
    bsz, seq, d = x.shape
    t = bsz * seq
    assert w_ada.shape[0] == DEPTH and seq % (ATT_BRANCHES[-1][0]) == 0
    x2 = x.reshape(t, d)
    bias = _bias_tables(rel_bias)
    for l in range(DEPTH):
        mod4 = _modulation(c, w_ada[l], b_ada[l]).reshape(bsz, 6, 1, d)
        w_in_bf = _cast_bf16(w_in[l], 256)
        hq, hf, hi, hg, *qkv = _in_projection(x2, mod4, w_in_bf, seq, 512)
        nbr = len(ATT_BRANCHES)
        y_hg = _hgrn2(hq, hf, hi, hg, hg_lower_bound, hg_norm_w[l], bsz, seq)
        os_, ls_ = [], []
        for g, (_, dil) in enumerate(ATT_BRANCHES):
            o, lse = _dilated_attention(qkv[g], qkv[nbr + g], qkv[2 * nbr + g], bias[g], bsz, seq, dil)
            os_.append(o)
            ls_.append(lse)
        w_out_bf = _cast_bf16(w_out[l], 256)
        x1, h2, h2p, logits_t = _out_projection(y_hg, os_, ls_, x2, mod4, w_out_bf, ln1_g[l], ln1_b[l],
                                                w_router[l].T, seq, 512)
        eidx, wk, rank, cnt = _route(logits_t, router_bias[l], 256)
        dest, blk_seq, dexp, meta, n_blocks = _dispatch_plan(cnt, eidx, rank, 512)
        row_token = _row_tables(dest, n_blocks * MOE_BLOCK)
        y_rows = _routed_experts(h2p, row_token, blk_seq, dexp, meta, n_blocks,
                                 w_e_gate[l], w_e_up[l], w_e_down[l])
        x2 = _combine_final(x1, h2, mod4, dest, wk, y_rows,
                            _cast_bf16(w_sh_gate[l], 256), _cast_bf16(w_sh_up[l], 256),
                            _cast_bf16(w_sh_down[l], 256), ln2_g[l], ln2_b[l], seq)
    return x2.reshape(bsz, seq, d)
```

```python
import dataclasses
import math

import jax
import jax.numpy as jnp
import numpy as np
from jax import lax
from jax.experimental import pallas as pl
from jax.experimental.pallas import tpu as pltpu
from jax.experimental.pallas import tpu_sc as plsc

HG_HEADS = 4
HG_DK = 128
HG_WIDTH = HG_HEADS * HG_DK
ATT_BRANCHES = ((128, 1), (512, 4), (2048, 16))
ATT_HEADS_PER_BRANCH = 4
ATT_HEAD_DIM = 64
ATT_BW = ATT_HEADS_PER_BRANCH * ATT_HEAD_DIM
ATT_BLOCK = 128
REL_BUCKETS = 32
REL_MAX_DIST = 2048
N_EXPERTS = 256
TOP_K = 8
N_GROUPS = 8
TOPK_GROUPS = 4
ROUTED_SCALE = 2.5
MOE_BLOCK = 128
DEPTH = 1
DN_ALPHA = (2 * DEPTH) ** 0.25
LN_EPS = 1e-5
RMS_EPS = 1e-6

LANES = 128
SUBLANES = 8
VMEM_LIMIT_BYTES = 56 * 1024 * 1024

F32 = jnp.float32
BF16 = jnp.bfloat16
NEG_INF = float("-inf")


def _cparams(sem):
    return pltpu.CompilerParams(dimension_semantics=sem, vmem_limit_bytes=VMEM_LIMIT_BYTES)


def _ln_rows(x):
    mu = jnp.mean(x, axis=-1, keepdims=True)
    xc = x - mu
    var = jnp.mean(xc * xc, axis=-1, keepdims=True)
    return xc * lax.rsqrt(var + LN_EPS)


def _dot(a, b):
    return jnp.dot(a, b, preferred_element_type=F32)


def _dot_nt(a, b):
    return lax.dot_general(a, b, (((1,), (1,)), ((), ())), preferred_element_type=F32)


def _dot_tn(a, b):
    return lax.dot_general(a, b, (((0,), (0,)), ((), ())), preferred_element_type=F32)


def _cast_kernel(w_ref, o_ref):
    o_ref[...] = w_ref[...].astype(o_ref.dtype)


def _cast_bf16(w, rows_per_step):
    r, c = w.shape
    return pl.pallas_call(
        _cast_kernel,
        grid=(r // rows_per_step,),
        in_specs=[pl.BlockSpec((rows_per_step, c), lambda i: (i, 0))],
        out_specs=pl.BlockSpec((rows_per_step, c), lambda i: (i, 0)),
        out_shape=jax.ShapeDtypeStruct((r, c), BF16),
        compiler_params=_cparams(("parallel",)),
        name="cast_bf16",
    )(w)


def _mod_kernel(c_ref, w_ref, b_ref, o_ref):
    c = c_ref[...]
    cond = c * jax.nn.sigmoid(c)
    o_ref[...] = jnp.dot(cond, w_ref[...], preferred_element_type=F32,
                         precision=lax.Precision.HIGHEST) + b_ref[...]


def _modulation(c, w_ada, b_ada):
    bsz, d = c.shape
    n = w_ada.shape[1]
    rows = -(-bsz // SUBLANES) * SUBLANES
    cpad = jnp.zeros((rows, d), F32).at[:bsz].set(c)
    tn = 1024
    out = pl.pallas_call(
        _mod_kernel,
        grid=(n // tn,),
        in_specs=[pl.BlockSpec((rows, d), lambda j: (0, 0)),
                  pl.BlockSpec((d, tn), lambda j: (0, j)),
                  pl.BlockSpec((1, tn), lambda j: (0, j))],
        out_specs=pl.BlockSpec((rows, tn), lambda j: (0, j)),
        out_shape=jax.ShapeDtypeStruct((rows, n), F32),
        compiler_params=_cparams(("parallel",)),
        name="adaln_modulation",
    )(cpad, w_ada, b_ada.reshape(1, n))
    return out[:bsz]


_IN_HG = 4
_IN_ATT = 3 * len(ATT_BRANCHES)


def _inproj_kernel(x_ref, sc_ref, sh_ref, w_ref, *refs):
    outs = refs[:_IN_HG + _IN_ATT]
    scratch = refs[_IN_HG + _IN_ATT:]
    x = x_ref[...]
    h = _ln_rows(x) * (1.0 + sc_ref[...]) + sh_ref[...]
    hb = h.astype(BF16)
    tm = x.shape[0]
    col = 0
    n_scr = 0
    for k, o_ref in enumerate(outs):
        width = HG_WIDTH if k < _IN_HG else ATT_BW
        y = _dot(hb, w_ref[:, col:col + width])
        col += width
        if k < _IN_HG:
            o_ref[...] = y.astype(o_ref.dtype)
            continue
        if k < _IN_HG + len(ATT_BRANCHES):
            y = y * (ATT_HEAD_DIM ** -0.5)
        dil = ATT_BRANCHES[(k - _IN_HG) % len(ATT_BRANCHES)][1]
        if dil == 1:
            o_ref[...] = y.astype(o_ref.dtype)
            continue
        scr = scratch[n_scr]
        n_scr += 1
        for half in range(ATT_BW // LANES):
            scr[half] = y[:, half * LANES:(half + 1) * LANES]
        for r in range(dil):
            for half in range(ATT_BW // LANES):
                c0 = r * ATT_BW + half * LANES
                o_ref[:, c0:c0 + LANES] = scr[half, pl.ds(r, tm // dil, stride=dil), :].astype(o_ref.dtype)


def _in_projection(x2, mod4, w_in_bf, seq, tm):
    t, d = x2.shape
    steps_per_batch = seq // tm
    dils = [dil for _, dil in ATT_BRANCHES] * 3
    shapes = [(t, HG_WIDTH)] * _IN_HG + [(t // dil, dil * ATT_BW) for dil in dils]
    blocks = [(tm, HG_WIDTH)] * _IN_HG + [(tm // dil, dil * ATT_BW) for dil in dils]
    dtypes = [BF16, F32, BF16, BF16] + [BF16] * _IN_ATT
    mod_spec = lambda row: pl.BlockSpec((None, None, 1, d),
                                        lambda i, row=row: (i // steps_per_batch, row, 0, 0))
    outs = pl.pallas_call(
        _inproj_kernel,
        grid=(t // tm,),
        in_specs=[pl.BlockSpec((tm, d), lambda i: (i, 0)),
                  mod_spec(1), mod_spec(0),
                  pl.BlockSpec(w_in_bf.shape, lambda i: (0, 0))],
        out_specs=[pl.BlockSpec(b, lambda i: (i, 0)) for b in blocks],
        out_shape=[jax.ShapeDtypeStruct(s, dt) for s, dt in zip(shapes, dtypes)],
        scratch_shapes=[pltpu.VMEM((ATT_BW // LANES, tm, LANES), F32) for dil in dils if dil > 1],
        compiler_params=_cparams(("parallel",)),
        name="ln_in_projection",
    )(x2, mod4, mod4, w_in_bf)
    return outs


HG_CHUNK = 64
HG_CHUNKS_PER_STEP = 4
HG_SUB = 8
HG_LEVELS = (64, 32, 16)
LOG2E = 1.4426950408889634


def _hgrn_chunk(q, z, iv, lb):
    c = HG_CHUNK
    f = lb + (1.0 - lb) * jax.nn.sigmoid(z)
    lf = jnp.log(f)
    kk = (1.0 - lb) * jax.nn.sigmoid(-z)
    r_i = lax.broadcasted_iota(jnp.int32, (c, c), 0)
    c_i = lax.broadcasted_iota(jnp.int32, (c, c), 1)
    tril = (c_i <= r_i).astype(F32)
    b = jnp.dot(tril, lf, preferred_element_type=F32, precision=lax.Precision.HIGHEST)
    bl = b * LOG2E

    row = lax.broadcasted_iota(jnp.int32, (c, HG_DK), 0)
    scores = jnp.zeros((c, c), F32)
    for m in HG_LEVELS:
        nb = c // m
        b3 = bl.reshape(nb, m, HG_DK)
        piv = jnp.broadcast_to(b3[:, m // 2 - 1:m // 2, :], (nb, m, HG_DK)).reshape(c, HG_DK)
        second = (row % m) >= (m // 2)
        qt = jnp.where(second, q * jnp.exp2(bl - piv), 0.0)
        kt = jnp.where(second, 0.0, kk * jnp.exp2(piv - bl))
        s_m = _dot_nt(qt.astype(BF16), kt.astype(BF16))
        if nb > 1:
            s_m = jnp.where((r_i // m) == (c_i // m), s_m, 0.0)
        scores = scores + s_m
    sub = HG_SUB
    t_i = lax.broadcasted_iota(jnp.int32, (sub, 1), 0)
    lane = lax.broadcasted_iota(jnp.int32, (sub, c), 1)
    diag_rows = []
    for j in range(c // sub):
        qb = q[j * sub:(j + 1) * sub]
        kb = kk[j * sub:(j + 1) * sub]
        bb = bl[j * sub:(j + 1) * sub]
        a_j = jnp.zeros((sub, c), F32)
        for s in range(sub):
            w = qb * kb[s:s + 1] * jnp.exp2(bb - bb[s:s + 1])
            a_j = jnp.where(lane == j * sub + s, jnp.sum(w, axis=-1, keepdims=True), a_j)
        diag_rows.append(jnp.where(lane - j * sub <= t_i, a_j, 0.0))
    scores = scores + jnp.concatenate(diag_rows, axis=0)

    ivb = iv.astype(BF16)
    intra = _dot(scores.astype(BF16), ivb)
    b_last = bl[c - 1:c]
    kdec = (kk * jnp.exp2(b_last - bl)).astype(BF16)
    return (q * jnp.exp2(bl)).astype(BF16), intra, jnp.exp2(b_last), _dot_tn(ivb, kdec)


def _hgrn_kernel(q_ref, f_ref, i_ref, g_ref, lbp_ref, nw_ref, o_ref, st_ref):
    @pl.when(pl.program_id(1) == 0)
    def _():
        st_ref[...] = jnp.zeros_like(st_ref)

    lbp = lbp_ref[...]
    e = jnp.exp(lbp - jnp.max(lbp, axis=0, keepdims=True))
    lb_all = e[0:1] / jnp.sum(e, axis=0, keepdims=True)
    heads = []
    for h in range(HG_HEADS):
        sl = slice(h * HG_DK, (h + 1) * HG_DK)
        st_t = st_ref[h]
        outs = []
        for n in range(HG_CHUNKS_PER_STEP):
            rows = slice(n * HG_CHUNK, (n + 1) * HG_CHUNK)
            qdec, intra, dec_last, kv = _hgrn_chunk(q_ref[rows, sl].astype(F32), f_ref[rows, sl],
                                                    i_ref[rows, sl].astype(F32), lb_all[:, sl])
            o = _dot_nt(qdec, st_t.astype(BF16)) + intra
            st_t = st_t * dec_last + kv
            outs.append(o * lax.rsqrt(jnp.mean(o * o, axis=-1, keepdims=True) + RMS_EPS))
        st_ref[h] = st_t
        heads.append(jnp.concatenate(outs, axis=0))
    o_all = jnp.concatenate(heads, axis=-1)
    g = g_ref[...].astype(F32)
    o_ref[...] = (o_all * nw_ref[...] * (g * jax.nn.sigmoid(g))).astype(o_ref.dtype)


def _hgrn2(hq, hf, hi, hg, lb_param, norm_w, bsz, seq):
    t = hq.shape[0]
    rows = HG_CHUNK * HG_CHUNKS_PER_STEP
    nc = seq // rows
    tok = lambda b, n: (b * nc + n, 0)
    spec = pl.BlockSpec((rows, HG_WIDTH), tok)
    return pl.pallas_call(
        _hgrn_kernel,
        grid=(bsz, nc),
        in_specs=[spec, spec, spec, spec,
                  pl.BlockSpec(lb_param.shape, lambda b, n: (0, 0)),
                  pl.BlockSpec((1, HG_WIDTH), lambda b, n: (0, 0))],
        out_specs=spec,
        out_shape=jax.ShapeDtypeStruct((t, HG_WIDTH), BF16),
        scratch_shapes=[pltpu.VMEM((HG_HEADS, HG_DK, HG_DK), F32)],
        compiler_params=_cparams(("parallel", "arbitrary")),
        name="hgrn2_scan",
    )(hq, hf, hi, hg, lb_param, norm_w.reshape(1, HG_WIDTH))


def _t5_bucket_np(dist):
    max_exact = REL_BUCKETS // 2
    n = np.maximum(dist, 0)
    nf = np.maximum(n, 1).astype(np.float32)
    large = max_exact + (np.log(nf / np.float32(max_exact)) / np.float32(math.log(REL_MAX_DIST / max_exact))
                         * np.float32(REL_BUCKETS - max_exact)).astype(np.int32)
    large = np.minimum(large, REL_BUCKETS - 1)
    return np.where(n < max_exact, n, large).astype(np.int32)


def _band_tables():
    w = ATT_BLOCK
    qi = np.arange(w)[:, None]
    ki = np.arange(2 * w)[None, :]
    m = w + qi - ki
    band = (m >= 0) & (m <= w)
    buckets = np.stack([_t5_bucket_np(m * dil) for _, dil in ATT_BRANCHES])
    return buckets, band


def _bias_kernel(rb_ref, bucket_ref, o_ref):
    g = pl.program_id(0)
    w = ATT_BLOCK
    bucket = bucket_ref[...]
    qi = lax.broadcasted_iota(jnp.int32, (w, 2 * w), 0)
    ki = lax.broadcasted_iota(jnp.int32, (w, 2 * w), 1)
    m = w + qi - ki
    band = (m >= 0) & (m <= w)
    for h in range(ATT_HEADS_PER_BRANCH):
        acc = jnp.zeros((w, 2 * w), F32)
        for c in range(REL_BUCKETS):
            acc = jnp.where(bucket == c, rb_ref[c, g * ATT_HEADS_PER_BRANCH + h], acc)
        full = jnp.where(band, acc, NEG_INF)
        o_ref[1, h] = full
        o_ref[0, h] = jnp.where(ki >= w, full, NEG_INF)


def _bias_tables(rel_bias):
    buckets, _ = _band_tables()
    g = len(ATT_BRANCHES)
    w = ATT_BLOCK
    return pl.pallas_call(
        _bias_kernel,
        grid=(g,),
        in_specs=[pl.BlockSpec(memory_space=pltpu.SMEM),
                  pl.BlockSpec((None, w, 2 * w), lambda i: (i, 0, 0))],
        out_specs=pl.BlockSpec((None, 2, ATT_HEADS_PER_BRANCH, w, 2 * w), lambda i: (i, 0, 0, 0, 0)),
        out_shape=jax.ShapeDtypeStruct((g, 2, ATT_HEADS_PER_BRANCH, w, 2 * w), F32),
        compiler_params=_cparams(("parallel",)),
        name="rel_bias_tables",
    )(rel_bias, jnp.asarray(buckets))


ATT_BLOCKS_PER_STEP = 8


def _attn_kernel(q_ref, kp_ref, kc_ref, vp_ref, vc_ref, bias_ref, o_ref, lse_ref):
    m = pl.program_id(2)
    w = ATT_BLOCK
    hb = ATT_HEADS_PER_BRANCH
    lane = lax.broadcasted_iota(jnp.int32, (w, ATT_BW), 1) // ATT_HEAD_DIM
    for res, blk in [(r, b) for r in range(q_ref.shape[1] // ATT_BW) for b in range(q_ref.shape[0] // w)]:
        cols = slice(res * ATT_BW, (res + 1) * ATT_BW)
        kall = jnp.concatenate([kp_ref[:, cols], kc_ref[:, cols]], axis=0)
        vall = jnp.concatenate([vp_ref[:, cols], vc_ref[:, cols]], axis=0)
        q = q_ref[blk * w:(blk + 1) * w, cols]
        q4 = jnp.concatenate([jnp.where(lane == h, q, jnp.zeros_like(q)) for h in range(hb)], axis=0)
        kk = kall[blk * w:(blk + 2) * w]
        vv = vall[blk * w:(blk + 2) * w]
        s4 = _dot_nt(q4, kk)
        bias = bias_ref[jnp.minimum(m, 1)] if blk == 0 else bias_ref[1]
        s4 = s4 + bias.reshape(hb * w, 2 * w)
        mx = jnp.max(s4, axis=-1, keepdims=True)
        p = jnp.exp(s4 - mx)
        l = jnp.sum(p, axis=-1, keepdims=True)
        o4 = _dot((p / l).astype(vv.dtype), vv)
        lse4 = mx + jnp.log(l)
        o = jnp.zeros((w, ATT_BW), F32)
        lse = jnp.zeros((w, ATT_BW), F32)
        for h in range(hb):
            o = jnp.where(lane == h, o4[h * w:(h + 1) * w], o)
            lse = jnp.where(lane == h, lse4[h * w:(h + 1) * w], lse)
        o_ref[blk * w:(blk + 1) * w, cols] = o.astype(o_ref.dtype)
        lse_ref[blk * w:(blk + 1) * w, cols] = lse


def _dilated_attention(q, k, v, bias_g, bsz, seq, dilation):
    w = ATT_BLOCK
    l = seq // dilation
    pstep = math.gcd(ATT_BLOCKS_PER_STEP, l // w)
    nb = l // (w * pstep)
    rstep = math.gcd(ATT_BLOCKS_PER_STEP // pstep, dilation)
    view = lambda a: a.reshape(bsz, l, dilation * ATT_BW)
    cur = pl.BlockSpec((None, pstep * w, rstep * ATT_BW), lambda b, r, n: (b, n, r))
    prev = pl.BlockSpec((None, w, rstep * ATT_BW), lambda b, r, n: (b, jnp.maximum(pstep * n - 1, 0), r))
    o, lse = pl.pallas_call(
        _attn_kernel,
        grid=(bsz, dilation // rstep, nb),
        in_specs=[cur, prev, cur, prev, cur,
                  pl.BlockSpec(bias_g.shape, lambda b, r, n: (0, 0, 0, 0))],
        out_specs=[cur, cur],
        out_shape=[jax.ShapeDtypeStruct((bsz, l, dilation * ATT_BW), BF16),
                   jax.ShapeDtypeStruct((bsz, l, dilation * ATT_BW), F32)],
        compiler_params=_cparams(("parallel", "parallel", "arbitrary")),
        name=f"dilated_attention_d{dilation}",
    )(view(q), view(k), view(k), view(v), view(v), bias_g)
    return o.reshape(bsz * l, dilation * ATT_BW), lse.reshape(bsz * l, dilation * ATT_BW)


def _split_bf16(a):
    hi = a.astype(BF16)
    lo = (a - hi.astype(F32)).astype(BF16)
    return hi, lo


H2P_CHUNKS = 4


def _token_order(ref, scr, dil):
    if dil == 1:
        return ref[...].astype(F32)
    n = ref.shape[0]
    halves = ATT_BW // LANES
    for r in range(dil):
        for half in range(halves):
            c0 = r * ATT_BW + half * LANES
            scr[half, pl.ds(r, n, stride=dil), :] = ref[:, c0:c0 + LANES].astype(F32)
    return jnp.concatenate([scr[half] for half in range(halves)], axis=1)


def _outproj_kernel(yhg_ref, o1_ref, o2_ref, o3_ref, l1_ref, l2_ref, l3_ref, x_ref,
                    g1_ref, sc2_ref, sh2_ref, wout_ref, lng_ref, lnb_ref, wrt_ref,
                    x1_ref, h2_ref, h2p_ref, lgt_ref, *scratch):
    dils = [dil for _, dil in ATT_BRANCHES]
    scr = iter(scratch)
    o1, o2, o3 = [_token_order(r, None if dil == 1 else next(scr), dil)
                  for r, dil in zip((o1_ref, o2_ref, o3_ref), dils)]
    l1, l2, l3 = [_token_order(r, None if dil == 1 else next(scr), dil)
                  for r, dil in zip((l1_ref, l2_ref, l3_ref), dils)]
    mx = jnp.maximum(jnp.maximum(l1, l2), l3)
    e1, e2, e3 = jnp.exp(l1 - mx), jnp.exp(l2 - mx), jnp.exp(l3 - mx)
    den = e1 + e2 + e3
    att = (e1 / den) * o1 + (e2 / den) * o2 + (e3 / den) * o3
    mix = _dot(yhg_ref[...], wout_ref[:HG_WIDTH, :]) + _dot(att.astype(BF16), wout_ref[HG_WIDTH:, :])
    x1 = _ln_rows(DN_ALPHA * x_ref[...] + g1_ref[...] * mix) * lng_ref[...] + lnb_ref[...]
    x1_ref[...] = x1
    h2 = _ln_rows(x1) * (1.0 + sc2_ref[...]) + sh2_ref[...]
    h_hi = h2.astype(BF16)
    h_hf = h_hi.astype(F32)
    h2_ref[...] = h_hi
    bits = lax.bitcast_convert_type(h_hf, jnp.uint32)
    for cidx in range(H2P_CHUNKS):
        lo = bits[:, 2 * LANES * cidx:2 * LANES * cidx + LANES]
        hi = bits[:, 2 * LANES * cidx + LANES:2 * LANES * (cidx + 1)]
        h2p_ref[:, cidx, :] = lax.bitcast_convert_type((lo >> 16) | hi, jnp.int32)
    h_lo = (h2 - h_hf).astype(BF16)
    w_hi, w_lo = _split_bf16(wrt_ref[...])
    lgt_ref[...] = _dot_nt(w_hi, h_hi) + (_dot_nt(w_hi, h_lo) + _dot_nt(w_lo, h_hi))


def _out_projection(yhg, os_, ls_, x2, mod4, w_out_bf, ln_g, ln_b, w_router_t, seq, tm):
    t, d = x2.shape
    spb = seq // tm
    ne = w_router_t.shape[0]
    row = lambda w: pl.BlockSpec((tm, w), lambda i: (i, 0))
    mod_spec = lambda r: pl.BlockSpec((None, None, 1, d), lambda i, r=r: (i // spb, r, 0, 0))
    full = lambda a: pl.BlockSpec(a.shape, lambda i: (0,) * a.ndim)
    ln_g2, ln_b2 = ln_g.reshape(1, d), ln_b.reshape(1, d)
    dils = [dil for _, dil in ATT_BRANCHES]
    branch = [pl.BlockSpec((tm // dil, dil * ATT_BW), lambda i: (i, 0)) for dil in dils]
    return pl.pallas_call(
        _outproj_kernel,
        grid=(t // tm,),
        in_specs=[row(HG_WIDTH)] + branch + branch + [row(d),
                  mod_spec(2), mod_spec(4), mod_spec(3),
                  full(w_out_bf), full(ln_g2), full(ln_b2), full(w_router_t)],
        out_specs=[row(d), row(d), pl.BlockSpec((tm, H2P_CHUNKS, LANES), lambda i: (i, 0, 0)),
                   pl.BlockSpec((ne, tm), lambda i: (0, i))],
        out_shape=[jax.ShapeDtypeStruct((t, d), F32), jax.ShapeDtypeStruct((t, d), BF16),
                   jax.ShapeDtypeStruct((t, H2P_CHUNKS, LANES), jnp.int32),
                   jax.ShapeDtypeStruct((ne, t), F32)],
        scratch_shapes=[pltpu.VMEM((ATT_BW // LANES, tm, LANES), F32) for dil in dils + dils if dil > 1],
        compiler_params=_cparams(("parallel",)),
        name="merge_outproj_ln",
    )(yhg, *os_, *ls_, x2, mod4, mod4, mod4, w_out_bf, ln_g2, ln_b2, w_router_t)


def _argmax_rows(cur, iota, nrows):
    m = jnp.max(cur, axis=0, keepdims=True)
    idx = jnp.min(jnp.where(cur == m, iota, nrows), axis=0, keepdims=True)
    return m, idx, iota == idx


def _route_kernel(lgt_ref, rb_ref, eidx_ref, w_ref, rank_ref, cnt_ref, carry):
    ne = N_EXPERTS
    gsz = ne // N_GROUPS
    tt = lgt_ref.shape[1]

    @pl.when(pl.program_id(0) == 0)
    def _():
        carry[...] = jnp.zeros_like(carry)

    sc = jax.nn.sigmoid(lgt_ref[...])
    biased = sc + rb_ref[...]
    g3 = biased.reshape(N_GROUPS, gsz, tt)
    io3 = lax.broadcasted_iota(jnp.int32, (N_GROUPS, gsz, tt), 1)
    m1 = jnp.max(g3, axis=1, keepdims=True)
    first = jnp.min(jnp.where(g3 == m1, io3, gsz), axis=1, keepdims=True)
    m2 = jnp.max(jnp.where(io3 == first, NEG_INF, g3), axis=1, keepdims=True)
    gs = (m1 + m2).reshape(N_GROUPS, tt)
    io8 = lax.broadcasted_iota(jnp.int32, (N_GROUPS, tt), 0)
    sel = jnp.zeros((N_GROUPS, tt), jnp.int32)
    cur = gs
    for _ in range(TOPK_GROUPS):
        _, _, pick = _argmax_rows(cur, io8, N_GROUPS)
        sel = jnp.where(pick, 1, sel)
        cur = jnp.where(pick, NEG_INF, cur)
    masked = jnp.where(sel.reshape(N_GROUPS, 1, tt) > 0, g3, NEG_INF).reshape(ne, tt)
    ioe = lax.broadcasted_iota(jnp.int32, (ne, tt), 0)
    cur = masked
    idxs, ws, picks = [], [], []
    for _ in range(TOP_K):
        _, idx, pick = _argmax_rows(cur, ioe, ne)
        idxs.append(idx)
        picks.append(pick)
        ws.append(jnp.sum(jnp.where(pick, sc, 0.0), axis=0, keepdims=True))
        cur = jnp.where(pick, NEG_INF, cur)
    wk = jnp.concatenate(ws, axis=0)
    eidx_ref[...] = jnp.concatenate(idxs, axis=0)
    w_ref[...] = wk / jnp.sum(wk, axis=0, keepdims=True) * ROUTED_SCALE
    chosen = jnp.where(cur == NEG_INF, jnp.where(masked == NEG_INF, 0.0, 1.0), 0.0)
    r_i = lax.broadcasted_iota(jnp.int32, (tt, tt), 0)
    c_i = lax.broadcasted_iota(jnp.int32, (tt, tt), 1)
    before = jnp.where(r_i < c_i, 1.0, 0.0).astype(BF16)
    pref = _dot(chosen.astype(BF16), before) + carry[...]
    rank_ref[...] = jnp.concatenate(
        [jnp.sum(jnp.where(p, pref, 0.0), axis=0, keepdims=True) for p in picks], axis=0).astype(jnp.int32)
    carry[...] = carry[...] + jnp.sum(chosen, axis=1, keepdims=True)
    cnt_ref[...] = carry[...]


def _route(logits_t, router_bias, tt):
    ne, t = logits_t.shape
    tok = pl.BlockSpec((TOP_K, tt), lambda i: (0, i))
    return pl.pallas_call(
        _route_kernel,
        grid=(t // tt,),
        in_specs=[pl.BlockSpec((ne, tt), lambda i: (0, i)),
                  pl.BlockSpec((ne, 1), lambda i: (0, 0))],
        out_specs=[tok, tok, tok, pl.BlockSpec((ne, 1), lambda i: (0, 0))],
        out_shape=[jax.ShapeDtypeStruct((TOP_K, t), jnp.int32), jax.ShapeDtypeStruct((TOP_K, t), F32),
                   jax.ShapeDtypeStruct((TOP_K, t), jnp.int32), jax.ShapeDtypeStruct((ne, 1), F32)],
        scratch_shapes=[pltpu.VMEM((ne, 1), F32)],
        compiler_params=_cparams(("arbitrary",)),
        name="router_topk",
    )(logits_t, router_bias.reshape(ne, 1))


def _plan_kernel(cnt_ref, eidx_ref, rank_ref, dest_ref, seq_ref, dexp_ref, meta_ref, pstart_ref):
    ne = N_EXPERTS
    tt = eidx_ref.shape[1]
    nblk = seq_ref.shape[1]

    @pl.when(pl.program_id(0) == 0)
    def _():
        cnt = cnt_ref[...].astype(jnp.int32)
        padded = ((cnt + (MOE_BLOCK - 1)) // MOE_BLOCK) * MOE_BLOCK
        r_i = lax.broadcasted_iota(jnp.int32, (ne, ne), 0)
        c_i = lax.broadcasted_iota(jnp.int32, (ne, ne), 1)
        incl = jnp.where(c_i <= r_i, 1.0, 0.0)
        pend = jnp.dot(incl, jnp.broadcast_to(padded.astype(F32), (ne, LANES)),
                       preferred_element_type=F32, precision=lax.Precision.HIGHEST)[:, 0:1]
        pend = pend.astype(jnp.int32)
        pstart_ref[...] = pend - padded
        blk0 = lax.broadcasted_iota(jnp.int32, (ne, nblk), 1) * MOE_BLOCK
        be = jnp.minimum(jnp.sum(jnp.where(pend <= blk0, 1, 0), axis=0, keepdims=True), ne - 1)
        present = cnt > 0
        strict = jnp.where(c_i < r_i, 1.0, 0.0).astype(BF16)
        sidx = _dot(strict, jnp.broadcast_to(jnp.where(present, 1.0, 0.0), (ne, LANES)).astype(BF16))[:, 0:1]
        sidx = sidx.astype(jnp.int32)
        dexp_ref[...] = jnp.sum(jnp.where(jnp.logical_and(present, sidx == c_i), r_i, 0), axis=0, keepdims=True)
        ioeb = lax.broadcasted_iota(jnp.int32, (ne, nblk), 0)
        nu = jnp.max(pend, axis=0, keepdims=True) // MOE_BLOCK
        nd = jnp.sum(jnp.where(present, 1, 0), axis=0, keepdims=True)
        seq_ref[...] = jnp.minimum(jnp.sum(jnp.where(ioeb == be, sidx, 0), axis=0, keepdims=True), nd - 1)
        lane = lax.broadcasted_iota(jnp.int32, (1, LANES), 1)
        meta_ref[...] = jnp.where(lane == 0, nu, jnp.where(lane == 1, nd, 0))

    pstart = pstart_ref[...]
    ioe = lax.broadcasted_iota(jnp.int32, (ne, tt), 0)
    rows = []
    for k in range(TOP_K):
        sel = ioe == eidx_ref[k:k + 1, :]
        rows.append(jnp.sum(jnp.where(sel, pstart, 0), axis=0, keepdims=True))
    dest_ref[...] = jnp.concatenate(rows, axis=0) + rank_ref[...] + PAIR_ROWS


def _dispatch_plan(cnt, eidx, rank, tt):
    k, t = eidx.shape
    n_blocks = -(-(t * k) // MOE_BLOCK) + N_EXPERTS
    tok = pl.BlockSpec((k, tt), lambda i: (0, i))
    one = lambda n: pl.BlockSpec((1, n), lambda i: (0, 0))
    dest, seq, dexp, meta = pl.pallas_call(
        _plan_kernel,
        grid=(t // tt,),
        in_specs=[pl.BlockSpec(cnt.shape, lambda i: (0, 0)), tok, tok],
        out_specs=[tok, one(n_blocks), one(N_EXPERTS), one(LANES)],
        out_shape=[jax.ShapeDtypeStruct((k, t), jnp.int32), jax.ShapeDtypeStruct((1, n_blocks), jnp.int32),
                   jax.ShapeDtypeStruct((1, N_EXPERTS), jnp.int32), jax.ShapeDtypeStruct((1, LANES), jnp.int32)],
        scratch_shapes=[pltpu.VMEM((N_EXPERTS, 1), jnp.int32)],
        compiler_params=_cparams(("arbitrary",)),
        name="dispatch_plan",
    )(cnt, eidx, rank)
    return dest, seq.reshape(n_blocks), dexp.reshape(N_EXPERTS), meta.reshape(LANES), n_blocks


SC_CORES = 2
SC_SUBCORES = 16
SC_LANES = 16
SC_CHUNK = 16384
SC_UNROLL = 4


def _row_tables(dest, n_rows):
    k, t = dest.shape
    a = k * t
    nw = SC_CORES * SC_SUBCORES
    per_w = n_rows // nw
    assert n_rows % (nw * SC_LANES) == 0 and t % SC_CHUNK == 0
    mesh = plsc.VectorSubcoreMesh(core_axis_name="c", subcore_axis_name="s")
    cp = pltpu.CompilerParams()
    if "needs_layout_passes" in pltpu.CompilerParams.__dataclass_fields__:
        cp = dataclasses.replace(cp, needs_layout_passes=False)

    def body(dest_hbm, tok_out, dbuf, tloc):
        wid = lax.axis_index("s") * SC_CORES + lax.axis_index("c")
        base = wid * per_w
        shifted = base + PAIR_ROWS

        @pl.loop(0, per_w // SC_LANES)
        def _(i):
            tloc[pl.ds(i * SC_LANES, SC_LANES)] = jnp.zeros((SC_LANES,), jnp.int32)

        lane = lax.iota(jnp.int32, SC_LANES)

        @pl.loop(0, a // SC_CHUNK)
        def _(c):
            pltpu.sync_copy(dest_hbm.at[pl.ds(c * SC_CHUNK, SC_CHUNK)], dbuf)
            tok0 = lax.rem(c * SC_CHUNK, t)

            @pl.loop(0, SC_CHUNK // (SC_LANES * SC_UNROLL))
            def _(j):
                for u in range(SC_UNROLL):
                    off = (j * SC_UNROLL + u) * SC_LANES
                    loc = dbuf[pl.ds(off, SC_LANES)] - shifted
                    mine = jnp.logical_and(loc >= 0, loc < per_w)
                    loc = jnp.where(mine, loc, 0)
                    plsc.store_scatter(tloc, [loc], tok0 + off + lane, mask=mine)

        pltpu.sync_copy(tloc, tok_out.at[pl.ds(base, per_w)])

    fn = pl.kernel(
        body,
        out_type=jax.ShapeDtypeStruct((n_rows,), jnp.int32),
        mesh=mesh,
        scratch_types=[pltpu.VMEM((SC_CHUNK,), jnp.int32), pltpu.VMEM((per_w,), jnp.int32)],
        compiler_params=cp,
        name="row_tables",
    )
    return fn(dest.reshape(a))


ROW_TILE = (SUBLANES, LANES)


FFN_GROUP = 4
PAIR_ROWS = 2 * MOE_BLOCK
W_SETS = 3
GATHER_BATCH = 32


def _gather_rows(h2p_ref, tok_ref, row, buf, base):
    for j0 in range(0, MOE_BLOCK, GATHER_BATCH):
        vals = [h2p_ref[tok_ref[row, j]] for j in range(j0, j0 + GATHER_BATCH)]
        for j, v in zip(range(j0, j0 + GATHER_BATCH), vals):
            buf[pl.ds(H2P_CHUNKS * (base + j), H2P_CHUNKS), :] = v


def _expert_rows(buf, row0, nrows, wset, out, out_row0):
    wg_c, wu_c, wd_c = wset
    parts = []
    for cidx in range(H2P_CHUNKS):
        word = buf[pl.ds(H2P_CHUNKS * row0 + cidx, nrows, stride=H2P_CHUNKS), :]
        parts.append(lax.bitcast_convert_type(word << 16, F32))
        parts.append(lax.bitcast_convert_type(word & jnp.int32(-65536), F32))
    x = jnp.concatenate(parts, axis=1)
    g = _dot(x, wg_c[...])
    u = _dot(x, wu_c[...])
    hm = (g * jax.nn.sigmoid(g)) * u
    out[pl.ds(out_row0, nrows), :] = _dot(hm, wd_c[...])


def _ffn_kernel(seq_ref, dexp_ref, meta_ref, tokc_ref, tokn_ref, h2p_ref, wg_hbm, wu_hbm, wd_hbm,
                y_ref, buf_0, buf_1, yraw_0, yraw_1, *rest):
    i = pl.program_id(0)
    n_used = meta_ref[0]
    n_exp = meta_ref[1]
    nblk = seq_ref.shape[0]
    wsets = tuple(tuple(rest[3 * n:3 * n + 3]) for n in range(W_SETS))
    started_ref, sems = rest[3 * W_SETS:]
    bufs = (buf_0, buf_1)
    yraws = (yraw_0, yraw_1)

    def weight_copies(s, par):
        e = dexp_ref[s]
        return [pltpu.make_async_copy(src.at[e], dst, sems.at[par, n])
                for n, (src, dst) in enumerate(zip((wg_hbm, wu_hbm, wd_hbm), wsets[par]))]

    def start_expert(s):
        for par in range(W_SETS):
            @pl.when(s % W_SETS == par)
            def _():
                for cp in weight_copies(s, par):
                    cp.start()

    @pl.when(i == 0)
    def _():
        _gather_rows(h2p_ref, tokc_ref, 0, buf_0, 0)
        _gather_rows(h2p_ref, tokc_ref, 1, buf_0, MOE_BLOCK)
        yraw_1[...] = jnp.zeros_like(yraw_1)
        start_expert(0)
        started_ref[0] = 0

    for p in range(FFN_GROUP // 2):
        b_a = i * FFN_GROUP + 2 * p
        used = b_a < n_used
        s_a = seq_ref[jnp.minimum(b_a, nblk - 1)]
        s_b = seq_ref[jnp.minimum(b_a + 1, nblk - 1)]
        first_a = jnp.logical_or(b_a == 0, s_a != seq_ref[jnp.clip(b_a - 1, 0, nblk - 1)])
        same = s_a == s_b
        cur, nxt = bufs[p], bufs[1 - p]
        out_rows = pl.ds(p * PAIR_ROWS, PAIR_ROWS)

        def emit_previous():
            y_ref[out_rows] = pltpu.einshape("r(cl)->rcl", yraws[1 - p][...], c=SUBLANES).astype(y_ref.dtype)

        def gather_next():
            if p == 0:
                _gather_rows(h2p_ref, tokc_ref, 2, nxt, 0)
                _gather_rows(h2p_ref, tokc_ref, 3, nxt, MOE_BLOCK)
            else:
                _gather_rows(h2p_ref, tokn_ref, 0, nxt, 0)
                _gather_rows(h2p_ref, tokn_ref, 1, nxt, MOE_BLOCK)

        @pl.when(used)
        def _():
            started = started_ref[0]
            limit = jnp.minimum(s_a + (W_SETS - 1), n_exp - 1)
            for _unused in range(W_SETS - 1):
                go = started < limit

                @pl.when(go)
                def _():
                    start_expert(started + 1)
                started = jnp.where(go, started + 1, started)
            started_ref[0] = started

        for par in range(W_SETS):
            @pl.when(jnp.logical_and(used, jnp.logical_and(same, s_a % W_SETS == par)))
            def _():
                @pl.when(first_a)
                def _():
                    for cp in weight_copies(s_a, par):
                        cp.wait()
                gather_next()
                emit_previous()
                _expert_rows(cur, 0, PAIR_ROWS, wsets[par], yraws[p], 0)

            @pl.when(jnp.logical_and(used, jnp.logical_and(jnp.logical_not(same), s_a % W_SETS == par)))
            def _():
                @pl.when(first_a)
                def _():
                    for cp in weight_copies(s_a, par):
                        cp.wait()
                for cp in weight_copies(s_b, (par + 1) % W_SETS):
                    cp.wait()
                gather_next()
                emit_previous()
                _expert_rows(cur, 0, MOE_BLOCK, wsets[par], yraws[p], 0)
                _expert_rows(cur, MOE_BLOCK, MOE_BLOCK, wsets[(par + 1) % W_SETS], yraws[p], MOE_BLOCK)

        @pl.when(jnp.logical_not(used))
        def _():
            prev_used = jnp.logical_and(b_a >= 2, b_a - 2 < n_used)

            @pl.when(prev_used)
            def _():
                emit_previous()

            @pl.when(jnp.logical_not(prev_used))
            def _():
                y_ref[out_rows] = jnp.zeros((PAIR_ROWS,) + ROW_TILE, y_ref.dtype)


def _routed_experts(h2p, row_token, seq, dexp, meta, n_blocks, wg, wu, wd):
    d = wg.shape[1]
    de = wg.shape[2]
    ng = n_blocks // FFN_GROUP
    assert n_blocks % FFN_GROUP == 0 and FFN_GROUP == 4
    tok3 = row_token.reshape(ng, FFN_GROUP, MOE_BLOCK)
    idle_step = lambda m: ((m[0] + 1) // 2) // 2 + 1
    smem = lambda imap: pl.BlockSpec((None, FFN_GROUP, MOE_BLOCK), imap, memory_space=pltpu.SMEM)
    grid_spec = pltpu.PrefetchScalarGridSpec(
        num_scalar_prefetch=3,
        grid=(ng + 1,),
        in_specs=[
            smem(lambda i, sq, dx, m: (jnp.minimum(i, ng - 1), 0, 0)),
            smem(lambda i, sq, dx, m: (jnp.minimum(i + 1, ng - 1), 0, 0)),
            pl.BlockSpec(h2p.shape, lambda i, sq, dx, m: (0, 0, 0), pipeline_mode=pl.Buffered(1)),
            pl.BlockSpec(memory_space=pl.ANY),
            pl.BlockSpec(memory_space=pl.ANY),
            pl.BlockSpec(memory_space=pl.ANY),
        ],
        out_specs=pl.BlockSpec((FFN_GROUP * MOE_BLOCK,) + ROW_TILE,
                               lambda i, sq, dx, m: (jnp.minimum(i, idle_step(m)), 0, 0)),
        scratch_shapes=[pltpu.VMEM((PAIR_ROWS * H2P_CHUNKS, LANES), jnp.int32)] * 2 + [
            pltpu.VMEM((PAIR_ROWS, d), F32)] * 2 + [
            pltpu.VMEM((d, de), F32), pltpu.VMEM((d, de), F32), pltpu.VMEM((de, d), F32)] * W_SETS + [
            pltpu.SMEM((1,), jnp.int32), pltpu.SemaphoreType.DMA((W_SETS, 3))],
    )
    return pl.pallas_call(
        _ffn_kernel,
        grid_spec=grid_spec,
        out_shape=jax.ShapeDtypeStruct(((ng + 2) * FFN_GROUP * MOE_BLOCK,) + ROW_TILE, F32),
        compiler_params=_cparams(("arbitrary",)),
        name="routed_experts",
    )(seq, dexp, meta, tok3, tok3, h2p, wg, wu, wd)


COMBINE_TOKENS = 128


def _row_copies(src_hbm, idx_ref, buf, sem):
    return [pltpu.make_async_copy(src_hbm.at[idx_ref[k, j]], buf.at[k, j], sem)
            for k in range(TOP_K) for j in range(COMBINE_TOKENS)]


def _shared_kernel(h_ref, wsg_ref, wsu_ref, wsd_ref, o_ref):
    hb = h_ref[...]
    g = _dot(hb, wsg_ref[...])
    u = _dot(hb, wsu_ref[...])
    o_ref[...] = _dot(((g * jax.nn.sigmoid(g)) * u).astype(BF16), wsd_ref[...])


def _shared_expert(h2b, wsg, wsu, wsd, tm):
    t, d = h2b.shape
    row = pl.BlockSpec((tm, d), lambda i: (i, 0))
    full = lambda a: pl.BlockSpec(a.shape, lambda i: (0,) * a.ndim)
    return pl.pallas_call(
        _shared_kernel,
        grid=(t // tm,),
        in_specs=[row, full(wsg), full(wsu), full(wsd)],
        out_specs=row,
        out_shape=jax.ShapeDtypeStruct((t, d), F32),
        compiler_params=_cparams(("parallel",)),
        name="shared_expert",
    )(h2b, wsg, wsu, wsd)


def _final_kernel(dc_ref, dn_ref, wk_ref, x1_ref, sh_ref, g2_ref, lng_ref, lnb_ref,
                  y_hbm, o_ref, ybuf, sems):
    i = pl.program_id(0)
    nsteps = pl.num_programs(0)
    slot = i % 2

    def issue(d_ref, s):
        for n, cp in enumerate(_row_copies(y_hbm, d_ref, ybuf.at[s], sems.at[s])):
            cp.start(priority=n % 2)

    @pl.when(i == 0)
    def _():
        issue(dc_ref, 0)

    @pl.when(i + 1 < nsteps)
    def _():
        issue(dn_ref, 1 - slot)

    for cp in _row_copies(y_hbm, dc_ref, ybuf.at[slot], sems.at[slot]):
        cp.wait()
    rows = []
    for j in range(COMBINE_TOKENS):
        acc = ybuf[slot, 0, j].astype(F32) * wk_ref[0, j]
        for k in range(1, TOP_K):
            acc = acc + ybuf[slot, k, j].astype(F32) * wk_ref[k, j]
        rows.append(acc)
    routed = pltpu.einshape("tcl->t(cl)", jnp.stack(rows, axis=0))
    x2 = DN_ALPHA * x1_ref[...] + g2_ref[...] * (routed + sh_ref[...])
    o_ref[...] = _ln_rows(x2) * lng_ref[...] + lnb_ref[...]


def _combine_final(x1, shared, mod4, dest, wk, y_rows, ln_g, ln_b, seq):
    t, d = x1.shape
    tt = COMBINE_TOKENS
    nsteps = t // tt
    spb = seq // tt
    row = pl.BlockSpec((tt, d), lambda i: (i, 0))
    full = lambda a: pl.BlockSpec(a.shape, lambda i: (0,) * a.ndim)
    ln_g2, ln_b2 = ln_g.reshape(1, d), ln_b.reshape(1, d)
    return pl.pallas_call(
        _final_kernel,
        grid=(nsteps,),
        in_specs=[pl.BlockSpec((TOP_K, tt), lambda i: (0, i), memory_space=pltpu.SMEM),
                  pl.BlockSpec((TOP_K, tt), lambda i: (0, jnp.minimum(i + 1, nsteps - 1)),
                               memory_space=pltpu.SMEM),
                  pl.BlockSpec((TOP_K, tt), lambda i: (0, i), memory_space=pltpu.SMEM),
                  row, row,
                  pl.BlockSpec((None, None, 1, d), lambda i: (i // spb, 5, 0, 0)),
                  full(ln_g2), full(ln_b2),
                  pl.BlockSpec(memory_space=pl.ANY)],
        out_specs=row,
        out_shape=jax.ShapeDtypeStruct((t, d), F32),
        scratch_shapes=[pltpu.VMEM((2, TOP_K, tt) + ROW_TILE, y_rows.dtype),
                        pltpu.SemaphoreType.DMA((2,))],
        compiler_params=_cparams(("arbitrary",)),
        name="shared_combine_ln",
    )(dest, dest, wk, x1, shared, mod4, ln_g2, ln_b2, y_rows)


def kernel(x, c, w_ada, b_ada, w_in, hg_lower_bound, hg_norm_w, rel_bias, w_out, ln1_g, ln1_b, w_router,
           router_bias, w_e_gate, w_e_up, w_e_down, w_sh_gate, w_sh_up, w_sh_down, ln2_g, ln2_b):
    bsz, seq, d = x.shape
    t = bsz * seq
    assert w_ada.shape[0] == DEPTH and seq % (ATT_BRANCHES[-1][0]) == 0
    x2 = x.reshape(t, d)
    bias = _bias_tables(rel_bias)
    for l in range(DEPTH):
        mod4 = _modulation(c, w_ada[l], b_ada[l]).reshape(bsz, 6, 1, d)
        w_in_bf = _cast_bf16(w_in[l], 256)
        hq, hf, hi, hg, *qkv = _in_projection(x2, mod4, w_in_bf, seq, 512)
        nbr = len(ATT_BRANCHES)
        y_hg = _hgrn2(hq, hf, hi, hg, hg_lower_bound, hg_norm_w[l], bsz, seq)
        os_, ls_ = [], []
        for g, (_, dil) in enumerate(ATT_BRANCHES):
            o, lse = _dilated_attention(qkv[g], qkv[nbr + g], qkv[2 * nbr + g], bias[g], bsz, seq, dil)
            os_.append(o)
            ls_.append(lse)
        w_out_bf = _cast_bf16(w_out[l], 256)
        x1, h2b, h2p, logits_t = _out_projection(y_hg, os_, ls_, x2, mod4, w_out_bf, ln1_g[l], ln1_b[l],
                                                 w_router[l].T, seq, 512)
        eidx, wk, rank, cnt = _route(logits_t, router_bias[l], 256)
        dest, blk_seq, dexp, meta, n_blocks = _dispatch_plan(cnt, eidx, rank, 512)
        row_token = _row_tables(dest, n_blocks * MOE_BLOCK)
        shared = _shared_expert(h2b, _cast_bf16(w_sh_gate[l], 256), _cast_bf16(w_sh_up[l], 256),
                                _cast_bf16(w_sh_down[l], 256), 512)
        y_rows = _routed_experts(h2p, row_token, blk_seq, dexp, meta, n_blocks,
                                 w_e_gate[l], w_e_up[l], w_e_down[l])
        x2 = _combine_final(x1, shared, mod4, dest, wk, y_rows, ln2_g[l], ln2_b[l], seq)
    return x2.reshape(bsz, seq, d)
```

```python
import dataclasses
import math

import jax
import jax.numpy as jnp
import numpy as np
from jax import lax
from jax.experimental import pallas as pl
from jax.experimental.pallas import tpu as pltpu
from jax.experimental.pallas import tpu_sc as plsc

HG_HEADS = 4
HG_DK = 128
HG_WIDTH = HG_HEADS * HG_DK
ATT_BRANCHES = ((128, 1), (512, 4), (2048, 16))
ATT_HEADS_PER_BRANCH = 4
ATT_HEAD_DIM = 64
ATT_BW = ATT_HEADS_PER_BRANCH * ATT_HEAD_DIM
ATT_BLOCK = 128
REL_BUCKETS = 32
REL_MAX_DIST = 2048
N_EXPERTS = 256
TOP_K = 8
N_GROUPS = 8
TOPK_GROUPS = 4
ROUTED_SCALE = 2.5
MOE_BLOCK = 128
DEPTH = 1
DN_ALPHA = (2 * DEPTH) ** 0.25
LN_EPS = 1e-5
RMS_EPS = 1e-6

LANES = 128
SUBLANES = 8
VMEM_LIMIT_BYTES = 56 * 1024 * 1024

F32 = jnp.float32
BF16 = jnp.bfloat16
NEG_INF = float("-inf")


def _cparams(sem):
    return pltpu.CompilerParams(dimension_semantics=sem, vmem_limit_bytes=VMEM_LIMIT_BYTES)


def _ln_rows(x):
    mu = jnp.mean(x, axis=-1, keepdims=True)
    xc = x - mu
    var = jnp.mean(xc * xc, axis=-1, keepdims=True)
    return xc * lax.rsqrt(var + LN_EPS)


def _dot(a, b):
    return jnp.dot(a, b, preferred_element_type=F32)


def _dot_nt(a, b):
    return lax.dot_general(a, b, (((1,), (1,)), ((), ())), preferred_element_type=F32)


def _dot_tn(a, b):
    return lax.dot_general(a, b, (((0,), (0,)), ((), ())), preferred_element_type=F32)


def _cast_kernel(w_ref, o_ref):
    o_ref[...] = w_ref[...].astype(o_ref.dtype)


def _cast_bf16(w, rows_per_step):
    r, c = w.shape
    return pl.pallas_call(
        _cast_kernel,
        grid=(r // rows_per_step,),
        in_specs=[pl.BlockSpec((rows_per_step, c), lambda i: (i, 0))],
        out_specs=pl.BlockSpec((rows_per_step, c), lambda i: (i, 0)),
        out_shape=jax.ShapeDtypeStruct((r, c), BF16),
        compiler_params=_cparams(("parallel",)),
        name="cast_bf16",
    )(w)


def _mod_kernel(c_ref, w_ref, b_ref, o_ref):
    c = c_ref[...]
    cond = c * jax.nn.sigmoid(c)
    o_ref[...] = jnp.dot(cond, w_ref[...], preferred_element_type=F32,
                         precision=lax.Precision.HIGHEST) + b_ref[...]


def _modulation(c, w_ada, b_ada):
    bsz, d = c.shape
    n = w_ada.shape[1]
    rows = -(-bsz // SUBLANES) * SUBLANES
    cpad = jnp.zeros((rows, d), F32).at[:bsz].set(c)
    tn = 1024
    out = pl.pallas_call(
        _mod_kernel,
        grid=(n // tn,),
        in_specs=[pl.BlockSpec((rows, d), lambda j: (0, 0)),
                  pl.BlockSpec((d, tn), lambda j: (0, j)),
                  pl.BlockSpec((1, tn), lambda j: (0, j))],
        out_specs=pl.BlockSpec((rows, tn), lambda j: (0, j)),
        out_shape=jax.ShapeDtypeStruct((rows, n), F32),
        compiler_params=_cparams(("parallel",)),
        name="adaln_modulation",
    )(cpad, w_ada, b_ada.reshape(1, n))
    return out[:bsz]


_IN_HG = 4
_IN_ATT = 3 * len(ATT_BRANCHES)


def _inproj_kernel(x_ref, sc_ref, sh_ref, w_ref, *refs):
    outs = refs[:_IN_HG + _IN_ATT]
    scratch = refs[_IN_HG + _IN_ATT:]
    x = x_ref[...]
    h = _ln_rows(x) * (1.0 + sc_ref[...]) + sh_ref[...]
    hb = h.astype(BF16)
    tm = x.shape[0]
    col = 0
    n_scr = 0
    for k, o_ref in enumerate(outs):
        width = HG_WIDTH if k < _IN_HG else ATT_BW
        y = _dot(hb, w_ref[:, col:col + width])
        col += width
        if k < _IN_HG:
            o_ref[...] = y.astype(o_ref.dtype)
            continue
        if k < _IN_HG + len(ATT_BRANCHES):
            y = y * (ATT_HEAD_DIM ** -0.5)
        dil = ATT_BRANCHES[(k - _IN_HG) % len(ATT_BRANCHES)][1]
        if dil == 1:
            o_ref[...] = y.astype(o_ref.dtype)
            continue
        scr = scratch[n_scr]
        n_scr += 1
        for half in range(ATT_BW // LANES):
            scr[half] = y[:, half * LANES:(half + 1) * LANES]
        for r in range(dil):
            for half in range(ATT_BW // LANES):
                c0 = r * ATT_BW + half * LANES
                o_ref[:, c0:c0 + LANES] = scr[half, pl.ds(r, tm // dil, stride=dil), :].astype(o_ref.dtype)


def _in_projection(x2, mod4, w_in_bf, seq, tm):
    t, d = x2.shape
    steps_per_batch = seq // tm
    dils = [dil for _, dil in ATT_BRANCHES] * 3
    shapes = [(t, HG_WIDTH)] * _IN_HG + [(t // dil, dil * ATT_BW) for dil in dils]
    blocks = [(tm, HG_WIDTH)] * _IN_HG + [(tm // dil, dil * ATT_BW) for dil in dils]
    dtypes = [BF16, F32, BF16, BF16] + [BF16] * _IN_ATT
    mod_spec = lambda row: pl.BlockSpec((None, None, 1, d),
                                        lambda i, row=row: (i // steps_per_batch, row, 0, 0))
    outs = pl.pallas_call(
        _inproj_kernel,
        grid=(t // tm,),
        in_specs=[pl.BlockSpec((tm, d), lambda i: (i, 0)),
                  mod_spec(1), mod_spec(0),
                  pl.BlockSpec(w_in_bf.shape, lambda i: (0, 0))],
        out_specs=[pl.BlockSpec(b, lambda i: (i, 0)) for b in blocks],
        out_shape=[jax.ShapeDtypeStruct(s, dt) for s, dt in zip(shapes, dtypes)],
        scratch_shapes=[pltpu.VMEM((ATT_BW // LANES, tm, LANES), F32) for dil in dils if dil > 1],
        compiler_params=_cparams(("parallel",)),
        name="ln_in_projection",
    )(x2, mod4, mod4, w_in_bf)
    return outs


HG_CHUNK = 64
HG_CHUNKS_PER_STEP = 4
HG_SUB = 8
HG_LEVELS = (64, 32, 16)
LOG2E = 1.4426950408889634


def _hgrn_chunk(q, z, iv, lb):
    c = HG_CHUNK
    f = lb + (1.0 - lb) * jax.nn.sigmoid(z)
    lf = jnp.log(f)
    kk = (1.0 - lb) * jax.nn.sigmoid(-z)
    r_i = lax.broadcasted_iota(jnp.int32, (c, c), 0)
    c_i = lax.broadcasted_iota(jnp.int32, (c, c), 1)
    tril = (c_i <= r_i).astype(F32)
    b = jnp.dot(tril, lf, preferred_element_type=F32, precision=lax.Precision.HIGHEST)
    bl = b * LOG2E

    row = lax.broadcasted_iota(jnp.int32, (c, HG_DK), 0)
    scores = jnp.zeros((c, c), F32)
    for m in HG_LEVELS:
        nb = c // m
        b3 = bl.reshape(nb, m, HG_DK)
        piv = jnp.broadcast_to(b3[:, m // 2 - 1:m // 2, :], (nb, m, HG_DK)).reshape(c, HG_DK)
        second = (row % m) >= (m // 2)
        qt = jnp.where(second, q * jnp.exp2(bl - piv), 0.0)
        kt = jnp.where(second, 0.0, kk * jnp.exp2(piv - bl))
        s_m = _dot_nt(qt.astype(BF16), kt.astype(BF16))
        if nb > 1:
            s_m = jnp.where((r_i // m) == (c_i // m), s_m, 0.0)
        scores = scores + s_m
    sub = HG_SUB
    t_i = lax.broadcasted_iota(jnp.int32, (sub, 1), 0)
    lane = lax.broadcasted_iota(jnp.int32, (sub, c), 1)
    diag_rows = []
    for j in range(c // sub):
        qb = q[j * sub:(j + 1) * sub]
        kb = kk[j * sub:(j + 1) * sub]
        bb = bl[j * sub:(j + 1) * sub]
        a_j = jnp.zeros((sub, c), F32)
        for s in range(sub):
            w = qb * kb[s:s + 1] * jnp.exp2(bb - bb[s:s + 1])
            a_j = jnp.where(lane == j * sub + s, jnp.sum(w, axis=-1, keepdims=True), a_j)
        diag_rows.append(jnp.where(lane - j * sub <= t_i, a_j, 0.0))
    scores = scores + jnp.concatenate(diag_rows, axis=0)

    ivb = iv.astype(BF16)
    intra = _dot(scores.astype(BF16), ivb)
    b_last = bl[c - 1:c]
    kdec = (kk * jnp.exp2(b_last - bl)).astype(BF16)
    return (q * jnp.exp2(bl)).astype(BF16), intra, jnp.exp2(b_last), _dot_tn(ivb, kdec)


def _hgrn_kernel(q_ref, f_ref, i_ref, g_ref, lbp_ref, nw_ref, o_ref, st_ref):
    @pl.when(pl.program_id(1) == 0)
    def _():
        st_ref[...] = jnp.zeros_like(st_ref)

    lbp = lbp_ref[...]
    e = jnp.exp(lbp - jnp.max(lbp, axis=0, keepdims=True))
    lb_all = e[0:1] / jnp.sum(e, axis=0, keepdims=True)
    heads = []
    for h in range(HG_HEADS):
        sl = slice(h * HG_DK, (h + 1) * HG_DK)
        st_t = st_ref[h]
        outs = []
        for n in range(HG_CHUNKS_PER_STEP):
            rows = slice(n * HG_CHUNK, (n + 1) * HG_CHUNK)
            qdec, intra, dec_last, kv = _hgrn_chunk(q_ref[rows, sl].astype(F32), f_ref[rows, sl],
                                                    i_ref[rows, sl].astype(F32), lb_all[:, sl])
            o = _dot_nt(qdec, st_t.astype(BF16)) + intra
            st_t = st_t * dec_last + kv
            outs.append(o * lax.rsqrt(jnp.mean(o * o, axis=-1, keepdims=True) + RMS_EPS))
        st_ref[h] = st_t
        heads.append(jnp.concatenate(outs, axis=0))
    o_all = jnp.concatenate(heads, axis=-1)
    g = g_ref[...].astype(F32)
    o_ref[...] = (o_all * nw_ref[...] * (g * jax.nn.sigmoid(g))).astype(o_ref.dtype)


def _hgrn2(hq, hf, hi, hg, lb_param, norm_w, bsz, seq):
    t = hq.shape[0]
    rows = HG_CHUNK * HG_CHUNKS_PER_STEP
    nc = seq // rows
    tok = lambda b, n: (b * nc + n, 0)
    spec = pl.BlockSpec((rows, HG_WIDTH), tok)
    return pl.pallas_call(
        _hgrn_kernel,
        grid=(bsz, nc),
        in_specs=[spec, spec, spec, spec,
                  pl.BlockSpec(lb_param.shape, lambda b, n: (0, 0)),
                  pl.BlockSpec((1, HG_WIDTH), lambda b, n: (0, 0))],
        out_specs=spec,
        out_shape=jax.ShapeDtypeStruct((t, HG_WIDTH), BF16),
        scratch_shapes=[pltpu.VMEM((HG_HEADS, HG_DK, HG_DK), F32)],
        compiler_params=_cparams(("parallel", "arbitrary")),
        name="hgrn2_scan",
    )(hq, hf, hi, hg, lb_param, norm_w.reshape(1, HG_WIDTH))


def _t5_bucket_np(dist):
    max_exact = REL_BUCKETS // 2
    n = np.maximum(dist, 0)
    nf = np.maximum(n, 1).astype(np.float32)
    large = max_exact + (np.log(nf / np.float32(max_exact)) / np.float32(math.log(REL_MAX_DIST / max_exact))
                         * np.float32(REL_BUCKETS - max_exact)).astype(np.int32)
    large = np.minimum(large, REL_BUCKETS - 1)
    return np.where(n < max_exact, n, large).astype(np.int32)


def _band_tables():
    w = ATT_BLOCK
    qi = np.arange(w)[:, None]
    ki = np.arange(2 * w)[None, :]
    m = w + qi - ki
    band = (m >= 0) & (m <= w)
    buckets = np.stack([_t5_bucket_np(m * dil) for _, dil in ATT_BRANCHES])
    return buckets, band


def _bias_kernel(rb_ref, bucket_ref, o_ref):
    g = pl.program_id(0)
    w = ATT_BLOCK
    bucket = bucket_ref[...]
    qi = lax.broadcasted_iota(jnp.int32, (w, 2 * w), 0)
    ki = lax.broadcasted_iota(jnp.int32, (w, 2 * w), 1)
    m = w + qi - ki
    band = (m >= 0) & (m <= w)
    for h in range(ATT_HEADS_PER_BRANCH):
        acc = jnp.zeros((w, 2 * w), F32)
        for c in range(REL_BUCKETS):
            acc = jnp.where(bucket == c, rb_ref[c, g * ATT_HEADS_PER_BRANCH + h], acc)
        full = jnp.where(band, acc, NEG_INF)
        o_ref[1, h] = full
        o_ref[0, h] = jnp.where(ki >= w, full, NEG_INF)


def _bias_tables(rel_bias):
    buckets, _ = _band_tables()
    g = len(ATT_BRANCHES)
    w = ATT_BLOCK
    return pl.pallas_call(
        _bias_kernel,
        grid=(g,),
        in_specs=[pl.BlockSpec(memory_space=pltpu.SMEM),
                  pl.BlockSpec((None, w, 2 * w), lambda i: (i, 0, 0))],
        out_specs=pl.BlockSpec((None, 2, ATT_HEADS_PER_BRANCH, w, 2 * w), lambda i: (i, 0, 0, 0, 0)),
        out_shape=jax.ShapeDtypeStruct((g, 2, ATT_HEADS_PER_BRANCH, w, 2 * w), F32),
        compiler_params=_cparams(("parallel",)),
        name="rel_bias_tables",
    )(rel_bias, jnp.asarray(buckets))


ATT_BLOCKS_PER_STEP = 8


def _attn_kernel(q_ref, kp_ref, kc_ref, vp_ref, vc_ref, bias_ref, o_ref, lse_ref):
    m = pl.program_id(2)
    w = ATT_BLOCK
    hb = ATT_HEADS_PER_BRANCH
    lane = lax.broadcasted_iota(jnp.int32, (w, ATT_BW), 1) // ATT_HEAD_DIM
    for res, blk in [(r, b) for r in range(q_ref.shape[1] // ATT_BW) for b in range(q_ref.shape[0] // w)]:
        cols = slice(res * ATT_BW, (res + 1) * ATT_BW)
        kall = jnp.concatenate([kp_ref[:, cols], kc_ref[:, cols]], axis=0)
        vall = jnp.concatenate([vp_ref[:, cols], vc_ref[:, cols]], axis=0)
        q = q_ref[blk * w:(blk + 1) * w, cols]
        q4 = jnp.concatenate([jnp.where(lane == h, q, jnp.zeros_like(q)) for h in range(hb)], axis=0)
        kk = kall[blk * w:(blk + 2) * w]
        vv = vall[blk * w:(blk + 2) * w]
        s4 = _dot_nt(q4, kk)
        bias = bias_ref[jnp.minimum(m, 1)] if blk == 0 else bias_ref[1]
        s4 = s4 + bias.reshape(hb * w, 2 * w)
        mx = jnp.max(s4, axis=-1, keepdims=True)
        p = jnp.exp(s4 - mx)
        l = jnp.sum(p, axis=-1, keepdims=True)
        o4 = _dot((p / l).astype(vv.dtype), vv)
        lse4 = mx + jnp.log(l)
        o = jnp.zeros((w, ATT_BW), F32)
        lse = jnp.zeros((w, ATT_BW), F32)
        for h in range(hb):
            o = jnp.where(lane == h, o4[h * w:(h + 1) * w], o)
            lse = jnp.where(lane == h, lse4[h * w:(h + 1) * w], lse)
        o_ref[blk * w:(blk + 1) * w, cols] = o.astype(o_ref.dtype)
        lse_ref[blk * w:(blk + 1) * w, cols] = lse


def _dilated_attention(q, k, v, bias_g, bsz, seq, dilation):
    w = ATT_BLOCK
    l = seq // dilation
    pstep = math.gcd(ATT_BLOCKS_PER_STEP, l // w)
    nb = l // (w * pstep)
    rstep = math.gcd(ATT_BLOCKS_PER_STEP // pstep, dilation)
    view = lambda a: a.reshape(bsz, l, dilation * ATT_BW)
    cur = pl.BlockSpec((None, pstep * w, rstep * ATT_BW), lambda b, r, n: (b, n, r))
    prev = pl.BlockSpec((None, w, rstep * ATT_BW), lambda b, r, n: (b, jnp.maximum(pstep * n - 1, 0), r))
    o, lse = pl.pallas_call(
        _attn_kernel,
        grid=(bsz, dilation // rstep, nb),
        in_specs=[cur, prev, cur, prev, cur,
                  pl.BlockSpec(bias_g.shape, lambda b, r, n: (0, 0, 0, 0))],
        out_specs=[cur, cur],
        out_shape=[jax.ShapeDtypeStruct((bsz, l, dilation * ATT_BW), BF16),
                   jax.ShapeDtypeStruct((bsz, l, dilation * ATT_BW), F32)],
        compiler_params=_cparams(("parallel", "parallel", "arbitrary")),
        name=f"dilated_attention_d{dilation}",
    )(view(q), view(k), view(k), view(v), view(v), bias_g)
    return o.reshape(bsz * l, dilation * ATT_BW), lse.reshape(bsz * l, dilation * ATT_BW)


def _split_bf16(a):
    hi = a.astype(BF16)
    lo = (a - hi.astype(F32)).astype(BF16)
    return hi, lo


H2P_CHUNKS = 4


def _token_order(ref, scr, dil):
    if dil == 1:
        return ref[...].astype(F32)
    n = ref.shape[0]
    halves = ATT_BW // LANES
    for r in range(dil):
        for half in range(halves):
            c0 = r * ATT_BW + half * LANES
            scr[half, pl.ds(r, n, stride=dil), :] = ref[:, c0:c0 + LANES].astype(F32)
    return jnp.concatenate([scr[half] for half in range(halves)], axis=1)


def _outproj_kernel(yhg_ref, o1_ref, o2_ref, o3_ref, l1_ref, l2_ref, l3_ref, x_ref,
                    g1_ref, sc2_ref, sh2_ref, wout_ref, lng_ref, lnb_ref, wrt_ref,
                    x1_ref, h2_ref, h2p_ref, lgt_ref, *scratch):
    dils = [dil for _, dil in ATT_BRANCHES]
    scr = iter(scratch)
    o1, o2, o3 = [_token_order(r, None if dil == 1 else next(scr), dil)
                  for r, dil in zip((o1_ref, o2_ref, o3_ref), dils)]
    l1, l2, l3 = [_token_order(r, None if dil == 1 else next(scr), dil)
                  for r, dil in zip((l1_ref, l2_ref, l3_ref), dils)]
    mx = jnp.maximum(jnp.maximum(l1, l2), l3)
    e1, e2, e3 = jnp.exp(l1 - mx), jnp.exp(l2 - mx), jnp.exp(l3 - mx)
    den = e1 + e2 + e3
    att = (e1 / den) * o1 + (e2 / den) * o2 + (e3 / den) * o3
    mix = _dot(yhg_ref[...], wout_ref[:HG_WIDTH, :]) + _dot(att.astype(BF16), wout_ref[HG_WIDTH:, :])
    x1 = _ln_rows(DN_ALPHA * x_ref[...] + g1_ref[...] * mix) * lng_ref[...] + lnb_ref[...]
    x1_ref[...] = x1
    h2 = _ln_rows(x1) * (1.0 + sc2_ref[...]) + sh2_ref[...]
    h_hi = h2.astype(BF16)
    h_hf = h_hi.astype(F32)
    h2_ref[...] = h_hi
    bits = lax.bitcast_convert_type(h_hf, jnp.uint32)
    for cidx in range(H2P_CHUNKS):
        lo = bits[:, 2 * LANES * cidx:2 * LANES * cidx + LANES]
        hi = bits[:, 2 * LANES * cidx + LANES:2 * LANES * (cidx + 1)]
        h2p_ref[:, cidx, :] = lax.bitcast_convert_type((lo >> 16) | hi, jnp.int32)
    h_lo = (h2 - h_hf).astype(BF16)
    w_hi, w_lo = _split_bf16(wrt_ref[...])
    lgt_ref[...] = _dot_nt(w_hi, h_hi) + (_dot_nt(w_hi, h_lo) + _dot_nt(w_lo, h_hi))


def _out_projection(yhg, os_, ls_, x2, mod4, w_out_bf, ln_g, ln_b, w_router_t, seq, tm):
    t, d = x2.shape
    spb = seq // tm
    ne = w_router_t.shape[0]
    row = lambda w: pl.BlockSpec((tm, w), lambda i: (i, 0))
    mod_spec = lambda r: pl.BlockSpec((None, None, 1, d), lambda i, r=r: (i // spb, r, 0, 0))
    full = lambda a: pl.BlockSpec(a.shape, lambda i: (0,) * a.ndim)
    ln_g2, ln_b2 = ln_g.reshape(1, d), ln_b.reshape(1, d)
    dils = [dil for _, dil in ATT_BRANCHES]
    branch = [pl.BlockSpec((tm // dil, dil * ATT_BW), lambda i: (i, 0)) for dil in dils]
    return pl.pallas_call(
        _outproj_kernel,
        grid=(t // tm,),
        in_specs=[row(HG_WIDTH)] + branch + branch + [row(d),
                  mod_spec(2), mod_spec(4), mod_spec(3),
                  full(w_out_bf), full(ln_g2), full(ln_b2), full(w_router_t)],
        out_specs=[row(d), row(d), pl.BlockSpec((tm, H2P_CHUNKS, LANES), lambda i: (i, 0, 0)),
                   pl.BlockSpec((ne, tm), lambda i: (0, i))],
        out_shape=[jax.ShapeDtypeStruct((t, d), F32), jax.ShapeDtypeStruct((t, d), BF16),
                   jax.ShapeDtypeStruct((t, H2P_CHUNKS, LANES), jnp.int32),
                   jax.ShapeDtypeStruct((ne, t), F32)],
        scratch_shapes=[pltpu.VMEM((ATT_BW // LANES, tm, LANES), F32) for dil in dils + dils if dil > 1],
        compiler_params=_cparams(("parallel",)),
        name="merge_outproj_ln",
    )(yhg, *os_, *ls_, x2, mod4, mod4, mod4, w_out_bf, ln_g2, ln_b2, w_router_t)


def _argmax_rows(cur, iota, nrows):
    m = jnp.max(cur, axis=0, keepdims=True)
    idx = jnp.min(jnp.where(cur == m, iota, nrows), axis=0, keepdims=True)
    return m, idx, iota == idx


def _route_kernel(lgt_ref, rb_ref, eidx_ref, w_ref, rank_ref, cnt_ref, carry):
    ne = N_EXPERTS
    gsz = ne // N_GROUPS
    tt = lgt_ref.shape[1]

    @pl.when(pl.program_id(0) == 0)
    def _():
        carry[...] = jnp.zeros_like(carry)

    sc = jax.nn.sigmoid(lgt_ref[...])
    biased = sc + rb_ref[...]
    g3 = biased.reshape(N_GROUPS, gsz, tt)
    io3 = lax.broadcasted_iota(jnp.int32, (N_GROUPS, gsz, tt), 1)
    m1 = jnp.max(g3, axis=1, keepdims=True)
    first = jnp.min(jnp.where(g3 == m1, io3, gsz), axis=1, keepdims=True)
    m2 = jnp.max(jnp.where(io3 == first, NEG_INF, g3), axis=1, keepdims=True)
    gs = (m1 + m2).reshape(N_GROUPS, tt)
    io8 = lax.broadcasted_iota(jnp.int32, (N_GROUPS, tt), 0)
    sel = jnp.zeros((N_GROUPS, tt), jnp.int32)
    cur = gs
    for _ in range(TOPK_GROUPS):
        _, _, pick = _argmax_rows(cur, io8, N_GROUPS)
        sel = jnp.where(pick, 1, sel)
        cur = jnp.where(pick, NEG_INF, cur)
    masked = jnp.where(sel.reshape(N_GROUPS, 1, tt) > 0, g3, NEG_INF).reshape(ne, tt)
    ioe = lax.broadcasted_iota(jnp.int32, (ne, tt), 0)
    cur = masked
    idxs, ws, picks = [], [], []
    for _ in range(TOP_K):
        _, idx, pick = _argmax_rows(cur, ioe, ne)
        idxs.append(idx)
        picks.append(pick)
        ws.append(jnp.sum(jnp.where(pick, sc, 0.0), axis=0, keepdims=True))
        cur = jnp.where(pick, NEG_INF, cur)
    wk = jnp.concatenate(ws, axis=0)
    eidx_ref[...] = jnp.concatenate(idxs, axis=0)
    w_ref[...] = wk / jnp.sum(wk, axis=0, keepdims=True) * ROUTED_SCALE
    chosen = jnp.where(cur == NEG_INF, jnp.where(masked == NEG_INF, 0.0, 1.0), 0.0)
    r_i = lax.broadcasted_iota(jnp.int32, (tt, tt), 0)
    c_i = lax.broadcasted_iota(jnp.int32, (tt, tt), 1)
    before = jnp.where(r_i < c_i, 1.0, 0.0).astype(BF16)
    pref = _dot(chosen.astype(BF16), before) + carry[...]
    rank_ref[...] = jnp.concatenate(
        [jnp.sum(jnp.where(p, pref, 0.0), axis=0, keepdims=True) for p in picks], axis=0).astype(jnp.int32)
    carry[...] = carry[...] + jnp.sum(chosen, axis=1, keepdims=True)
    cnt_ref[...] = carry[...]


def _route(logits_t, router_bias, tt):
    ne, t = logits_t.shape
    tok = pl.BlockSpec((TOP_K, tt), lambda i: (0, i))
    return pl.pallas_call(
        _route_kernel,
        grid=(t // tt,),
        in_specs=[pl.BlockSpec((ne, tt), lambda i: (0, i)),
                  pl.BlockSpec((ne, 1), lambda i: (0, 0))],
        out_specs=[tok, tok, tok, pl.BlockSpec((ne, 1), lambda i: (0, 0))],
        out_shape=[jax.ShapeDtypeStruct((TOP_K, t), jnp.int32), jax.ShapeDtypeStruct((TOP_K, t), F32),
                   jax.ShapeDtypeStruct((TOP_K, t), jnp.int32), jax.ShapeDtypeStruct((ne, 1), F32)],
        scratch_shapes=[pltpu.VMEM((ne, 1), F32)],
        compiler_params=_cparams(("arbitrary",)),
        name="router_topk",
    )(logits_t, router_bias.reshape(ne, 1))


def _plan_kernel(cnt_ref, eidx_ref, rank_ref, dest_ref, seq_ref, dexp_ref, meta_ref, pstart_ref):
    ne = N_EXPERTS
    tt = eidx_ref.shape[1]
    nblk = seq_ref.shape[1]

    @pl.when(pl.program_id(0) == 0)
    def _():
        cnt = cnt_ref[...].astype(jnp.int32)
        padded = ((cnt + (MOE_BLOCK - 1)) // MOE_BLOCK) * MOE_BLOCK
        r_i = lax.broadcasted_iota(jnp.int32, (ne, ne), 0)
        c_i = lax.broadcasted_iota(jnp.int32, (ne, ne), 1)
        incl = jnp.where(c_i <= r_i, 1.0, 0.0)
        pend = jnp.dot(incl, jnp.broadcast_to(padded.astype(F32), (ne, LANES)),
                       preferred_element_type=F32, precision=lax.Precision.HIGHEST)[:, 0:1]
        pend = pend.astype(jnp.int32)
        pstart_ref[...] = pend - padded
        blk0 = lax.broadcasted_iota(jnp.int32, (ne, nblk), 1) * MOE_BLOCK
        be = jnp.minimum(jnp.sum(jnp.where(pend <= blk0, 1, 0), axis=0, keepdims=True), ne - 1)
        present = cnt > 0
        strict = jnp.where(c_i < r_i, 1.0, 0.0).astype(BF16)
        sidx = _dot(strict, jnp.broadcast_to(jnp.where(present, 1.0, 0.0), (ne, LANES)).astype(BF16))[:, 0:1]
        sidx = sidx.astype(jnp.int32)
        dexp_ref[...] = jnp.sum(jnp.where(jnp.logical_and(present, sidx == c_i), r_i, 0), axis=0, keepdims=True)
        ioeb = lax.broadcasted_iota(jnp.int32, (ne, nblk), 0)
        nu = jnp.max(pend, axis=0, keepdims=True) // MOE_BLOCK
        nd = jnp.sum(jnp.where(present, 1, 0), axis=0, keepdims=True)
        seq_ref[...] = jnp.minimum(jnp.sum(jnp.where(ioeb == be, sidx, 0), axis=0, keepdims=True), nd - 1)
        lane = lax.broadcasted_iota(jnp.int32, (1, LANES), 1)
        meta_ref[...] = jnp.where(lane == 0, nu, jnp.where(lane == 1, nd, 0))

    pstart = pstart_ref[...]
    ioe = lax.broadcasted_iota(jnp.int32, (ne, tt), 0)
    rows = []
    for k in range(TOP_K):
        sel = ioe == eidx_ref[k:k + 1, :]
        rows.append(jnp.sum(jnp.where(sel, pstart, 0), axis=0, keepdims=True))
    dest_ref[...] = jnp.concatenate(rows, axis=0) + rank_ref[...]


def _dispatch_plan(cnt, eidx, rank, tt):
    k, t = eidx.shape
    n_blocks = -(-(t * k) // MOE_BLOCK) + N_EXPERTS
    tok = pl.BlockSpec((k, tt), lambda i: (0, i))
    one = lambda n: pl.BlockSpec((1, n), lambda i: (0, 0))
    dest, seq, dexp, meta = pl.pallas_call(
        _plan_kernel,
        grid=(t // tt,),
        in_specs=[pl.BlockSpec(cnt.shape, lambda i: (0, 0)), tok, tok],
        out_specs=[tok, one(n_blocks), one(N_EXPERTS), one(LANES)],
        out_shape=[jax.ShapeDtypeStruct((k, t), jnp.int32), jax.ShapeDtypeStruct((1, n_blocks), jnp.int32),
                   jax.ShapeDtypeStruct((1, N_EXPERTS), jnp.int32), jax.ShapeDtypeStruct((1, LANES), jnp.int32)],
        scratch_shapes=[pltpu.VMEM((N_EXPERTS, 1), jnp.int32)],
        compiler_params=_cparams(("arbitrary",)),
        name="dispatch_plan",
    )(cnt, eidx, rank)
    return dest, seq.reshape(n_blocks), dexp.reshape(N_EXPERTS), meta.reshape(LANES), n_blocks


SC_CORES = 2
SC_SUBCORES = 16
SC_LANES = 16
SC_CHUNK = 16384
SC_UNROLL = 4


def _row_tables(dest, n_rows):
    k, t = dest.shape
    a = k * t
    nw = SC_CORES * SC_SUBCORES
    per_w = n_rows // nw
    assert n_rows % (nw * SC_LANES) == 0 and t % SC_CHUNK == 0
    mesh = plsc.VectorSubcoreMesh(core_axis_name="c", subcore_axis_name="s")
    cp = pltpu.CompilerParams()
    if "needs_layout_passes" in pltpu.CompilerParams.__dataclass_fields__:
        cp = dataclasses.replace(cp, needs_layout_passes=False)

    def body(dest_hbm, tok_out, dbuf, tloc):
        wid = lax.axis_index("s") * SC_CORES + lax.axis_index("c")
        base = wid * per_w

        @pl.loop(0, per_w // SC_LANES)
        def _(i):
            tloc[pl.ds(i * SC_LANES, SC_LANES)] = jnp.zeros((SC_LANES,), jnp.int32)

        lane = lax.iota(jnp.int32, SC_LANES)

        @pl.loop(0, a // SC_CHUNK)
        def _(c):
            pltpu.sync_copy(dest_hbm.at[pl.ds(c * SC_CHUNK, SC_CHUNK)], dbuf)
            tok0 = lax.rem(c * SC_CHUNK, t)

            @pl.loop(0, SC_CHUNK // (SC_LANES * SC_UNROLL))
            def _(j):
                for u in range(SC_UNROLL):
                    off = (j * SC_UNROLL + u) * SC_LANES
                    loc = dbuf[pl.ds(off, SC_LANES)] - base
                    mine = jnp.logical_and(loc >= 0, loc < per_w)
                    loc = jnp.where(mine, loc, 0)
                    plsc.store_scatter(tloc, [loc], tok0 + off + lane, mask=mine)

        pltpu.sync_copy(tloc, tok_out.at[pl.ds(base, per_w)])

    fn = pl.kernel(
        body,
        out_type=jax.ShapeDtypeStruct((n_rows,), jnp.int32),
        mesh=mesh,
        scratch_types=[pltpu.VMEM((SC_CHUNK,), jnp.int32), pltpu.VMEM((per_w,), jnp.int32)],
        compiler_params=cp,
        name="row_tables",
    )
    return fn(dest.reshape(a))


ROW_TILE = (SUBLANES, LANES)


FFN_GROUP = 4
PAIR_ROWS = 2 * MOE_BLOCK
W_SETS = 3
GATHER_BATCH = 32


def _gather_rows(h2p_ref, tok_ref, row, buf, base):
    for j0 in range(0, MOE_BLOCK, GATHER_BATCH):
        vals = [h2p_ref[tok_ref[row, j]] for j in range(j0, j0 + GATHER_BATCH)]
        for j, v in zip(range(j0, j0 + GATHER_BATCH), vals):
            buf[pl.ds(H2P_CHUNKS * (base + j), H2P_CHUNKS), :] = v


def _expert_rows(buf, row0, nrows, wset, out, out_row0):
    wg_c, wu_c, wd_c = wset
    parts = []
    for cidx in range(H2P_CHUNKS):
        word = buf[pl.ds(H2P_CHUNKS * row0 + cidx, nrows, stride=H2P_CHUNKS), :]
        parts.append(lax.bitcast_convert_type(word << 16, F32))
        parts.append(lax.bitcast_convert_type(word & jnp.int32(-65536), F32))
    x = jnp.concatenate(parts, axis=1)
    g = _dot(x, wg_c[...])
    u = _dot(x, wu_c[...])
    hm = (g * jax.nn.sigmoid(g)) * u
    res = _dot(hm, wd_c[...])
    nch = res.shape[1] // LANES
    for c in range(nch):
        out[pl.ds(nch * out_row0 + c, nrows, stride=nch), :] = res[:, c * LANES:(c + 1) * LANES]


def _ffn_kernel(seq_ref, dexp_ref, meta_ref, tokc_ref, tokn_ref, h2p_ref, wg_hbm, wu_hbm, wd_hbm,
                y_ref, buf_0, buf_1, *rest):
    i = pl.program_id(0)
    n_used = meta_ref[0]
    n_exp = meta_ref[1]
    nblk = seq_ref.shape[0]
    wsets = tuple(tuple(rest[3 * n:3 * n + 3]) for n in range(W_SETS))
    started_ref, sems = rest[3 * W_SETS:]
    bufs = (buf_0, buf_1)

    def weight_copies(s, par):
        e = dexp_ref[s]
        return [pltpu.make_async_copy(src.at[e], dst, sems.at[par, n])
                for n, (src, dst) in enumerate(zip((wg_hbm, wu_hbm, wd_hbm), wsets[par]))]

    def start_expert(s):
        for par in range(W_SETS):
            @pl.when(s % W_SETS == par)
            def _():
                for cp in weight_copies(s, par):
                    cp.start()

    @pl.when(i == 0)
    def _():
        _gather_rows(h2p_ref, tokc_ref, 0, buf_0, 0)
        _gather_rows(h2p_ref, tokc_ref, 1, buf_0, MOE_BLOCK)
        start_expert(0)
        started_ref[0] = 0

    for p in range(FFN_GROUP // 2):
        b_a = i * FFN_GROUP + 2 * p
        used = b_a < n_used
        s_a = seq_ref[jnp.minimum(b_a, nblk - 1)]
        s_b = seq_ref[jnp.minimum(b_a + 1, nblk - 1)]
        first_a = jnp.logical_or(b_a == 0, s_a != seq_ref[jnp.clip(b_a - 1, 0, nblk - 1)])
        same = s_a == s_b
        cur, nxt = bufs[p], bufs[1 - p]
        out_row0 = p * PAIR_ROWS

        def gather_next():
            if p == 0:
                _gather_rows(h2p_ref, tokc_ref, 2, nxt, 0)
                _gather_rows(h2p_ref, tokc_ref, 3, nxt, MOE_BLOCK)
            else:
                _gather_rows(h2p_ref, tokn_ref, 0, nxt, 0)
                _gather_rows(h2p_ref, tokn_ref, 1, nxt, MOE_BLOCK)

        @pl.when(used)
        def _():
            started = started_ref[0]
            limit = jnp.minimum(s_a + (W_SETS - 1), n_exp - 1)
            for _unused in range(W_SETS - 1):
                go = started < limit

                @pl.when(go)
                def _():
                    start_expert(started + 1)
                started = jnp.where(go, started + 1, started)
            started_ref[0] = started

        for par in range(W_SETS):
            @pl.when(jnp.logical_and(used, jnp.logical_and(same, s_a % W_SETS == par)))
            def _():
                @pl.when(first_a)
                def _():
                    for cp in weight_copies(s_a, par):
                        cp.wait()
                gather_next()
                _expert_rows(cur, 0, PAIR_ROWS, wsets[par], y_ref, out_row0)

            @pl.when(jnp.logical_and(used, jnp.logical_and(jnp.logical_not(same), s_a % W_SETS == par)))
            def _():
                @pl.when(first_a)
                def _():
                    for cp in weight_copies(s_a, par):
                        cp.wait()
                for cp in weight_copies(s_b, (par + 1) % W_SETS):
                    cp.wait()
                gather_next()
                _expert_rows(cur, 0, MOE_BLOCK, wsets[par], y_ref, out_row0)
                _expert_rows(cur, MOE_BLOCK, MOE_BLOCK, wsets[(par + 1) % W_SETS], y_ref, out_row0 + MOE_BLOCK)

        @pl.when(jnp.logical_not(used))
        def _():
            y_ref[pl.ds(out_row0 * SUBLANES, PAIR_ROWS * SUBLANES), :] = jnp.zeros(
                (PAIR_ROWS * SUBLANES, LANES), y_ref.dtype)


def _routed_experts(h2p, row_token, seq, dexp, meta, n_blocks, wg, wu, wd):
    d = wg.shape[1]
    de = wg.shape[2]
    ng = n_blocks // FFN_GROUP
    assert n_blocks % FFN_GROUP == 0 and FFN_GROUP == 4 and d == SUBLANES * LANES
    tok3 = row_token.reshape(ng, FFN_GROUP, MOE_BLOCK)
    idle_step = lambda m: jnp.minimum((m[0] + FFN_GROUP - 1) // FFN_GROUP, ng - 1)
    smem = lambda imap: pl.BlockSpec((None, FFN_GROUP, MOE_BLOCK), imap, memory_space=pltpu.SMEM)
    grid_spec = pltpu.PrefetchScalarGridSpec(
        num_scalar_prefetch=3,
        grid=(ng,),
        in_specs=[
            smem(lambda i, sq, dx, m: (jnp.minimum(i, ng - 1), 0, 0)),
            smem(lambda i, sq, dx, m: (jnp.minimum(i + 1, ng - 1), 0, 0)),
            pl.BlockSpec(h2p.shape, lambda i, sq, dx, m: (0, 0, 0), pipeline_mode=pl.Buffered(1)),
            pl.BlockSpec(memory_space=pl.ANY),
            pl.BlockSpec(memory_space=pl.ANY),
            pl.BlockSpec(memory_space=pl.ANY),
        ],
        out_specs=pl.BlockSpec((FFN_GROUP * MOE_BLOCK * SUBLANES, LANES),
                               lambda i, sq, dx, m: (jnp.minimum(i, idle_step(m)), 0)),
        scratch_shapes=[pltpu.VMEM((PAIR_ROWS * H2P_CHUNKS, LANES), jnp.int32)] * 2 + [
            pltpu.VMEM((d, de), F32), pltpu.VMEM((d, de), F32), pltpu.VMEM((de, d), F32)] * W_SETS + [
            pltpu.SMEM((1,), jnp.int32), pltpu.SemaphoreType.DMA((W_SETS, 3))],
    )
    y = pl.pallas_call(
        _ffn_kernel,
        grid_spec=grid_spec,
        out_shape=jax.ShapeDtypeStruct((n_blocks * MOE_BLOCK * SUBLANES, LANES), F32),
        compiler_params=_cparams(("arbitrary",)),
        name="routed_experts",
    )(seq, dexp, meta, tok3, tok3, h2p, wg, wu, wd)
    return y.reshape((n_blocks * MOE_BLOCK,) + ROW_TILE)


COMBINE_TOKENS = 128


def _row_copies(src_hbm, idx_ref, buf, sem):
    return [pltpu.make_async_copy(src_hbm.at[idx_ref[k, j]], buf.at[k, j], sem)
            for k in range(TOP_K) for j in range(COMBINE_TOKENS)]


def _shared_kernel(h_ref, wsg_ref, wsu_ref, wsd_ref, o_ref):
    hb = h_ref[...]
    g = _dot(hb, wsg_ref[...])
    u = _dot(hb, wsu_ref[...])
    o_ref[...] = _dot(((g * jax.nn.sigmoid(g)) * u).astype(BF16), wsd_ref[...])


def _shared_expert(h2b, wsg, wsu, wsd, tm):
    t, d = h2b.shape
    row = pl.BlockSpec((tm, d), lambda i: (i, 0))
    full = lambda a: pl.BlockSpec(a.shape, lambda i: (0,) * a.ndim)
    return pl.pallas_call(
        _shared_kernel,
        grid=(t // tm,),
        in_specs=[row, full(wsg), full(wsu), full(wsd)],
        out_specs=row,
        out_shape=jax.ShapeDtypeStruct((t, d), F32),
        compiler_params=_cparams(("parallel",)),
        name="shared_expert",
    )(h2b, wsg, wsu, wsd)


def _final_kernel(dc_ref, dn_ref, wk_ref, x1_ref, sh_ref, g2_ref, lng_ref, lnb_ref,
                  y_hbm, o_ref, ybuf, sems):
    i = pl.program_id(0)
    nsteps = pl.num_programs(0)
    slot = i % 2

    def issue(d_ref, s):
        for n, cp in enumerate(_row_copies(y_hbm, d_ref, ybuf.at[s], sems.at[s])):
            cp.start(priority=n % 2)

    @pl.when(i == 0)
    def _():
        issue(dc_ref, 0)

    @pl.when(i + 1 < nsteps)
    def _():
        issue(dn_ref, 1 - slot)

    for cp in _row_copies(y_hbm, dc_ref, ybuf.at[slot], sems.at[slot]):
        cp.wait()
    rows = []
    for j in range(COMBINE_TOKENS):
        acc = ybuf[slot, 0, j].astype(F32) * wk_ref[0, j]
        for k in range(1, TOP_K):
            acc = acc + ybuf[slot, k, j].astype(F32) * wk_ref[k, j]
        rows.append(acc)
    routed = pltpu.einshape("tcl->t(cl)", jnp.stack(rows, axis=0))
    x2 = DN_ALPHA * x1_ref[...] + g2_ref[...] * (routed + sh_ref[...])
    o_ref[...] = _ln_rows(x2) * lng_ref[...] + lnb_ref[...]


def _combine_final(x1, shared, mod4, dest, wk, y_rows, ln_g, ln_b, seq):
    t, d = x1.shape
    tt = COMBINE_TOKENS
    nsteps = t // tt
    spb = seq // tt
    row = pl.BlockSpec((tt, d), lambda i: (i, 0))
    full = lambda a: pl.BlockSpec(a.shape, lambda i: (0,) * a.ndim)
    ln_g2, ln_b2 = ln_g.reshape(1, d), ln_b.reshape(1, d)
    return pl.pallas_call(
        _final_kernel,
        grid=(nsteps,),
        in_specs=[pl.BlockSpec((TOP_K, tt), lambda i: (0, i), memory_space=pltpu.SMEM),
                  pl.BlockSpec((TOP_K, tt), lambda i: (0, jnp.minimum(i + 1, nsteps - 1)),
                               memory_space=pltpu.SMEM),
                  pl.BlockSpec((TOP_K, tt), lambda i: (0, i), memory_space=pltpu.SMEM),
                  row, row,
                  pl.BlockSpec((None, None, 1, d), lambda i: (i // spb, 5, 0, 0)),
                  full(ln_g2), full(ln_b2),
                  pl.BlockSpec(memory_space=pl.ANY)],
        out_specs=row,
        out_shape=jax.ShapeDtypeStruct((t, d), F32),
        scratch_shapes=[pltpu.VMEM((2, TOP_K, tt) + ROW_TILE, y_rows.dtype),
                        pltpu.SemaphoreType.DMA((2,))],
        compiler_params=_cparams(("arbitrary",)),
        name="shared_combine_ln",
    )(dest, dest, wk, x1, shared, mod4, ln_g2, ln_b2, y_rows)


def kernel(x, c, w_ada, b_ada, w_in, hg_lower_bound, hg_norm_w, rel_bias, w_out, ln1_g, ln1_b, w_router,
           router_bias, w_e_gate, w_e_up, w_e_down, w_sh_gate, w_sh_up, w_sh_down, ln2_g, ln2_b):
    bsz, seq, d = x.shape
    t = bsz * seq
    assert w_ada.shape[0] == DEPTH and seq % (ATT_BRANCHES[-1][0]) == 0
    x2 = x.reshape(t, d)
    bias = _bias_tables(rel_bias)
    for l in range(DEPTH):
        mod4 = _modulation(c, w_ada[l], b_ada[l]).reshape(bsz, 6, 1, d)
        w_in_bf = _cast_bf16(w_in[l], 256)
        hq, hf, hi, hg, *qkv = _in_projection(x2, mod4, w_in_bf, seq, 512)
        nbr = len(ATT_BRANCHES)
        y_hg = _hgrn2(hq, hf, hi, hg, hg_lower_bound, hg_norm_w[l], bsz, seq)
        os_, ls_ = [], []
        for g, (_, dil) in enumerate(ATT_BRANCHES):
            o, lse = _dilated_attention(qkv[g], qkv[nbr + g], qkv[2 * nbr + g], bias[g], bsz, seq, dil)
            os_.append(o)
            ls_.append(lse)
        w_out_bf = _cast_bf16(w_out[l], 256)
        x1, h2b, h2p, logits_t = _out_projection(y_hg, os_, ls_, x2, mod4, w_out_bf, ln1_g[l], ln1_b[l],
                                                 w_router[l].T, seq, 512)
        eidx, wk, rank, cnt = _route(logits_t, router_bias[l], 256)
        dest, blk_seq, dexp, meta, n_blocks = _dispatch_plan(cnt, eidx, rank, 512)
        row_token = _row_tables(dest, n_blocks * MOE_BLOCK)
        shared = _shared_expert(h2b, _cast_bf16(w_sh_gate[l], 256), _cast_bf16(w_sh_up[l], 256),
                                _cast_bf16(w_sh_down[l], 256), 512)
        y_rows = _routed_experts(h2p, row_token, blk_seq, dexp, meta, n_blocks,
                                 w_e_gate[l], w_e_up[l], w_e_down[l])
        x2 = _combine_final(x1, shared, mod4, dest, wk, y_rows, ln2_g[l], ln2_b[l], seq)
    return x2.reshape(bsz, seq, d)
```

```python
import dataclasses
import math

import jax
import jax.numpy as jnp
import numpy as np
from jax import lax
from jax.experimental import pallas as pl
from jax.experimental.pallas import tpu as pltpu
from jax.experimental.pallas import tpu_sc as plsc

HG_HEADS = 4
HG_DK = 128
HG_WIDTH = HG_HEADS * HG_DK
ATT_BRANCHES = ((128, 1), (512, 4), (2048, 16))
ATT_HEADS_PER_BRANCH = 4
ATT_HEAD_DIM = 64
ATT_BW = ATT_HEADS_PER_BRANCH * ATT_HEAD_DIM
ATT_BLOCK = 128
REL_BUCKETS = 32
REL_MAX_DIST = 2048
N_EXPERTS = 256
TOP_K = 8
N_GROUPS = 8
TOPK_GROUPS = 4
ROUTED_SCALE = 2.5
MOE_BLOCK = 128
DEPTH = 1
DN_ALPHA = (2 * DEPTH) ** 0.25
LN_EPS = 1e-5
RMS_EPS = 1e-6

LANES = 128
SUBLANES = 8
VMEM_LIMIT_BYTES = 56 * 1024 * 1024

F32 = jnp.float32
BF16 = jnp.bfloat16
NEG_INF = float("-inf")


def _cparams(sem):
    return pltpu.CompilerParams(dimension_semantics=sem, vmem_limit_bytes=VMEM_LIMIT_BYTES)


def _ln_rows(x):
    mu = jnp.mean(x, axis=-1, keepdims=True)
    xc = x - mu
    var = jnp.mean(xc * xc, axis=-1, keepdims=True)
    return xc * lax.rsqrt(var + LN_EPS)


def _dot(a, b):
    return jnp.dot(a, b, preferred_element_type=F32)


def _dot_nt(a, b):
    return lax.dot_general(a, b, (((1,), (1,)), ((), ())), preferred_element_type=F32)


def _dot_tn(a, b):
    return lax.dot_general(a, b, (((0,), (0,)), ((), ())), preferred_element_type=F32)


def _cast_kernel(w_ref, o_ref):
    o_ref[...] = w_ref[...].astype(o_ref.dtype)


def _cast_bf16(w, rows_per_step):
    r, c = w.shape
    return pl.pallas_call(
        _cast_kernel,
        grid=(r // rows_per_step,),
        in_specs=[pl.BlockSpec((rows_per_step, c), lambda i: (i, 0))],
        out_specs=pl.BlockSpec((rows_per_step, c), lambda i: (i, 0)),
        out_shape=jax.ShapeDtypeStruct((r, c), BF16),
        compiler_params=_cparams(("parallel",)),
        name="cast_bf16",
    )(w)


def _mod_kernel(c_ref, w_ref, b_ref, o_ref):
    c = c_ref[...]
    cond = c * jax.nn.sigmoid(c)
    o_ref[...] = jnp.dot(cond, w_ref[...], preferred_element_type=F32,
                         precision=lax.Precision.HIGHEST) + b_ref[...]


def _modulation(c, w_ada, b_ada):
    bsz, d = c.shape
    n = w_ada.shape[1]
    rows = -(-bsz // SUBLANES) * SUBLANES
    cpad = jnp.zeros((rows, d), F32).at[:bsz].set(c)
    tn = 1024
    out = pl.pallas_call(
        _mod_kernel,
        grid=(n // tn,),
        in_specs=[pl.BlockSpec((rows, d), lambda j: (0, 0)),
                  pl.BlockSpec((d, tn), lambda j: (0, j)),
                  pl.BlockSpec((1, tn), lambda j: (0, j))],
        out_specs=pl.BlockSpec((rows, tn), lambda j: (0, j)),
        out_shape=jax.ShapeDtypeStruct((rows, n), F32),
        compiler_params=_cparams(("parallel",)),
        name="adaln_modulation",
    )(cpad, w_ada, b_ada.reshape(1, n))
    return out[:bsz]


_IN_HG = 4
_IN_ATT = 3 * len(ATT_BRANCHES)


def _inproj_kernel(x_ref, sc_ref, sh_ref, w_ref, *refs):
    outs = refs[:_IN_HG + _IN_ATT]
    scratch = refs[_IN_HG + _IN_ATT:]
    x = x_ref[...]
    h = _ln_rows(x) * (1.0 + sc_ref[...]) + sh_ref[...]
    hb = h.astype(BF16)
    tm = x.shape[0]
    col = 0
    n_scr = 0
    for k, o_ref in enumerate(outs):
        width = HG_WIDTH if k < _IN_HG else ATT_BW
        y = _dot(hb, w_ref[:, col:col + width])
        col += width
        if k < _IN_HG:
            o_ref[...] = y.astype(o_ref.dtype)
            continue
        if k < _IN_HG + len(ATT_BRANCHES):
            y = y * (ATT_HEAD_DIM ** -0.5)
        dil = ATT_BRANCHES[(k - _IN_HG) % len(ATT_BRANCHES)][1]
        if dil == 1:
            o_ref[...] = y.astype(o_ref.dtype)
            continue
        scr = scratch[n_scr]
        n_scr += 1
        for half in range(ATT_BW // LANES):
            scr[half] = y[:, half * LANES:(half + 1) * LANES]
        for r in range(dil):
            for half in range(ATT_BW // LANES):
                c0 = r * ATT_BW + half * LANES
                o_ref[:, c0:c0 + LANES] = scr[half, pl.ds(r, tm // dil, stride=dil), :].astype(o_ref.dtype)


def _in_projection(x2, mod4, w_in_bf, seq, tm):
    t, d = x2.shape
    steps_per_batch = seq // tm
    dils = [dil for _, dil in ATT_BRANCHES] * 3
    shapes = [(t, HG_WIDTH)] * _IN_HG + [(t // dil, dil * ATT_BW) for dil in dils]
    blocks = [(tm, HG_WIDTH)] * _IN_HG + [(tm // dil, dil * ATT_BW) for dil in dils]
    dtypes = [BF16, F32, BF16, BF16] + [BF16] * _IN_ATT
    mod_spec = lambda row: pl.BlockSpec((None, None, 1, d),
                                        lambda i, row=row: (i // steps_per_batch, row, 0, 0))
    outs = pl.pallas_call(
        _inproj_kernel,
        grid=(t // tm,),
        in_specs=[pl.BlockSpec((tm, d), lambda i: (i, 0)),
                  mod_spec(1), mod_spec(0),
                  pl.BlockSpec(w_in_bf.shape, lambda i: (0, 0))],
        out_specs=[pl.BlockSpec(b, lambda i: (i, 0)) for b in blocks],
        out_shape=[jax.ShapeDtypeStruct(s, dt) for s, dt in zip(shapes, dtypes)],
        scratch_shapes=[pltpu.VMEM((ATT_BW // LANES, tm, LANES), F32) for dil in dils if dil > 1],
        compiler_params=_cparams(("parallel",)),
        name="ln_in_projection",
    )(x2, mod4, mod4, w_in_bf)
    return outs


HG_CHUNK = 64
HG_CHUNKS_PER_STEP = 4
HG_SUB = 8
HG_LEVELS = (64, 32, 16)
LOG2E = 1.4426950408889634


def _hgrn_chunk(q, z, iv, lb):
    c = HG_CHUNK
    f = lb + (1.0 - lb) * jax.nn.sigmoid(z)
    lf = jnp.log(f)
    kk = (1.0 - lb) * jax.nn.sigmoid(-z)
    r_i = lax.broadcasted_iota(jnp.int32, (c, c), 0)
    c_i = lax.broadcasted_iota(jnp.int32, (c, c), 1)
    tril = (c_i <= r_i).astype(F32)
    b = jnp.dot(tril, lf, preferred_element_type=F32, precision=lax.Precision.HIGHEST)
    bl = b * LOG2E

    row = lax.broadcasted_iota(jnp.int32, (c, HG_DK), 0)
    scores = jnp.zeros((c, c), F32)
    for m in HG_LEVELS:
        nb = c // m
        b3 = bl.reshape(nb, m, HG_DK)
        piv = jnp.broadcast_to(b3[:, m // 2 - 1:m // 2, :], (nb, m, HG_DK)).reshape(c, HG_DK)
        second = (row % m) >= (m // 2)
        qt = jnp.where(second, q * jnp.exp2(bl - piv), 0.0)
        kt = jnp.where(second, 0.0, kk * jnp.exp2(piv - bl))
        s_m = _dot_nt(qt.astype(BF16), kt.astype(BF16))
        if nb > 1:
            s_m = jnp.where((r_i // m) == (c_i // m), s_m, 0.0)
        scores = scores + s_m
    sub = HG_SUB
    t_i = lax.broadcasted_iota(jnp.int32, (sub, 1), 0)
    lane = lax.broadcasted_iota(jnp.int32, (sub, c), 1)
    diag_rows = []
    for j in range(c // sub):
        qb = q[j * sub:(j + 1) * sub]
        kb = kk[j * sub:(j + 1) * sub]
        bb = bl[j * sub:(j + 1) * sub]
        a_j = jnp.zeros((sub, c), F32)
        for s in range(sub):
            w = qb * kb[s:s + 1] * jnp.exp2(bb - bb[s:s + 1])
            a_j = jnp.where(lane == j * sub + s, jnp.sum(w, axis=-1, keepdims=True), a_j)
        diag_rows.append(jnp.where(lane - j * sub <= t_i, a_j, 0.0))
    scores = scores + jnp.concatenate(diag_rows, axis=0)

    ivb = iv.astype(BF16)
    intra = _dot(scores.astype(BF16), ivb)
    b_last = bl[c - 1:c]
    kdec = (kk * jnp.exp2(b_last - bl)).astype(BF16)
    return (q * jnp.exp2(bl)).astype(BF16), intra, jnp.exp2(b_last), _dot_tn(ivb, kdec)


def _hgrn_kernel(q_ref, f_ref, i_ref, g_ref, lbp_ref, nw_ref, o_ref, st_ref):
    @pl.when(pl.program_id(1) == 0)
    def _():
        st_ref[...] = jnp.zeros_like(st_ref)

    lbp = lbp_ref[...]
    e = jnp.exp(lbp - jnp.max(lbp, axis=0, keepdims=True))
    lb_all = e[0:1] / jnp.sum(e, axis=0, keepdims=True)
    heads = []
    for h in range(HG_HEADS):
        sl = slice(h * HG_DK, (h + 1) * HG_DK)
        st_t = st_ref[h]
        outs = []
        for n in range(HG_CHUNKS_PER_STEP):
            rows = slice(n * HG_CHUNK, (n + 1) * HG_CHUNK)
            qdec, intra, dec_last, kv = _hgrn_chunk(q_ref[rows, sl].astype(F32), f_ref[rows, sl],
                                                    i_ref[rows, sl].astype(F32), lb_all[:, sl])
            o = _dot_nt(qdec, st_t.astype(BF16)) + intra
            st_t = st_t * dec_last + kv
            outs.append(o * lax.rsqrt(jnp.mean(o * o, axis=-1, keepdims=True) + RMS_EPS))
        st_ref[h] = st_t
        heads.append(jnp.concatenate(outs, axis=0))
    o_all = jnp.concatenate(heads, axis=-1)
    g = g_ref[...].astype(F32)
    o_ref[...] = (o_all * nw_ref[...] * (g * jax.nn.sigmoid(g))).astype(o_ref.dtype)


def _hgrn2(hq, hf, hi, hg, lb_param, norm_w, bsz, seq):
    t = hq.shape[0]
    rows = HG_CHUNK * HG_CHUNKS_PER_STEP
    nc = seq // rows
    tok = lambda b, n: (b * nc + n, 0)
    spec = pl.BlockSpec((rows, HG_WIDTH), tok)
    return pl.pallas_call(
        _hgrn_kernel,
        grid=(bsz, nc),
        in_specs=[spec, spec, spec, spec,
                  pl.BlockSpec(lb_param.shape, lambda b, n: (0, 0)),
                  pl.BlockSpec((1, HG_WIDTH), lambda b, n: (0, 0))],
        out_specs=spec,
        out_shape=jax.ShapeDtypeStruct((t, HG_WIDTH), BF16),
        scratch_shapes=[pltpu.VMEM((HG_HEADS, HG_DK, HG_DK), F32)],
        compiler_params=_cparams(("parallel", "arbitrary")),
        name="hgrn2_scan",
    )(hq, hf, hi, hg, lb_param, norm_w.reshape(1, HG_WIDTH))


def _t5_bucket_np(dist):
    max_exact = REL_BUCKETS // 2
    n = np.maximum(dist, 0)
    nf = np.maximum(n, 1).astype(np.float32)
    large = max_exact + (np.log(nf / np.float32(max_exact)) / np.float32(math.log(REL_MAX_DIST / max_exact))
                         * np.float32(REL_BUCKETS - max_exact)).astype(np.int32)
    large = np.minimum(large, REL_BUCKETS - 1)
    return np.where(n < max_exact, n, large).astype(np.int32)


def _band_tables():
    w = ATT_BLOCK
    qi = np.arange(w)[:, None]
    ki = np.arange(2 * w)[None, :]
    m = w + qi - ki
    band = (m >= 0) & (m <= w)
    buckets = np.stack([_t5_bucket_np(m * dil) for _, dil in ATT_BRANCHES])
    return buckets, band


def _bias_kernel(rb_ref, bucket_ref, o_ref):
    g = pl.program_id(0)
    w = ATT_BLOCK
    bucket = bucket_ref[...]
    qi = lax.broadcasted_iota(jnp.int32, (w, 2 * w), 0)
    ki = lax.broadcasted_iota(jnp.int32, (w, 2 * w), 1)
    m = w + qi - ki
    band = (m >= 0) & (m <= w)
    for h in range(ATT_HEADS_PER_BRANCH):
        acc = jnp.zeros((w, 2 * w), F32)
        for c in range(REL_BUCKETS):
            acc = jnp.where(bucket == c, rb_ref[c, g * ATT_HEADS_PER_BRANCH + h], acc)
        full = jnp.where(band, acc, NEG_INF)
        o_ref[1, h] = full
        o_ref[0, h] = jnp.where(ki >= w, full, NEG_INF)


def _bias_tables(rel_bias):
    buckets, _ = _band_tables()
    g = len(ATT_BRANCHES)
    w = ATT_BLOCK
    return pl.pallas_call(
        _bias_kernel,
        grid=(g,),
        in_specs=[pl.BlockSpec(memory_space=pltpu.SMEM),
                  pl.BlockSpec((None, w, 2 * w), lambda i: (i, 0, 0))],
        out_specs=pl.BlockSpec((None, 2, ATT_HEADS_PER_BRANCH, w, 2 * w), lambda i: (i, 0, 0, 0, 0)),
        out_shape=jax.ShapeDtypeStruct((g, 2, ATT_HEADS_PER_BRANCH, w, 2 * w), F32),
        compiler_params=_cparams(("parallel",)),
        name="rel_bias_tables",
    )(rel_bias, jnp.asarray(buckets))


ATT_BLOCKS_PER_STEP = 8


def _attn_kernel(q_ref, kp_ref, kc_ref, vp_ref, vc_ref, bias_ref, o_ref, lse_ref):
    m = pl.program_id(2)
    w = ATT_BLOCK
    hb = ATT_HEADS_PER_BRANCH
    lane = lax.broadcasted_iota(jnp.int32, (w, ATT_BW), 1) // ATT_HEAD_DIM
    for res, blk in [(r, b) for r in range(q_ref.shape[1] // ATT_BW) for b in range(q_ref.shape[0] // w)]:
        cols = slice(res * ATT_BW, (res + 1) * ATT_BW)
        kall = jnp.concatenate([kp_ref[:, cols], kc_ref[:, cols]], axis=0)
        vall = jnp.concatenate([vp_ref[:, cols], vc_ref[:, cols]], axis=0)
        q = q_ref[blk * w:(blk + 1) * w, cols]
        q4 = jnp.concatenate([jnp.where(lane == h, q, jnp.zeros_like(q)) for h in range(hb)], axis=0)
        kk = kall[blk * w:(blk + 2) * w]
        vv = vall[blk * w:(blk + 2) * w]
        s4 = _dot_nt(q4, kk)
        bias = bias_ref[jnp.minimum(m, 1)] if blk == 0 else bias_ref[1]
        s4 = s4 + bias.reshape(hb * w, 2 * w)
        mx = jnp.max(s4, axis=-1, keepdims=True)
        p = jnp.exp(s4 - mx)
        l = jnp.sum(p, axis=-1, keepdims=True)
        o4 = _dot((p / l).astype(vv.dtype), vv)
        lse4 = mx + jnp.log(l)
        o = jnp.zeros((w, ATT_BW), F32)
        lse = jnp.zeros((w, ATT_BW), F32)
        for h in range(hb):
            o = jnp.where(lane == h, o4[h * w:(h + 1) * w], o)
            lse = jnp.where(lane == h, lse4[h * w:(h + 1) * w], lse)
        o_ref[blk * w:(blk + 1) * w, cols] = o.astype(o_ref.dtype)
        lse_ref[blk * w:(blk + 1) * w, cols] = lse


def _dilated_attention(q, k, v, bias_g, bsz, seq, dilation):
    w = ATT_BLOCK
    l = seq // dilation
    pstep = math.gcd(ATT_BLOCKS_PER_STEP, l // w)
    nb = l // (w * pstep)
    rstep = math.gcd(ATT_BLOCKS_PER_STEP // pstep, dilation)
    view = lambda a: a.reshape(bsz, l, dilation * ATT_BW)
    cur = pl.BlockSpec((None, pstep * w, rstep * ATT_BW), lambda b, r, n: (b, n, r))
    prev = pl.BlockSpec((None, w, rstep * ATT_BW), lambda b, r, n: (b, jnp.maximum(pstep * n - 1, 0), r))
    o, lse = pl.pallas_call(
        _attn_kernel,
        grid=(bsz, dilation // rstep, nb),
        in_specs=[cur, prev, cur, prev, cur,
                  pl.BlockSpec(bias_g.shape, lambda b, r, n: (0, 0, 0, 0))],
        out_specs=[cur, cur],
        out_shape=[jax.ShapeDtypeStruct((bsz, l, dilation * ATT_BW), BF16),
                   jax.ShapeDtypeStruct((bsz, l, dilation * ATT_BW), F32)],
        compiler_params=_cparams(("parallel", "parallel", "arbitrary")),
        name=f"dilated_attention_d{dilation}",
    )(view(q), view(k), view(k), view(v), view(v), bias_g)
    return o.reshape(bsz * l, dilation * ATT_BW), lse.reshape(bsz * l, dilation * ATT_BW)


def _split_bf16(a):
    hi = a.astype(BF16)
    lo = (a - hi.astype(F32)).astype(BF16)
    return hi, lo


H2P_CHUNKS = 4


def _token_order(ref, scr, dil):
    if dil == 1:
        return ref[...].astype(F32)
    n = ref.shape[0]
    halves = ATT_BW // LANES
    for r in range(dil):
        for half in range(halves):
            c0 = r * ATT_BW + half * LANES
            scr[half, pl.ds(r, n, stride=dil), :] = ref[:, c0:c0 + LANES].astype(F32)
    return jnp.concatenate([scr[half] for half in range(halves)], axis=1)


def _outproj_kernel(yhg_ref, o1_ref, o2_ref, o3_ref, l1_ref, l2_ref, l3_ref, x_ref,
                    g1_ref, sc2_ref, sh2_ref, wout_ref, lng_ref, lnb_ref, wrt_ref,
                    x1_ref, h2_ref, h2p_ref, lgt_ref, *scratch):
    dils = [dil for _, dil in ATT_BRANCHES]
    scr = iter(scratch)
    o1, o2, o3 = [_token_order(r, None if dil == 1 else next(scr), dil)
                  for r, dil in zip((o1_ref, o2_ref, o3_ref), dils)]
    l1, l2, l3 = [_token_order(r, None if dil == 1 else next(scr), dil)
                  for r, dil in zip((l1_ref, l2_ref, l3_ref), dils)]
    mx = jnp.maximum(jnp.maximum(l1, l2), l3)
    e1, e2, e3 = jnp.exp(l1 - mx), jnp.exp(l2 - mx), jnp.exp(l3 - mx)
    den = e1 + e2 + e3
    att = (e1 / den) * o1 + (e2 / den) * o2 + (e3 / den) * o3
    mix = _dot(yhg_ref[...], wout_ref[:HG_WIDTH, :]) + _dot(att.astype(BF16), wout_ref[HG_WIDTH:, :])
    x1 = _ln_rows(DN_ALPHA * x_ref[...] + g1_ref[...] * mix) * lng_ref[...] + lnb_ref[...]
    x1_ref[...] = x1
    h2 = _ln_rows(x1) * (1.0 + sc2_ref[...]) + sh2_ref[...]
    h_hi = h2.astype(BF16)
    h_hf = h_hi.astype(F32)
    h2_ref[...] = h_hi
    bits = lax.bitcast_convert_type(h_hf, jnp.uint32)
    for cidx in range(H2P_CHUNKS):
        lo = bits[:, 2 * LANES * cidx:2 * LANES * cidx + LANES]
        hi = bits[:, 2 * LANES * cidx + LANES:2 * LANES * (cidx + 1)]
        h2p_ref[:, cidx, :] = lax.bitcast_convert_type((lo >> 16) | hi, jnp.int32)
    h_lo = (h2 - h_hf).astype(BF16)
    w_hi, w_lo = _split_bf16(wrt_ref[...])
    lgt_ref[...] = _dot_nt(w_hi, h_hi) + (_dot_nt(w_hi, h_lo) + _dot_nt(w_lo, h_hi))


def _out_projection(yhg, os_, ls_, x2, mod4, w_out_bf, ln_g, ln_b, w_router_t, seq, tm):
    t, d = x2.shape
    spb = seq // tm
    ne = w_router_t.shape[0]
    row = lambda w: pl.BlockSpec((tm, w), lambda i: (i, 0))
    mod_spec = lambda r: pl.BlockSpec((None, None, 1, d), lambda i, r=r: (i // spb, r, 0, 0))
    full = lambda a: pl.BlockSpec(a.shape, lambda i: (0,) * a.ndim)
    ln_g2, ln_b2 = ln_g.reshape(1, d), ln_b.reshape(1, d)
    dils = [dil for _, dil in ATT_BRANCHES]
    branch = [pl.BlockSpec((tm // dil, dil * ATT_BW), lambda i: (i, 0)) for dil in dils]
    return pl.pallas_call(
        _outproj_kernel,
        grid=(t // tm,),
        in_specs=[row(HG_WIDTH)] + branch + branch + [row(d),
                  mod_spec(2), mod_spec(4), mod_spec(3),
                  full(w_out_bf), full(ln_g2), full(ln_b2), full(w_router_t)],
        out_specs=[row(d), row(d), pl.BlockSpec((tm, H2P_CHUNKS, LANES), lambda i: (i, 0, 0)),
                   pl.BlockSpec((ne, tm), lambda i: (0, i))],
        out_shape=[jax.ShapeDtypeStruct((t, d), F32), jax.ShapeDtypeStruct((t, d), BF16),
                   jax.ShapeDtypeStruct((t, H2P_CHUNKS, LANES), jnp.int32),
                   jax.ShapeDtypeStruct((ne, t), F32)],
        scratch_shapes=[pltpu.VMEM((ATT_BW // LANES, tm, LANES), F32) for dil in dils + dils if dil > 1],
        compiler_params=_cparams(("parallel",)),
        name="merge_outproj_ln",
    )(yhg, *os_, *ls_, x2, mod4, mod4, mod4, w_out_bf, ln_g2, ln_b2, w_router_t)


def _argmax_rows(cur, iota, nrows):
    m = jnp.max(cur, axis=0, keepdims=True)
    idx = jnp.min(jnp.where(cur == m, iota, nrows), axis=0, keepdims=True)
    return m, idx, iota == idx


def _route_kernel(lgt_ref, rb_ref, eidx_ref, w_ref, rank_ref, cnt_ref, carry):
    ne = N_EXPERTS
    gsz = ne // N_GROUPS
    tt = lgt_ref.shape[1]

    @pl.when(pl.program_id(0) == 0)
    def _():
        carry[...] = jnp.zeros_like(carry)

    sc = jax.nn.sigmoid(lgt_ref[...])
    biased = sc + rb_ref[...]
    g3 = biased.reshape(N_GROUPS, gsz, tt)
    io3 = lax.broadcasted_iota(jnp.int32, (N_GROUPS, gsz, tt), 1)
    m1 = jnp.max(g3, axis=1, keepdims=True)
    first = jnp.min(jnp.where(g3 == m1, io3, gsz), axis=1, keepdims=True)
    m2 = jnp.max(jnp.where(io3 == first, NEG_INF, g3), axis=1, keepdims=True)
    gs = (m1 + m2).reshape(N_GROUPS, tt)
    io8 = lax.broadcasted_iota(jnp.int32, (N_GROUPS, tt), 0)
    sel = jnp.zeros((N_GROUPS, tt), jnp.int32)
    cur = gs
    for _ in range(TOPK_GROUPS):
        _, _, pick = _argmax_rows(cur, io8, N_GROUPS)
        sel = jnp.where(pick, 1, sel)
        cur = jnp.where(pick, NEG_INF, cur)
    masked = jnp.where(sel.reshape(N_GROUPS, 1, tt) > 0, g3, NEG_INF).reshape(ne, tt)
    ioe = lax.broadcasted_iota(jnp.int32, (ne, tt), 0)
    cur = masked
    idxs, ws, picks = [], [], []
    for _ in range(TOP_K):
        _, idx, pick = _argmax_rows(cur, ioe, ne)
        idxs.append(idx)
        picks.append(pick)
        ws.append(jnp.sum(jnp.where(pick, sc, 0.0), axis=0, keepdims=True))
        cur = jnp.where(pick, NEG_INF, cur)
    wk = jnp.concatenate(ws, axis=0)
    eidx_ref[...] = jnp.concatenate(idxs, axis=0)
    w_ref[...] = wk / jnp.sum(wk, axis=0, keepdims=True) * ROUTED_SCALE
    chosen = jnp.where(cur == NEG_INF, jnp.where(masked == NEG_INF, 0.0, 1.0), 0.0)
    r_i = lax.broadcasted_iota(jnp.int32, (tt, tt), 0)
    c_i = lax.broadcasted_iota(jnp.int32, (tt, tt), 1)
    before = jnp.where(r_i < c_i, 1.0, 0.0).astype(BF16)
    pref = _dot(chosen.astype(BF16), before) + carry[...]
    rank_ref[...] = jnp.concatenate(
        [jnp.sum(jnp.where(p, pref, 0.0), axis=0, keepdims=True) for p in picks], axis=0).astype(jnp.int32)
    carry[...] = carry[...] + jnp.sum(chosen, axis=1, keepdims=True)
    cnt_ref[...] = carry[...]


def _route(logits_t, router_bias, tt):
    ne, t = logits_t.shape
    tok = pl.BlockSpec((TOP_K, tt), lambda i: (0, i))
    return pl.pallas_call(
        _route_kernel,
        grid=(t // tt,),
        in_specs=[pl.BlockSpec((ne, tt), lambda i: (0, i)),
                  pl.BlockSpec((ne, 1), lambda i: (0, 0))],
        out_specs=[tok, tok, tok, pl.BlockSpec((ne, 1), lambda i: (0, 0))],
        out_shape=[jax.ShapeDtypeStruct((TOP_K, t), jnp.int32), jax.ShapeDtypeStruct((TOP_K, t), F32),
                   jax.ShapeDtypeStruct((TOP_K, t), jnp.int32), jax.ShapeDtypeStruct((ne, 1), F32)],
        scratch_shapes=[pltpu.VMEM((ne, 1), F32)],
        compiler_params=_cparams(("arbitrary",)),
        name="router_topk",
    )(logits_t, router_bias.reshape(ne, 1))


def _plan_kernel(cnt_ref, eidx_ref, rank_ref, dest_ref, seq_ref, dexp_ref, meta_ref, pstart_ref):
    ne = N_EXPERTS
    tt = eidx_ref.shape[1]
    nblk = seq_ref.shape[1]

    @pl.when(pl.program_id(0) == 0)
    def _():
        cnt = cnt_ref[...].astype(jnp.int32)
        padded = ((cnt + (MOE_BLOCK - 1)) // MOE_BLOCK) * MOE_BLOCK
        r_i = lax.broadcasted_iota(jnp.int32, (ne, ne), 0)
        c_i = lax.broadcasted_iota(jnp.int32, (ne, ne), 1)
        incl = jnp.where(c_i <= r_i, 1.0, 0.0)
        pend = jnp.dot(incl, jnp.broadcast_to(padded.astype(F32), (ne, LANES)),
                       preferred_element_type=F32, precision=lax.Precision.HIGHEST)[:, 0:1]
        pend = pend.astype(jnp.int32)
        pstart_ref[...] = pend - padded
        blk0 = lax.broadcasted_iota(jnp.int32, (ne, nblk), 1) * MOE_BLOCK
        be = jnp.minimum(jnp.sum(jnp.where(pend <= blk0, 1, 0), axis=0, keepdims=True), ne - 1)
        present = cnt > 0
        strict = jnp.where(c_i < r_i, 1.0, 0.0).astype(BF16)
        sidx = _dot(strict, jnp.broadcast_to(jnp.where(present, 1.0, 0.0), (ne, LANES)).astype(BF16))[:, 0:1]
        sidx = sidx.astype(jnp.int32)
        dexp_ref[...] = jnp.sum(jnp.where(jnp.logical_and(present, sidx == c_i), r_i, 0), axis=0, keepdims=True)
        ioeb = lax.broadcasted_iota(jnp.int32, (ne, nblk), 0)
        nu = jnp.max(pend, axis=0, keepdims=True) // MOE_BLOCK
        nd = jnp.sum(jnp.where(present, 1, 0), axis=0, keepdims=True)
        seq_ref[...] = jnp.minimum(jnp.sum(jnp.where(ioeb == be, sidx, 0), axis=0, keepdims=True), nd - 1)
        lane = lax.broadcasted_iota(jnp.int32, (1, LANES), 1)
        meta_ref[...] = jnp.where(lane == 0, nu, jnp.where(lane == 1, nd, 0))

    pstart = pstart_ref[...]
    ioe = lax.broadcasted_iota(jnp.int32, (ne, tt), 0)
    rows = []
    for k in range(TOP_K):
        sel = ioe == eidx_ref[k:k + 1, :]
        rows.append(jnp.sum(jnp.where(sel, pstart, 0), axis=0, keepdims=True))
    dest_ref[...] = jnp.concatenate(rows, axis=0) + rank_ref[...]


def _dispatch_plan(cnt, eidx, rank, tt):
    k, t = eidx.shape
    n_blocks = -(-(t * k) // MOE_BLOCK) + N_EXPERTS
    tok = pl.BlockSpec((k, tt), lambda i: (0, i))
    one = lambda n: pl.BlockSpec((1, n), lambda i: (0, 0))
    dest, seq, dexp, meta = pl.pallas_call(
        _plan_kernel,
        grid=(t // tt,),
        in_specs=[pl.BlockSpec(cnt.shape, lambda i: (0, 0)), tok, tok],
        out_specs=[tok, one(n_blocks), one(N_EXPERTS), one(LANES)],
        out_shape=[jax.ShapeDtypeStruct((k, t), jnp.int32), jax.ShapeDtypeStruct((1, n_blocks), jnp.int32),
                   jax.ShapeDtypeStruct((1, N_EXPERTS), jnp.int32), jax.ShapeDtypeStruct((1, LANES), jnp.int32)],
        scratch_shapes=[pltpu.VMEM((N_EXPERTS, 1), jnp.int32)],
        compiler_params=_cparams(("arbitrary",)),
        name="dispatch_plan",
    )(cnt, eidx, rank)
    return dest, seq.reshape(n_blocks), dexp.reshape(N_EXPERTS), meta.reshape(LANES), n_blocks


SC_CORES = 2
SC_SUBCORES = 16
SC_LANES = 16
SC_CHUNK = 16384
SC_UNROLL = 4


def _row_tables(dest, n_rows):
    k, t = dest.shape
    a = k * t
    nw = SC_CORES * SC_SUBCORES
    per_w = n_rows // nw
    assert n_rows % (nw * SC_LANES) == 0 and t % SC_CHUNK == 0
    mesh = plsc.VectorSubcoreMesh(core_axis_name="c", subcore_axis_name="s")
    cp = pltpu.CompilerParams()
    if "needs_layout_passes" in pltpu.CompilerParams.__dataclass_fields__:
        cp = dataclasses.replace(cp, needs_layout_passes=False)

    def body(dest_hbm, tok_out, dbuf, tloc):
        wid = lax.axis_index("s") * SC_CORES + lax.axis_index("c")
        base = wid * per_w

        @pl.loop(0, per_w // SC_LANES)
        def _(i):
            tloc[pl.ds(i * SC_LANES, SC_LANES)] = jnp.zeros((SC_LANES,), jnp.int32)

        lane = lax.iota(jnp.int32, SC_LANES)

        @pl.loop(0, a // SC_CHUNK)
        def _(c):
            pltpu.sync_copy(dest_hbm.at[pl.ds(c * SC_CHUNK, SC_CHUNK)], dbuf)
            tok0 = lax.rem(c * SC_CHUNK, t)

            @pl.loop(0, SC_CHUNK // (SC_LANES * SC_UNROLL))
            def _(j):
                for u in range(SC_UNROLL):
                    off = (j * SC_UNROLL + u) * SC_LANES
                    loc = dbuf[pl.ds(off, SC_LANES)] - base
                    mine = jnp.logical_and(loc >= 0, loc < per_w)
                    loc = jnp.where(mine, loc, 0)
                    plsc.store_scatter(tloc, [loc], tok0 + off + lane, mask=mine)

        pltpu.sync_copy(tloc, tok_out.at[pl.ds(base, per_w)])

    fn = pl.kernel(
        body,
        out_type=jax.ShapeDtypeStruct((n_rows,), jnp.int32),
        mesh=mesh,
        scratch_types=[pltpu.VMEM((SC_CHUNK,), jnp.int32), pltpu.VMEM((per_w,), jnp.int32)],
        compiler_params=cp,
        name="row_tables",
    )
    return fn(dest.reshape(a))


ROW_TILE = (SUBLANES, LANES)


FFN_GROUP = 4
PAIR_ROWS = 2 * MOE_BLOCK
FFN_CHAIN = 128
W_SETS = 3
GATHER_BATCH = 32


def _gather_rows(h2p_ref, tok_ref, row, buf, base):
    for j0 in range(0, MOE_BLOCK, GATHER_BATCH):
        vals = [h2p_ref[tok_ref[row, j]] for j in range(j0, j0 + GATHER_BATCH)]
        for j, v in zip(range(j0, j0 + GATHER_BATCH), vals):
            buf[pl.ds(H2P_CHUNKS * (base + j), H2P_CHUNKS), :] = v


def _expert_rows(buf, chains, out):
    hms = []
    for row0, nrows, (wg_c, wu_c, _), _ in chains:
        parts = []
        for cidx in range(H2P_CHUNKS):
            word = buf[pl.ds(H2P_CHUNKS * row0 + cidx, nrows, stride=H2P_CHUNKS), :]
            parts.append(lax.bitcast_convert_type(word << 16, F32))
            parts.append(lax.bitcast_convert_type(word & jnp.int32(-65536), F32))
        x = jnp.concatenate(parts, axis=1)
        g = _dot(x, wg_c[...])
        u = _dot(x, wu_c[...])
        hms.append((g * jax.nn.sigmoid(g)) * u)
    for (_, nrows, (_, _, wd_c), out_row0), hm in zip(chains, hms):
        res = _dot(hm, wd_c[...])
        nch = res.shape[1] // LANES
        for c in range(nch):
            out[pl.ds(nch * out_row0 + c, nrows, stride=nch), :] = res[:, c * LANES:(c + 1) * LANES]


def _ffn_kernel(seq_ref, dexp_ref, meta_ref, tokc_ref, tokn_ref, h2p_ref, wg_hbm, wu_hbm, wd_hbm,
                y_ref, buf_0, buf_1, *rest):
    i = pl.program_id(0)
    n_used = meta_ref[0]
    n_exp = meta_ref[1]
    nblk = seq_ref.shape[0]
    wsets = tuple(tuple(rest[3 * n:3 * n + 3]) for n in range(W_SETS))
    started_ref, sems = rest[3 * W_SETS:]
    bufs = (buf_0, buf_1)

    def weight_copies(s, par):
        e = dexp_ref[s]
        return [pltpu.make_async_copy(src.at[e], dst, sems.at[par, n])
                for n, (src, dst) in enumerate(zip((wg_hbm, wu_hbm, wd_hbm), wsets[par]))]

    def start_expert(s):
        for par in range(W_SETS):
            @pl.when(s % W_SETS == par)
            def _():
                for cp in weight_copies(s, par):
                    cp.start()

    @pl.when(i == 0)
    def _():
        _gather_rows(h2p_ref, tokc_ref, 0, buf_0, 0)
        _gather_rows(h2p_ref, tokc_ref, 1, buf_0, MOE_BLOCK)
        start_expert(0)
        started_ref[0] = 0

    for p in range(FFN_GROUP // 2):
        b_a = i * FFN_GROUP + 2 * p
        used = b_a < n_used
        s_a = seq_ref[jnp.minimum(b_a, nblk - 1)]
        s_b = seq_ref[jnp.minimum(b_a + 1, nblk - 1)]
        first_a = jnp.logical_or(b_a == 0, s_a != seq_ref[jnp.clip(b_a - 1, 0, nblk - 1)])
        same = s_a == s_b
        cur, nxt = bufs[p], bufs[1 - p]
        out_row0 = p * PAIR_ROWS

        def gather_next():
            if p == 0:
                _gather_rows(h2p_ref, tokc_ref, 2, nxt, 0)
                _gather_rows(h2p_ref, tokc_ref, 3, nxt, MOE_BLOCK)
            else:
                _gather_rows(h2p_ref, tokn_ref, 0, nxt, 0)
                _gather_rows(h2p_ref, tokn_ref, 1, nxt, MOE_BLOCK)

        @pl.when(used)
        def _():
            started = started_ref[0]
            limit = jnp.minimum(s_a + (W_SETS - 1), n_exp - 1)
            for _unused in range(W_SETS - 1):
                go = started < limit

                @pl.when(go)
                def _():
                    start_expert(started + 1)
                started = jnp.where(go, started + 1, started)
            started_ref[0] = started

        for par in range(W_SETS):
            @pl.when(jnp.logical_and(used, jnp.logical_and(same, s_a % W_SETS == par)))
            def _():
                @pl.when(first_a)
                def _():
                    for cp in weight_copies(s_a, par):
                        cp.wait()
                gather_next()
                _expert_rows(cur, [(r0, FFN_CHAIN, wsets[par], out_row0 + r0)
                                   for r0 in range(0, PAIR_ROWS, FFN_CHAIN)], y_ref)

            @pl.when(jnp.logical_and(used, jnp.logical_and(jnp.logical_not(same), s_a % W_SETS == par)))
            def _():
                @pl.when(first_a)
                def _():
                    for cp in weight_copies(s_a, par):
                        cp.wait()
                for cp in weight_copies(s_b, (par + 1) % W_SETS):
                    cp.wait()
                gather_next()
                _expert_rows(cur, [(r0, FFN_CHAIN, wsets[(par + r0 // MOE_BLOCK) % W_SETS], out_row0 + r0)
                                   for r0 in range(0, PAIR_ROWS, FFN_CHAIN)], y_ref)

        @pl.when(jnp.logical_not(used))
        def _():
            y_ref[pl.ds(out_row0 * SUBLANES, PAIR_ROWS * SUBLANES), :] = jnp.zeros(
                (PAIR_ROWS * SUBLANES, LANES), y_ref.dtype)


def _routed_experts(h2p, row_token, seq, dexp, meta, n_blocks, wg, wu, wd):
    d = wg.shape[1]
    de = wg.shape[2]
    ng = n_blocks // FFN_GROUP
    assert n_blocks % FFN_GROUP == 0 and FFN_GROUP == 4 and d == SUBLANES * LANES
    tok3 = row_token.reshape(ng, FFN_GROUP, MOE_BLOCK)
    idle_step = lambda m: jnp.minimum((m[0] + FFN_GROUP - 1) // FFN_GROUP, ng - 1)
    smem = lambda imap: pl.BlockSpec((None, FFN_GROUP, MOE_BLOCK), imap, memory_space=pltpu.SMEM)
    grid_spec = pltpu.PrefetchScalarGridSpec(
        num_scalar_prefetch=3,
        grid=(ng,),
        in_specs=[
            smem(lambda i, sq, dx, m: (jnp.minimum(i, ng - 1), 0, 0)),
            smem(lambda i, sq, dx, m: (jnp.minimum(i + 1, ng - 1), 0, 0)),
            pl.BlockSpec(h2p.shape, lambda i, sq, dx, m: (0, 0, 0), pipeline_mode=pl.Buffered(1)),
            pl.BlockSpec(memory_space=pl.ANY),
            pl.BlockSpec(memory_space=pl.ANY),
            pl.BlockSpec(memory_space=pl.ANY),
        ],
        out_specs=pl.BlockSpec((FFN_GROUP * MOE_BLOCK * SUBLANES, LANES),
                               lambda i, sq, dx, m: (jnp.minimum(i, idle_step(m)), 0)),
        scratch_shapes=[pltpu.VMEM((PAIR_ROWS * H2P_CHUNKS, LANES), jnp.int32)] * 2 + [
            pltpu.VMEM((d, de), F32), pltpu.VMEM((d, de), F32), pltpu.VMEM((de, d), F32)] * W_SETS + [
            pltpu.SMEM((1,), jnp.int32), pltpu.SemaphoreType.DMA((W_SETS, 3))],
    )
    y = pl.pallas_call(
        _ffn_kernel,
        grid_spec=grid_spec,
        out_shape=jax.ShapeDtypeStruct((n_blocks * MOE_BLOCK * SUBLANES, LANES), F32),
        compiler_params=_cparams(("arbitrary",)),
        name="routed_experts",
    )(seq, dexp, meta, tok3, tok3, h2p, wg, wu, wd)
    return y.reshape((n_blocks * MOE_BLOCK,) + ROW_TILE)


COMBINE_TOKENS = 128


def _row_copies(src_hbm, idx_ref, buf, sem):
    return [pltpu.make_async_copy(src_hbm.at[idx_ref[k, j]], buf.at[k, j], sem)
            for k in range(TOP_K) for j in range(COMBINE_TOKENS)]


def _shared_kernel(h_ref, wsg_ref, wsu_ref, wsd_ref, o_ref):
    hb = h_ref[...]
    g = _dot(hb, wsg_ref[...])
    u = _dot(hb, wsu_ref[...])
    o_ref[...] = _dot(((g * jax.nn.sigmoid(g)) * u).astype(BF16), wsd_ref[...])


def _shared_expert(h2b, wsg, wsu, wsd, tm):
    t, d = h2b.shape
    row = pl.BlockSpec((tm, d), lambda i: (i, 0))
    full = lambda a: pl.BlockSpec(a.shape, lambda i: (0,) * a.ndim)
    return pl.pallas_call(
        _shared_kernel,
        grid=(t // tm,),
        in_specs=[row, full(wsg), full(wsu), full(wsd)],
        out_specs=row,
        out_shape=jax.ShapeDtypeStruct((t, d), F32),
        compiler_params=_cparams(("parallel",)),
        name="shared_expert",
    )(h2b, wsg, wsu, wsd)


def _final_kernel(dc_ref, dn_ref, wk_ref, x1_ref, sh_ref, g2_ref, lng_ref, lnb_ref,
                  y_hbm, o_ref, ybuf, sems):
    i = pl.program_id(0)
    nsteps = pl.num_programs(0)
    slot = i % 2

    def issue(d_ref, s):
        for n, cp in enumerate(_row_copies(y_hbm, d_ref, ybuf.at[s], sems.at[s])):
            cp.start(priority=n % 2)

    @pl.when(i == 0)
    def _():
        issue(dc_ref, 0)

    @pl.when(i + 1 < nsteps)
    def _():
        issue(dn_ref, 1 - slot)

    for cp in _row_copies(y_hbm, dc_ref, ybuf.at[slot], sems.at[slot]):
        cp.wait()
    rows = []
    for j in range(COMBINE_TOKENS):
        acc = ybuf[slot, 0, j].astype(F32) * wk_ref[0, j]
        for k in range(1, TOP_K):
            acc = acc + ybuf[slot, k, j].astype(F32) * wk_ref[k, j]
        rows.append(acc)
    routed = pltpu.einshape("tcl->t(cl)", jnp.stack(rows, axis=0))
    x2 = DN_ALPHA * x1_ref[...] + g2_ref[...] * (routed + sh_ref[...])
    o_ref[...] = _ln_rows(x2) * lng_ref[...] + lnb_ref[...]


def _combine_final(x1, shared, mod4, dest, wk, y_rows, ln_g, ln_b, seq):
    t, d = x1.shape
    tt = COMBINE_TOKENS
    nsteps = t // tt
    spb = seq // tt
    row = pl.BlockSpec((tt, d), lambda i: (i, 0))
    full = lambda a: pl.BlockSpec(a.shape, lambda i: (0,) * a.ndim)
    ln_g2, ln_b2 = ln_g.reshape(1, d), ln_b.reshape(1, d)
    return pl.pallas_call(
        _final_kernel,
        grid=(nsteps,),
        in_specs=[pl.BlockSpec((TOP_K, tt), lambda i: (0, i), memory_space=pltpu.SMEM),
                  pl.BlockSpec((TOP_K, tt), lambda i: (0, jnp.minimum(i + 1, nsteps - 1)),
                               memory_space=pltpu.SMEM),
                  pl.BlockSpec((TOP_K, tt), lambda i: (0, i), memory_space=pltpu.SMEM),
                  row, row,
                  pl.BlockSpec((None, None, 1, d), lambda i: (i // spb, 5, 0, 0)),
                  full(ln_g2), full(ln_b2),
                  pl.BlockSpec(memory_space=pl.ANY)],
        out_specs=row,
        out_shape=jax.ShapeDtypeStruct((t, d), F32),
        scratch_shapes=[pltpu.VMEM((2, TOP_K, tt) + ROW_TILE, y_rows.dtype),
                        pltpu.SemaphoreType.DMA((2,))],
        compiler_params=_cparams(("arbitrary",)),
        name="shared_combine_ln",
    )(dest, dest, wk, x1, shared, mod4, ln_g2, ln_b2, y_rows)


def kernel(x, c, w_ada, b_ada, w_in, hg_lower_bound, hg_norm_w, rel_bias, w_out, ln1_g, ln1_b, w_router,
           router_bias, w_e_gate, w_e_up, w_e_down, w_sh_gate, w_sh_up, w_sh_down, ln2_g, ln2_b):
    bsz, seq, d = x.shape
    t = bsz * seq
    assert w_ada.shape[0] == DEPTH and seq % (ATT_BRANCHES[-1][0]) == 0
    x2 = x.reshape(t, d)
    bias = _bias_tables(rel_bias)
    for l in range(DEPTH):
        mod4 = _modulation(c, w_ada[l], b_ada[l]).reshape(bsz, 6, 1, d)
        w_in_bf = _cast_bf16(w_in[l], 256)
        hq, hf, hi, hg, *qkv = _in_projection(x2, mod4, w_in_bf, seq, 512)
        nbr = len(ATT_BRANCHES)
        y_hg = _hgrn2(hq, hf, hi, hg, hg_lower_bound, hg_norm_w[l], bsz, seq)
        os_, ls_ = [], []
        for g, (_, dil) in enumerate(ATT_BRANCHES):
            o, lse = _dilated_attention(qkv[g], qkv[nbr + g], qkv[2 * nbr + g], bias[g], bsz, seq, dil)
            os_.append(o)
            ls_.append(lse)
        w_out_bf = _cast_bf16(w_out[l], 256)
        x1, h2b, h2p, logits_t = _out_projection(y_hg, os_, ls_, x2, mod4, w_out_bf, ln1_g[l], ln1_b[l],
                                                 w_router[l].T, seq, 512)
        eidx, wk, rank, cnt = _route(logits_t, router_bias[l], 256)
        dest, blk_seq, dexp, meta, n_blocks = _dispatch_plan(cnt, eidx, rank, 512)
        row_token = _row_tables(dest, n_blocks * MOE_BLOCK)
        shared = _shared_expert(h2b, _cast_bf16(w_sh_gate[l], 256), _cast_bf16(w_sh_up[l], 256),
                                _cast_bf16(w_sh_down[l], 256), 512)
        y_rows = _routed_experts(h2p, row_token, blk_seq, dexp, meta, n_blocks,
                                 w_e_gate[l], w_e_up[l], w_e_down[l])
        x2 = _combine_final(x1, shared, mod4, dest, wk, y_rows, ln2_g[l], ln2_b[l], seq)
    return x2.reshape(bsz, seq, d)
```

```python
import dataclasses
import math

import jax
import jax.numpy as jnp
import numpy as np
from jax import lax
from jax.experimental import pallas as pl
from jax.experimental.pallas import tpu as pltpu
from jax.experimental.pallas import tpu_sc as plsc

HG_HEADS = 4
HG_DK = 128
HG_WIDTH = HG_HEADS * HG_DK
ATT_BRANCHES = ((128, 1), (512, 4), (2048, 16))
ATT_HEADS_PER_BRANCH = 4
ATT_HEAD_DIM = 64
ATT_BW = ATT_HEADS_PER_BRANCH * ATT_HEAD_DIM
ATT_BLOCK = 128
REL_BUCKETS = 32
REL_MAX_DIST = 2048
N_EXPERTS = 256
TOP_K = 8
N_GROUPS = 8
TOPK_GROUPS = 4
ROUTED_SCALE = 2.5
MOE_BLOCK = 128
DEPTH = 1
DN_ALPHA = (2 * DEPTH) ** 0.25
LN_EPS = 1e-5
RMS_EPS = 1e-6

LANES = 128
SUBLANES = 8
VMEM_LIMIT_BYTES = 56 * 1024 * 1024

F32 = jnp.float32
BF16 = jnp.bfloat16
NEG_INF = float("-inf")


def _cparams(sem):
    return pltpu.CompilerParams(dimension_semantics=sem, vmem_limit_bytes=VMEM_LIMIT_BYTES)


def _ln_rows(x):
    mu = jnp.mean(x, axis=-1, keepdims=True)
    xc = x - mu
    var = jnp.mean(xc * xc, axis=-1, keepdims=True)
    return xc * lax.rsqrt(var + LN_EPS)


def _dot(a, b):
    return jnp.dot(a, b, preferred_element_type=F32)


def _dot_nt(a, b):
    return lax.dot_general(a, b, (((1,), (1,)), ((), ())), preferred_element_type=F32)


def _dot_tn(a, b):
    return lax.dot_general(a, b, (((0,), (0,)), ((), ())), preferred_element_type=F32)


def _cast_kernel(w_ref, o_ref):
    o_ref[...] = w_ref[...].astype(o_ref.dtype)


def _cast_bf16(w, rows_per_step):
    r, c = w.shape
    return pl.pallas_call(
        _cast_kernel,
        grid=(r // rows_per_step,),
        in_specs=[pl.BlockSpec((rows_per_step, c), lambda i: (i, 0))],
        out_specs=pl.BlockSpec((rows_per_step, c), lambda i: (i, 0)),
        out_shape=jax.ShapeDtypeStruct((r, c), BF16),
        compiler_params=_cparams(("parallel",)),
        name="cast_bf16",
    )(w)


def _mod_kernel(c_ref, w_ref, b_ref, o_ref):
    c = c_ref[...]
    cond = c * jax.nn.sigmoid(c)
    o_ref[...] = jnp.dot(cond, w_ref[...], preferred_element_type=F32,
                         precision=lax.Precision.HIGHEST) + b_ref[...]


def _modulation(c, w_ada, b_ada):
    bsz, d = c.shape
    n = w_ada.shape[1]
    rows = -(-bsz // SUBLANES) * SUBLANES
    cpad = jnp.zeros((rows, d), F32).at[:bsz].set(c)
    tn = 1024
    out = pl.pallas_call(
        _mod_kernel,
        grid=(n // tn,),
        in_specs=[pl.BlockSpec((rows, d), lambda j: (0, 0)),
                  pl.BlockSpec((d, tn), lambda j: (0, j)),
                  pl.BlockSpec((1, tn), lambda j: (0, j))],
        out_specs=pl.BlockSpec((rows, tn), lambda j: (0, j)),
        out_shape=jax.ShapeDtypeStruct((rows, n), F32),
        compiler_params=_cparams(("parallel",)),
        name="adaln_modulation",
    )(cpad, w_ada, b_ada.reshape(1, n))
    return out[:bsz]


_IN_HG = 4
_IN_ATT = 3 * len(ATT_BRANCHES)


def _inproj_kernel(x_ref, sc_ref, sh_ref, w_ref, *refs):
    outs = refs[:_IN_HG + _IN_ATT]
    scratch = refs[_IN_HG + _IN_ATT:]
    x = x_ref[...]
    h = _ln_rows(x) * (1.0 + sc_ref[...]) + sh_ref[...]
    hb = h.astype(BF16)
    tm = x.shape[0]
    col = 0
    n_scr = 0
    for k, o_ref in enumerate(outs):
        width = HG_WIDTH if k < _IN_HG else ATT_BW
        y = _dot(hb, w_ref[:, col:col + width])
        col += width
        if k < _IN_HG:
            o_ref[...] = y.astype(o_ref.dtype)
            continue
        if k < _IN_HG + len(ATT_BRANCHES):
            y = y * (ATT_HEAD_DIM ** -0.5)
        dil = ATT_BRANCHES[(k - _IN_HG) % len(ATT_BRANCHES)][1]
        if dil == 1:
            o_ref[...] = y.astype(o_ref.dtype)
            continue
        scr = scratch[n_scr]
        n_scr += 1
        for half in range(ATT_BW // LANES):
            scr[half] = y[:, half * LANES:(half + 1) * LANES]
        for r in range(dil):
            for half in range(ATT_BW // LANES):
                c0 = r * ATT_BW + half * LANES
                o_ref[:, c0:c0 + LANES] = scr[half, pl.ds(r, tm // dil, stride=dil), :].astype(o_ref.dtype)


def _in_projection(x2, mod4, w_in_bf, seq, tm):
    t, d = x2.shape
    steps_per_batch = seq // tm
    dils = [dil for _, dil in ATT_BRANCHES] * 3
    shapes = [(t, HG_WIDTH)] * _IN_HG + [(t // dil, dil * ATT_BW) for dil in dils]
    blocks = [(tm, HG_WIDTH)] * _IN_HG + [(tm // dil, dil * ATT_BW) for dil in dils]
    dtypes = [BF16, F32, BF16, BF16] + [BF16] * _IN_ATT
    mod_spec = lambda row: pl.BlockSpec((None, None, 1, d),
                                        lambda i, row=row: (i // steps_per_batch, row, 0, 0))
    outs = pl.pallas_call(
        _inproj_kernel,
        grid=(t // tm,),
        in_specs=[pl.BlockSpec((tm, d), lambda i: (i, 0)),
                  mod_spec(1), mod_spec(0),
                  pl.BlockSpec(w_in_bf.shape, lambda i: (0, 0))],
        out_specs=[pl.BlockSpec(b, lambda i: (i, 0)) for b in blocks],
        out_shape=[jax.ShapeDtypeStruct(s, dt) for s, dt in zip(shapes, dtypes)],
        scratch_shapes=[pltpu.VMEM((ATT_BW // LANES, tm, LANES), F32) for dil in dils if dil > 1],
        compiler_params=_cparams(("parallel",)),
        name="ln_in_projection",
    )(x2, mod4, mod4, w_in_bf)
    return outs


HG_CHUNK = 64
HG_CHUNKS_PER_STEP = 4
HG_SUB = 8
HG_LEVELS = (64, 32, 16)
LOG2E = 1.4426950408889634


def _hgrn_chunk(q, z, iv, lb):
    c = HG_CHUNK
    f = lb + (1.0 - lb) * jax.nn.sigmoid(z)
    lf = jnp.log(f)
    kk = (1.0 - lb) * jax.nn.sigmoid(-z)
    r_i = lax.broadcasted_iota(jnp.int32, (c, c), 0)
    c_i = lax.broadcasted_iota(jnp.int32, (c, c), 1)
    tril = (c_i <= r_i).astype(F32)
    b = jnp.dot(tril, lf, preferred_element_type=F32, precision=lax.Precision.HIGHEST)
    bl = b * LOG2E

    row = lax.broadcasted_iota(jnp.int32, (c, HG_DK), 0)
    scores = jnp.zeros((c, c), F32)
    for m in HG_LEVELS:
        nb = c // m
        b3 = bl.reshape(nb, m, HG_DK)
        piv = jnp.broadcast_to(b3[:, m // 2 - 1:m // 2, :], (nb, m, HG_DK)).reshape(c, HG_DK)
        second = (row % m) >= (m // 2)
        qt = jnp.where(second, q * jnp.exp2(bl - piv), 0.0)
        kt = jnp.where(second, 0.0, kk * jnp.exp2(piv - bl))
        s_m = _dot_nt(qt.astype(BF16), kt.astype(BF16))
        if nb > 1:
            s_m = jnp.where((r_i // m) == (c_i // m), s_m, 0.0)
        scores = scores + s_m
    sub = HG_SUB
    t_i = lax.broadcasted_iota(jnp.int32, (sub, 1), 0)
    lane = lax.broadcasted_iota(jnp.int32, (sub, c), 1)
    diag_rows = []
    for j in range(c // sub):
        qb = q[j * sub:(j + 1) * sub]
        kb = kk[j * sub:(j + 1) * sub]
        bb = bl[j * sub:(j + 1) * sub]
        a_j = jnp.zeros((sub, c), F32)
        for s in range(sub):
            w = qb * kb[s:s + 1] * jnp.exp2(bb - bb[s:s + 1])
            a_j = jnp.where(lane == j * sub + s, jnp.sum(w, axis=-1, keepdims=True), a_j)
        diag_rows.append(jnp.where(lane - j * sub <= t_i, a_j, 0.0))
    scores = scores + jnp.concatenate(diag_rows, axis=0)

    ivb = iv.astype(BF16)
    intra = _dot(scores.astype(BF16), ivb)
    b_last = bl[c - 1:c]
    kdec = (kk * jnp.exp2(b_last - bl)).astype(BF16)
    return (q * jnp.exp2(bl)).astype(BF16), intra, jnp.exp2(b_last), _dot_tn(ivb, kdec)


def _hgrn_kernel(q_ref, f_ref, i_ref, g_ref, lbp_ref, nw_ref, o_ref, st_ref):
    @pl.when(pl.program_id(1) == 0)
    def _():
        st_ref[...] = jnp.zeros_like(st_ref)

    lbp = lbp_ref[...]
    e = jnp.exp(lbp - jnp.max(lbp, axis=0, keepdims=True))
    lb_all = e[0:1] / jnp.sum(e, axis=0, keepdims=True)
    heads = []
    for h in range(HG_HEADS):
        sl = slice(h * HG_DK, (h + 1) * HG_DK)
        st_t = st_ref[h]
        outs = []
        for n in range(HG_CHUNKS_PER_STEP):
            rows = slice(n * HG_CHUNK, (n + 1) * HG_CHUNK)
            qdec, intra, dec_last, kv = _hgrn_chunk(q_ref[rows, sl].astype(F32), f_ref[rows, sl],
                                                    i_ref[rows, sl].astype(F32), lb_all[:, sl])
            o = _dot_nt(qdec, st_t.astype(BF16)) + intra
            st_t = st_t * dec_last + kv
            outs.append(o * lax.rsqrt(jnp.mean(o * o, axis=-1, keepdims=True) + RMS_EPS))
        st_ref[h] = st_t
        heads.append(jnp.concatenate(outs, axis=0))
    o_all = jnp.concatenate(heads, axis=-1)
    g = g_ref[...].astype(F32)
    o_ref[...] = (o_all * nw_ref[...] * (g * jax.nn.sigmoid(g))).astype(o_ref.dtype)


def _hgrn2(hq, hf, hi, hg, lb_param, norm_w, bsz, seq):
    t = hq.shape[0]
    rows = HG_CHUNK * HG_CHUNKS_PER_STEP
    nc = seq // rows
    tok = lambda b, n: (b * nc + n, 0)
    spec = pl.BlockSpec((rows, HG_WIDTH), tok)
    return pl.pallas_call(
        _hgrn_kernel,
        grid=(bsz, nc),
        in_specs=[spec, spec, spec, spec,
                  pl.BlockSpec(lb_param.shape, lambda b, n: (0, 0)),
                  pl.BlockSpec((1, HG_WIDTH), lambda b, n: (0, 0))],
        out_specs=spec,
        out_shape=jax.ShapeDtypeStruct((t, HG_WIDTH), BF16),
        scratch_shapes=[pltpu.VMEM((HG_HEADS, HG_DK, HG_DK), F32)],
        compiler_params=_cparams(("parallel", "arbitrary")),
        name="hgrn2_scan",
    )(hq, hf, hi, hg, lb_param, norm_w.reshape(1, HG_WIDTH))


def _t5_bucket_np(dist):
    max_exact = REL_BUCKETS // 2
    n = np.maximum(dist, 0)
    nf = np.maximum(n, 1).astype(np.float32)
    large = max_exact + (np.log(nf / np.float32(max_exact)) / np.float32(math.log(REL_MAX_DIST / max_exact))
                         * np.float32(REL_BUCKETS - max_exact)).astype(np.int32)
    large = np.minimum(large, REL_BUCKETS - 1)
    return np.where(n < max_exact, n, large).astype(np.int32)


def _band_tables():
    w = ATT_BLOCK
    qi = np.arange(w)[:, None]
    ki = np.arange(2 * w)[None, :]
    m = w + qi - ki
    band = (m >= 0) & (m <= w)
    buckets = np.stack([_t5_bucket_np(m * dil) for _, dil in ATT_BRANCHES])
    return buckets, band


def _bias_kernel(rb_ref, bucket_ref, o_ref):
    g = pl.program_id(0)
    w = ATT_BLOCK
    bucket = bucket_ref[...]
    qi = lax.broadcasted_iota(jnp.int32, (w, 2 * w), 0)
    ki = lax.broadcasted_iota(jnp.int32, (w, 2 * w), 1)
    m = w + qi - ki
    band = (m >= 0) & (m <= w)
    for h in range(ATT_HEADS_PER_BRANCH):
        acc = jnp.zeros((w, 2 * w), F32)
        for c in range(REL_BUCKETS):
            acc = jnp.where(bucket == c, rb_ref[c, g * ATT_HEADS_PER_BRANCH + h], acc)
        full = jnp.where(band, acc, NEG_INF)
        o_ref[1, h] = full
        o_ref[0, h] = jnp.where(ki >= w, full, NEG_INF)


def _bias_tables(rel_bias):
    buckets, _ = _band_tables()
    g = len(ATT_BRANCHES)
    w = ATT_BLOCK
    return pl.pallas_call(
        _bias_kernel,
        grid=(g,),
        in_specs=[pl.BlockSpec(memory_space=pltpu.SMEM),
                  pl.BlockSpec((None, w, 2 * w), lambda i: (i, 0, 0))],
        out_specs=pl.BlockSpec((None, 2, ATT_HEADS_PER_BRANCH, w, 2 * w), lambda i: (i, 0, 0, 0, 0)),
        out_shape=jax.ShapeDtypeStruct((g, 2, ATT_HEADS_PER_BRANCH, w, 2 * w), F32),
        compiler_params=_cparams(("parallel",)),
        name="rel_bias_tables",
    )(rel_bias, jnp.asarray(buckets))


ATT_BLOCKS_PER_STEP = 16


def _attn_kernel(q_ref, kp_ref, kc_ref, vp_ref, vc_ref, bias_ref, o_ref, lse_ref):
    m = pl.program_id(2)
    w = ATT_BLOCK
    hb = ATT_HEADS_PER_BRANCH
    lane = lax.broadcasted_iota(jnp.int32, (w, ATT_BW), 1) // ATT_HEAD_DIM
    for res, blk in [(r, b) for r in range(q_ref.shape[1] // ATT_BW) for b in range(q_ref.shape[0] // w)]:
        cols = slice(res * ATT_BW, (res + 1) * ATT_BW)
        kall = jnp.concatenate([kp_ref[:, cols], kc_ref[:, cols]], axis=0)
        vall = jnp.concatenate([vp_ref[:, cols], vc_ref[:, cols]], axis=0)
        q = q_ref[blk * w:(blk + 1) * w, cols]
        q4 = jnp.concatenate([jnp.where(lane == h, q, jnp.zeros_like(q)) for h in range(hb)], axis=0)
        kk = kall[blk * w:(blk + 2) * w]
        vv = vall[blk * w:(blk + 2) * w]
        s4 = _dot_nt(q4, kk)
        bias = bias_ref[jnp.minimum(m, 1)] if blk == 0 else bias_ref[1]
        s4 = s4 + bias.reshape(hb * w, 2 * w)
        mx = jnp.max(s4, axis=-1, keepdims=True)
        p = jnp.exp(s4 - mx)
        l = jnp.sum(p, axis=-1, keepdims=True)
        o4 = _dot((p / l).astype(vv.dtype), vv)
        lse4 = mx + jnp.log(l)
        o = jnp.zeros((w, ATT_BW), F32)
        lse = jnp.zeros((w, ATT_BW), F32)
        for h in range(hb):
            o = jnp.where(lane == h, o4[h * w:(h + 1) * w], o)
            lse = jnp.where(lane == h, lse4[h * w:(h + 1) * w], lse)
        o_ref[blk * w:(blk + 1) * w, cols] = o.astype(o_ref.dtype)
        lse_ref[blk * w:(blk + 1) * w, cols] = lse


def _dilated_attention(q, k, v, bias_g, bsz, seq, dilation):
    w = ATT_BLOCK
    l = seq // dilation
    pstep = math.gcd(ATT_BLOCKS_PER_STEP, l // w)
    nb = l // (w * pstep)
    rstep = math.gcd(ATT_BLOCKS_PER_STEP // pstep, dilation)
    view = lambda a: a.reshape(bsz, l, dilation * ATT_BW)
    cur = pl.BlockSpec((None, pstep * w, rstep * ATT_BW), lambda b, r, n: (b, n, r))
    prev = pl.BlockSpec((None, w, rstep * ATT_BW), lambda b, r, n: (b, jnp.maximum(pstep * n - 1, 0), r))
    o, lse = pl.pallas_call(
        _attn_kernel,
        grid=(bsz, dilation // rstep, nb),
        in_specs=[cur, prev, cur, prev, cur,
                  pl.BlockSpec(bias_g.shape, lambda b, r, n: (0, 0, 0, 0))],
        out_specs=[cur, cur],
        out_shape=[jax.ShapeDtypeStruct((bsz, l, dilation * ATT_BW), BF16),
                   jax.ShapeDtypeStruct((bsz, l, dilation * ATT_BW), F32)],
        compiler_params=_cparams(("parallel", "parallel", "arbitrary")),
        name=f"dilated_attention_d{dilation}",
    )(view(q), view(k), view(k), view(v), view(v), bias_g)
    return o.reshape(bsz * l, dilation * ATT_BW), lse.reshape(bsz * l, dilation * ATT_BW)


def _split_bf16(a):
    hi = a.astype(BF16)
    lo = (a - hi.astype(F32)).astype(BF16)
    return hi, lo


H2P_CHUNKS = 4


def _token_order(ref, scr, dil):
    if dil == 1:
        return ref[...].astype(F32)
    n = ref.shape[0]
    halves = ATT_BW // LANES
    for r in range(dil):
        for half in range(halves):
            c0 = r * ATT_BW + half * LANES
            scr[half, pl.ds(r, n, stride=dil), :] = ref[:, c0:c0 + LANES].astype(F32)
    return jnp.concatenate([scr[half] for half in range(halves)], axis=1)


def _outproj_kernel(yhg_ref, o1_ref, o2_ref, o3_ref, l1_ref, l2_ref, l3_ref, x_ref,
                    g1_ref, sc2_ref, sh2_ref, wout_ref, lng_ref, lnb_ref, wrt_ref,
                    x1_ref, h2_ref, h2p_ref, lgt_ref, *scratch):
    dils = [dil for _, dil in ATT_BRANCHES]
    scr = iter(scratch)
    o1, o2, o3 = [_token_order(r, None if dil == 1 else next(scr), dil)
                  for r, dil in zip((o1_ref, o2_ref, o3_ref), dils)]
    l1, l2, l3 = [_token_order(r, None if dil == 1 else next(scr), dil)
                  for r, dil in zip((l1_ref, l2_ref, l3_ref), dils)]
    mx = jnp.maximum(jnp.maximum(l1, l2), l3)
    e1, e2, e3 = jnp.exp(l1 - mx), jnp.exp(l2 - mx), jnp.exp(l3 - mx)
    den = e1 + e2 + e3
    att = (e1 / den) * o1 + (e2 / den) * o2 + (e3 / den) * o3
    mix = _dot(yhg_ref[...], wout_ref[:HG_WIDTH, :]) + _dot(att.astype(BF16), wout_ref[HG_WIDTH:, :])
    x1 = _ln_rows(DN_ALPHA * x_ref[...] + g1_ref[...] * mix) * lng_ref[...] + lnb_ref[...]
    x1_ref[...] = x1
    h2 = _ln_rows(x1) * (1.0 + sc2_ref[...]) + sh2_ref[...]
    h_hi = h2.astype(BF16)
    h_hf = h_hi.astype(F32)
    h2_ref[...] = h_hi
    bits = lax.bitcast_convert_type(h_hf, jnp.uint32)
    for cidx in range(H2P_CHUNKS):
        lo = bits[:, 2 * LANES * cidx:2 * LANES * cidx + LANES]
        hi = bits[:, 2 * LANES * cidx + LANES:2 * LANES * (cidx + 1)]
        h2p_ref[pl.ds(cidx, h2.shape[0], stride=H2P_CHUNKS), :] = lax.bitcast_convert_type((lo >> 16) | hi,
                                                                                          jnp.int32)
    h_lo = (h2 - h_hf).astype(BF16)
    w_hi, w_lo = _split_bf16(wrt_ref[...])
    lgt_ref[...] = _dot_nt(w_hi, h_hi) + (_dot_nt(w_hi, h_lo) + _dot_nt(w_lo, h_hi))


def _out_projection(yhg, os_, ls_, x2, mod4, w_out_bf, ln_g, ln_b, w_router_t, seq, tm):
    t, d = x2.shape
    spb = seq // tm
    ne = w_router_t.shape[0]
    row = lambda w: pl.BlockSpec((tm, w), lambda i: (i, 0))
    mod_spec = lambda r: pl.BlockSpec((None, None, 1, d), lambda i, r=r: (i // spb, r, 0, 0))
    full = lambda a: pl.BlockSpec(a.shape, lambda i: (0,) * a.ndim)
    ln_g2, ln_b2 = ln_g.reshape(1, d), ln_b.reshape(1, d)
    dils = [dil for _, dil in ATT_BRANCHES]
    branch = [pl.BlockSpec((tm // dil, dil * ATT_BW), lambda i: (i, 0)) for dil in dils]
    return pl.pallas_call(
        _outproj_kernel,
        grid=(t // tm,),
        in_specs=[row(HG_WIDTH)] + branch + branch + [row(d),
                  mod_spec(2), mod_spec(4), mod_spec(3),
                  full(w_out_bf), full(ln_g2), full(ln_b2), full(w_router_t)],
        out_specs=[row(d), row(d), pl.BlockSpec((tm * H2P_CHUNKS, LANES), lambda i: (i, 0)),
                   pl.BlockSpec((ne, tm), lambda i: (0, i))],
        out_shape=[jax.ShapeDtypeStruct((t, d), F32), jax.ShapeDtypeStruct((t, d), BF16),
                   jax.ShapeDtypeStruct((t * H2P_CHUNKS, LANES), jnp.int32),
                   jax.ShapeDtypeStruct((ne, t), F32)],
        scratch_shapes=[pltpu.VMEM((ATT_BW // LANES, tm, LANES), F32) for dil in dils + dils if dil > 1],
        compiler_params=_cparams(("parallel",)),
        name="merge_outproj_ln",
    )(yhg, *os_, *ls_, x2, mod4, mod4, mod4, w_out_bf, ln_g2, ln_b2, w_router_t)


def _argmax_rows(cur, iota, nrows):
    m = jnp.max(cur, axis=0, keepdims=True)
    idx = jnp.min(jnp.where(cur == m, iota, nrows), axis=0, keepdims=True)
    return m, idx, iota == idx


def _route_kernel(lgt_ref, rb_ref, eidx_ref, w_ref, rank_ref, cnt_ref, carry):
    ne = N_EXPERTS
    gsz = ne // N_GROUPS
    tt = lgt_ref.shape[1]

    @pl.when(pl.program_id(0) == 0)
    def _():
        carry[...] = jnp.zeros_like(carry)

    sc = jax.nn.sigmoid(lgt_ref[...])
    biased = sc + rb_ref[...]
    g3 = biased.reshape(N_GROUPS, gsz, tt)
    io3 = lax.broadcasted_iota(jnp.int32, (N_GROUPS, gsz, tt), 1)
    m1 = jnp.max(g3, axis=1, keepdims=True)
    first = jnp.min(jnp.where(g3 == m1, io3, gsz), axis=1, keepdims=True)
    m2 = jnp.max(jnp.where(io3 == first, NEG_INF, g3), axis=1, keepdims=True)
    gs = (m1 + m2).reshape(N_GROUPS, tt)
    io8 = lax.broadcasted_iota(jnp.int32, (N_GROUPS, tt), 0)
    sel = jnp.zeros((N_GROUPS, tt), jnp.int32)
    cur = gs
    for _ in range(TOPK_GROUPS):
        _, _, pick = _argmax_rows(cur, io8, N_GROUPS)
        sel = jnp.where(pick, 1, sel)
        cur = jnp.where(pick, NEG_INF, cur)
    masked = jnp.where(sel.reshape(N_GROUPS, 1, tt) > 0, g3, NEG_INF).reshape(ne, tt)
    ioe = lax.broadcasted_iota(jnp.int32, (ne, tt), 0)
    cur = masked
    idxs, ws, picks = [], [], []
    for _ in range(TOP_K):
        _, idx, pick = _argmax_rows(cur, ioe, ne)
        idxs.append(idx)
        picks.append(pick)
        ws.append(jnp.sum(jnp.where(pick, sc, 0.0), axis=0, keepdims=True))
        cur = jnp.where(pick, NEG_INF, cur)
    wk = jnp.concatenate(ws, axis=0)
    eidx_ref[...] = jnp.concatenate(idxs, axis=0)
    w_ref[...] = wk / jnp.sum(wk, axis=0, keepdims=True) * ROUTED_SCALE
    chosen = jnp.where(cur == NEG_INF, jnp.where(masked == NEG_INF, 0.0, 1.0), 0.0)
    r_i = lax.broadcasted_iota(jnp.int32, (tt, tt), 0)
    c_i = lax.broadcasted_iota(jnp.int32, (tt, tt), 1)
    before = jnp.where(r_i < c_i, 1.0, 0.0).astype(BF16)
    pref = _dot(chosen.astype(BF16), before) + carry[...]
    rank_ref[...] = jnp.concatenate(
        [jnp.sum(jnp.where(p, pref, 0.0), axis=0, keepdims=True) for p in picks], axis=0).astype(jnp.int32)
    carry[...] = carry[...] + jnp.sum(chosen, axis=1, keepdims=True)
    cnt_ref[...] = carry[...]


def _route(logits_t, router_bias, tt):
    ne, t = logits_t.shape
    tok = pl.BlockSpec((TOP_K, tt), lambda i: (0, i))
    return pl.pallas_call(
        _route_kernel,
        grid=(t // tt,),
        in_specs=[pl.BlockSpec((ne, tt), lambda i: (0, i)),
                  pl.BlockSpec((ne, 1), lambda i: (0, 0))],
        out_specs=[tok, tok, tok, pl.BlockSpec((ne, 1), lambda i: (0, 0))],
        out_shape=[jax.ShapeDtypeStruct((TOP_K, t), jnp.int32), jax.ShapeDtypeStruct((TOP_K, t), F32),
                   jax.ShapeDtypeStruct((TOP_K, t), jnp.int32), jax.ShapeDtypeStruct((ne, 1), F32)],
        scratch_shapes=[pltpu.VMEM((ne, 1), F32)],
        compiler_params=_cparams(("arbitrary",)),
        name="router_topk",
    )(logits_t, router_bias.reshape(ne, 1))


def _plan_kernel(cnt_ref, eidx_ref, rank_ref, dest_ref, seq_ref, dexp_ref, meta_ref, pstart_ref):
    ne = N_EXPERTS
    tt = eidx_ref.shape[1]
    nblk = seq_ref.shape[1]

    @pl.when(pl.program_id(0) == 0)
    def _():
        cnt = cnt_ref[...].astype(jnp.int32)
        padded = ((cnt + (MOE_BLOCK - 1)) // MOE_BLOCK) * MOE_BLOCK
        r_i = lax.broadcasted_iota(jnp.int32, (ne, ne), 0)
        c_i = lax.broadcasted_iota(jnp.int32, (ne, ne), 1)
        incl = jnp.where(c_i <= r_i, 1.0, 0.0)
        pend = jnp.dot(incl, jnp.broadcast_to(padded.astype(F32), (ne, LANES)),
                       preferred_element_type=F32, precision=lax.Precision.HIGHEST)[:, 0:1]
        pend = pend.astype(jnp.int32)
        pstart_ref[...] = pend - padded
        blk0 = lax.broadcasted_iota(jnp.int32, (ne, nblk), 1) * MOE_BLOCK
        be = jnp.minimum(jnp.sum(jnp.where(pend <= blk0, 1, 0), axis=0, keepdims=True), ne - 1)
        present = cnt > 0
        strict = jnp.where(c_i < r_i, 1.0, 0.0).astype(BF16)
        sidx = _dot(strict, jnp.broadcast_to(jnp.where(present, 1.0, 0.0), (ne, LANES)).astype(BF16))[:, 0:1]
        sidx = sidx.astype(jnp.int32)
        dexp_ref[...] = jnp.sum(jnp.where(jnp.logical_and(present, sidx == c_i), r_i, 0), axis=0, keepdims=True)
        ioeb = lax.broadcasted_iota(jnp.int32, (ne, nblk), 0)
        nu = jnp.max(pend, axis=0, keepdims=True) // MOE_BLOCK
        nd = jnp.sum(jnp.where(present, 1, 0), axis=0, keepdims=True)
        seq_ref[...] = jnp.minimum(jnp.sum(jnp.where(ioeb == be, sidx, 0), axis=0, keepdims=True), nd - 1)
        lane = lax.broadcasted_iota(jnp.int32, (1, LANES), 1)
        meta_ref[...] = jnp.where(lane == 0, nu, jnp.where(lane == 1, nd, 0))

    pstart = pstart_ref[...]
    ioe = lax.broadcasted_iota(jnp.int32, (ne, tt), 0)
    rows = []
    for k in range(TOP_K):
        sel = ioe == eidx_ref[k:k + 1, :]
        rows.append(jnp.sum(jnp.where(sel, pstart, 0), axis=0, keepdims=True))
    dest_ref[...] = jnp.concatenate(rows, axis=0) + rank_ref[...]


def _dispatch_plan(cnt, eidx, rank, tt):
    k, t = eidx.shape
    n_blocks = -(-(t * k) // MOE_BLOCK) + N_EXPERTS
    tok = pl.BlockSpec((k, tt), lambda i: (0, i))
    one = lambda n: pl.BlockSpec((1, n), lambda i: (0, 0))
    dest, seq, dexp, meta = pl.pallas_call(
        _plan_kernel,
        grid=(t // tt,),
        in_specs=[pl.BlockSpec(cnt.shape, lambda i: (0, 0)), tok, tok],
        out_specs=[tok, one(n_blocks), one(N_EXPERTS), one(LANES)],
        out_shape=[jax.ShapeDtypeStruct((k, t), jnp.int32), jax.ShapeDtypeStruct((1, n_blocks), jnp.int32),
                   jax.ShapeDtypeStruct((1, N_EXPERTS), jnp.int32), jax.ShapeDtypeStruct((1, LANES), jnp.int32)],
        scratch_shapes=[pltpu.VMEM((N_EXPERTS, 1), jnp.int32)],
        compiler_params=_cparams(("arbitrary",)),
        name="dispatch_plan",
    )(cnt, eidx, rank)
    return dest, seq.reshape(n_blocks), dexp.reshape(N_EXPERTS), meta.reshape(LANES), n_blocks


SC_CORES = 2
SC_SUBCORES = 16
SC_LANES = 16
SC_CHUNK = 16384
SC_UNROLL = 4


def _row_tables(dest, n_rows):
    k, t = dest.shape
    a = k * t
    nw = SC_CORES * SC_SUBCORES
    per_w = n_rows // nw
    assert n_rows % (nw * SC_LANES) == 0 and t % SC_CHUNK == 0
    mesh = plsc.VectorSubcoreMesh(core_axis_name="c", subcore_axis_name="s")
    cp = pltpu.CompilerParams()
    if "needs_layout_passes" in pltpu.CompilerParams.__dataclass_fields__:
        cp = dataclasses.replace(cp, needs_layout_passes=False)

    def body(dest_hbm, tok_out, dbuf, tloc):
        wid = lax.axis_index("s") * SC_CORES + lax.axis_index("c")
        base = wid * per_w

        @pl.loop(0, per_w // SC_LANES)
        def _(i):
            tloc[pl.ds(i * SC_LANES, SC_LANES)] = jnp.zeros((SC_LANES,), jnp.int32)

        lane = lax.iota(jnp.int32, SC_LANES)

        @pl.loop(0, a // SC_CHUNK)
        def _(c):
            pltpu.sync_copy(dest_hbm.at[pl.ds(c * SC_CHUNK, SC_CHUNK)], dbuf)
            tok0 = lax.rem(c * SC_CHUNK, t)

            @pl.loop(0, SC_CHUNK // (SC_LANES * SC_UNROLL))
            def _(j):
                for u in range(SC_UNROLL):
                    off = (j * SC_UNROLL + u) * SC_LANES
                    loc = dbuf[pl.ds(off, SC_LANES)] - base
                    mine = jnp.logical_and(loc >= 0, loc < per_w)
                    loc = jnp.where(mine, loc, 0)
                    plsc.store_scatter(tloc, [loc], tok0 + off + lane, mask=mine)

        pltpu.sync_copy(tloc, tok_out.at[pl.ds(base, per_w)])

    fn = pl.kernel(
        body,
        out_type=jax.ShapeDtypeStruct((n_rows,), jnp.int32),
        mesh=mesh,
        scratch_types=[pltpu.VMEM((SC_CHUNK,), jnp.int32), pltpu.VMEM((per_w,), jnp.int32)],
        compiler_params=cp,
        name="row_tables",
    )
    return fn(dest.reshape(a))


ROW_TILE = (SUBLANES, LANES)


FFN_GROUP = 4
PAIR_ROWS = 2 * MOE_BLOCK
FFN_CHAIN = 128
W_SETS = 3
GATHER_BATCH = 32


def _gather_rows(h2p_ref, tok_ref, row, buf, base):
    for j0 in range(0, MOE_BLOCK, GATHER_BATCH):
        vals = [h2p_ref[pl.ds(pl.multiple_of(tok_ref[row, j] * H2P_CHUNKS, H2P_CHUNKS), H2P_CHUNKS), :]
                for j in range(j0, j0 + GATHER_BATCH)]
        for j, v in zip(range(j0, j0 + GATHER_BATCH), vals):
            buf[pl.ds(H2P_CHUNKS * (base + j), H2P_CHUNKS), :] = v


def _expert_rows(buf, chains, out):
    hms = []
    for row0, nrows, (wg_c, wu_c, _), _ in chains:
        parts = []
        for cidx in range(H2P_CHUNKS):
            word = buf[pl.ds(H2P_CHUNKS * row0 + cidx, nrows, stride=H2P_CHUNKS), :]
            parts.append(lax.bitcast_convert_type(word << 16, F32))
            parts.append(lax.bitcast_convert_type(word & jnp.int32(-65536), F32))
        x = jnp.concatenate(parts, axis=1)
        g = _dot(x, wg_c[...])
        u = _dot(x, wu_c[...])
        hms.append((g * jax.nn.sigmoid(g)) * u)
    for (_, nrows, (_, _, wd_c), out_row0), hm in zip(chains, hms):
        res = _dot(hm, wd_c[...])
        nch = res.shape[1] // LANES
        for c in range(nch):
            out[pl.ds(nch * out_row0 + c, nrows, stride=nch), :] = res[:, c * LANES:(c + 1) * LANES]


def _ffn_kernel(seq_ref, dexp_ref, meta_ref, tokc_ref, tokn_ref, h2p_ref, wg_hbm, wu_hbm, wd_hbm,
                y_ref, buf_0, buf_1, *rest):
    i = pl.program_id(0)
    n_used = meta_ref[0]
    n_exp = meta_ref[1]
    nblk = seq_ref.shape[0]
    wsets = tuple(tuple(rest[3 * n:3 * n + 3]) for n in range(W_SETS))
    started_ref, sems = rest[3 * W_SETS:]
    bufs = (buf_0, buf_1)

    def weight_copies(s, par):
        e = dexp_ref[s]
        return [pltpu.make_async_copy(src.at[e], dst, sems.at[par, n])
                for n, (src, dst) in enumerate(zip((wg_hbm, wu_hbm, wd_hbm), wsets[par]))]

    def start_expert(s):
        for par in range(W_SETS):
            @pl.when(s % W_SETS == par)
            def _():
                for cp in weight_copies(s, par):
                    cp.start()

    @pl.when(i == 0)
    def _():
        _gather_rows(h2p_ref, tokc_ref, 0, buf_0, 0)
        _gather_rows(h2p_ref, tokc_ref, 1, buf_0, MOE_BLOCK)
        start_expert(0)
        started_ref[0] = 0

    for p in range(FFN_GROUP // 2):
        b_a = i * FFN_GROUP + 2 * p
        used = b_a < n_used
        s_a = seq_ref[jnp.minimum(b_a, nblk - 1)]
        s_b = seq_ref[jnp.minimum(b_a + 1, nblk - 1)]
        first_a = jnp.logical_or(b_a == 0, s_a != seq_ref[jnp.clip(b_a - 1, 0, nblk - 1)])
        same = s_a == s_b
        cur, nxt = bufs[p], bufs[1 - p]
        out_row0 = p * PAIR_ROWS

        def gather_next():
            if p == 0:
                _gather_rows(h2p_ref, tokc_ref, 2, nxt, 0)
                _gather_rows(h2p_ref, tokc_ref, 3, nxt, MOE_BLOCK)
            else:
                _gather_rows(h2p_ref, tokn_ref, 0, nxt, 0)
                _gather_rows(h2p_ref, tokn_ref, 1, nxt, MOE_BLOCK)

        @pl.when(used)
        def _():
            started = started_ref[0]
            limit = jnp.minimum(s_a + (W_SETS - 1), n_exp - 1)
            for _unused in range(W_SETS - 1):
                go = started < limit

                @pl.when(go)
                def _():
                    start_expert(started + 1)
                started = jnp.where(go, started + 1, started)
            started_ref[0] = started

        for par in range(W_SETS):
            @pl.when(jnp.logical_and(used, jnp.logical_and(same, s_a % W_SETS == par)))
            def _():
                @pl.when(first_a)
                def _():
                    for cp in weight_copies(s_a, par):
                        cp.wait()
                gather_next()
                _expert_rows(cur, [(r0, FFN_CHAIN, wsets[par], out_row0 + r0)
                                   for r0 in range(0, PAIR_ROWS, FFN_CHAIN)], y_ref)

            @pl.when(jnp.logical_and(used, jnp.logical_and(jnp.logical_not(same), s_a % W_SETS == par)))
            def _():
                @pl.when(first_a)
                def _():
                    for cp in weight_copies(s_a, par):
                        cp.wait()
                for cp in weight_copies(s_b, (par + 1) % W_SETS):
                    cp.wait()
                gather_next()
                _expert_rows(cur, [(r0, FFN_CHAIN, wsets[(par + r0 // MOE_BLOCK) % W_SETS], out_row0 + r0)
                                   for r0 in range(0, PAIR_ROWS, FFN_CHAIN)], y_ref)

        @pl.when(jnp.logical_not(used))
        def _():
            y_ref[pl.ds(out_row0 * SUBLANES, PAIR_ROWS * SUBLANES), :] = jnp.zeros(
                (PAIR_ROWS * SUBLANES, LANES), y_ref.dtype)


def _routed_experts(h2p, row_token, seq, dexp, meta, n_blocks, wg, wu, wd):
    d = wg.shape[1]
    de = wg.shape[2]
    ng = n_blocks // FFN_GROUP
    assert n_blocks % FFN_GROUP == 0 and FFN_GROUP == 4 and d == SUBLANES * LANES
    tok3 = row_token.reshape(ng, FFN_GROUP, MOE_BLOCK)
    idle_step = lambda m: jnp.minimum((m[0] + FFN_GROUP - 1) // FFN_GROUP, ng - 1)
    smem = lambda imap: pl.BlockSpec((None, FFN_GROUP, MOE_BLOCK), imap, memory_space=pltpu.SMEM)
    grid_spec = pltpu.PrefetchScalarGridSpec(
        num_scalar_prefetch=3,
        grid=(ng,),
        in_specs=[
            smem(lambda i, sq, dx, m: (jnp.minimum(i, ng - 1), 0, 0)),
            smem(lambda i, sq, dx, m: (jnp.minimum(i + 1, ng - 1), 0, 0)),
            pl.BlockSpec(h2p.shape, lambda i, sq, dx, m: (0, 0), pipeline_mode=pl.Buffered(1)),
            pl.BlockSpec(memory_space=pl.ANY),
            pl.BlockSpec(memory_space=pl.ANY),
            pl.BlockSpec(memory_space=pl.ANY),
        ],
        out_specs=pl.BlockSpec((FFN_GROUP * MOE_BLOCK * SUBLANES, LANES),
                               lambda i, sq, dx, m: (jnp.minimum(i, idle_step(m)), 0)),
        scratch_shapes=[pltpu.VMEM((PAIR_ROWS * H2P_CHUNKS, LANES), jnp.int32)] * 2 + [
            pltpu.VMEM((d, de), F32), pltpu.VMEM((d, de), F32), pltpu.VMEM((de, d), F32)] * W_SETS + [
            pltpu.SMEM((1,), jnp.int32), pltpu.SemaphoreType.DMA((W_SETS, 3))],
    )
    y = pl.pallas_call(
        _ffn_kernel,
        grid_spec=grid_spec,
        out_shape=jax.ShapeDtypeStruct((n_blocks * MOE_BLOCK * SUBLANES, LANES), F32),
        compiler_params=_cparams(("arbitrary",)),
        name="routed_experts",
    )(seq, dexp, meta, tok3, tok3, h2p, wg, wu, wd)
    return y.reshape((n_blocks * MOE_BLOCK,) + ROW_TILE)


COMBINE_TOKENS = 128


def _row_copies(src_hbm, idx_ref, buf, sem):
    return [pltpu.make_async_copy(src_hbm.at[idx_ref[k, j]], buf.at[k, j], sem)
            for k in range(TOP_K) for j in range(COMBINE_TOKENS)]


def _shared_kernel(h_ref, wsg_ref, wsu_ref, wsd_ref, o_ref):
    hb = h_ref[...]
    g = _dot(hb, wsg_ref[...])
    u = _dot(hb, wsu_ref[...])
    o_ref[...] = _dot(((g * jax.nn.sigmoid(g)) * u).astype(BF16), wsd_ref[...])


def _shared_expert(h2b, wsg, wsu, wsd, tm):
    t, d = h2b.shape
    row = pl.BlockSpec((tm, d), lambda i: (i, 0))
    full = lambda a: pl.BlockSpec(a.shape, lambda i: (0,) * a.ndim)
    return pl.pallas_call(
        _shared_kernel,
        grid=(t // tm,),
        in_specs=[row, full(wsg), full(wsu), full(wsd)],
        out_specs=row,
        out_shape=jax.ShapeDtypeStruct((t, d), F32),
        compiler_params=_cparams(("parallel",)),
        name="shared_expert",
    )(h2b, wsg, wsu, wsd)


def _final_kernel(dc_ref, dn_ref, wk_ref, x1_ref, sh_ref, g2_ref, lng_ref, lnb_ref,
                  y_hbm, o_ref, ybuf, sems):
    i = pl.program_id(0)
    nsteps = pl.num_programs(0)
    slot = i % 2

    def issue(d_ref, s):
        for n, cp in enumerate(_row_copies(y_hbm, d_ref, ybuf.at[s], sems.at[s])):
            cp.start(priority=n % 2)

    @pl.when(i == 0)
    def _():
        issue(dc_ref, 0)

    @pl.when(i + 1 < nsteps)
    def _():
        issue(dn_ref, 1 - slot)

    for cp in _row_copies(y_hbm, dc_ref, ybuf.at[slot], sems.at[slot]):
        cp.wait()
    rows = []
    for j in range(COMBINE_TOKENS):
        acc = ybuf[slot, 0, j].astype(F32) * wk_ref[0, j]
        for k in range(1, TOP_K):
            acc = acc + ybuf[slot, k, j].astype(F32) * wk_ref[k, j]
        rows.append(acc)
    routed = pltpu.einshape("tcl->t(cl)", jnp.stack(rows, axis=0))
    x2 = DN_ALPHA * x1_ref[...] + g2_ref[...] * (routed + sh_ref[...])
    o_ref[...] = _ln_rows(x2) * lng_ref[...] + lnb_ref[...]


def _combine_final(x1, shared, mod4, dest, wk, y_rows, ln_g, ln_b, seq):
    t, d = x1.shape
    tt = COMBINE_TOKENS
    nsteps = t // tt
    spb = seq // tt
    row = pl.BlockSpec((tt, d), lambda i: (i, 0))
    full = lambda a: pl.BlockSpec(a.shape, lambda i: (0,) * a.ndim)
    ln_g2, ln_b2 = ln_g.reshape(1, d), ln_b.reshape(1, d)
    return pl.pallas_call(
        _final_kernel,
        grid=(nsteps,),
        in_specs=[pl.BlockSpec((TOP_K, tt), lambda i: (0, i), memory_space=pltpu.SMEM),
                  pl.BlockSpec((TOP_K, tt), lambda i: (0, jnp.minimum(i + 1, nsteps - 1)),
                               memory_space=pltpu.SMEM),
                  pl.BlockSpec((TOP_K, tt), lambda i: (0, i), memory_space=pltpu.SMEM),
                  row, row,
                  pl.BlockSpec((None, None, 1, d), lambda i: (i // spb, 5, 0, 0)),
                  full(ln_g2), full(ln_b2),
                  pl.BlockSpec(memory_space=pl.ANY)],
        out_specs=row,
        out_shape=jax.ShapeDtypeStruct((t, d), F32),
        scratch_shapes=[pltpu.VMEM((2, TOP_K, tt) + ROW_TILE, y_rows.dtype),
                        pltpu.SemaphoreType.DMA((2,))],
        compiler_params=_cparams(("arbitrary",)),
        name="shared_combine_ln",
    )(dest, dest, wk, x1, shared, mod4, ln_g2, ln_b2, y_rows)


def kernel(x, c, w_ada, b_ada, w_in, hg_lower_bound, hg_norm_w, rel_bias, w_out, ln1_g, ln1_b, w_router,
           router_bias, w_e_gate, w_e_up, w_e_down, w_sh_gate, w_sh_up, w_sh_down, ln2_g, ln2_b):
    bsz, seq, d = x.shape
    t = bsz * seq
    assert w_ada.shape[0] == DEPTH and seq % (ATT_BRANCHES[-1][0]) == 0
    x2 = x.reshape(t, d)
    bias = _bias_tables(rel_bias)
    for l in range(DEPTH):
        mod4 = _modulation(c, w_ada[l], b_ada[l]).reshape(bsz, 6, 1, d)
        w_in_bf = _cast_bf16(w_in[l], 256)
        hq, hf, hi, hg, *qkv = _in_projection(x2, mod4, w_in_bf, seq, 512)
        nbr = len(ATT_BRANCHES)
        y_hg = _hgrn2(hq, hf, hi, hg, hg_lower_bound, hg_norm_w[l], bsz, seq)
        os_, ls_ = [], []
        for g, (_, dil) in enumerate(ATT_BRANCHES):
            o, lse = _dilated_attention(qkv[g], qkv[nbr + g], qkv[2 * nbr + g], bias[g], bsz, seq, dil)
            os_.append(o)
            ls_.append(lse)
        w_out_bf = _cast_bf16(w_out[l], 256)
        x1, h2b, h2p, logits_t = _out_projection(y_hg, os_, ls_, x2, mod4, w_out_bf, ln1_g[l], ln1_b[l],
                                                 w_router[l].T, seq, 512)
        eidx, wk, rank, cnt = _route(logits_t, router_bias[l], 256)
        dest, blk_seq, dexp, meta, n_blocks = _dispatch_plan(cnt, eidx, rank, 512)
        row_token = _row_tables(dest, n_blocks * MOE_BLOCK)
        shared = _shared_expert(h2b, _cast_bf16(w_sh_gate[l], 256), _cast_bf16(w_sh_up[l], 256),
                                _cast_bf16(w_sh_down[l], 256), 512)
        y_rows = _routed_experts(h2p, row_token, blk_seq, dexp, meta, n_blocks,
                                 w_e_gate[l], w_e_up[l], w_e_down[l])
        x2 = _combine_final(x1, shared, mod4, dest, wk, y_rows, ln2_g[l], ln2_b[l], seq)
    return x2.reshape(bsz, seq, d)
```

```python
import dataclasses
import math

import jax
import jax.numpy as jnp
import numpy as np
from jax import lax
from jax.experimental import pallas as pl
from jax.experimental.pallas import tpu as pltpu
from jax.experimental.pallas import tpu_sc as plsc

HG_HEADS = 4
HG_DK = 128
HG_WIDTH = HG_HEADS * HG_DK
ATT_BRANCHES = ((128, 1), (512, 4), (2048, 16))
ATT_HEADS_PER_BRANCH = 4
ATT_HEAD_DIM = 64
ATT_BW = ATT_HEADS_PER_BRANCH * ATT_HEAD_DIM
ATT_BLOCK = 128
REL_BUCKETS = 32
REL_MAX_DIST = 2048
N_EXPERTS = 256
TOP_K = 8
N_GROUPS = 8
TOPK_GROUPS = 4
ROUTED_SCALE = 2.5
MOE_BLOCK = 128
DEPTH = 1
DN_ALPHA = (2 * DEPTH) ** 0.25
LN_EPS = 1e-5
RMS_EPS = 1e-6

LANES = 128
SUBLANES = 8
VMEM_LIMIT_BYTES = 56 * 1024 * 1024

F32 = jnp.float32
BF16 = jnp.bfloat16
NEG_INF = float("-inf")


def _cparams(sem):
    return pltpu.CompilerParams(dimension_semantics=sem, vmem_limit_bytes=VMEM_LIMIT_BYTES)


def _ln_rows(x):
    mu = jnp.mean(x, axis=-1, keepdims=True)
    xc = x - mu
    var = jnp.mean(xc * xc, axis=-1, keepdims=True)
    return xc * lax.rsqrt(var + LN_EPS)


def _dot(a, b):
    return jnp.dot(a, b, preferred_element_type=F32)


def _dot_nt(a, b):
    return lax.dot_general(a, b, (((1,), (1,)), ((), ())), preferred_element_type=F32)


def _dot_tn(a, b):
    return lax.dot_general(a, b, (((0,), (0,)), ((), ())), preferred_element_type=F32)


def _cast_kernel(w_ref, o_ref):
    o_ref[...] = w_ref[...].astype(o_ref.dtype)


def _cast_bf16(w, rows_per_step):
    r, c = w.shape
    return pl.pallas_call(
        _cast_kernel,
        grid=(r // rows_per_step,),
        in_specs=[pl.BlockSpec((rows_per_step, c), lambda i: (i, 0))],
        out_specs=pl.BlockSpec((rows_per_step, c), lambda i: (i, 0)),
        out_shape=jax.ShapeDtypeStruct((r, c), BF16),
        compiler_params=_cparams(("parallel",)),
        name="cast_bf16",
    )(w)


def _mod_kernel(c_ref, w_ref, b_ref, o_ref):
    c = c_ref[...]
    cond = c * jax.nn.sigmoid(c)
    o_ref[...] = jnp.dot(cond, w_ref[...], preferred_element_type=F32,
                         precision=lax.Precision.HIGHEST) + b_ref[...]


def _modulation(c, w_ada, b_ada):
    bsz, d = c.shape
    n = w_ada.shape[1]
    rows = -(-bsz // SUBLANES) * SUBLANES
    cpad = jnp.zeros((rows, d), F32).at[:bsz].set(c)
    tn = 1024
    out = pl.pallas_call(
        _mod_kernel,
        grid=(n // tn,),
        in_specs=[pl.BlockSpec((rows, d), lambda j: (0, 0)),
                  pl.BlockSpec((d, tn), lambda j: (0, j)),
                  pl.BlockSpec((1, tn), lambda j: (0, j))],
        out_specs=pl.BlockSpec((rows, tn), lambda j: (0, j)),
        out_shape=jax.ShapeDtypeStruct((rows, n), F32),
        compiler_params=_cparams(("parallel",)),
        name="adaln_modulation",
    )(cpad, w_ada, b_ada.reshape(1, n))
    return out[:bsz]


_IN_HG = 4
_IN_ATT = 3 * len(ATT_BRANCHES)


def _inproj_kernel(x_ref, sc_ref, sh_ref, w_ref, *refs):
    outs = refs[:_IN_HG + _IN_ATT]
    scratch = refs[_IN_HG + _IN_ATT:]
    x = x_ref[...]
    h = _ln_rows(x) * (1.0 + sc_ref[...]) + sh_ref[...]
    hb = h.astype(BF16)
    tm = x.shape[0]
    col = 0
    n_scr = 0
    for k, o_ref in enumerate(outs):
        width = HG_WIDTH if k < _IN_HG else ATT_BW
        y = _dot(hb, w_ref[:, col:col + width])
        col += width
        if k < _IN_HG:
            o_ref[...] = y.astype(o_ref.dtype)
            continue
        if k < _IN_HG + len(ATT_BRANCHES):
            y = y * (ATT_HEAD_DIM ** -0.5)
        dil = ATT_BRANCHES[(k - _IN_HG) % len(ATT_BRANCHES)][1]
        if dil == 1:
            o_ref[...] = y.astype(o_ref.dtype)
            continue
        scr = scratch[n_scr]
        n_scr += 1
        for half in range(ATT_BW // LANES):
            scr[half] = y[:, half * LANES:(half + 1) * LANES]
        for r in range(dil):
            for half in range(ATT_BW // LANES):
                c0 = r * ATT_BW + half * LANES
                o_ref[:, c0:c0 + LANES] = scr[half, pl.ds(r, tm // dil, stride=dil), :].astype(o_ref.dtype)


def _in_projection(x2, mod4, w_in_bf, seq, tm):
    t, d = x2.shape
    steps_per_batch = seq // tm
    dils = [dil for _, dil in ATT_BRANCHES] * 3
    shapes = [(t, HG_WIDTH)] * _IN_HG + [(t // dil, dil * ATT_BW) for dil in dils]
    blocks = [(tm, HG_WIDTH)] * _IN_HG + [(tm // dil, dil * ATT_BW) for dil in dils]
    dtypes = [BF16, F32, BF16, BF16] + [BF16] * _IN_ATT
    mod_spec = lambda row: pl.BlockSpec((None, None, 1, d),
                                        lambda i, row=row: (i // steps_per_batch, row, 0, 0))
    outs = pl.pallas_call(
        _inproj_kernel,
        grid=(t // tm,),
        in_specs=[pl.BlockSpec((tm, d), lambda i: (i, 0)),
                  mod_spec(1), mod_spec(0),
                  pl.BlockSpec(w_in_bf.shape, lambda i: (0, 0))],
        out_specs=[pl.BlockSpec(b, lambda i: (i, 0)) for b in blocks],
        out_shape=[jax.ShapeDtypeStruct(s, dt) for s, dt in zip(shapes, dtypes)],
        scratch_shapes=[pltpu.VMEM((ATT_BW // LANES, tm, LANES), F32) for dil in dils if dil > 1],
        compiler_params=_cparams(("parallel",)),
        name="ln_in_projection",
    )(x2, mod4, mod4, w_in_bf)
    return outs


HG_CHUNK = 64
HG_CHUNKS_PER_STEP = 16
HG_SUB = 8
HG_LEVELS = (64, 32, 16)
LOG2E = 1.4426950408889634


def _hgrn_chunk(q, z, iv, lb):
    c = HG_CHUNK
    f = lb + (1.0 - lb) * jax.nn.sigmoid(z)
    lf = jnp.log(f)
    kk = (1.0 - lb) * jax.nn.sigmoid(-z)
    r_i = lax.broadcasted_iota(jnp.int32, (c, c), 0)
    c_i = lax.broadcasted_iota(jnp.int32, (c, c), 1)
    tril = (c_i <= r_i).astype(F32)
    b = jnp.dot(tril, lf, preferred_element_type=F32, precision=lax.Precision.HIGHEST)
    bl = b * LOG2E

    row = lax.broadcasted_iota(jnp.int32, (c, HG_DK), 0)
    scores = jnp.zeros((c, c), F32)
    for m in HG_LEVELS:
        nb = c // m
        b3 = bl.reshape(nb, m, HG_DK)
        piv = jnp.broadcast_to(b3[:, m // 2 - 1:m // 2, :], (nb, m, HG_DK)).reshape(c, HG_DK)
        second = (row % m) >= (m // 2)
        qt = jnp.where(second, q * jnp.exp2(bl - piv), 0.0)
        kt = jnp.where(second, 0.0, kk * jnp.exp2(piv - bl))
        s_m = _dot_nt(qt.astype(BF16), kt.astype(BF16))
        if nb > 1:
            s_m = jnp.where((r_i // m) == (c_i // m), s_m, 0.0)
        scores = scores + s_m
    sub = HG_SUB
    t_i = lax.broadcasted_iota(jnp.int32, (sub, 1), 0)
    lane = lax.broadcasted_iota(jnp.int32, (sub, c), 1)
    diag_rows = []
    for j in range(c // sub):
        qb = q[j * sub:(j + 1) * sub]
        kb = kk[j * sub:(j + 1) * sub]
        bb = bl[j * sub:(j + 1) * sub]
        a_j = jnp.zeros((sub, c), F32)
        for s in range(sub):
            w = qb * kb[s:s + 1] * jnp.exp2(bb - bb[s:s + 1])
            a_j = jnp.where(lane == j * sub + s, jnp.sum(w, axis=-1, keepdims=True), a_j)
        diag_rows.append(jnp.where(lane - j * sub <= t_i, a_j, 0.0))
    scores = scores + jnp.concatenate(diag_rows, axis=0)

    ivb = iv.astype(BF16)
    intra = _dot(scores.astype(BF16), ivb)
    b_last = bl[c - 1:c]
    kdec = (kk * jnp.exp2(b_last - bl)).astype(BF16)
    return (q * jnp.exp2(bl)).astype(BF16), intra, jnp.exp2(b_last), _dot_tn(ivb, kdec)


def _hgrn_kernel(q_ref, f_ref, i_ref, g_ref, lbp_ref, nw_ref, o_ref, st_ref):
    @pl.when(pl.program_id(1) == 0)
    def _():
        st_ref[...] = jnp.zeros_like(st_ref)

    lbp = lbp_ref[...]
    e = jnp.exp(lbp - jnp.max(lbp, axis=0, keepdims=True))
    lb_all = e[0:1] / jnp.sum(e, axis=0, keepdims=True)
    heads = []
    for h in range(HG_HEADS):
        sl = slice(h * HG_DK, (h + 1) * HG_DK)
        st_t = st_ref[h]
        outs = []
        for n in range(HG_CHUNKS_PER_STEP):
            rows = slice(n * HG_CHUNK, (n + 1) * HG_CHUNK)
            qdec, intra, dec_last, kv = _hgrn_chunk(q_ref[rows, sl].astype(F32), f_ref[rows, sl],
                                                    i_ref[rows, sl].astype(F32), lb_all[:, sl])
            o = _dot_nt(qdec, st_t.astype(BF16)) + intra
            st_t = st_t * dec_last + kv
            outs.append(o * lax.rsqrt(jnp.mean(o * o, axis=-1, keepdims=True) + RMS_EPS))
        st_ref[h] = st_t
        heads.append(jnp.concatenate(outs, axis=0))
    o_all = jnp.concatenate(heads, axis=-1)
    g = g_ref[...].astype(F32)
    o_ref[...] = (o_all * nw_ref[...] * (g * jax.nn.sigmoid(g))).astype(o_ref.dtype)


def _hgrn2(hq, hf, hi, hg, lb_param, norm_w, bsz, seq):
    t = hq.shape[0]
    rows = HG_CHUNK * HG_CHUNKS_PER_STEP
    nc = seq // rows
    tok = lambda b, n: (b * nc + n, 0)
    spec = pl.BlockSpec((rows, HG_WIDTH), tok)
    return pl.pallas_call(
        _hgrn_kernel,
        grid=(bsz, nc),
        in_specs=[spec, spec, spec, spec,
                  pl.BlockSpec(lb_param.shape, lambda b, n: (0, 0)),
                  pl.BlockSpec((1, HG_WIDTH), lambda b, n: (0, 0))],
        out_specs=spec,
        out_shape=jax.ShapeDtypeStruct((t, HG_WIDTH), BF16),
        scratch_shapes=[pltpu.VMEM((HG_HEADS, HG_DK, HG_DK), F32)],
        compiler_params=_cparams(("parallel", "arbitrary")),
        name="hgrn2_scan",
    )(hq, hf, hi, hg, lb_param, norm_w.reshape(1, HG_WIDTH))


def _t5_bucket_np(dist):
    max_exact = REL_BUCKETS // 2
    n = np.maximum(dist, 0)
    nf = np.maximum(n, 1).astype(np.float32)
    large = max_exact + (np.log(nf / np.float32(max_exact)) / np.float32(math.log(REL_MAX_DIST / max_exact))
                         * np.float32(REL_BUCKETS - max_exact)).astype(np.int32)
    large = np.minimum(large, REL_BUCKETS - 1)
    return np.where(n < max_exact, n, large).astype(np.int32)


def _band_tables():
    w = ATT_BLOCK
    qi = np.arange(w)[:, None]
    ki = np.arange(2 * w)[None, :]
    m = w + qi - ki
    band = (m >= 0) & (m <= w)
    buckets = np.stack([_t5_bucket_np(m * dil) for _, dil in ATT_BRANCHES])
    return buckets, band


def _bias_kernel(rb_ref, bucket_ref, o_ref):
    g = pl.program_id(0)
    w = ATT_BLOCK
    bucket = bucket_ref[...]
    qi = lax.broadcasted_iota(jnp.int32, (w, 2 * w), 0)
    ki = lax.broadcasted_iota(jnp.int32, (w, 2 * w), 1)
    m = w + qi - ki
    band = (m >= 0) & (m <= w)
    for h in range(ATT_HEADS_PER_BRANCH):
        acc = jnp.zeros((w, 2 * w), F32)
        for c in range(REL_BUCKETS):
            acc = jnp.where(bucket == c, rb_ref[c, g * ATT_HEADS_PER_BRANCH + h], acc)
        full = jnp.where(band, acc, NEG_INF)
        o_ref[1, h] = full
        o_ref[0, h] = jnp.where(ki >= w, full, NEG_INF)


def _bias_tables(rel_bias):
    buckets, _ = _band_tables()
    g = len(ATT_BRANCHES)
    w = ATT_BLOCK
    return pl.pallas_call(
        _bias_kernel,
        grid=(g,),
        in_specs=[pl.BlockSpec(memory_space=pltpu.SMEM),
                  pl.BlockSpec((None, w, 2 * w), lambda i: (i, 0, 0))],
        out_specs=pl.BlockSpec((None, 2, ATT_HEADS_PER_BRANCH, w, 2 * w), lambda i: (i, 0, 0, 0, 0)),
        out_shape=jax.ShapeDtypeStruct((g, 2, ATT_HEADS_PER_BRANCH, w, 2 * w), F32),
        compiler_params=_cparams(("parallel",)),
        name="rel_bias_tables",
    )(rel_bias, jnp.asarray(buckets))


ATT_BLOCKS_PER_STEP = 16


def _attn_kernel(q_ref, kp_ref, kc_ref, vp_ref, vc_ref, bias_ref, o_ref, lse_ref):
    m = pl.program_id(2)
    w = ATT_BLOCK
    hb = ATT_HEADS_PER_BRANCH
    lane = lax.broadcasted_iota(jnp.int32, (w, ATT_BW), 1) // ATT_HEAD_DIM
    for res, blk in [(r, b) for r in range(q_ref.shape[1] // ATT_BW) for b in range(q_ref.shape[0] // w)]:
        cols = slice(res * ATT_BW, (res + 1) * ATT_BW)
        kall = jnp.concatenate([kp_ref[:, cols], kc_ref[:, cols]], axis=0)
        vall = jnp.concatenate([vp_ref[:, cols], vc_ref[:, cols]], axis=0)
        q = q_ref[blk * w:(blk + 1) * w, cols]
        q4 = jnp.concatenate([jnp.where(lane == h, q, jnp.zeros_like(q)) for h in range(hb)], axis=0)
        kk = kall[blk * w:(blk + 2) * w]
        vv = vall[blk * w:(blk + 2) * w]
        s4 = _dot_nt(q4, kk)
        bias = bias_ref[jnp.minimum(m, 1)] if blk == 0 else bias_ref[1]
        s4 = s4 + bias.reshape(hb * w, 2 * w)
        mx = jnp.max(s4, axis=-1, keepdims=True)
        p = jnp.exp(s4 - mx)
        l = jnp.sum(p, axis=-1, keepdims=True)
        o4 = _dot((p / l).astype(vv.dtype), vv)
        lse4 = mx + jnp.log(l)
        o = jnp.zeros((w, ATT_BW), F32)
        lse = jnp.zeros((w, ATT_BW), F32)
        for h in range(hb):
            o = jnp.where(lane == h, o4[h * w:(h + 1) * w], o)
            lse = jnp.where(lane == h, lse4[h * w:(h + 1) * w], lse)
        o_ref[blk * w:(blk + 1) * w, cols] = o.astype(o_ref.dtype)
        lse_ref[blk * w:(blk + 1) * w, cols] = lse


def _dilated_attention(q, k, v, bias_g, bsz, seq, dilation):
    w = ATT_BLOCK
    l = seq // dilation
    pstep = math.gcd(ATT_BLOCKS_PER_STEP, l // w)
    nb = l // (w * pstep)
    rstep = math.gcd(ATT_BLOCKS_PER_STEP // pstep, dilation)
    view = lambda a: a.reshape(bsz, l, dilation * ATT_BW)
    cur = pl.BlockSpec((None, pstep * w, rstep * ATT_BW), lambda b, r, n: (b, n, r))
    prev = pl.BlockSpec((None, w, rstep * ATT_BW), lambda b, r, n: (b, jnp.maximum(pstep * n - 1, 0), r))
    o, lse = pl.pallas_call(
        _attn_kernel,
        grid=(bsz, dilation // rstep, nb),
        in_specs=[cur, prev, cur, prev, cur,
                  pl.BlockSpec(bias_g.shape, lambda b, r, n: (0, 0, 0, 0))],
        out_specs=[cur, cur],
        out_shape=[jax.ShapeDtypeStruct((bsz, l, dilation * ATT_BW), BF16),
                   jax.ShapeDtypeStruct((bsz, l, dilation * ATT_BW), F32)],
        compiler_params=_cparams(("parallel", "parallel", "arbitrary")),
        name=f"dilated_attention_d{dilation}",
    )(view(q), view(k), view(k), view(v), view(v), bias_g)
    return o.reshape(bsz * l, dilation * ATT_BW), lse.reshape(bsz * l, dilation * ATT_BW)


def _split_bf16(a):
    hi = a.astype(BF16)
    lo = (a - hi.astype(F32)).astype(BF16)
    return hi, lo


H2P_CHUNKS = 4


def _token_order(ref, scr, dil):
    if dil == 1:
        return ref[...].astype(F32)
    n = ref.shape[0]
    halves = ATT_BW // LANES
    for r in range(dil):
        for half in range(halves):
            c0 = r * ATT_BW + half * LANES
            scr[half, pl.ds(r, n, stride=dil), :] = ref[:, c0:c0 + LANES].astype(F32)
    return jnp.concatenate([scr[half] for half in range(halves)], axis=1)


def _outproj_kernel(yhg_ref, o1_ref, o2_ref, o3_ref, l1_ref, l2_ref, l3_ref, x_ref,
                    g1_ref, sc2_ref, sh2_ref, wout_ref, lng_ref, lnb_ref, wrt_ref,
                    x1_ref, h2_ref, h2p_ref, lgt_ref, *scratch):
    dils = [dil for _, dil in ATT_BRANCHES]
    scr = iter(scratch)
    o1, o2, o3 = [_token_order(r, None if dil == 1 else next(scr), dil)
                  for r, dil in zip((o1_ref, o2_ref, o3_ref), dils)]
    l1, l2, l3 = [_token_order(r, None if dil == 1 else next(scr), dil)
                  for r, dil in zip((l1_ref, l2_ref, l3_ref), dils)]
    mx = jnp.maximum(jnp.maximum(l1, l2), l3)
    e1, e2, e3 = jnp.exp(l1 - mx), jnp.exp(l2 - mx), jnp.exp(l3 - mx)
    den = e1 + e2 + e3
    att = (e1 / den) * o1 + (e2 / den) * o2 + (e3 / den) * o3
    mix = _dot(yhg_ref[...], wout_ref[:HG_WIDTH, :]) + _dot(att.astype(BF16), wout_ref[HG_WIDTH:, :])
    x1 = _ln_rows(DN_ALPHA * x_ref[...] + g1_ref[...] * mix) * lng_ref[...] + lnb_ref[...]
    x1_ref[...] = x1
    h2 = _ln_rows(x1) * (1.0 + sc2_ref[...]) + sh2_ref[...]
    h_hi = h2.astype(BF16)
    h_hf = h_hi.astype(F32)
    h2_ref[...] = h_hi
    bits = lax.bitcast_convert_type(h_hf, jnp.uint32)
    for cidx in range(H2P_CHUNKS):
        lo = bits[:, 2 * LANES * cidx:2 * LANES * cidx + LANES]
        hi = bits[:, 2 * LANES * cidx + LANES:2 * LANES * (cidx + 1)]
        h2p_ref[pl.ds(cidx, h2.shape[0], stride=H2P_CHUNKS), :] = lax.bitcast_convert_type((lo >> 16) | hi,
                                                                                          jnp.int32)
    h_lo = (h2 - h_hf).astype(BF16)
    w_hi, w_lo = _split_bf16(wrt_ref[...])
    lgt_ref[...] = _dot_nt(w_hi, h_hi) + (_dot_nt(w_hi, h_lo) + _dot_nt(w_lo, h_hi))


def _out_projection(yhg, os_, ls_, x2, mod4, w_out_bf, ln_g, ln_b, w_router_t, seq, tm):
    t, d = x2.shape
    spb = seq // tm
    ne = w_router_t.shape[0]
    row = lambda w: pl.BlockSpec((tm, w), lambda i: (i, 0))
    mod_spec = lambda r: pl.BlockSpec((None, None, 1, d), lambda i, r=r: (i // spb, r, 0, 0))
    full = lambda a: pl.BlockSpec(a.shape, lambda i: (0,) * a.ndim)
    ln_g2, ln_b2 = ln_g.reshape(1, d), ln_b.reshape(1, d)
    dils = [dil for _, dil in ATT_BRANCHES]
    branch = [pl.BlockSpec((tm // dil, dil * ATT_BW), lambda i: (i, 0)) for dil in dils]
    return pl.pallas_call(
        _outproj_kernel,
        grid=(t // tm,),
        in_specs=[row(HG_WIDTH)] + branch + branch + [row(d),
                  mod_spec(2), mod_spec(4), mod_spec(3),
                  full(w_out_bf), full(ln_g2), full(ln_b2), full(w_router_t)],
        out_specs=[row(d), row(d), pl.BlockSpec((tm * H2P_CHUNKS, LANES), lambda i: (i, 0)),
                   pl.BlockSpec((ne, tm), lambda i: (0, i))],
        out_shape=[jax.ShapeDtypeStruct((t, d), F32), jax.ShapeDtypeStruct((t, d), BF16),
                   jax.ShapeDtypeStruct((t * H2P_CHUNKS, LANES), jnp.int32),
                   jax.ShapeDtypeStruct((ne, t), F32)],
        scratch_shapes=[pltpu.VMEM((ATT_BW // LANES, tm, LANES), F32) for dil in dils + dils if dil > 1],
        compiler_params=_cparams(("parallel",)),
        name="merge_outproj_ln",
    )(yhg, *os_, *ls_, x2, mod4, mod4, mod4, w_out_bf, ln_g2, ln_b2, w_router_t)


def _argmax_rows(cur, iota, nrows):
    m = jnp.max(cur, axis=0, keepdims=True)
    idx = jnp.min(jnp.where(cur == m, iota, nrows), axis=0, keepdims=True)
    return m, idx, iota == idx


def _route_kernel(lgt_ref, rb_ref, eidx_ref, w_ref, rank_ref, cnt_ref, carry):
    ne = N_EXPERTS
    gsz = ne // N_GROUPS
    tt = lgt_ref.shape[1]

    @pl.when(pl.program_id(0) == 0)
    def _():
        carry[...] = jnp.zeros_like(carry)

    sc = jax.nn.sigmoid(lgt_ref[...])
    biased = sc + rb_ref[...]
    g3 = biased.reshape(N_GROUPS, gsz, tt)
    io3 = lax.broadcasted_iota(jnp.int32, (N_GROUPS, gsz, tt), 1)
    m1 = jnp.max(g3, axis=1, keepdims=True)
    first = jnp.min(jnp.where(g3 == m1, io3, gsz), axis=1, keepdims=True)
    m2 = jnp.max(jnp.where(io3 == first, NEG_INF, g3), axis=1, keepdims=True)
    gs = (m1 + m2).reshape(N_GROUPS, tt)
    io8 = lax.broadcasted_iota(jnp.int32, (N_GROUPS, tt), 0)
    sel = jnp.zeros((N_GROUPS, tt), jnp.int32)
    cur = gs
    for _ in range(TOPK_GROUPS):
        _, _, pick = _argmax_rows(cur, io8, N_GROUPS)
        sel = jnp.where(pick, 1, sel)
        cur = jnp.where(pick, NEG_INF, cur)
    masked = jnp.where(sel.reshape(N_GROUPS, 1, tt) > 0, g3, NEG_INF).reshape(ne, tt)
    ioe = lax.broadcasted_iota(jnp.int32, (ne, tt), 0)
    cur = masked
    idxs, ws, picks = [], [], []
    for _ in range(TOP_K):
        _, idx, pick = _argmax_rows(cur, ioe, ne)
        idxs.append(idx)
        picks.append(pick)
        ws.append(jnp.sum(jnp.where(pick, sc, 0.0), axis=0, keepdims=True))
        cur = jnp.where(pick, NEG_INF, cur)
    wk = jnp.concatenate(ws, axis=0)
    eidx_ref[...] = jnp.concatenate(idxs, axis=0)
    w_ref[...] = wk / jnp.sum(wk, axis=0, keepdims=True) * ROUTED_SCALE
    chosen = jnp.where(cur == NEG_INF, jnp.where(masked == NEG_INF, 0.0, 1.0), 0.0)
    r_i = lax.broadcasted_iota(jnp.int32, (tt, tt), 0)
    c_i = lax.broadcasted_iota(jnp.int32, (tt, tt), 1)
    before = jnp.where(r_i < c_i, 1.0, 0.0).astype(BF16)
    pref = _dot(chosen.astype(BF16), before) + carry[...]
    rank_ref[...] = jnp.concatenate(
        [jnp.sum(jnp.where(p, pref, 0.0), axis=0, keepdims=True) for p in picks], axis=0).astype(jnp.int32)
    carry[...] = carry[...] + jnp.sum(chosen, axis=1, keepdims=True)
    cnt_ref[...] = carry[...]


def _route(logits_t, router_bias, tt):
    ne, t = logits_t.shape
    tok = pl.BlockSpec((TOP_K, tt), lambda i: (0, i))
    return pl.pallas_call(
        _route_kernel,
        grid=(t // tt,),
        in_specs=[pl.BlockSpec((ne, tt), lambda i: (0, i)),
                  pl.BlockSpec((ne, 1), lambda i: (0, 0))],
        out_specs=[tok, tok, tok, pl.BlockSpec((ne, 1), lambda i: (0, 0))],
        out_shape=[jax.ShapeDtypeStruct((TOP_K, t), jnp.int32), jax.ShapeDtypeStruct((TOP_K, t), F32),
                   jax.ShapeDtypeStruct((TOP_K, t), jnp.int32), jax.ShapeDtypeStruct((ne, 1), F32)],
        scratch_shapes=[pltpu.VMEM((ne, 1), F32)],
        compiler_params=_cparams(("arbitrary",)),
        name="router_topk",
    )(logits_t, router_bias.reshape(ne, 1))


def _plan_kernel(cnt_ref, eidx_ref, rank_ref, dest_ref, seq_ref, dexp_ref, meta_ref, pstart_ref):
    ne = N_EXPERTS
    tt = eidx_ref.shape[1]
    nblk = seq_ref.shape[1]

    @pl.when(pl.program_id(0) == 0)
    def _():
        cnt = cnt_ref[...].astype(jnp.int32)
        padded = ((cnt + (MOE_BLOCK - 1)) // MOE_BLOCK) * MOE_BLOCK
        r_i = lax.broadcasted_iota(jnp.int32, (ne, ne), 0)
        c_i = lax.broadcasted_iota(jnp.int32, (ne, ne), 1)
        incl = jnp.where(c_i <= r_i, 1.0, 0.0)
        pend = jnp.dot(incl, jnp.broadcast_to(padded.astype(F32), (ne, LANES)),
                       preferred_element_type=F32, precision=lax.Precision.HIGHEST)[:, 0:1]
        pend = pend.astype(jnp.int32)
        pstart_ref[...] = pend - padded
        blk0 = lax.broadcasted_iota(jnp.int32, (ne, nblk), 1) * MOE_BLOCK
        be = jnp.minimum(jnp.sum(jnp.where(pend <= blk0, 1, 0), axis=0, keepdims=True), ne - 1)
        present = cnt > 0
        strict = jnp.where(c_i < r_i, 1.0, 0.0).astype(BF16)
        sidx = _dot(strict, jnp.broadcast_to(jnp.where(present, 1.0, 0.0), (ne, LANES)).astype(BF16))[:, 0:1]
        sidx = sidx.astype(jnp.int32)
        dexp_ref[...] = jnp.sum(jnp.where(jnp.logical_and(present, sidx == c_i), r_i, 0), axis=0, keepdims=True)
        ioeb = lax.broadcasted_iota(jnp.int32, (ne, nblk), 0)
        nu = jnp.max(pend, axis=0, keepdims=True) // MOE_BLOCK
        nd = jnp.sum(jnp.where(present, 1, 0), axis=0, keepdims=True)
        seq_ref[...] = jnp.minimum(jnp.sum(jnp.where(ioeb == be, sidx, 0), axis=0, keepdims=True), nd - 1)
        lane = lax.broadcasted_iota(jnp.int32, (1, LANES), 1)
        meta_ref[...] = jnp.where(lane == 0, nu, jnp.where(lane == 1, nd, 0))

    pstart = pstart_ref[...]
    ioe = lax.broadcasted_iota(jnp.int32, (ne, tt), 0)
    rows = []
    for k in range(TOP_K):
        sel = ioe == eidx_ref[k:k + 1, :]
        rows.append(jnp.sum(jnp.where(sel, pstart, 0), axis=0, keepdims=True))
    dest_ref[...] = jnp.concatenate(rows, axis=0) + rank_ref[...]


def _dispatch_plan(cnt, eidx, rank, tt):
    k, t = eidx.shape
    n_blocks = -(-(t * k) // MOE_BLOCK) + N_EXPERTS
    tok = pl.BlockSpec((k, tt), lambda i: (0, i))
    one = lambda n: pl.BlockSpec((1, n), lambda i: (0, 0))
    dest, seq, dexp, meta = pl.pallas_call(
        _plan_kernel,
        grid=(t // tt,),
        in_specs=[pl.BlockSpec(cnt.shape, lambda i: (0, 0)), tok, tok],
        out_specs=[tok, one(n_blocks), one(N_EXPERTS), one(LANES)],
        out_shape=[jax.ShapeDtypeStruct((k, t), jnp.int32), jax.ShapeDtypeStruct((1, n_blocks), jnp.int32),
                   jax.ShapeDtypeStruct((1, N_EXPERTS), jnp.int32), jax.ShapeDtypeStruct((1, LANES), jnp.int32)],
        scratch_shapes=[pltpu.VMEM((N_EXPERTS, 1), jnp.int32)],
        compiler_params=_cparams(("arbitrary",)),
        name="dispatch_plan",
    )(cnt, eidx, rank)
    return dest, seq.reshape(n_blocks), dexp.reshape(N_EXPERTS), meta.reshape(LANES), n_blocks


SC_CORES = 2
SC_SUBCORES = 16
SC_LANES = 16
SC_CHUNK = 16384
SC_UNROLL = 4


def _row_tables(dest, n_rows):
    k, t = dest.shape
    a = k * t
    nw = SC_CORES * SC_SUBCORES
    per_w = n_rows // nw
    assert n_rows % (nw * SC_LANES) == 0 and t % SC_CHUNK == 0
    mesh = plsc.VectorSubcoreMesh(core_axis_name="c", subcore_axis_name="s")
    cp = pltpu.CompilerParams()
    if "needs_layout_passes" in pltpu.CompilerParams.__dataclass_fields__:
        cp = dataclasses.replace(cp, needs_layout_passes=False)

    def body(dest_hbm, tok_out, dbuf, tloc):
        wid = lax.axis_index("s") * SC_CORES + lax.axis_index("c")
        base = wid * per_w

        @pl.loop(0, per_w // SC_LANES)
        def _(i):
            tloc[pl.ds(i * SC_LANES, SC_LANES)] = jnp.zeros((SC_LANES,), jnp.int32)

        lane = lax.iota(jnp.int32, SC_LANES)

        @pl.loop(0, a // SC_CHUNK)
        def _(c):
            pltpu.sync_copy(dest_hbm.at[pl.ds(c * SC_CHUNK, SC_CHUNK)], dbuf)
            tok0 = lax.rem(c * SC_CHUNK, t)

            @pl.loop(0, SC_CHUNK // (SC_LANES * SC_UNROLL))
            def _(j):
                for u in range(SC_UNROLL):
                    off = (j * SC_UNROLL + u) * SC_LANES
                    loc = dbuf[pl.ds(off, SC_LANES)] - base
                    mine = jnp.logical_and(loc >= 0, loc < per_w)
                    loc = jnp.where(mine, loc, 0)
                    plsc.store_scatter(tloc, [loc], tok0 + off + lane, mask=mine)

        pltpu.sync_copy(tloc, tok_out.at[pl.ds(base, per_w)])

    fn = pl.kernel(
        body,
        out_type=jax.ShapeDtypeStruct((n_rows,), jnp.int32),
        mesh=mesh,
        scratch_types=[pltpu.VMEM((SC_CHUNK,), jnp.int32), pltpu.VMEM((per_w,), jnp.int32)],
        compiler_params=cp,
        name="row_tables",
    )
    return fn(dest.reshape(a))


ROW_TILE = (SUBLANES, LANES)


FFN_GROUP = 4
PAIR_ROWS = 2 * MOE_BLOCK
FFN_CHAIN = 128
W_SETS = 3
GATHER_BATCH = 32


def _gather_rows(h2p_ref, tok_ref, row, buf, base):
    for j0 in range(0, MOE_BLOCK, GATHER_BATCH):
        vals = [h2p_ref[pl.ds(pl.multiple_of(tok_ref[row, j] * H2P_CHUNKS, H2P_CHUNKS), H2P_CHUNKS), :]
                for j in range(j0, j0 + GATHER_BATCH)]
        for j, v in zip(range(j0, j0 + GATHER_BATCH), vals):
            buf[pl.ds(H2P_CHUNKS * (base + j), H2P_CHUNKS), :] = v


def _expert_rows(buf, chains, out):
    hms = []
    for row0, nrows, (wg_c, wu_c, _), _ in chains:
        parts = []
        for cidx in range(H2P_CHUNKS):
            word = buf[pl.ds(H2P_CHUNKS * row0 + cidx, nrows, stride=H2P_CHUNKS), :]
            parts.append(lax.bitcast_convert_type(word << 16, F32))
            parts.append(lax.bitcast_convert_type(word & jnp.int32(-65536), F32))
        x = jnp.concatenate(parts, axis=1)
        g = _dot(x, wg_c[...])
        u = _dot(x, wu_c[...])
        hms.append((g * jax.nn.sigmoid(g)) * u)
    for (_, nrows, (_, _, wd_c), out_row0), hm in zip(chains, hms):
        res = _dot(hm, wd_c[...])
        nch = res.shape[1] // LANES
        for c in range(nch):
            out[pl.ds(nch * out_row0 + c, nrows, stride=nch), :] = res[:, c * LANES:(c + 1) * LANES]


def _ffn_kernel(seq_ref, dexp_ref, meta_ref, tokc_ref, tokn_ref, h2p_ref, wg_hbm, wu_hbm, wd_hbm,
                y_ref, buf_0, buf_1, *rest):
    i = pl.program_id(0)
    n_used = meta_ref[0]
    n_exp = meta_ref[1]
    nblk = seq_ref.shape[0]
    wsets = tuple(tuple(rest[3 * n:3 * n + 3]) for n in range(W_SETS))
    started_ref, sems = rest[3 * W_SETS:]
    bufs = (buf_0, buf_1)

    def weight_copies(s, par):
        e = dexp_ref[s]
        return [pltpu.make_async_copy(src.at[e], dst, sems.at[par, n])
                for n, (src, dst) in enumerate(zip((wg_hbm, wu_hbm, wd_hbm), wsets[par]))]

    def start_expert(s):
        for par in range(W_SETS):
            @pl.when(s % W_SETS == par)
            def _():
                for cp in weight_copies(s, par):
                    cp.start()

    @pl.when(i == 0)
    def _():
        _gather_rows(h2p_ref, tokc_ref, 0, buf_0, 0)
        _gather_rows(h2p_ref, tokc_ref, 1, buf_0, MOE_BLOCK)
        start_expert(0)
        started_ref[0] = 0

    for p in range(FFN_GROUP // 2):
        b_a = i * FFN_GROUP + 2 * p
        used = b_a < n_used
        s_a = seq_ref[jnp.minimum(b_a, nblk - 1)]
        s_b = seq_ref[jnp.minimum(b_a + 1, nblk - 1)]
        first_a = jnp.logical_or(b_a == 0, s_a != seq_ref[jnp.clip(b_a - 1, 0, nblk - 1)])
        same = s_a == s_b
        cur, nxt = bufs[p], bufs[1 - p]
        out_row0 = p * PAIR_ROWS

        def gather_next():
            if p == 0:
                _gather_rows(h2p_ref, tokc_ref, 2, nxt, 0)
                _gather_rows(h2p_ref, tokc_ref, 3, nxt, MOE_BLOCK)
            else:
                _gather_rows(h2p_ref, tokn_ref, 0, nxt, 0)
                _gather_rows(h2p_ref, tokn_ref, 1, nxt, MOE_BLOCK)

        @pl.when(used)
        def _():
            started = started_ref[0]
            limit = jnp.minimum(s_a + (W_SETS - 1), n_exp - 1)
            for _unused in range(W_SETS - 1):
                go = started < limit

                @pl.when(go)
                def _():
                    start_expert(started + 1)
                started = jnp.where(go, started + 1, started)
            started_ref[0] = started

        for par in range(W_SETS):
            @pl.when(jnp.logical_and(used, jnp.logical_and(same, s_a % W_SETS == par)))
            def _():
                @pl.when(first_a)
                def _():
                    for cp in weight_copies(s_a, par):
                        cp.wait()
                gather_next()
                _expert_rows(cur, [(r0, FFN_CHAIN, wsets[par], out_row0 + r0)
                                   for r0 in range(0, PAIR_ROWS, FFN_CHAIN)], y_ref)

            @pl.when(jnp.logical_and(used, jnp.logical_and(jnp.logical_not(same), s_a % W_SETS == par)))
            def _():
                @pl.when(first_a)
                def _():
                    for cp in weight_copies(s_a, par):
                        cp.wait()
                for cp in weight_copies(s_b, (par + 1) % W_SETS):
                    cp.wait()
                gather_next()
                _expert_rows(cur, [(r0, FFN_CHAIN, wsets[(par + r0 // MOE_BLOCK) % W_SETS], out_row0 + r0)
                                   for r0 in range(0, PAIR_ROWS, FFN_CHAIN)], y_ref)

        @pl.when(jnp.logical_not(used))
        def _():
            y_ref[pl.ds(out_row0 * SUBLANES, PAIR_ROWS * SUBLANES), :] = jnp.zeros(
                (PAIR_ROWS * SUBLANES, LANES), y_ref.dtype)


def _routed_experts(h2p, row_token, seq, dexp, meta, n_blocks, wg, wu, wd):
    d = wg.shape[1]
    de = wg.shape[2]
    ng = n_blocks // FFN_GROUP
    assert n_blocks % FFN_GROUP == 0 and FFN_GROUP == 4 and d == SUBLANES * LANES
    tok3 = row_token.reshape(ng, FFN_GROUP, MOE_BLOCK)
    idle_step = lambda m: jnp.minimum((m[0] + FFN_GROUP - 1) // FFN_GROUP, ng - 1)
    smem = lambda imap: pl.BlockSpec((None, FFN_GROUP, MOE_BLOCK), imap, memory_space=pltpu.SMEM)
    grid_spec = pltpu.PrefetchScalarGridSpec(
        num_scalar_prefetch=3,
        grid=(ng,),
        in_specs=[
            smem(lambda i, sq, dx, m: (jnp.minimum(i, ng - 1), 0, 0)),
            smem(lambda i, sq, dx, m: (jnp.minimum(i + 1, ng - 1), 0, 0)),
            pl.BlockSpec(h2p.shape, lambda i, sq, dx, m: (0, 0), pipeline_mode=pl.Buffered(1)),
            pl.BlockSpec(memory_space=pl.ANY),
            pl.BlockSpec(memory_space=pl.ANY),
            pl.BlockSpec(memory_space=pl.ANY),
        ],
        out_specs=pl.BlockSpec((FFN_GROUP * MOE_BLOCK * SUBLANES, LANES),
                               lambda i, sq, dx, m: (jnp.minimum(i, idle_step(m)), 0)),
        scratch_shapes=[pltpu.VMEM((PAIR_ROWS * H2P_CHUNKS, LANES), jnp.int32)] * 2 + [
            pltpu.VMEM((d, de), F32), pltpu.VMEM((d, de), F32), pltpu.VMEM((de, d), F32)] * W_SETS + [
            pltpu.SMEM((1,), jnp.int32), pltpu.SemaphoreType.DMA((W_SETS, 3))],
    )
    y = pl.pallas_call(
        _ffn_kernel,
        grid_spec=grid_spec,
        out_shape=jax.ShapeDtypeStruct((n_blocks * MOE_BLOCK * SUBLANES, LANES), F32),
        compiler_params=_cparams(("arbitrary",)),
        name="routed_experts",
    )(seq, dexp, meta, tok3, tok3, h2p, wg, wu, wd)
    return y.reshape((n_blocks * MOE_BLOCK,) + ROW_TILE)


COMBINE_TOKENS = 128


def _row_copies(src_hbm, idx_ref, buf, sem):
    return [pltpu.make_async_copy(src_hbm.at[idx_ref[k, j]], buf.at[k, j], sem)
            for k in range(TOP_K) for j in range(COMBINE_TOKENS)]


def _shared_kernel(h_ref, wsg_ref, wsu_ref, wsd_ref, o_ref):
    hb = h_ref[...]
    g = _dot(hb, wsg_ref[...])
    u = _dot(hb, wsu_ref[...])
    o_ref[...] = _dot(((g * jax.nn.sigmoid(g)) * u).astype(BF16), wsd_ref[...])


def _shared_expert(h2b, wsg, wsu, wsd, tm):
    t, d = h2b.shape
    row = pl.BlockSpec((tm, d), lambda i: (i, 0))
    full = lambda a: pl.BlockSpec(a.shape, lambda i: (0,) * a.ndim)
    return pl.pallas_call(
        _shared_kernel,
        grid=(t // tm,),
        in_specs=[row, full(wsg), full(wsu), full(wsd)],
        out_specs=row,
        out_shape=jax.ShapeDtypeStruct((t, d), F32),
        compiler_params=_cparams(("parallel",)),
        name="shared_expert",
    )(h2b, wsg, wsu, wsd)


def _final_kernel(dc_ref, dn_ref, wk_ref, x1_ref, sh_ref, g2_ref, lng_ref, lnb_ref,
                  y_hbm, o_ref, ybuf, sems):
    i = pl.program_id(0)
    nsteps = pl.num_programs(0)
    slot = i % 2

    def issue(d_ref, s):
        for n, cp in enumerate(_row_copies(y_hbm, d_ref, ybuf.at[s], sems.at[s])):
            cp.start(priority=n % 2)

    @pl.when(i == 0)
    def _():
        issue(dc_ref, 0)

    @pl.when(i + 1 < nsteps)
    def _():
        issue(dn_ref, 1 - slot)

    for cp in _row_copies(y_hbm, dc_ref, ybuf.at[slot], sems.at[slot]):
        cp.wait()
    rows = []
    for j in range(COMBINE_TOKENS):
        acc = ybuf[slot, 0, j].astype(F32) * wk_ref[0, j]
        for k in range(1, TOP_K):
            acc = acc + ybuf[slot, k, j].astype(F32) * wk_ref[k, j]
        rows.append(acc)
    routed = pltpu.einshape("tcl->t(cl)", jnp.stack(rows, axis=0))
    x2 = DN_ALPHA * x1_ref[...] + g2_ref[...] * (routed + sh_ref[...])
    o_ref[...] = _ln_rows(x2) * lng_ref[...] + lnb_ref[...]


def _combine_final(x1, shared, mod4, dest, wk, y_rows, ln_g, ln_b, seq):
    t, d = x1.shape
    tt = COMBINE_TOKENS
    nsteps = t // tt
    spb = seq // tt
    row = pl.BlockSpec((tt, d), lambda i: (i, 0))
    full = lambda a: pl.BlockSpec(a.shape, lambda i: (0,) * a.ndim)
    ln_g2, ln_b2 = ln_g.reshape(1, d), ln_b.reshape(1, d)
    return pl.pallas_call(
        _final_kernel,
        grid=(nsteps,),
        in_specs=[pl.BlockSpec((TOP_K, tt), lambda i: (0, i), memory_space=pltpu.SMEM),
                  pl.BlockSpec((TOP_K, tt), lambda i: (0, jnp.minimum(i + 1, nsteps - 1)),
                               memory_space=pltpu.SMEM),
                  pl.BlockSpec((TOP_K, tt), lambda i: (0, i), memory_space=pltpu.SMEM),
                  row, row,
                  pl.BlockSpec((None, None, 1, d), lambda i: (i // spb, 5, 0, 0)),
                  full(ln_g2), full(ln_b2),
                  pl.BlockSpec(memory_space=pl.ANY)],
        out_specs=row,
        out_shape=jax.ShapeDtypeStruct((t, d), F32),
        scratch_shapes=[pltpu.VMEM((2, TOP_K, tt) + ROW_TILE, y_rows.dtype),
                        pltpu.SemaphoreType.DMA((2,))],
        compiler_params=_cparams(("arbitrary",)),
        name="shared_combine_ln",
    )(dest, dest, wk, x1, shared, mod4, ln_g2, ln_b2, y_rows)


def kernel(x, c, w_ada, b_ada, w_in, hg_lower_bound, hg_norm_w, rel_bias, w_out, ln1_g, ln1_b, w_router,
           router_bias, w_e_gate, w_e_up, w_e_down, w_sh_gate, w_sh_up, w_sh_down, ln2_g, ln2_b):
    bsz, seq, d = x.shape
    t = bsz * seq
    assert w_ada.shape[0] == DEPTH and seq % (ATT_BRANCHES[-1][0]) == 0
    x2 = x.reshape(t, d)
    bias = _bias_tables(rel_bias)
    for l in range(DEPTH):
        mod4 = _modulation(c, w_ada[l], b_ada[l]).reshape(bsz, 6, 1, d)
        w_in_bf = _cast_bf16(w_in[l], 256)
        hq, hf, hi, hg, *qkv = _in_projection(x2, mod4, w_in_bf, seq, 512)
        nbr = len(ATT_BRANCHES)
        y_hg = _hgrn2(hq, hf, hi, hg, hg_lower_bound, hg_norm_w[l], bsz, seq)
        os_, ls_ = [], []
        for g, (_, dil) in enumerate(ATT_BRANCHES):
            o, lse = _dilated_attention(qkv[g], qkv[nbr + g], qkv[2 * nbr + g], bias[g], bsz, seq, dil)
            os_.append(o)
            ls_.append(lse)
        w_out_bf = _cast_bf16(w_out[l], 256)
        x1, h2b, h2p, logits_t = _out_projection(y_hg, os_, ls_, x2, mod4, w_out_bf, ln1_g[l], ln1_b[l],
                                                 w_router[l].T, seq, 512)
        eidx, wk, rank, cnt = _route(logits_t, router_bias[l], 512)
        dest, blk_seq, dexp, meta, n_blocks = _dispatch_plan(cnt, eidx, rank, 512)
        row_token = _row_tables(dest, n_blocks * MOE_BLOCK)
        shared = _shared_expert(h2b, _cast_bf16(w_sh_gate[l], 256), _cast_bf16(w_sh_up[l], 256),
                                _cast_bf16(w_sh_down[l], 256), 512)
        y_rows = _routed_experts(h2p, row_token, blk_seq, dexp, meta, n_blocks,
                                 w_e_gate[l], w_e_up[l], w_e_down[l])
        x2 = _combine_final(x1, shared, mod4, dest, wk, y_rows, ln2_g[l], ln2_b[l], seq)
    return x2.reshape(bsz, seq, d)
```

```python
import dataclasses
import math

import jax
import jax.numpy as jnp
import numpy as np
from jax import lax
from jax.experimental import pallas as pl
from jax.experimental.pallas import tpu as pltpu
from jax.experimental.pallas import tpu_sc as plsc

HG_HEADS = 4
HG_DK = 128
HG_WIDTH = HG_HEADS * HG_DK
ATT_BRANCHES = ((128, 1), (512, 4), (2048, 16))
ATT_HEADS_PER_BRANCH = 4
ATT_HEAD_DIM = 64
ATT_BW = ATT_HEADS_PER_BRANCH * ATT_HEAD_DIM
ATT_BLOCK = 128
REL_BUCKETS = 32
REL_MAX_DIST = 2048
N_EXPERTS = 256
TOP_K = 8
N_GROUPS = 8
TOPK_GROUPS = 4
ROUTED_SCALE = 2.5
MOE_BLOCK = 128
DEPTH = 1
DN_ALPHA = (2 * DEPTH) ** 0.25
LN_EPS = 1e-5
RMS_EPS = 1e-6

LANES = 128
SUBLANES = 8
VMEM_LIMIT_BYTES = 56 * 1024 * 1024

F32 = jnp.float32
BF16 = jnp.bfloat16
NEG_INF = float("-inf")


def _cparams(sem):
    return pltpu.CompilerParams(dimension_semantics=sem, vmem_limit_bytes=VMEM_LIMIT_BYTES)


def _ln_rows(x):
    mu = jnp.mean(x, axis=-1, keepdims=True)
    xc = x - mu
    var = jnp.mean(xc * xc, axis=-1, keepdims=True)
    return xc * lax.rsqrt(var + LN_EPS)


def _dot(a, b):
    return jnp.dot(a, b, preferred_element_type=F32)


def _dot_nt(a, b):
    return lax.dot_general(a, b, (((1,), (1,)), ((), ())), preferred_element_type=F32)


def _dot_tn(a, b):
    return lax.dot_general(a, b, (((0,), (0,)), ((), ())), preferred_element_type=F32)


def _cast_kernel(w_ref, o_ref):
    o_ref[...] = w_ref[...].astype(o_ref.dtype)


def _cast_bf16(w, rows_per_step):
    r, c = w.shape
    return pl.pallas_call(
        _cast_kernel,
        grid=(r // rows_per_step,),
        in_specs=[pl.BlockSpec((rows_per_step, c), lambda i: (i, 0))],
        out_specs=pl.BlockSpec((rows_per_step, c), lambda i: (i, 0)),
        out_shape=jax.ShapeDtypeStruct((r, c), BF16),
        compiler_params=_cparams(("parallel",)),
        name="cast_bf16",
    )(w)


def _mod_kernel(c_ref, w_ref, b_ref, o_ref):
    c = c_ref[...]
    cond = c * jax.nn.sigmoid(c)
    o_ref[...] = jnp.dot(cond, w_ref[...], preferred_element_type=F32,
                         precision=lax.Precision.HIGHEST) + b_ref[...]


def _modulation(c, w_ada, b_ada):
    bsz, d = c.shape
    n = w_ada.shape[1]
    rows = -(-bsz // SUBLANES) * SUBLANES
    cpad = jnp.zeros((rows, d), F32).at[:bsz].set(c)
    tn = 1024
    out = pl.pallas_call(
        _mod_kernel,
        grid=(n // tn,),
        in_specs=[pl.BlockSpec((rows, d), lambda j: (0, 0)),
                  pl.BlockSpec((d, tn), lambda j: (0, j)),
                  pl.BlockSpec((1, tn), lambda j: (0, j))],
        out_specs=pl.BlockSpec((rows, tn), lambda j: (0, j)),
        out_shape=jax.ShapeDtypeStruct((rows, n), F32),
        compiler_params=_cparams(("parallel",)),
        name="adaln_modulation",
    )(cpad, w_ada, b_ada.reshape(1, n))
    return out[:bsz]


_IN_HG = 4
_IN_ATT = 3 * len(ATT_BRANCHES)


def _inproj_kernel(x_ref, sc_ref, sh_ref, w_ref, *refs):
    outs = refs[:_IN_HG + _IN_ATT]
    scratch = refs[_IN_HG + _IN_ATT:]
    x = x_ref[...]
    h = _ln_rows(x) * (1.0 + sc_ref[...]) + sh_ref[...]
    hb = h.astype(BF16)
    tm = x.shape[0]
    col = 0
    n_scr = 0
    for k, o_ref in enumerate(outs):
        width = HG_WIDTH if k < _IN_HG else ATT_BW
        y = _dot(hb, w_ref[:, col:col + width])
        col += width
        if k < _IN_HG:
            o_ref[...] = y.astype(o_ref.dtype)
            continue
        if k < _IN_HG + len(ATT_BRANCHES):
            y = y * (ATT_HEAD_DIM ** -0.5)
        dil = ATT_BRANCHES[(k - _IN_HG) % len(ATT_BRANCHES)][1]
        if dil == 1:
            o_ref[...] = y.astype(o_ref.dtype)
            continue
        scr = scratch[n_scr]
        n_scr += 1
        for half in range(ATT_BW // LANES):
            scr[half] = y[:, half * LANES:(half + 1) * LANES]
        for r in range(dil):
            for half in range(ATT_BW // LANES):
                c0 = r * ATT_BW + half * LANES
                o_ref[:, c0:c0 + LANES] = scr[half, pl.ds(r, tm // dil, stride=dil), :].astype(o_ref.dtype)


def _in_projection(x2, mod4, w_in_bf, seq, tm):
    t, d = x2.shape
    steps_per_batch = seq // tm
    dils = [dil for _, dil in ATT_BRANCHES] * 3
    shapes = [(t, HG_WIDTH)] * _IN_HG + [(t // dil, dil * ATT_BW) for dil in dils]
    blocks = [(tm, HG_WIDTH)] * _IN_HG + [(tm // dil, dil * ATT_BW) for dil in dils]
    dtypes = [BF16, F32, BF16, BF16] + [BF16] * _IN_ATT
    mod_spec = lambda row: pl.BlockSpec((None, None, 1, d),
                                        lambda i, row=row: (i // steps_per_batch, row, 0, 0))
    outs = pl.pallas_call(
        _inproj_kernel,
        grid=(t // tm,),
        in_specs=[pl.BlockSpec((tm, d), lambda i: (i, 0)),
                  mod_spec(1), mod_spec(0),
                  pl.BlockSpec(w_in_bf.shape, lambda i: (0, 0))],
        out_specs=[pl.BlockSpec(b, lambda i: (i, 0)) for b in blocks],
        out_shape=[jax.ShapeDtypeStruct(s, dt) for s, dt in zip(shapes, dtypes)],
        scratch_shapes=[pltpu.VMEM((ATT_BW // LANES, tm, LANES), F32) for dil in dils if dil > 1],
        compiler_params=_cparams(("parallel",)),
        name="ln_in_projection",
    )(x2, mod4, mod4, w_in_bf)
    return outs


HG_CHUNK = 64
HG_CHUNKS_PER_STEP = 16
HG_SUB = 8
HG_LEVELS = (64, 32, 16)
LOG2E = 1.4426950408889634


def _hgrn_chunk(q, z, iv, lb):
    c = HG_CHUNK
    f = lb + (1.0 - lb) * jax.nn.sigmoid(z)
    lf = jnp.log(f)
    kk = (1.0 - lb) * jax.nn.sigmoid(-z)
    r_i = lax.broadcasted_iota(jnp.int32, (c, c), 0)
    c_i = lax.broadcasted_iota(jnp.int32, (c, c), 1)
    tril = (c_i <= r_i).astype(F32)
    b = jnp.dot(tril, lf, preferred_element_type=F32, precision=lax.Precision.HIGHEST)
    bl = b * LOG2E

    row = lax.broadcasted_iota(jnp.int32, (c, HG_DK), 0)
    scores = jnp.zeros((c, c), F32)
    for m in HG_LEVELS:
        nb = c // m
        b3 = bl.reshape(nb, m, HG_DK)
        piv = jnp.broadcast_to(b3[:, m // 2 - 1:m // 2, :], (nb, m, HG_DK)).reshape(c, HG_DK)
        second = (row % m) >= (m // 2)
        qt = jnp.where(second, q * jnp.exp2(bl - piv), 0.0)
        kt = jnp.where(second, 0.0, kk * jnp.exp2(piv - bl))
        s_m = _dot_nt(qt.astype(BF16), kt.astype(BF16))
        if nb > 1:
            s_m = jnp.where((r_i // m) == (c_i // m), s_m, 0.0)
        scores = scores + s_m
    sub = HG_SUB
    t_i = lax.broadcasted_iota(jnp.int32, (sub, 1), 0)
    lane = lax.broadcasted_iota(jnp.int32, (sub, c), 1)
    diag_rows = []
    for j in range(c // sub):
        qb = q[j * sub:(j + 1) * sub]
        kb = kk[j * sub:(j + 1) * sub]
        bb = bl[j * sub:(j + 1) * sub]
        a_j = jnp.zeros((sub, c), F32)
        for s in range(sub):
            w = qb * kb[s:s + 1] * jnp.exp2(bb - bb[s:s + 1])
            a_j = jnp.where(lane == j * sub + s, jnp.sum(w, axis=-1, keepdims=True), a_j)
        diag_rows.append(jnp.where(lane - j * sub <= t_i, a_j, 0.0))
    scores = scores + jnp.concatenate(diag_rows, axis=0)

    ivb = iv.astype(BF16)
    intra = _dot(scores.astype(BF16), ivb)
    b_last = bl[c - 1:c]
    kdec = (kk * jnp.exp2(b_last - bl)).astype(BF16)
    return (q * jnp.exp2(bl)).astype(BF16), intra, jnp.exp2(b_last), _dot_tn(ivb, kdec)


def _hgrn_kernel(q_ref, f_ref, i_ref, g_ref, lbp_ref, nw_ref, o_ref, st_ref):
    @pl.when(pl.program_id(1) == 0)
    def _():
        st_ref[...] = jnp.zeros_like(st_ref)

    lbp = lbp_ref[...]
    e = jnp.exp(lbp - jnp.max(lbp, axis=0, keepdims=True))
    lb_all = e[0:1] / jnp.sum(e, axis=0, keepdims=True)
    heads = []
    for h in range(HG_HEADS):
        sl = slice(h * HG_DK, (h + 1) * HG_DK)
        st_t = st_ref[h]
        outs = []
        for n in range(HG_CHUNKS_PER_STEP):
            rows = slice(n * HG_CHUNK, (n + 1) * HG_CHUNK)
            qdec, intra, dec_last, kv = _hgrn_chunk(q_ref[rows, sl].astype(F32), f_ref[rows, sl],
                                                    i_ref[rows, sl].astype(F32), lb_all[:, sl])
            o = _dot_nt(qdec, st_t.astype(BF16)) + intra
            st_t = st_t * dec_last + kv
            outs.append(o * lax.rsqrt(jnp.mean(o * o, axis=-1, keepdims=True) + RMS_EPS))
        st_ref[h] = st_t
        heads.append(jnp.concatenate(outs, axis=0))
    o_all = jnp.concatenate(heads, axis=-1)
    g = g_ref[...].astype(F32)
    o_ref[...] = (o_all * nw_ref[...] * (g * jax.nn.sigmoid(g))).astype(o_ref.dtype)


def _hgrn2(hq, hf, hi, hg, lb_param, norm_w, bsz, seq):
    t = hq.shape[0]
    rows = HG_CHUNK * HG_CHUNKS_PER_STEP
    nc = seq // rows
    tok = lambda b, n: (b * nc + n, 0)
    spec = pl.BlockSpec((rows, HG_WIDTH), tok)
    return pl.pallas_call(
        _hgrn_kernel,
        grid=(bsz, nc),
        in_specs=[spec, spec, spec, spec,
                  pl.BlockSpec(lb_param.shape, lambda b, n: (0, 0)),
                  pl.BlockSpec((1, HG_WIDTH), lambda b, n: (0, 0))],
        out_specs=spec,
        out_shape=jax.ShapeDtypeStruct((t, HG_WIDTH), BF16),
        scratch_shapes=[pltpu.VMEM((HG_HEADS, HG_DK, HG_DK), F32)],
        compiler_params=_cparams(("parallel", "arbitrary")),
        name="hgrn2_scan",
    )(hq, hf, hi, hg, lb_param, norm_w.reshape(1, HG_WIDTH))


def _t5_bucket_np(dist):
    max_exact = REL_BUCKETS // 2
    n = np.maximum(dist, 0)
    nf = np.maximum(n, 1).astype(np.float32)
    large = max_exact + (np.log(nf / np.float32(max_exact)) / np.float32(math.log(REL_MAX_DIST / max_exact))
                         * np.float32(REL_BUCKETS - max_exact)).astype(np.int32)
    large = np.minimum(large, REL_BUCKETS - 1)
    return np.where(n < max_exact, n, large).astype(np.int32)


def _band_tables():
    w = ATT_BLOCK
    qi = np.arange(w)[:, None]
    ki = np.arange(2 * w)[None, :]
    m = w + qi - ki
    band = (m >= 0) & (m <= w)
    buckets = np.stack([_t5_bucket_np(m * dil) for _, dil in ATT_BRANCHES])
    return buckets, band


def _bias_kernel(rb_ref, bucket_ref, o_ref):
    g = pl.program_id(0)
    w = ATT_BLOCK
    bucket = bucket_ref[...]
    qi = lax.broadcasted_iota(jnp.int32, (w, 2 * w), 0)
    ki = lax.broadcasted_iota(jnp.int32, (w, 2 * w), 1)
    m = w + qi - ki
    band = (m >= 0) & (m <= w)
    for h in range(ATT_HEADS_PER_BRANCH):
        acc = jnp.zeros((w, 2 * w), F32)
        for c in range(REL_BUCKETS):
            acc = jnp.where(bucket == c, rb_ref[c, g * ATT_HEADS_PER_BRANCH + h], acc)
        full = jnp.where(band, acc, NEG_INF)
        o_ref[1, h] = full
        o_ref[0, h] = jnp.where(ki >= w, full, NEG_INF)


def _bias_tables(rel_bias):
    buckets, _ = _band_tables()
    g = len(ATT_BRANCHES)
    w = ATT_BLOCK
    return pl.pallas_call(
        _bias_kernel,
        grid=(g,),
        in_specs=[pl.BlockSpec(memory_space=pltpu.SMEM),
                  pl.BlockSpec((None, w, 2 * w), lambda i: (i, 0, 0))],
        out_specs=pl.BlockSpec((None, 2, ATT_HEADS_PER_BRANCH, w, 2 * w), lambda i: (i, 0, 0, 0, 0)),
        out_shape=jax.ShapeDtypeStruct((g, 2, ATT_HEADS_PER_BRANCH, w, 2 * w), F32),
        compiler_params=_cparams(("parallel",)),
        name="rel_bias_tables",
    )(rel_bias, jnp.asarray(buckets))


ATT_BLOCKS_PER_STEP = 16


def _attn_kernel(q_ref, kp_ref, kc_ref, vp_ref, vc_ref, bias_ref, o_ref, lse_ref):
    m = pl.program_id(2)
    w = ATT_BLOCK
    hb = ATT_HEADS_PER_BRANCH
    lane = lax.broadcasted_iota(jnp.int32, (w, ATT_BW), 1) // ATT_HEAD_DIM
    for res, blk in [(r, b) for r in range(q_ref.shape[1] // ATT_BW) for b in range(q_ref.shape[0] // w)]:
        cols = slice(res * ATT_BW, (res + 1) * ATT_BW)
        kall = jnp.concatenate([kp_ref[:, cols], kc_ref[:, cols]], axis=0)
        vall = jnp.concatenate([vp_ref[:, cols], vc_ref[:, cols]], axis=0)
        q = q_ref[blk * w:(blk + 1) * w, cols]
        q4 = jnp.concatenate([jnp.where(lane == h, q, jnp.zeros_like(q)) for h in range(hb)], axis=0)
        kk = kall[blk * w:(blk + 2) * w]
        vv = vall[blk * w:(blk + 2) * w]
        s4 = _dot_nt(q4, kk)
        bias = bias_ref[jnp.minimum(m, 1)] if blk == 0 else bias_ref[1]
        s4 = s4 + bias.reshape(hb * w, 2 * w)
        mx = jnp.max(s4, axis=-1, keepdims=True)
        p = jnp.exp(s4 - mx)
        l = jnp.sum(p, axis=-1, keepdims=True)
        o4 = _dot((p / l).astype(vv.dtype), vv)
        lse4 = mx + jnp.log(l)
        o = jnp.zeros((w, ATT_BW), F32)
        lse = jnp.zeros((w, ATT_BW), F32)
        for h in range(hb):
            o = jnp.where(lane == h, o4[h * w:(h + 1) * w], o)
            lse = jnp.where(lane == h, lse4[h * w:(h + 1) * w], lse)
        o_ref[blk * w:(blk + 1) * w, cols] = o.astype(o_ref.dtype)
        lse_ref[blk * w:(blk + 1) * w, cols] = lse


def _dilated_attention(q, k, v, bias_g, bsz, seq, dilation):
    w = ATT_BLOCK
    l = seq // dilation
    pstep = math.gcd(ATT_BLOCKS_PER_STEP, l // w)
    nb = l // (w * pstep)
    rstep = math.gcd(ATT_BLOCKS_PER_STEP // pstep, dilation)
    view = lambda a: a.reshape(bsz, l, dilation * ATT_BW)
    cur = pl.BlockSpec((None, pstep * w, rstep * ATT_BW), lambda b, r, n: (b, n, r))
    prev = pl.BlockSpec((None, w, rstep * ATT_BW), lambda b, r, n: (b, jnp.maximum(pstep * n - 1, 0), r))
    o, lse = pl.pallas_call(
        _attn_kernel,
        grid=(bsz, dilation // rstep, nb),
        in_specs=[cur, prev, cur, prev, cur,
                  pl.BlockSpec(bias_g.shape, lambda b, r, n: (0, 0, 0, 0))],
        out_specs=[cur, cur],
        out_shape=[jax.ShapeDtypeStruct((bsz, l, dilation * ATT_BW), BF16),
                   jax.ShapeDtypeStruct((bsz, l, dilation * ATT_BW), F32)],
        compiler_params=_cparams(("parallel", "parallel", "arbitrary")),
        name=f"dilated_attention_d{dilation}",
    )(view(q), view(k), view(k), view(v), view(v), bias_g)
    return o.reshape(bsz * l, dilation * ATT_BW), lse.reshape(bsz * l, dilation * ATT_BW)


def _split_bf16(a):
    hi = a.astype(BF16)
    lo = (a - hi.astype(F32)).astype(BF16)
    return hi, lo


H2P_CHUNKS = 4


def _token_order(ref, scr, dil):
    if dil == 1:
        return ref[...].astype(F32)
    n = ref.shape[0]
    halves = ATT_BW // LANES
    for r in range(dil):
        for half in range(halves):
            c0 = r * ATT_BW + half * LANES
            scr[half, pl.ds(r, n, stride=dil), :] = ref[:, c0:c0 + LANES].astype(F32)
    return jnp.concatenate([scr[half] for half in range(halves)], axis=1)


def _outproj_kernel(yhg_ref, o1_ref, o2_ref, o3_ref, l1_ref, l2_ref, l3_ref, x_ref,
                    g1_ref, sc2_ref, sh2_ref, wout_ref, lng_ref, lnb_ref, wrt_ref,
                    x1_ref, h2_ref, h2p_ref, lgt_ref, *scratch):
    dils = [dil for _, dil in ATT_BRANCHES]
    scr = iter(scratch)
    o1, o2, o3 = [_token_order(r, None if dil == 1 else next(scr), dil)
                  for r, dil in zip((o1_ref, o2_ref, o3_ref), dils)]
    l1, l2, l3 = [_token_order(r, None if dil == 1 else next(scr), dil)
                  for r, dil in zip((l1_ref, l2_ref, l3_ref), dils)]
    mx = jnp.maximum(jnp.maximum(l1, l2), l3)
    e1, e2, e3 = jnp.exp(l1 - mx), jnp.exp(l2 - mx), jnp.exp(l3 - mx)
    den = e1 + e2 + e3
    att = (e1 / den) * o1 + (e2 / den) * o2 + (e3 / den) * o3
    mix = _dot(yhg_ref[...], wout_ref[:HG_WIDTH, :]) + _dot(att.astype(BF16), wout_ref[HG_WIDTH:, :])
    x1 = _ln_rows(DN_ALPHA * x_ref[...] + g1_ref[...] * mix) * lng_ref[...] + lnb_ref[...]
    x1_ref[...] = x1
    h2 = _ln_rows(x1) * (1.0 + sc2_ref[...]) + sh2_ref[...]
    h_hi = h2.astype(BF16)
    h_hf = h_hi.astype(F32)
    h2_ref[...] = h_hi
    bits = lax.bitcast_convert_type(h_hf, jnp.uint32)
    for cidx in range(H2P_CHUNKS):
        lo = bits[:, 2 * LANES * cidx:2 * LANES * cidx + LANES]
        hi = bits[:, 2 * LANES * cidx + LANES:2 * LANES * (cidx + 1)]
        h2p_ref[pl.ds(cidx, h2.shape[0], stride=H2P_CHUNKS), :] = lax.bitcast_convert_type((lo >> 16) | hi,
                                                                                          jnp.int32)
    h_lo = (h2 - h_hf).astype(BF16)
    w_hi, w_lo = _split_bf16(wrt_ref[...])
    lgt_ref[...] = _dot_nt(w_hi, h_hi) + (_dot_nt(w_hi, h_lo) + _dot_nt(w_lo, h_hi))


def _out_projection(yhg, os_, ls_, x2, mod4, w_out_bf, ln_g, ln_b, w_router_t, seq, tm):
    t, d = x2.shape
    spb = seq // tm
    ne = w_router_t.shape[0]
    row = lambda w: pl.BlockSpec((tm, w), lambda i: (i, 0))
    mod_spec = lambda r: pl.BlockSpec((None, None, 1, d), lambda i, r=r: (i // spb, r, 0, 0))
    full = lambda a: pl.BlockSpec(a.shape, lambda i: (0,) * a.ndim)
    ln_g2, ln_b2 = ln_g.reshape(1, d), ln_b.reshape(1, d)
    dils = [dil for _, dil in ATT_BRANCHES]
    branch = [pl.BlockSpec((tm // dil, dil * ATT_BW), lambda i: (i, 0)) for dil in dils]
    return pl.pallas_call(
        _outproj_kernel,
        grid=(t // tm,),
        in_specs=[row(HG_WIDTH)] + branch + branch + [row(d),
                  mod_spec(2), mod_spec(4), mod_spec(3),
                  full(w_out_bf), full(ln_g2), full(ln_b2), full(w_router_t)],
        out_specs=[row(d), row(d), pl.BlockSpec((tm * H2P_CHUNKS, LANES), lambda i: (i, 0)),
                   pl.BlockSpec((ne, tm), lambda i: (0, i))],
        out_shape=[jax.ShapeDtypeStruct((t, d), F32), jax.ShapeDtypeStruct((t, d), BF16),
                   jax.ShapeDtypeStruct((t * H2P_CHUNKS, LANES), jnp.int32),
                   jax.ShapeDtypeStruct((ne, t), F32)],
        scratch_shapes=[pltpu.VMEM((ATT_BW // LANES, tm, LANES), F32) for dil in dils + dils if dil > 1],
        compiler_params=_cparams(("parallel",)),
        name="merge_outproj_ln",
    )(yhg, *os_, *ls_, x2, mod4, mod4, mod4, w_out_bf, ln_g2, ln_b2, w_router_t)


def _argmax_rows(cur, iota, nrows):
    m = jnp.max(cur, axis=0, keepdims=True)
    idx = jnp.min(jnp.where(cur == m, iota, nrows), axis=0, keepdims=True)
    return m, idx, iota == idx


def _route_kernel(lgt_ref, rb_ref, eidx_ref, w_ref, rank_ref, cnt_ref, carry):
    ne = N_EXPERTS
    gsz = ne // N_GROUPS
    tt = lgt_ref.shape[1]

    @pl.when(pl.program_id(0) == 0)
    def _():
        carry[...] = jnp.zeros_like(carry)

    sc = jax.nn.sigmoid(lgt_ref[...])
    biased = sc + rb_ref[...]
    g3 = biased.reshape(N_GROUPS, gsz, tt)
    io3 = lax.broadcasted_iota(jnp.int32, (N_GROUPS, gsz, tt), 1)
    m1 = jnp.max(g3, axis=1, keepdims=True)
    first = jnp.min(jnp.where(g3 == m1, io3, gsz), axis=1, keepdims=True)
    m2 = jnp.max(jnp.where(io3 == first, NEG_INF, g3), axis=1, keepdims=True)
    gs = (m1 + m2).reshape(N_GROUPS, tt)
    io8 = lax.broadcasted_iota(jnp.int32, (N_GROUPS, tt), 0)
    sel = jnp.zeros((N_GROUPS, tt), jnp.int32)
    cur = gs
    for _ in range(TOPK_GROUPS):
        _, _, pick = _argmax_rows(cur, io8, N_GROUPS)
        sel = jnp.where(pick, 1, sel)
        cur = jnp.where(pick, NEG_INF, cur)
    masked = jnp.where(sel.reshape(N_GROUPS, 1, tt) > 0, g3, NEG_INF).reshape(ne, tt)
    ioe = lax.broadcasted_iota(jnp.int32, (ne, tt), 0)
    cur = masked
    idxs, ws, picks = [], [], []
    for _ in range(TOP_K):
        _, idx, pick = _argmax_rows(cur, ioe, ne)
        idxs.append(idx)
        picks.append(pick)
        ws.append(jnp.sum(jnp.where(pick, sc, 0.0), axis=0, keepdims=True))
        cur = jnp.where(pick, NEG_INF, cur)
    wk = jnp.concatenate(ws, axis=0)
    eidx_ref[...] = jnp.concatenate(idxs, axis=0)
    w_ref[...] = wk / jnp.sum(wk, axis=0, keepdims=True) * ROUTED_SCALE
    chosen = jnp.where(cur == NEG_INF, jnp.where(masked == NEG_INF, 0.0, 1.0), 0.0)
    r_i = lax.broadcasted_iota(jnp.int32, (tt, tt), 0)
    c_i = lax.broadcasted_iota(jnp.int32, (tt, tt), 1)
    before = jnp.where(r_i < c_i, 1.0, 0.0).astype(BF16)
    pref = _dot(chosen.astype(BF16), before) + carry[...]
    rank_ref[...] = jnp.concatenate(
        [jnp.sum(jnp.where(p, pref, 0.0), axis=0, keepdims=True) for p in picks], axis=0).astype(jnp.int32)
    carry[...] = carry[...] + jnp.sum(chosen, axis=1, keepdims=True)
    cnt_ref[...] = carry[...]


def _route(logits_t, router_bias, tt):
    ne, t = logits_t.shape
    tok = pl.BlockSpec((TOP_K, tt), lambda i: (0, i))
    return pl.pallas_call(
        _route_kernel,
        grid=(t // tt,),
        in_specs=[pl.BlockSpec((ne, tt), lambda i: (0, i)),
                  pl.BlockSpec((ne, 1), lambda i: (0, 0))],
        out_specs=[tok, tok, tok, pl.BlockSpec((ne, 1), lambda i: (0, 0))],
        out_shape=[jax.ShapeDtypeStruct((TOP_K, t), jnp.int32), jax.ShapeDtypeStruct((TOP_K, t), F32),
                   jax.ShapeDtypeStruct((TOP_K, t), jnp.int32), jax.ShapeDtypeStruct((ne, 1), F32)],
        scratch_shapes=[pltpu.VMEM((ne, 1), F32)],
        compiler_params=_cparams(("arbitrary",)),
        name="router_topk",
    )(logits_t, router_bias.reshape(ne, 1))


def _plan_kernel(cnt_ref, eidx_ref, rank_ref, dest_ref, seq_ref, dexp_ref, meta_ref, pstart_ref):
    ne = N_EXPERTS
    tt = eidx_ref.shape[1]
    nblk = seq_ref.shape[1]

    @pl.when(pl.program_id(0) == 0)
    def _():
        cnt = cnt_ref[...].astype(jnp.int32)
        padded = ((cnt + (MOE_BLOCK - 1)) // MOE_BLOCK) * MOE_BLOCK
        r_i = lax.broadcasted_iota(jnp.int32, (ne, ne), 0)
        c_i = lax.broadcasted_iota(jnp.int32, (ne, ne), 1)
        incl = jnp.where(c_i <= r_i, 1.0, 0.0)
        pend = jnp.dot(incl, jnp.broadcast_to(padded.astype(F32), (ne, LANES)),
                       preferred_element_type=F32, precision=lax.Precision.HIGHEST)[:, 0:1]
        pend = pend.astype(jnp.int32)
        pstart_ref[...] = pend - padded
        blk0 = lax.broadcasted_iota(jnp.int32, (ne, nblk), 1) * MOE_BLOCK
        be = jnp.minimum(jnp.sum(jnp.where(pend <= blk0, 1, 0), axis=0, keepdims=True), ne - 1)
        present = cnt > 0
        strict = jnp.where(c_i < r_i, 1.0, 0.0).astype(BF16)
        sidx = _dot(strict, jnp.broadcast_to(jnp.where(present, 1.0, 0.0), (ne, LANES)).astype(BF16))[:, 0:1]
        sidx = sidx.astype(jnp.int32)
        dexp_ref[...] = jnp.sum(jnp.where(jnp.logical_and(present, sidx == c_i), r_i, 0), axis=0, keepdims=True)
        ioeb = lax.broadcasted_iota(jnp.int32, (ne, nblk), 0)
        nu = jnp.max(pend, axis=0, keepdims=True) // MOE_BLOCK
        nd = jnp.sum(jnp.where(present, 1, 0), axis=0, keepdims=True)
        seq_ref[...] = jnp.minimum(jnp.sum(jnp.where(ioeb == be, sidx, 0), axis=0, keepdims=True), nd - 1)
        lane = lax.broadcasted_iota(jnp.int32, (1, LANES), 1)
        meta_ref[...] = jnp.where(lane == 0, nu, jnp.where(lane == 1, nd, 0))

    pstart = pstart_ref[...]
    ioe = lax.broadcasted_iota(jnp.int32, (ne, tt), 0)
    rows = []
    for k in range(TOP_K):
        sel = ioe == eidx_ref[k:k + 1, :]
        rows.append(jnp.sum(jnp.where(sel, pstart, 0), axis=0, keepdims=True))
    dest_ref[...] = jnp.concatenate(rows, axis=0) + rank_ref[...]


def _dispatch_plan(cnt, eidx, rank, tt):
    k, t = eidx.shape
    n_blocks = -(-(t * k) // MOE_BLOCK) + N_EXPERTS
    tok = pl.BlockSpec((k, tt), lambda i: (0, i))
    one = lambda n: pl.BlockSpec((1, n), lambda i: (0, 0))
    dest, seq, dexp, meta = pl.pallas_call(
        _plan_kernel,
        grid=(t // tt,),
        in_specs=[pl.BlockSpec(cnt.shape, lambda i: (0, 0)), tok, tok],
        out_specs=[tok, one(n_blocks), one(N_EXPERTS), one(LANES)],
        out_shape=[jax.ShapeDtypeStruct((k, t), jnp.int32), jax.ShapeDtypeStruct((1, n_blocks), jnp.int32),
                   jax.ShapeDtypeStruct((1, N_EXPERTS), jnp.int32), jax.ShapeDtypeStruct((1, LANES), jnp.int32)],
        scratch_shapes=[pltpu.VMEM((N_EXPERTS, 1), jnp.int32)],
        compiler_params=_cparams(("arbitrary",)),
        name="dispatch_plan",
    )(cnt, eidx, rank)
    return dest, seq.reshape(n_blocks), dexp.reshape(N_EXPERTS), meta.reshape(LANES), n_blocks


SC_CORES = 2
SC_SUBCORES = 16
SC_LANES = 16
SC_CHUNK = 16384
SC_UNROLL = 4


def _row_tables(dest, n_rows):
    k, t = dest.shape
    a = k * t
    nw = SC_CORES * SC_SUBCORES
    per_w = n_rows // nw
    assert n_rows % (nw * SC_LANES) == 0 and t % SC_CHUNK == 0
    mesh = plsc.VectorSubcoreMesh(core_axis_name="c", subcore_axis_name="s")
    cp = pltpu.CompilerParams()
    if "needs_layout_passes" in pltpu.CompilerParams.__dataclass_fields__:
        cp = dataclasses.replace(cp, needs_layout_passes=False)

    n_chunks = a // SC_CHUNK
    assert n_chunks % 2 == 0

    def body(dest_hbm, tok_out, dbuf_0, dbuf_1, tloc, sem_0, sem_1):
        wid = lax.axis_index("s") * SC_CORES + lax.axis_index("c")
        base = wid * per_w

        def chunk_copy(c, buf, sem):
            return pltpu.make_async_copy(dest_hbm.at[pl.ds(c * SC_CHUNK, SC_CHUNK)], buf, sem)

        chunk_copy(0, dbuf_0, sem_0).start()

        @pl.loop(0, per_w // SC_LANES)
        def _(i):
            tloc[pl.ds(i * SC_LANES, SC_LANES)] = jnp.zeros((SC_LANES,), jnp.int32)

        lane = lax.iota(jnp.int32, SC_LANES)

        def scan(c, dbuf):
            tok0 = lax.rem(c * SC_CHUNK, t)

            @pl.loop(0, SC_CHUNK // (SC_LANES * SC_UNROLL))
            def _(j):
                for u in range(SC_UNROLL):
                    off = (j * SC_UNROLL + u) * SC_LANES
                    loc = dbuf[pl.ds(off, SC_LANES)] - base
                    mine = jnp.logical_and(loc >= 0, loc < per_w)
                    loc = jnp.where(mine, loc, 0)
                    plsc.store_scatter(tloc, [loc], tok0 + off + lane, mask=mine)

        @pl.loop(0, n_chunks // 2)
        def _(i):
            c0 = 2 * i
            chunk_copy(c0 + 1, dbuf_1, sem_1).start()
            chunk_copy(c0, dbuf_0, sem_0).wait()
            scan(c0, dbuf_0)
            chunk_copy(jnp.minimum(c0 + 2, n_chunks - 1), dbuf_0, sem_0).start()
            chunk_copy(c0 + 1, dbuf_1, sem_1).wait()
            scan(c0 + 1, dbuf_1)

        chunk_copy(n_chunks - 1, dbuf_0, sem_0).wait()
        pltpu.sync_copy(tloc, tok_out.at[pl.ds(base, per_w)])

    fn = pl.kernel(
        body,
        out_type=jax.ShapeDtypeStruct((n_rows,), jnp.int32),
        mesh=mesh,
        scratch_types=[pltpu.VMEM((SC_CHUNK,), jnp.int32), pltpu.VMEM((SC_CHUNK,), jnp.int32),
                       pltpu.VMEM((per_w,), jnp.int32), pltpu.SemaphoreType.DMA, pltpu.SemaphoreType.DMA],
        compiler_params=cp,
        name="row_tables",
    )
    return fn(dest.reshape(a))


ROW_TILE = (SUBLANES, LANES)


FFN_GROUP = 4
PAIR_ROWS = 2 * MOE_BLOCK
FFN_CHAIN = 128
W_SETS = 3
GATHER_BATCH = 32


def _gather_rows(h2p_ref, tok_ref, row, buf, base):
    for j0 in range(0, MOE_BLOCK, GATHER_BATCH):
        vals = [h2p_ref[pl.ds(pl.multiple_of(tok_ref[row, j] * H2P_CHUNKS, H2P_CHUNKS), H2P_CHUNKS), :]
                for j in range(j0, j0 + GATHER_BATCH)]
        for j, v in zip(range(j0, j0 + GATHER_BATCH), vals):
            buf[pl.ds(H2P_CHUNKS * (base + j), H2P_CHUNKS), :] = v


def _expert_rows(buf, chains, out):
    hms = []
    for row0, nrows, (wg_c, wu_c, _), _ in chains:
        parts = []
        for cidx in range(H2P_CHUNKS):
            word = buf[pl.ds(H2P_CHUNKS * row0 + cidx, nrows, stride=H2P_CHUNKS), :]
            parts.append(lax.bitcast_convert_type(word << 16, F32))
            parts.append(lax.bitcast_convert_type(word & jnp.int32(-65536), F32))
        x = jnp.concatenate(parts, axis=1)
        g = _dot(x, wg_c[...])
        u = _dot(x, wu_c[...])
        hms.append((g * jax.nn.sigmoid(g)) * u)
    for (_, nrows, (_, _, wd_c), out_row0), hm in zip(chains, hms):
        res = _dot(hm, wd_c[...])
        nch = res.shape[1] // LANES
        for c in range(nch):
            out[pl.ds(nch * out_row0 + c, nrows, stride=nch), :] = res[:, c * LANES:(c + 1) * LANES]


def _ffn_kernel(seq_ref, dexp_ref, meta_ref, tokc_ref, tokn_ref, h2p_ref, wg_hbm, wu_hbm, wd_hbm,
                y_ref, buf_0, buf_1, *rest):
    i = pl.program_id(0)
    n_used = meta_ref[0]
    n_exp = meta_ref[1]
    nblk = seq_ref.shape[0]
    wsets = tuple(tuple(rest[3 * n:3 * n + 3]) for n in range(W_SETS))
    started_ref, sems = rest[3 * W_SETS:]
    bufs = (buf_0, buf_1)

    def weight_copies(s, par):
        e = dexp_ref[s]
        return [pltpu.make_async_copy(src.at[e], dst, sems.at[par, n])
                for n, (src, dst) in enumerate(zip((wg_hbm, wu_hbm, wd_hbm), wsets[par]))]

    def start_expert(s):
        for par in range(W_SETS):
            @pl.when(s % W_SETS == par)
            def _():
                for cp in weight_copies(s, par):
                    cp.start()

    @pl.when(i == 0)
    def _():
        _gather_rows(h2p_ref, tokc_ref, 0, buf_0, 0)
        _gather_rows(h2p_ref, tokc_ref, 1, buf_0, MOE_BLOCK)
        start_expert(0)
        started_ref[0] = 0

    for p in range(FFN_GROUP // 2):
        b_a = i * FFN_GROUP + 2 * p
        used = b_a < n_used
        s_a = seq_ref[jnp.minimum(b_a, nblk - 1)]
        s_b = seq_ref[jnp.minimum(b_a + 1, nblk - 1)]
        first_a = jnp.logical_or(b_a == 0, s_a != seq_ref[jnp.clip(b_a - 1, 0, nblk - 1)])
        same = s_a == s_b
        cur, nxt = bufs[p], bufs[1 - p]
        out_row0 = p * PAIR_ROWS

        def gather_next():
            if p == 0:
                _gather_rows(h2p_ref, tokc_ref, 2, nxt, 0)
                _gather_rows(h2p_ref, tokc_ref, 3, nxt, MOE_BLOCK)
            else:
                _gather_rows(h2p_ref, tokn_ref, 0, nxt, 0)
                _gather_rows(h2p_ref, tokn_ref, 1, nxt, MOE_BLOCK)

        @pl.when(used)
        def _():
            started = started_ref[0]
            limit = jnp.minimum(s_a + (W_SETS - 1), n_exp - 1)
            for _unused in range(W_SETS - 1):
                go = started < limit

                @pl.when(go)
                def _():
                    start_expert(started + 1)
                started = jnp.where(go, started + 1, started)
            started_ref[0] = started

        for par in range(W_SETS):
            @pl.when(jnp.logical_and(used, jnp.logical_and(same, s_a % W_SETS == par)))
            def _():
                @pl.when(first_a)
                def _():
                    for cp in weight_copies(s_a, par):
                        cp.wait()
                gather_next()
                _expert_rows(cur, [(r0, FFN_CHAIN, wsets[par], out_row0 + r0)
                                   for r0 in range(0, PAIR_ROWS, FFN_CHAIN)], y_ref)

            @pl.when(jnp.logical_and(used, jnp.logical_and(jnp.logical_not(same), s_a % W_SETS == par)))
            def _():
                @pl.when(first_a)
                def _():
                    for cp in weight_copies(s_a, par):
                        cp.wait()
                for cp in weight_copies(s_b, (par + 1) % W_SETS):
                    cp.wait()
                gather_next()
                _expert_rows(cur, [(r0, FFN_CHAIN, wsets[(par + r0 // MOE_BLOCK) % W_SETS], out_row0 + r0)
                                   for r0 in range(0, PAIR_ROWS, FFN_CHAIN)], y_ref)

        @pl.when(jnp.logical_not(used))
        def _():
            y_ref[pl.ds(out_row0 * SUBLANES, PAIR_ROWS * SUBLANES), :] = jnp.zeros(
                (PAIR_ROWS * SUBLANES, LANES), y_ref.dtype)


def _routed_experts(h2p, row_token, seq, dexp, meta, n_blocks, wg, wu, wd):
    d = wg.shape[1]
    de = wg.shape[2]
    ng = n_blocks // FFN_GROUP
    assert n_blocks % FFN_GROUP == 0 and FFN_GROUP == 4 and d == SUBLANES * LANES
    tok3 = row_token.reshape(ng, FFN_GROUP, MOE_BLOCK)
    idle_step = lambda m: jnp.minimum((m[0] + FFN_GROUP - 1) // FFN_GROUP, ng - 1)
    smem = lambda imap: pl.BlockSpec((None, FFN_GROUP, MOE_BLOCK), imap, memory_space=pltpu.SMEM)
    grid_spec = pltpu.PrefetchScalarGridSpec(
        num_scalar_prefetch=3,
        grid=(ng,),
        in_specs=[
            smem(lambda i, sq, dx, m: (jnp.minimum(i, ng - 1), 0, 0)),
            smem(lambda i, sq, dx, m: (jnp.minimum(i + 1, ng - 1), 0, 0)),
            pl.BlockSpec(h2p.shape, lambda i, sq, dx, m: (0, 0), pipeline_mode=pl.Buffered(1)),
            pl.BlockSpec(memory_space=pl.ANY),
            pl.BlockSpec(memory_space=pl.ANY),
            pl.BlockSpec(memory_space=pl.ANY),
        ],
        out_specs=pl.BlockSpec((FFN_GROUP * MOE_BLOCK * SUBLANES, LANES),
                               lambda i, sq, dx, m: (jnp.minimum(i, idle_step(m)), 0)),
        scratch_shapes=[pltpu.VMEM((PAIR_ROWS * H2P_CHUNKS, LANES), jnp.int32)] * 2 + [
            pltpu.VMEM((d, de), F32), pltpu.VMEM((d, de), F32), pltpu.VMEM((de, d), F32)] * W_SETS + [
            pltpu.SMEM((1,), jnp.int32), pltpu.SemaphoreType.DMA((W_SETS, 3))],
    )
    y = pl.pallas_call(
        _ffn_kernel,
        grid_spec=grid_spec,
        out_shape=jax.ShapeDtypeStruct((n_blocks * MOE_BLOCK * SUBLANES, LANES), F32),
        compiler_params=_cparams(("arbitrary",)),
        name="routed_experts",
    )(seq, dexp, meta, tok3, tok3, h2p, wg, wu, wd)
    return y.reshape((n_blocks * MOE_BLOCK,) + ROW_TILE)


COMBINE_TOKENS = 128


def _row_copies(src_hbm, idx_ref, buf, sem):
    return [pltpu.make_async_copy(src_hbm.at[idx_ref[k, j]], buf.at[k, j], sem)
            for k in range(TOP_K) for j in range(COMBINE_TOKENS)]


def _shared_kernel(h_ref, wsg_ref, wsu_ref, wsd_ref, o_ref):
    hb = h_ref[...]
    g = _dot(hb, wsg_ref[...])
    u = _dot(hb, wsu_ref[...])
    o_ref[...] = _dot(((g * jax.nn.sigmoid(g)) * u).astype(BF16), wsd_ref[...])


def _shared_expert(h2b, wsg, wsu, wsd, tm):
    t, d = h2b.shape
    row = pl.BlockSpec((tm, d), lambda i: (i, 0))
    full = lambda a: pl.BlockSpec(a.shape, lambda i: (0,) * a.ndim)
    return pl.pallas_call(
        _shared_kernel,
        grid=(t // tm,),
        in_specs=[row, full(wsg), full(wsu), full(wsd)],
        out_specs=row,
        out_shape=jax.ShapeDtypeStruct((t, d), F32),
        compiler_params=_cparams(("parallel",)),
        name="shared_expert",
    )(h2b, wsg, wsu, wsd)


def _final_kernel(dc_ref, dn_ref, wk_ref, x1_ref, sh_ref, g2_ref, lng_ref, lnb_ref,
                  y_hbm, o_ref, ybuf, sems):
    i = pl.program_id(0)
    nsteps = pl.num_programs(0)
    slot = i % 2

    def issue(d_ref, s):
        for n, cp in enumerate(_row_copies(y_hbm, d_ref, ybuf.at[s], sems.at[s])):
            cp.start(priority=n % 2)

    @pl.when(i == 0)
    def _():
        issue(dc_ref, 0)

    @pl.when(i + 1 < nsteps)
    def _():
        issue(dn_ref, 1 - slot)

    for cp in _row_copies(y_hbm, dc_ref, ybuf.at[slot], sems.at[slot]):
        cp.wait()
    rows = []
    for j in range(COMBINE_TOKENS):
        acc = ybuf[slot, 0, j].astype(F32) * wk_ref[0, j]
        for k in range(1, TOP_K):
            acc = acc + ybuf[slot, k, j].astype(F32) * wk_ref[k, j]
        rows.append(acc)
    routed = pltpu.einshape("tcl->t(cl)", jnp.stack(rows, axis=0))
    x2 = DN_ALPHA * x1_ref[...] + g2_ref[...] * (routed + sh_ref[...])
    o_ref[...] = _ln_rows(x2) * lng_ref[...] + lnb_ref[...]


def _combine_final(x1, shared, mod4, dest, wk, y_rows, ln_g, ln_b, seq):
    t, d = x1.shape
    tt = COMBINE_TOKENS
    nsteps = t // tt
    spb = seq // tt
    row = pl.BlockSpec((tt, d), lambda i: (i, 0))
    full = lambda a: pl.BlockSpec(a.shape, lambda i: (0,) * a.ndim)
    ln_g2, ln_b2 = ln_g.reshape(1, d), ln_b.reshape(1, d)
    return pl.pallas_call(
        _final_kernel,
        grid=(nsteps,),
        in_specs=[pl.BlockSpec((TOP_K, tt), lambda i: (0, i), memory_space=pltpu.SMEM),
                  pl.BlockSpec((TOP_K, tt), lambda i: (0, jnp.minimum(i + 1, nsteps - 1)),
                               memory_space=pltpu.SMEM),
                  pl.BlockSpec((TOP_K, tt), lambda i: (0, i), memory_space=pltpu.SMEM),
                  row, row,
                  pl.BlockSpec((None, None, 1, d), lambda i: (i // spb, 5, 0, 0)),
                  full(ln_g2), full(ln_b2),
                  pl.BlockSpec(memory_space=pl.ANY)],
        out_specs=row,
        out_shape=jax.ShapeDtypeStruct((t, d), F32),
        scratch_shapes=[pltpu.VMEM((2, TOP_K, tt) + ROW_TILE, y_rows.dtype),
                        pltpu.SemaphoreType.DMA((2,))],
        compiler_params=_cparams(("arbitrary",)),
        name="shared_combine_ln",
    )(dest, dest, wk, x1, shared, mod4, ln_g2, ln_b2, y_rows)


def kernel(x, c, w_ada, b_ada, w_in, hg_lower_bound, hg_norm_w, rel_bias, w_out, ln1_g, ln1_b, w_router,
           router_bias, w_e_gate, w_e_up, w_e_down, w_sh_gate, w_sh_up, w_sh_down, ln2_g, ln2_b):
    bsz, seq, d = x.shape
    t = bsz * seq
    assert w_ada.shape[0] == DEPTH and seq % (ATT_BRANCHES[-1][0]) == 0
    x2 = x.reshape(t, d)
    bias = _bias_tables(rel_bias)
    for l in range(DEPTH):
        mod4 = _modulation(c, w_ada[l], b_ada[l]).reshape(bsz, 6, 1, d)
        w_in_bf = _cast_bf16(w_in[l], 256)
        hq, hf, hi, hg, *qkv = _in_projection(x2, mod4, w_in_bf, seq, 512)
        nbr = len(ATT_BRANCHES)
        y_hg = _hgrn2(hq, hf, hi, hg, hg_lower_bound, hg_norm_w[l], bsz, seq)
        os_, ls_ = [], []
        for g, (_, dil) in enumerate(ATT_BRANCHES):
            o, lse = _dilated_attention(qkv[g], qkv[nbr + g], qkv[2 * nbr + g], bias[g], bsz, seq, dil)
            os_.append(o)
            ls_.append(lse)
        w_out_bf = _cast_bf16(w_out[l], 256)
        x1, h2b, h2p, logits_t = _out_projection(y_hg, os_, ls_, x2, mod4, w_out_bf, ln1_g[l], ln1_b[l],
                                                 w_router[l].T, seq, 512)
        eidx, wk, rank, cnt = _route(logits_t, router_bias[l], 512)
        dest, blk_seq, dexp, meta, n_blocks = _dispatch_plan(cnt, eidx, rank, 2048)
        row_token = _row_tables(dest, n_blocks * MOE_BLOCK)
        shared = _shared_expert(h2b, _cast_bf16(w_sh_gate[l], 256), _cast_bf16(w_sh_up[l], 256),
                                _cast_bf16(w_sh_down[l], 256), 512)
        y_rows = _routed_experts(h2p, row_token, blk_seq, dexp, meta, n_blocks,
                                 w_e_gate[l], w_e_up[l], w_e_down[l])
        x2 = _combine_final(x1, shared, mod4, dest, wk, y_rows, ln2_g[l], ln2_b[l], seq)
    return x2.reshape(bsz, seq, d)
```

```python
import dataclasses
import math

import jax
import jax.numpy as jnp
import numpy as np
from jax import lax
from jax.experimental import pallas as pl
from jax.experimental.pallas import tpu as pltpu
from jax.experimental.pallas import tpu_sc as plsc

HG_HEADS = 4
HG_DK = 128
HG_WIDTH = HG_HEADS * HG_DK
ATT_BRANCHES = ((128, 1), (512, 4), (2048, 16))
ATT_HEADS_PER_BRANCH = 4
ATT_HEAD_DIM = 64
ATT_BW = ATT_HEADS_PER_BRANCH * ATT_HEAD_DIM
ATT_BLOCK = 128
REL_BUCKETS = 32
REL_MAX_DIST = 2048
N_EXPERTS = 256
TOP_K = 8
N_GROUPS = 8
TOPK_GROUPS = 4
ROUTED_SCALE = 2.5
MOE_BLOCK = 128
DEPTH = 1
DN_ALPHA = (2 * DEPTH) ** 0.25
LN_EPS = 1e-5
RMS_EPS = 1e-6

LANES = 128
SUBLANES = 8
VMEM_LIMIT_BYTES = 56 * 1024 * 1024

F32 = jnp.float32
BF16 = jnp.bfloat16
NEG_INF = float("-inf")


def _cparams(sem):
    return pltpu.CompilerParams(dimension_semantics=sem, vmem_limit_bytes=VMEM_LIMIT_BYTES)


def _ln_rows(x):
    mu = jnp.mean(x, axis=-1, keepdims=True)
    xc = x - mu
    var = jnp.mean(xc * xc, axis=-1, keepdims=True)
    return xc * lax.rsqrt(var + LN_EPS)


def _dot(a, b):
    return jnp.dot(a, b, preferred_element_type=F32)


def _dot_nt(a, b):
    return lax.dot_general(a, b, (((1,), (1,)), ((), ())), preferred_element_type=F32)


def _dot_tn(a, b):
    return lax.dot_general(a, b, (((0,), (0,)), ((), ())), preferred_element_type=F32)


def _cast_kernel(w_ref, o_ref):
    o_ref[...] = w_ref[...].astype(o_ref.dtype)


def _cast_bf16(w, rows_per_step):
    r, c = w.shape
    return pl.pallas_call(
        _cast_kernel,
        grid=(r // rows_per_step,),
        in_specs=[pl.BlockSpec((rows_per_step, c), lambda i: (i, 0))],
        out_specs=pl.BlockSpec((rows_per_step, c), lambda i: (i, 0)),
        out_shape=jax.ShapeDtypeStruct((r, c), BF16),
        compiler_params=_cparams(("parallel",)),
        name="cast_bf16",
    )(w)


def _mod_kernel(c_ref, w_ref, b_ref, o_ref):
    c = c_ref[...]
    cond = c * jax.nn.sigmoid(c)
    o_ref[...] = jnp.dot(cond, w_ref[...], preferred_element_type=F32,
                         precision=lax.Precision.HIGHEST) + b_ref[...]


def _modulation(c, w_ada, b_ada):
    bsz, d = c.shape
    n = w_ada.shape[1]
    rows = -(-bsz // SUBLANES) * SUBLANES
    cpad = jnp.zeros((rows, d), F32).at[:bsz].set(c)
    tn = 1024
    out = pl.pallas_call(
        _mod_kernel,
        grid=(n // tn,),
        in_specs=[pl.BlockSpec((rows, d), lambda j: (0, 0)),
                  pl.BlockSpec((d, tn), lambda j: (0, j)),
                  pl.BlockSpec((1, tn), lambda j: (0, j))],
        out_specs=pl.BlockSpec((rows, tn), lambda j: (0, j)),
        out_shape=jax.ShapeDtypeStruct((rows, n), F32),
        compiler_params=_cparams(("parallel",)),
        name="adaln_modulation",
    )(cpad, w_ada, b_ada.reshape(1, n))
    return out[:bsz]


_IN_HG = 4
_IN_ATT = 3 * len(ATT_BRANCHES)


def _inproj_kernel(x_ref, sc_ref, sh_ref, w_ref, *refs):
    outs = refs[:_IN_HG + _IN_ATT]
    scratch = refs[_IN_HG + _IN_ATT:]
    x = x_ref[...]
    h = _ln_rows(x) * (1.0 + sc_ref[...]) + sh_ref[...]
    hb = h.astype(BF16)
    tm = x.shape[0]
    col = 0
    n_scr = 0
    for k, o_ref in enumerate(outs):
        width = HG_WIDTH if k < _IN_HG else ATT_BW
        y = _dot(hb, w_ref[:, col:col + width])
        col += width
        if k < _IN_HG:
            o_ref[...] = y.astype(o_ref.dtype)
            continue
        if k < _IN_HG + len(ATT_BRANCHES):
            y = y * (ATT_HEAD_DIM ** -0.5)
        dil = ATT_BRANCHES[(k - _IN_HG) % len(ATT_BRANCHES)][1]
        if dil == 1:
            o_ref[...] = y.astype(o_ref.dtype)
            continue
        scr = scratch[n_scr]
        n_scr += 1
        for half in range(ATT_BW // LANES):
            scr[half] = y[:, half * LANES:(half + 1) * LANES]
        for r in range(dil):
            for half in range(ATT_BW // LANES):
                c0 = r * ATT_BW + half * LANES
                o_ref[:, c0:c0 + LANES] = scr[half, pl.ds(r, tm // dil, stride=dil), :].astype(o_ref.dtype)


def _in_projection(x2, mod4, w_in_bf, seq, tm):
    t, d = x2.shape
    steps_per_batch = seq // tm
    dils = [dil for _, dil in ATT_BRANCHES] * 3
    shapes = [(t, HG_WIDTH)] * _IN_HG + [(t // dil, dil * ATT_BW) for dil in dils]
    blocks = [(tm, HG_WIDTH)] * _IN_HG + [(tm // dil, dil * ATT_BW) for dil in dils]
    dtypes = [BF16, F32, BF16, BF16] + [BF16] * _IN_ATT
    mod_spec = lambda row: pl.BlockSpec((None, None, 1, d),
                                        lambda i, row=row: (i // steps_per_batch, row, 0, 0))
    outs = pl.pallas_call(
        _inproj_kernel,
        grid=(t // tm,),
        in_specs=[pl.BlockSpec((tm, d), lambda i: (i, 0)),
                  mod_spec(1), mod_spec(0),
                  pl.BlockSpec(w_in_bf.shape, lambda i: (0, 0))],
        out_specs=[pl.BlockSpec(b, lambda i: (i, 0)) for b in blocks],
        out_shape=[jax.ShapeDtypeStruct(s, dt) for s, dt in zip(shapes, dtypes)],
        scratch_shapes=[pltpu.VMEM((ATT_BW // LANES, tm, LANES), F32) for dil in dils if dil > 1],
        compiler_params=_cparams(("parallel",)),
        name="ln_in_projection",
    )(x2, mod4, mod4, w_in_bf)
    return outs


HG_CHUNK = 64
HG_CHUNKS_PER_STEP = 16
HG_SUB = 8
HG_LEVELS = (64, 32, 16)
LOG2E = 1.4426950408889634


def _hgrn_chunk(q, z, iv, lb):
    c = HG_CHUNK
    f = lb + (1.0 - lb) * jax.nn.sigmoid(z)
    lf = jnp.log(f)
    kk = (1.0 - lb) * jax.nn.sigmoid(-z)
    r_i = lax.broadcasted_iota(jnp.int32, (c, c), 0)
    c_i = lax.broadcasted_iota(jnp.int32, (c, c), 1)
    tril = (c_i <= r_i).astype(F32)
    b = jnp.dot(tril, lf, preferred_element_type=F32, precision=lax.Precision.HIGHEST)
    bl = b * LOG2E

    row = lax.broadcasted_iota(jnp.int32, (c, HG_DK), 0)
    scores = jnp.zeros((c, c), F32)
    for m in HG_LEVELS:
        nb = c // m
        b3 = bl.reshape(nb, m, HG_DK)
        piv = jnp.broadcast_to(b3[:, m // 2 - 1:m // 2, :], (nb, m, HG_DK)).reshape(c, HG_DK)
        second = (row % m) >= (m // 2)
        qt = jnp.where(second, q * jnp.exp2(bl - piv), 0.0)
        kt = jnp.where(second, 0.0, kk * jnp.exp2(piv - bl))
        s_m = _dot_nt(qt.astype(BF16), kt.astype(BF16))
        if nb > 1:
            s_m = jnp.where((r_i // m) == (c_i // m), s_m, 0.0)
        scores = scores + s_m
    sub = HG_SUB
    t_i = lax.broadcasted_iota(jnp.int32, (sub, 1), 0)
    lane = lax.broadcasted_iota(jnp.int32, (sub, c), 1)
    diag_rows = []
    for j in range(c // sub):
        qb = q[j * sub:(j + 1) * sub]
        kb = kk[j * sub:(j + 1) * sub]
        bb = bl[j * sub:(j + 1) * sub]
        a_j = jnp.zeros((sub, c), F32)
        for s in range(sub):
            w = qb * kb[s:s + 1] * jnp.exp2(bb - bb[s:s + 1])
            a_j = jnp.where(lane == j * sub + s, jnp.sum(w, axis=-1, keepdims=True), a_j)
        diag_rows.append(jnp.where(lane - j * sub <= t_i, a_j, 0.0))
    scores = scores + jnp.concatenate(diag_rows, axis=0)

    ivb = iv.astype(BF16)
    intra = _dot(scores.astype(BF16), ivb)
    b_last = bl[c - 1:c]
    kdec = (kk * jnp.exp2(b_last - bl)).astype(BF16)
    return (q * jnp.exp2(bl)).astype(BF16), intra, jnp.exp2(b_last), _dot_tn(ivb, kdec)


def _hgrn_kernel(q_ref, f_ref, i_ref, g_ref, lbp_ref, nw_ref, o_ref, st_ref):
    @pl.when(pl.program_id(1) == 0)
    def _():
        st_ref[...] = jnp.zeros_like(st_ref)

    lbp = lbp_ref[...]
    e = jnp.exp(lbp - jnp.max(lbp, axis=0, keepdims=True))
    lb_all = e[0:1] / jnp.sum(e, axis=0, keepdims=True)
    heads = []
    for h in range(HG_HEADS):
        sl = slice(h * HG_DK, (h + 1) * HG_DK)
        st_t = st_ref[h]
        outs = []
        for n in range(HG_CHUNKS_PER_STEP):
            rows = slice(n * HG_CHUNK, (n + 1) * HG_CHUNK)
            qdec, intra, dec_last, kv = _hgrn_chunk(q_ref[rows, sl].astype(F32), f_ref[rows, sl],
                                                    i_ref[rows, sl].astype(F32), lb_all[:, sl])
            o = _dot_nt(qdec, st_t.astype(BF16)) + intra
            st_t = st_t * dec_last + kv
            outs.append(o * lax.rsqrt(jnp.mean(o * o, axis=-1, keepdims=True) + RMS_EPS))
        st_ref[h] = st_t
        heads.append(jnp.concatenate(outs, axis=0))
    o_all = jnp.concatenate(heads, axis=-1)
    g = g_ref[...].astype(F32)
    o_ref[...] = (o_all * nw_ref[...] * (g * jax.nn.sigmoid(g))).astype(o_ref.dtype)


def _hgrn2(hq, hf, hi, hg, lb_param, norm_w, bsz, seq):
    t = hq.shape[0]
    rows = HG_CHUNK * HG_CHUNKS_PER_STEP
    nc = seq // rows
    tok = lambda b, n: (b * nc + n, 0)
    spec = pl.BlockSpec((rows, HG_WIDTH), tok)
    return pl.pallas_call(
        _hgrn_kernel,
        grid=(bsz, nc),
        in_specs=[spec, spec, spec, spec,
                  pl.BlockSpec(lb_param.shape, lambda b, n: (0, 0)),
                  pl.BlockSpec((1, HG_WIDTH), lambda b, n: (0, 0))],
        out_specs=spec,
        out_shape=jax.ShapeDtypeStruct((t, HG_WIDTH), BF16),
        scratch_shapes=[pltpu.VMEM((HG_HEADS, HG_DK, HG_DK), F32)],
        compiler_params=_cparams(("parallel", "arbitrary")),
        name="hgrn2_scan",
    )(hq, hf, hi, hg, lb_param, norm_w.reshape(1, HG_WIDTH))


def _t5_bucket_np(dist):
    max_exact = REL_BUCKETS // 2
    n = np.maximum(dist, 0)
    nf = np.maximum(n, 1).astype(np.float32)
    large = max_exact + (np.log(nf / np.float32(max_exact)) / np.float32(math.log(REL_MAX_DIST / max_exact))
                         * np.float32(REL_BUCKETS - max_exact)).astype(np.int32)
    large = np.minimum(large, REL_BUCKETS - 1)
    return np.where(n < max_exact, n, large).astype(np.int32)


def _band_tables():
    w = ATT_BLOCK
    qi = np.arange(w)[:, None]
    ki = np.arange(2 * w)[None, :]
    m = w + qi - ki
    band = (m >= 0) & (m <= w)
    buckets = np.stack([_t5_bucket_np(m * dil) for _, dil in ATT_BRANCHES])
    return buckets, band


def _bias_kernel(rb_ref, bucket_ref, o_ref):
    g = pl.program_id(0)
    w = ATT_BLOCK
    bucket = bucket_ref[...]
    qi = lax.broadcasted_iota(jnp.int32, (w, 2 * w), 0)
    ki = lax.broadcasted_iota(jnp.int32, (w, 2 * w), 1)
    m = w + qi - ki
    band = (m >= 0) & (m <= w)
    for h in range(ATT_HEADS_PER_BRANCH):
        acc = jnp.zeros((w, 2 * w), F32)
        for c in range(REL_BUCKETS):
            acc = jnp.where(bucket == c, rb_ref[c, g * ATT_HEADS_PER_BRANCH + h], acc)
        full = jnp.where(band, acc, NEG_INF)
        o_ref[1, h] = full
        o_ref[0, h] = jnp.where(ki >= w, full, NEG_INF)


def _bias_tables(rel_bias):
    buckets, _ = _band_tables()
    g = len(ATT_BRANCHES)
    w = ATT_BLOCK
    return pl.pallas_call(
        _bias_kernel,
        grid=(g,),
        in_specs=[pl.BlockSpec(memory_space=pltpu.SMEM),
                  pl.BlockSpec((None, w, 2 * w), lambda i: (i, 0, 0))],
        out_specs=pl.BlockSpec((None, 2, ATT_HEADS_PER_BRANCH, w, 2 * w), lambda i: (i, 0, 0, 0, 0)),
        out_shape=jax.ShapeDtypeStruct((g, 2, ATT_HEADS_PER_BRANCH, w, 2 * w), F32),
        compiler_params=_cparams(("parallel",)),
        name="rel_bias_tables",
    )(rel_bias, jnp.asarray(buckets))


ATT_BLOCKS_PER_STEP = 16


def _attn_kernel(q_ref, kp_ref, kc_ref, vp_ref, vc_ref, bias_ref, o_ref, lse_ref):
    m = pl.program_id(2)
    w = ATT_BLOCK
    hb = ATT_HEADS_PER_BRANCH
    lane = lax.broadcasted_iota(jnp.int32, (w, ATT_BW), 1) // ATT_HEAD_DIM
    for res, blk in [(r, b) for r in range(q_ref.shape[1] // ATT_BW) for b in range(q_ref.shape[0] // w)]:
        cols = slice(res * ATT_BW, (res + 1) * ATT_BW)
        kall = jnp.concatenate([kp_ref[:, cols], kc_ref[:, cols]], axis=0)
        vall = jnp.concatenate([vp_ref[:, cols], vc_ref[:, cols]], axis=0)
        q = q_ref[blk * w:(blk + 1) * w, cols]
        q4 = jnp.concatenate([jnp.where(lane == h, q, jnp.zeros_like(q)) for h in range(hb)], axis=0)
        kk = kall[blk * w:(blk + 2) * w]
        vv = vall[blk * w:(blk + 2) * w]
        s4 = _dot_nt(q4, kk)
        bias = bias_ref[jnp.minimum(m, 1)] if blk == 0 else bias_ref[1]
        s4 = s4 + bias.reshape(hb * w, 2 * w)
        mx = jnp.max(s4, axis=-1, keepdims=True)
        p = jnp.exp(s4 - mx)
        l = jnp.sum(p, axis=-1, keepdims=True)
        o4 = _dot((p / l).astype(vv.dtype), vv)
        lse4 = mx + jnp.log(l)
        o = jnp.zeros((w, ATT_BW), F32)
        lse = jnp.zeros((w, ATT_BW), F32)
        for h in range(hb):
            o = jnp.where(lane == h, o4[h * w:(h + 1) * w], o)
            lse = jnp.where(lane == h, lse4[h * w:(h + 1) * w], lse)
        o_ref[blk * w:(blk + 1) * w, cols] = o.astype(o_ref.dtype)
        lse_ref[blk * w:(blk + 1) * w, cols] = lse


def _dilated_attention(q, k, v, bias_g, bsz, seq, dilation):
    w = ATT_BLOCK
    l = seq // dilation
    pstep = math.gcd(ATT_BLOCKS_PER_STEP, l // w)
    nb = l // (w * pstep)
    rstep = math.gcd(ATT_BLOCKS_PER_STEP // pstep, dilation)
    view = lambda a: a.reshape(bsz, l, dilation * ATT_BW)
    cur = pl.BlockSpec((None, pstep * w, rstep * ATT_BW), lambda b, r, n: (b, n, r))
    prev = pl.BlockSpec((None, w, rstep * ATT_BW), lambda b, r, n: (b, jnp.maximum(pstep * n - 1, 0), r))
    o, lse = pl.pallas_call(
        _attn_kernel,
        grid=(bsz, dilation // rstep, nb),
        in_specs=[cur, prev, cur, prev, cur,
                  pl.BlockSpec(bias_g.shape, lambda b, r, n: (0, 0, 0, 0))],
        out_specs=[cur, cur],
        out_shape=[jax.ShapeDtypeStruct((bsz, l, dilation * ATT_BW), BF16),
                   jax.ShapeDtypeStruct((bsz, l, dilation * ATT_BW), F32)],
        compiler_params=_cparams(("parallel", "parallel", "arbitrary")),
        name=f"dilated_attention_d{dilation}",
    )(view(q), view(k), view(k), view(v), view(v), bias_g)
    return o.reshape(bsz * l, dilation * ATT_BW), lse.reshape(bsz * l, dilation * ATT_BW)


H2P_CHUNKS = 4


def _token_order(ref, scr, dil):
    if dil == 1:
        return ref[...].astype(F32)
    n = ref.shape[0]
    halves = ATT_BW // LANES
    for r in range(dil):
        for half in range(halves):
            c0 = r * ATT_BW + half * LANES
            scr[half, pl.ds(r, n, stride=dil), :] = ref[:, c0:c0 + LANES].astype(F32)
    return jnp.concatenate([scr[half] for half in range(halves)], axis=1)


def _outproj_kernel(yhg_ref, o1_ref, o2_ref, o3_ref, l1_ref, l2_ref, l3_ref, x_ref,
                    g1_ref, sc2_ref, sh2_ref, wout_ref, lng_ref, lnb_ref, wrt_ref,
                    x1_ref, h2_ref, h2p_ref, lgt_ref, *scratch):
    dils = [dil for _, dil in ATT_BRANCHES]
    scr = iter(scratch)
    o1, o2, o3 = [_token_order(r, None if dil == 1 else next(scr), dil)
                  for r, dil in zip((o1_ref, o2_ref, o3_ref), dils)]
    l1, l2, l3 = [_token_order(r, None if dil == 1 else next(scr), dil)
                  for r, dil in zip((l1_ref, l2_ref, l3_ref), dils)]
    mx = jnp.maximum(jnp.maximum(l1, l2), l3)
    e1, e2, e3 = jnp.exp(l1 - mx), jnp.exp(l2 - mx), jnp.exp(l3 - mx)
    den = e1 + e2 + e3
    att = (e1 / den) * o1 + (e2 / den) * o2 + (e3 / den) * o3
    mix = _dot(yhg_ref[...], wout_ref[:HG_WIDTH, :]) + _dot(att.astype(BF16), wout_ref[HG_WIDTH:, :])
    x1 = _ln_rows(DN_ALPHA * x_ref[...] + g1_ref[...] * mix) * lng_ref[...] + lnb_ref[...]
    x1_ref[...] = x1
    h2 = _ln_rows(x1) * (1.0 + sc2_ref[...]) + sh2_ref[...]
    h_hi = h2.astype(BF16)
    h_hf = h_hi.astype(F32)
    h2_ref[...] = h_hi
    bits = lax.bitcast_convert_type(h_hf, jnp.uint32)
    for cidx in range(H2P_CHUNKS):
        lo = bits[:, 2 * LANES * cidx:2 * LANES * cidx + LANES]
        hi = bits[:, 2 * LANES * cidx + LANES:2 * LANES * (cidx + 1)]
        h2p_ref[pl.ds(cidx, h2.shape[0], stride=H2P_CHUNKS), :] = lax.bitcast_convert_type((lo >> 16) | hi,
                                                                                          jnp.int32)
    lgt_ref[...] = _dot_nt(wrt_ref[...].astype(BF16), h_hi)


def _out_projection(yhg, os_, ls_, x2, mod4, w_out_bf, ln_g, ln_b, w_router_t, seq, tm):
    t, d = x2.shape
    spb = seq // tm
    ne = w_router_t.shape[0]
    row = lambda w: pl.BlockSpec((tm, w), lambda i: (i, 0))
    mod_spec = lambda r: pl.BlockSpec((None, None, 1, d), lambda i, r=r: (i // spb, r, 0, 0))
    full = lambda a: pl.BlockSpec(a.shape, lambda i: (0,) * a.ndim)
    ln_g2, ln_b2 = ln_g.reshape(1, d), ln_b.reshape(1, d)
    dils = [dil for _, dil in ATT_BRANCHES]
    branch = [pl.BlockSpec((tm // dil, dil * ATT_BW), lambda i: (i, 0)) for dil in dils]
    return pl.pallas_call(
        _outproj_kernel,
        grid=(t // tm,),
        in_specs=[row(HG_WIDTH)] + branch + branch + [row(d),
                  mod_spec(2), mod_spec(4), mod_spec(3),
                  full(w_out_bf), full(ln_g2), full(ln_b2), full(w_router_t)],
        out_specs=[row(d), row(d), pl.BlockSpec((tm * H2P_CHUNKS, LANES), lambda i: (i, 0)),
                   pl.BlockSpec((ne, tm), lambda i: (0, i))],
        out_shape=[jax.ShapeDtypeStruct((t, d), F32), jax.ShapeDtypeStruct((t, d), BF16),
                   jax.ShapeDtypeStruct((t * H2P_CHUNKS, LANES), jnp.int32),
                   jax.ShapeDtypeStruct((ne, t), F32)],
        scratch_shapes=[pltpu.VMEM((ATT_BW // LANES, tm, LANES), F32) for dil in dils + dils if dil > 1],
        compiler_params=_cparams(("parallel",)),
        name="merge_outproj_ln",
    )(yhg, *os_, *ls_, x2, mod4, mod4, mod4, w_out_bf, ln_g2, ln_b2, w_router_t)


def _argmax_rows(cur, iota, nrows):
    m = jnp.max(cur, axis=0, keepdims=True)
    idx = jnp.min(jnp.where(cur == m, iota, nrows), axis=0, keepdims=True)
    return m, idx, iota == idx


def _route_kernel(lgt_ref, rb_ref, eidx_ref, w_ref, rank_ref, cnt_ref, carry):
    ne = N_EXPERTS
    gsz = ne // N_GROUPS
    tt = lgt_ref.shape[1]

    @pl.when(pl.program_id(0) == 0)
    def _():
        carry[...] = jnp.zeros_like(carry)

    sc = jax.nn.sigmoid(lgt_ref[...])
    biased = sc + rb_ref[...]
    g3 = biased.reshape(N_GROUPS, gsz, tt)
    io3 = lax.broadcasted_iota(jnp.int32, (N_GROUPS, gsz, tt), 1)
    m1 = jnp.max(g3, axis=1, keepdims=True)
    first = jnp.min(jnp.where(g3 == m1, io3, gsz), axis=1, keepdims=True)
    m2 = jnp.max(jnp.where(io3 == first, NEG_INF, g3), axis=1, keepdims=True)
    gs = (m1 + m2).reshape(N_GROUPS, tt)
    io8 = lax.broadcasted_iota(jnp.int32, (N_GROUPS, tt), 0)
    sel = jnp.zeros((N_GROUPS, tt), jnp.int32)
    cur = gs
    for _ in range(TOPK_GROUPS):
        _, _, pick = _argmax_rows(cur, io8, N_GROUPS)
        sel = jnp.where(pick, 1, sel)
        cur = jnp.where(pick, NEG_INF, cur)
    masked = jnp.where(sel.reshape(N_GROUPS, 1, tt) > 0, g3, NEG_INF).reshape(ne, tt)
    ioe = lax.broadcasted_iota(jnp.int32, (ne, tt), 0)
    cur = masked
    idxs, ws, picks = [], [], []
    for _ in range(TOP_K):
        _, idx, pick = _argmax_rows(cur, ioe, ne)
        idxs.append(idx)
        picks.append(pick)
        ws.append(jnp.sum(jnp.where(pick, sc, 0.0), axis=0, keepdims=True))
        cur = jnp.where(pick, NEG_INF, cur)
    wk = jnp.concatenate(ws, axis=0)
    eidx_ref[...] = jnp.concatenate(idxs, axis=0)
    w_ref[...] = wk / jnp.sum(wk, axis=0, keepdims=True) * ROUTED_SCALE
    chosen = jnp.where(cur == NEG_INF, jnp.where(masked == NEG_INF, 0.0, 1.0), 0.0)
    r_i = lax.broadcasted_iota(jnp.int32, (tt, tt), 0)
    c_i = lax.broadcasted_iota(jnp.int32, (tt, tt), 1)
    before = jnp.where(r_i < c_i, 1.0, 0.0).astype(BF16)
    pref = _dot(chosen.astype(BF16), before) + carry[...]
    rank_ref[...] = jnp.concatenate(
        [jnp.sum(jnp.where(p, pref, 0.0), axis=0, keepdims=True) for p in picks], axis=0).astype(jnp.int32)
    carry[...] = carry[...] + jnp.sum(chosen, axis=1, keepdims=True)
    cnt_ref[...] = carry[...]


def _route(logits_t, router_bias, tt):
    ne, t = logits_t.shape
    tok = pl.BlockSpec((TOP_K, tt), lambda i: (0, i))
    return pl.pallas_call(
        _route_kernel,
        grid=(t // tt,),
        in_specs=[pl.BlockSpec((ne, tt), lambda i: (0, i)),
                  pl.BlockSpec((ne, 1), lambda i: (0, 0))],
        out_specs=[tok, tok, tok, pl.BlockSpec((ne, 1), lambda i: (0, 0))],
        out_shape=[jax.ShapeDtypeStruct((TOP_K, t), jnp.int32), jax.ShapeDtypeStruct((TOP_K, t), F32),
                   jax.ShapeDtypeStruct((TOP_K, t), jnp.int32), jax.ShapeDtypeStruct((ne, 1), F32)],
        scratch_shapes=[pltpu.VMEM((ne, 1), F32)],
        compiler_params=_cparams(("arbitrary",)),
        name="router_topk",
    )(logits_t, router_bias.reshape(ne, 1))


def _plan_kernel(cnt_ref, eidx_ref, rank_ref, dest_ref, seq_ref, dexp_ref, meta_ref, pstart_ref):
    ne = N_EXPERTS
    tt = eidx_ref.shape[1]
    nblk = seq_ref.shape[1]

    @pl.when(pl.program_id(0) == 0)
    def _():
        cnt = cnt_ref[...].astype(jnp.int32)
        padded = ((cnt + (MOE_BLOCK - 1)) // MOE_BLOCK) * MOE_BLOCK
        r_i = lax.broadcasted_iota(jnp.int32, (ne, ne), 0)
        c_i = lax.broadcasted_iota(jnp.int32, (ne, ne), 1)
        incl = jnp.where(c_i <= r_i, 1.0, 0.0)
        pend = jnp.dot(incl, jnp.broadcast_to(padded.astype(F32), (ne, LANES)),
                       preferred_element_type=F32, precision=lax.Precision.HIGHEST)[:, 0:1]
        pend = pend.astype(jnp.int32)
        pstart_ref[...] = pend - padded
        blk0 = lax.broadcasted_iota(jnp.int32, (ne, nblk), 1) * MOE_BLOCK
        be = jnp.minimum(jnp.sum(jnp.where(pend <= blk0, 1, 0), axis=0, keepdims=True), ne - 1)
        present = cnt > 0
        strict = jnp.where(c_i < r_i, 1.0, 0.0).astype(BF16)
        sidx = _dot(strict, jnp.broadcast_to(jnp.where(present, 1.0, 0.0), (ne, LANES)).astype(BF16))[:, 0:1]
        sidx = sidx.astype(jnp.int32)
        dexp_ref[...] = jnp.sum(jnp.where(jnp.logical_and(present, sidx == c_i), r_i, 0), axis=0, keepdims=True)
        ioeb = lax.broadcasted_iota(jnp.int32, (ne, nblk), 0)
        nu = jnp.max(pend, axis=0, keepdims=True) // MOE_BLOCK
        nd = jnp.sum(jnp.where(present, 1, 0), axis=0, keepdims=True)
        seq_ref[...] = jnp.minimum(jnp.sum(jnp.where(ioeb == be, sidx, 0), axis=0, keepdims=True), nd - 1)
        lane = lax.broadcasted_iota(jnp.int32, (1, LANES), 1)
        meta_ref[...] = jnp.where(lane == 0, nu, jnp.where(lane == 1, nd, 0))

    pstart = pstart_ref[...]
    ioe = lax.broadcasted_iota(jnp.int32, (ne, tt), 0)
    rows = []
    for k in range(TOP_K):
        sel = ioe == eidx_ref[k:k + 1, :]
        rows.append(jnp.sum(jnp.where(sel, pstart, 0), axis=0, keepdims=True))
    dest_ref[...] = jnp.concatenate(rows, axis=0) + rank_ref[...]


def _dispatch_plan(cnt, eidx, rank, tt):
    k, t = eidx.shape
    n_blocks = -(-(t * k) // MOE_BLOCK) + N_EXPERTS
    tok = pl.BlockSpec((k, tt), lambda i: (0, i))
    one = lambda n: pl.BlockSpec((1, n), lambda i: (0, 0))
    dest, seq, dexp, meta = pl.pallas_call(
        _plan_kernel,
        grid=(t // tt,),
        in_specs=[pl.BlockSpec(cnt.shape, lambda i: (0, 0)), tok, tok],
        out_specs=[tok, one(n_blocks), one(N_EXPERTS), one(LANES)],
        out_shape=[jax.ShapeDtypeStruct((k, t), jnp.int32), jax.ShapeDtypeStruct((1, n_blocks), jnp.int32),
                   jax.ShapeDtypeStruct((1, N_EXPERTS), jnp.int32), jax.ShapeDtypeStruct((1, LANES), jnp.int32)],
        scratch_shapes=[pltpu.VMEM((N_EXPERTS, 1), jnp.int32)],
        compiler_params=_cparams(("arbitrary",)),
        name="dispatch_plan",
    )(cnt, eidx, rank)
    return dest, seq.reshape(n_blocks), dexp.reshape(N_EXPERTS), meta.reshape(LANES), n_blocks


SC_CORES = 2
SC_SUBCORES = 16
SC_LANES = 16
SC_CHUNK = 16384
SC_UNROLL = 4


def _row_tables(dest, wk, n_rows):
    k, t = dest.shape
    a = k * t
    per_w = n_rows // SC_SUBCORES
    assert n_rows % (SC_SUBCORES * SC_LANES) == 0 and t % SC_CHUNK == 0 and SC_CORES == 2
    mesh = plsc.VectorSubcoreMesh(core_axis_name="c", subcore_axis_name="s")
    cp = pltpu.CompilerParams()
    if "needs_layout_passes" in pltpu.CompilerParams.__dataclass_fields__:
        cp = dataclasses.replace(cp, needs_layout_passes=False)

    n_chunks = a // SC_CHUNK
    assert n_chunks % 2 == 0

    def body(dest_hbm, wbit_hbm, tok_out, w_out, dbuf_0, dbuf_1, wbuf_0, wbuf_1, tloc,
             sem_0, sem_1, wsem_0, wsem_1):
        base = lax.axis_index("s") * per_w
        role = lax.axis_index("c")
        bufs = ((dbuf_0, wbuf_0, sem_0, wsem_0), (dbuf_1, wbuf_1, sem_1, wsem_1))

        @pl.loop(0, per_w // SC_LANES)
        def _(i):
            tloc[pl.ds(i * SC_LANES, SC_LANES)] = jnp.zeros((SC_LANES,), jnp.int32)

        lane = lax.iota(jnp.int32, SC_LANES)

        def run(weights, out_ref):
            def chunk_copies(c, slot):
                dbuf, wbuf, sem, wsem = bufs[slot]
                rng = pl.ds(c * SC_CHUNK, SC_CHUNK)
                copies = [pltpu.make_async_copy(dest_hbm.at[rng], dbuf, sem)]
                if weights:
                    copies.append(pltpu.make_async_copy(wbit_hbm.at[rng], wbuf, wsem))
                return copies

            def start(c, slot):
                for cp_ in chunk_copies(c, slot):
                    cp_.start()

            def wait(c, slot):
                for cp_ in chunk_copies(c, slot):
                    cp_.wait()

            def scan(c, slot):
                dbuf, wbuf = bufs[slot][:2]
                tok0 = lax.rem(c * SC_CHUNK, t)

                @pl.loop(0, SC_CHUNK // (SC_LANES * SC_UNROLL))
                def _(j):
                    for u in range(SC_UNROLL):
                        off = (j * SC_UNROLL + u) * SC_LANES
                        loc = dbuf[pl.ds(off, SC_LANES)] - base
                        mine = jnp.logical_and(loc >= 0, loc < per_w)
                        loc = jnp.where(mine, loc, 0)
                        val = wbuf[pl.ds(off, SC_LANES)] if weights else tok0 + off + lane
                        plsc.store_scatter(tloc, [loc], val, mask=mine)

            start(0, 0)

            @pl.loop(0, n_chunks // 2)
            def _(i):
                c0 = 2 * i
                start(c0 + 1, 1)
                wait(c0, 0)
                scan(c0, 0)
                start(jnp.minimum(c0 + 2, n_chunks - 1), 0)
                wait(c0 + 1, 1)
                scan(c0 + 1, 1)

            wait(n_chunks - 1, 0)
            pltpu.sync_copy(tloc, out_ref.at[pl.ds(base, per_w)])

        @pl.when(role == 0)
        def _():
            run(False, tok_out)

        @pl.when(role == 1)
        def _():
            run(True, w_out)

    fn = pl.kernel(
        body,
        out_type=[jax.ShapeDtypeStruct((n_rows,), jnp.int32), jax.ShapeDtypeStruct((n_rows,), jnp.int32)],
        mesh=mesh,
        scratch_types=[pltpu.VMEM((SC_CHUNK,), jnp.int32)] * 4 + [pltpu.VMEM((per_w,), jnp.int32)]
        + [pltpu.SemaphoreType.DMA] * 4,
        compiler_params=cp,
        name="row_tables",
    )
    row_token, row_wbits = fn(dest.reshape(a), lax.bitcast_convert_type(wk, jnp.int32).reshape(a))
    return row_token, lax.bitcast_convert_type(row_wbits, F32)


ROW_TILE = (SUBLANES, LANES)


FFN_GROUP = 4
PAIR_ROWS = 2 * MOE_BLOCK
FFN_CHAIN = 128
W_SETS = 3
GATHER_BATCH = 32


def _gather_rows(h2p_ref, tok_ref, row, buf, base):
    for j0 in range(0, MOE_BLOCK, GATHER_BATCH):
        vals = [h2p_ref[pl.ds(pl.multiple_of(tok_ref[row, j] * H2P_CHUNKS, H2P_CHUNKS), H2P_CHUNKS), :]
                for j in range(j0, j0 + GATHER_BATCH)]
        for j, v in zip(range(j0, j0 + GATHER_BATCH), vals):
            buf[pl.ds(H2P_CHUNKS * (base + j), H2P_CHUNKS), :] = v


def _expert_rows(buf, chains, out):
    hms = []
    for row0, nrows, (wg_c, wu_c, _), _, _ in chains:
        parts = []
        for cidx in range(H2P_CHUNKS):
            word = buf[pl.ds(H2P_CHUNKS * row0 + cidx, nrows, stride=H2P_CHUNKS), :]
            parts.append(lax.bitcast_convert_type(word << 16, F32))
            parts.append(lax.bitcast_convert_type(word & jnp.int32(-65536), F32))
        x = jnp.concatenate(parts, axis=1)
        g = _dot(x, wg_c[...])
        u = _dot(x, wu_c[...])
        hms.append((g * jax.nn.sigmoid(g)) * u)
    for (_, nrows, (_, _, wd_c), out_row0, wcol), hm in zip(chains, hms):
        res = _dot(hm, wd_c[...])
        nch = res.shape[1] // LANES
        for c in range(nch):
            out[pl.ds(nch * out_row0 + c, nrows, stride=nch), :] = res[:, c * LANES:(c + 1) * LANES] * wcol


def _ffn_kernel(seq_ref, dexp_ref, meta_ref, tokc_ref, tokn_ref, roww_ref, h2p_ref, wg_hbm, wu_hbm, wd_hbm,
                y_ref, buf_0, buf_1, *rest):
    i = pl.program_id(0)
    n_used = meta_ref[0]
    n_exp = meta_ref[1]
    nblk = seq_ref.shape[0]
    wsets = tuple(tuple(rest[3 * n:3 * n + 3]) for n in range(W_SETS))
    started_ref, sems = rest[3 * W_SETS:]
    bufs = (buf_0, buf_1)

    def weight_copies(s, par):
        e = dexp_ref[s]
        return [pltpu.make_async_copy(src.at[e], dst, sems.at[par, n])
                for n, (src, dst) in enumerate(zip((wg_hbm, wu_hbm, wd_hbm), wsets[par]))]

    def start_expert(s):
        for par in range(W_SETS):
            @pl.when(s % W_SETS == par)
            def _():
                for cp in weight_copies(s, par):
                    cp.start()

    @pl.when(i == 0)
    def _():
        _gather_rows(h2p_ref, tokc_ref, 0, buf_0, 0)
        _gather_rows(h2p_ref, tokc_ref, 1, buf_0, MOE_BLOCK)
        start_expert(0)
        started_ref[0] = 0

    for p in range(FFN_GROUP // 2):
        b_a = i * FFN_GROUP + 2 * p
        used = b_a < n_used
        s_a = seq_ref[jnp.minimum(b_a, nblk - 1)]
        s_b = seq_ref[jnp.minimum(b_a + 1, nblk - 1)]
        first_a = jnp.logical_or(b_a == 0, s_a != seq_ref[jnp.clip(b_a - 1, 0, nblk - 1)])
        same = s_a == s_b
        cur, nxt = bufs[p], bufs[1 - p]
        out_row0 = p * PAIR_ROWS

        def gate_col(r0):
            blk = 2 * p + r0 // MOE_BLOCK
            return jnp.broadcast_to(roww_ref[blk:blk + 1, :], (MOE_BLOCK, MOE_BLOCK)).T

        def gather_next():
            if p == 0:
                _gather_rows(h2p_ref, tokc_ref, 2, nxt, 0)
                _gather_rows(h2p_ref, tokc_ref, 3, nxt, MOE_BLOCK)
            else:
                _gather_rows(h2p_ref, tokn_ref, 0, nxt, 0)
                _gather_rows(h2p_ref, tokn_ref, 1, nxt, MOE_BLOCK)

        @pl.when(used)
        def _():
            started = started_ref[0]
            limit = jnp.minimum(s_a + (W_SETS - 1), n_exp - 1)
            for _unused in range(W_SETS - 1):
                go = started < limit

                @pl.when(go)
                def _():
                    start_expert(started + 1)
                started = jnp.where(go, started + 1, started)
            started_ref[0] = started

        for par in range(W_SETS):
            @pl.when(jnp.logical_and(used, jnp.logical_and(same, s_a % W_SETS == par)))
            def _():
                @pl.when(first_a)
                def _():
                    for cp in weight_copies(s_a, par):
                        cp.wait()
                gather_next()
                _expert_rows(cur, [(r0, FFN_CHAIN, wsets[par], out_row0 + r0, gate_col(r0))
                                   for r0 in range(0, PAIR_ROWS, FFN_CHAIN)], y_ref)

            @pl.when(jnp.logical_and(used, jnp.logical_and(jnp.logical_not(same), s_a % W_SETS == par)))
            def _():
                @pl.when(first_a)
                def _():
                    for cp in weight_copies(s_a, par):
                        cp.wait()
                for cp in weight_copies(s_b, (par + 1) % W_SETS):
                    cp.wait()
                gather_next()
                _expert_rows(cur, [(r0, FFN_CHAIN, wsets[(par + r0 // MOE_BLOCK) % W_SETS], out_row0 + r0,
                                    gate_col(r0)) for r0 in range(0, PAIR_ROWS, FFN_CHAIN)], y_ref)

        @pl.when(jnp.logical_not(used))
        def _():
            y_ref[pl.ds(out_row0 * SUBLANES, PAIR_ROWS * SUBLANES), :] = jnp.zeros(
                (PAIR_ROWS * SUBLANES, LANES), y_ref.dtype)


def _routed_experts(h2p, row_token, row_weight, seq, dexp, meta, n_blocks, wg, wu, wd):
    d = wg.shape[1]
    de = wg.shape[2]
    ng = n_blocks // FFN_GROUP
    assert n_blocks % FFN_GROUP == 0 and FFN_GROUP == 4 and d == SUBLANES * LANES
    tok3 = row_token.reshape(ng, FFN_GROUP, MOE_BLOCK)
    w3 = row_weight.reshape(ng, FFN_GROUP, MOE_BLOCK)
    idle_step = lambda m: jnp.minimum((m[0] + FFN_GROUP - 1) // FFN_GROUP, ng - 1)
    smem = lambda imap: pl.BlockSpec((None, FFN_GROUP, MOE_BLOCK), imap, memory_space=pltpu.SMEM)
    grid_spec = pltpu.PrefetchScalarGridSpec(
        num_scalar_prefetch=3,
        grid=(ng,),
        in_specs=[
            smem(lambda i, sq, dx, m: (jnp.minimum(i, ng - 1), 0, 0)),
            smem(lambda i, sq, dx, m: (jnp.minimum(i + 1, ng - 1), 0, 0)),
            pl.BlockSpec((None, FFN_GROUP, MOE_BLOCK), lambda i, sq, dx, m: (i, 0, 0)),
            pl.BlockSpec(h2p.shape, lambda i, sq, dx, m: (0, 0), pipeline_mode=pl.Buffered(1)),
            pl.BlockSpec(memory_space=pl.ANY),
            pl.BlockSpec(memory_space=pl.ANY),
            pl.BlockSpec(memory_space=pl.ANY),
        ],
        out_specs=pl.BlockSpec((FFN_GROUP * MOE_BLOCK * SUBLANES, LANES),
                               lambda i, sq, dx, m: (jnp.minimum(i, idle_step(m)), 0)),
        scratch_shapes=[pltpu.VMEM((PAIR_ROWS * H2P_CHUNKS, LANES), jnp.int32)] * 2 + [
            pltpu.VMEM((d, de), F32), pltpu.VMEM((d, de), F32), pltpu.VMEM((de, d), F32)] * W_SETS + [
            pltpu.SMEM((1,), jnp.int32), pltpu.SemaphoreType.DMA((W_SETS, 3))],
    )
    y = pl.pallas_call(
        _ffn_kernel,
        grid_spec=grid_spec,
        out_shape=jax.ShapeDtypeStruct((n_blocks * MOE_BLOCK * SUBLANES, LANES), F32),
        compiler_params=_cparams(("arbitrary",)),
        name="routed_experts",
    )(seq, dexp, meta, tok3, tok3, w3, h2p, wg, wu, wd)
    return y.reshape((n_blocks * MOE_BLOCK,) + ROW_TILE)


COMBINE_TOKENS = 128


def _row_copies(src_hbm, idx_ref, buf, sem):
    return [pltpu.make_async_copy(src_hbm.at[idx_ref[k, j]], buf.at[k, j], sem)
            for k in range(TOP_K) for j in range(COMBINE_TOKENS)]


def _shared_kernel(h_ref, wsg_ref, wsu_ref, wsd_ref, o_ref):
    hb = h_ref[...]
    g = _dot(hb, wsg_ref[...])
    u = _dot(hb, wsu_ref[...])
    o_ref[...] = _dot(((g * jax.nn.sigmoid(g)) * u).astype(BF16), wsd_ref[...])


def _shared_expert(h2b, wsg, wsu, wsd, tm):
    t, d = h2b.shape
    row = pl.BlockSpec((tm, d), lambda i: (i, 0))
    full = lambda a: pl.BlockSpec(a.shape, lambda i: (0,) * a.ndim)
    return pl.pallas_call(
        _shared_kernel,
        grid=(t // tm,),
        in_specs=[row, full(wsg), full(wsu), full(wsd)],
        out_specs=row,
        out_shape=jax.ShapeDtypeStruct((t, d), F32),
        compiler_params=_cparams(("parallel",)),
        name="shared_expert",
    )(h2b, wsg, wsu, wsd)


def _final_kernel(dc_ref, dn_ref, x1_ref, sh_ref, g2_ref, lng_ref, lnb_ref,
                  y_hbm, o_ref, ybuf, sems):
    i = pl.program_id(0)
    nsteps = pl.num_programs(0)
    slot = i % 2

    def issue(d_ref, s):
        for n, cp in enumerate(_row_copies(y_hbm, d_ref, ybuf.at[s], sems.at[s])):
            cp.start(priority=n % 2)

    @pl.when(i == 0)
    def _():
        issue(dc_ref, 0)

    @pl.when(i + 1 < nsteps)
    def _():
        issue(dn_ref, 1 - slot)

    for cp in _row_copies(y_hbm, dc_ref, ybuf.at[slot], sems.at[slot]):
        cp.wait()
    rows = []
    for j in range(COMBINE_TOKENS):
        acc = ybuf[slot, 0, j].astype(F32)
        for k in range(1, TOP_K):
            acc = acc + ybuf[slot, k, j].astype(F32)
        rows.append(acc)
    routed = pltpu.einshape("tcl->t(cl)", jnp.stack(rows, axis=0))
    x2 = DN_ALPHA * x1_ref[...] + g2_ref[...] * (routed + sh_ref[...])
    o_ref[...] = _ln_rows(x2) * lng_ref[...] + lnb_ref[...]


def _combine_final(x1, shared, mod4, dest, y_rows, ln_g, ln_b, seq):
    t, d = x1.shape
    tt = COMBINE_TOKENS
    nsteps = t // tt
    spb = seq // tt
    row = pl.BlockSpec((tt, d), lambda i: (i, 0))
    full = lambda a: pl.BlockSpec(a.shape, lambda i: (0,) * a.ndim)
    ln_g2, ln_b2 = ln_g.reshape(1, d), ln_b.reshape(1, d)
    return pl.pallas_call(
        _final_kernel,
        grid=(nsteps,),
        in_specs=[pl.BlockSpec((TOP_K, tt), lambda i: (0, i), memory_space=pltpu.SMEM),
                  pl.BlockSpec((TOP_K, tt), lambda i: (0, jnp.minimum(i + 1, nsteps - 1)),
                               memory_space=pltpu.SMEM),
                  row, row,
                  pl.BlockSpec((None, None, 1, d), lambda i: (i // spb, 5, 0, 0)),
                  full(ln_g2), full(ln_b2),
                  pl.BlockSpec(memory_space=pl.ANY)],
        out_specs=row,
        out_shape=jax.ShapeDtypeStruct((t, d), F32),
        scratch_shapes=[pltpu.VMEM((2, TOP_K, tt) + ROW_TILE, y_rows.dtype),
                        pltpu.SemaphoreType.DMA((2,))],
        compiler_params=_cparams(("arbitrary",)),
        name="shared_combine_ln",
    )(dest, dest, x1, shared, mod4, ln_g2, ln_b2, y_rows)


def kernel(x, c, w_ada, b_ada, w_in, hg_lower_bound, hg_norm_w, rel_bias, w_out, ln1_g, ln1_b, w_router,
           router_bias, w_e_gate, w_e_up, w_e_down, w_sh_gate, w_sh_up, w_sh_down, ln2_g, ln2_b):
    bsz, seq, d = x.shape
    t = bsz * seq
    assert w_ada.shape[0] == DEPTH and seq % (ATT_BRANCHES[-1][0]) == 0
    x2 = x.reshape(t, d)
    bias = _bias_tables(rel_bias)
    for l in range(DEPTH):
        mod4 = _modulation(c, w_ada[l], b_ada[l]).reshape(bsz, 6, 1, d)
        w_in_bf = _cast_bf16(w_in[l], 256)
        hq, hf, hi, hg, *qkv = _in_projection(x2, mod4, w_in_bf, seq, 512)
        nbr = len(ATT_BRANCHES)
        y_hg = _hgrn2(hq, hf, hi, hg, hg_lower_bound, hg_norm_w[l], bsz, seq)
        os_, ls_ = [], []
        for g, (_, dil) in enumerate(ATT_BRANCHES):
            o, lse = _dilated_attention(qkv[g], qkv[nbr + g], qkv[2 * nbr + g], bias[g], bsz, seq, dil)
            os_.append(o)
            ls_.append(lse)
        w_out_bf = _cast_bf16(w_out[l], 256)
        x1, h2b, h2p, logits_t = _out_projection(y_hg, os_, ls_, x2, mod4, w_out_bf, ln1_g[l], ln1_b[l],
                                                 w_router[l].T, seq, 512)
        eidx, wk, rank, cnt = _route(logits_t, router_bias[l], 512)
        dest, blk_seq, dexp, meta, n_blocks = _dispatch_plan(cnt, eidx, rank, 2048)
        row_token, row_weight = _row_tables(dest, wk, n_blocks * MOE_BLOCK)
        shared = _shared_expert(h2b, _cast_bf16(w_sh_gate[l], 256), _cast_bf16(w_sh_up[l], 256),
                                _cast_bf16(w_sh_down[l], 256), 1024)
        y_rows = _routed_experts(h2p, row_token, row_weight, blk_seq, dexp, meta, n_blocks,
                                 w_e_gate[l], w_e_up[l], w_e_down[l])
        x2 = _combine_final(x1, shared, mod4, dest, y_rows, ln2_g[l], ln2_b[l], seq)
    return x2.reshape(bsz, seq, d)
```

```python
import dataclasses
import math

import jax
import jax.numpy as jnp
import numpy as np
from jax import lax
from jax.experimental import pallas as pl
from jax.experimental.pallas import tpu as pltpu
from jax.experimental.pallas import tpu_sc as plsc

HG_HEADS = 4
HG_DK = 128
HG_WIDTH = HG_HEADS * HG_DK
ATT_BRANCHES = ((128, 1), (512, 4), (2048, 16))
ATT_HEADS_PER_BRANCH = 4
ATT_HEAD_DIM = 64
ATT_BW = ATT_HEADS_PER_BRANCH * ATT_HEAD_DIM
ATT_BLOCK = 128
REL_BUCKETS = 32
REL_MAX_DIST = 2048
N_EXPERTS = 256
TOP_K = 8
N_GROUPS = 8
TOPK_GROUPS = 4
ROUTED_SCALE = 2.5
MOE_BLOCK = 128
DEPTH = 1
DN_ALPHA = (2 * DEPTH) ** 0.25
LN_EPS = 1e-5
RMS_EPS = 1e-6

LANES = 128
SUBLANES = 8
VMEM_LIMIT_BYTES = 56 * 1024 * 1024

F32 = jnp.float32
BF16 = jnp.bfloat16
NEG_INF = float("-inf")


def _cparams(sem):
    return pltpu.CompilerParams(dimension_semantics=sem, vmem_limit_bytes=VMEM_LIMIT_BYTES)


def _ln_rows(x):
    mu = jnp.mean(x, axis=-1, keepdims=True)
    xc = x - mu
    var = jnp.mean(xc * xc, axis=-1, keepdims=True)
    return xc * lax.rsqrt(var + LN_EPS)


def _dot(a, b):
    return jnp.dot(a, b, preferred_element_type=F32)


def _dot_nt(a, b):
    return lax.dot_general(a, b, (((1,), (1,)), ((), ())), preferred_element_type=F32)


def _dot_tn(a, b):
    return lax.dot_general(a, b, (((0,), (0,)), ((), ())), preferred_element_type=F32)


def _cast_kernel(w_ref, o_ref):
    o_ref[...] = w_ref[...].astype(o_ref.dtype)


def _cast_bf16(w, rows_per_step):
    r, c = w.shape
    return pl.pallas_call(
        _cast_kernel,
        grid=(r // rows_per_step,),
        in_specs=[pl.BlockSpec((rows_per_step, c), lambda i: (i, 0))],
        out_specs=pl.BlockSpec((rows_per_step, c), lambda i: (i, 0)),
        out_shape=jax.ShapeDtypeStruct((r, c), BF16),
        compiler_params=_cparams(("parallel",)),
        name="cast_bf16",
    )(w)


def _mod_kernel(c_ref, w_ref, b_ref, o_ref):
    c = c_ref[...]
    cond = c * jax.nn.sigmoid(c)
    o_ref[...] = jnp.dot(cond, w_ref[...], preferred_element_type=F32,
                         precision=lax.Precision.HIGHEST) + b_ref[...]


def _modulation(c, w_ada, b_ada):
    bsz, d = c.shape
    n = w_ada.shape[1]
    rows = -(-bsz // SUBLANES) * SUBLANES
    cpad = jnp.zeros((rows, d), F32).at[:bsz].set(c)
    tn = 1024
    out = pl.pallas_call(
        _mod_kernel,
        grid=(n // tn,),
        in_specs=[pl.BlockSpec((rows, d), lambda j: (0, 0)),
                  pl.BlockSpec((d, tn), lambda j: (0, j)),
                  pl.BlockSpec((1, tn), lambda j: (0, j))],
        out_specs=pl.BlockSpec((rows, tn), lambda j: (0, j)),
        out_shape=jax.ShapeDtypeStruct((rows, n), F32),
        compiler_params=_cparams(("parallel",)),
        name="adaln_modulation",
    )(cpad, w_ada, b_ada.reshape(1, n))
    return out[:bsz]


_IN_HG = 4
_IN_ATT = 3 * len(ATT_BRANCHES)


def _inproj_kernel(x_ref, sc_ref, sh_ref, w_ref, *refs):
    outs = refs[:_IN_HG + _IN_ATT]
    scratch = refs[_IN_HG + _IN_ATT:]
    x = x_ref[...]
    h = _ln_rows(x) * (1.0 + sc_ref[...]) + sh_ref[...]
    hb = h.astype(BF16)
    tm = x.shape[0]
    col = 0
    n_scr = 0
    for k, o_ref in enumerate(outs):
        width = HG_WIDTH if k < _IN_HG else ATT_BW
        y = _dot(hb, w_ref[:, col:col + width])
        col += width
        if k < _IN_HG:
            o_ref[...] = y.astype(o_ref.dtype)
            continue
        if k < _IN_HG + len(ATT_BRANCHES):
            y = y * (ATT_HEAD_DIM ** -0.5)
        dil = ATT_BRANCHES[(k - _IN_HG) % len(ATT_BRANCHES)][1]
        if dil == 1:
            o_ref[...] = y.astype(o_ref.dtype)
            continue
        scr = scratch[n_scr]
        n_scr += 1
        for half in range(ATT_BW // LANES):
            scr[half] = y[:, half * LANES:(half + 1) * LANES]
        for r in range(dil):
            for half in range(ATT_BW // LANES):
                c0 = r * ATT_BW + half * LANES
                o_ref[:, c0:c0 + LANES] = scr[half, pl.ds(r, tm // dil, stride=dil), :].astype(o_ref.dtype)


def _in_projection(x2, mod4, w_in_bf, seq, tm):
    t, d = x2.shape
    steps_per_batch = seq // tm
    dils = [dil for _, dil in ATT_BRANCHES] * 3
    shapes = [(t, HG_WIDTH)] * _IN_HG + [(t // dil, dil * ATT_BW) for dil in dils]
    blocks = [(tm, HG_WIDTH)] * _IN_HG + [(tm // dil, dil * ATT_BW) for dil in dils]
    dtypes = [BF16, F32, BF16, BF16] + [BF16] * _IN_ATT
    mod_spec = lambda row: pl.BlockSpec((None, None, 1, d),
                                        lambda i, row=row: (i // steps_per_batch, row, 0, 0))
    outs = pl.pallas_call(
        _inproj_kernel,
        grid=(t // tm,),
        in_specs=[pl.BlockSpec((tm, d), lambda i: (i, 0)),
                  mod_spec(1), mod_spec(0),
                  pl.BlockSpec(w_in_bf.shape, lambda i: (0, 0))],
        out_specs=[pl.BlockSpec(b, lambda i: (i, 0)) for b in blocks],
        out_shape=[jax.ShapeDtypeStruct(s, dt) for s, dt in zip(shapes, dtypes)],
        scratch_shapes=[pltpu.VMEM((ATT_BW // LANES, tm, LANES), F32) for dil in dils if dil > 1],
        compiler_params=_cparams(("parallel",)),
        name="ln_in_projection",
    )(x2, mod4, mod4, w_in_bf)
    return outs


HG_CHUNK = 64
HG_CHUNKS_PER_STEP = 16
HG_SUB = 8
HG_LEVELS = (64, 32, 16)
LOG2E = 1.4426950408889634


def _hgrn_chunk(q, z, iv, lb):
    c = HG_CHUNK
    f = lb + (1.0 - lb) * jax.nn.sigmoid(z)
    lf = jnp.log(f)
    kk = (1.0 - lb) * jax.nn.sigmoid(-z)
    r_i = lax.broadcasted_iota(jnp.int32, (c, c), 0)
    c_i = lax.broadcasted_iota(jnp.int32, (c, c), 1)
    tril = (c_i <= r_i).astype(F32)
    b = jnp.dot(tril, lf, preferred_element_type=F32, precision=lax.Precision.HIGHEST)
    bl = b * LOG2E

    row = lax.broadcasted_iota(jnp.int32, (c, HG_DK), 0)
    scores = jnp.zeros((c, c), F32)
    for m in HG_LEVELS:
        nb = c // m
        b3 = bl.reshape(nb, m, HG_DK)
        piv = jnp.broadcast_to(b3[:, m // 2 - 1:m // 2, :], (nb, m, HG_DK)).reshape(c, HG_DK)
        second = (row % m) >= (m // 2)
        qt = jnp.where(second, q * jnp.exp2(bl - piv), 0.0)
        kt = jnp.where(second, 0.0, kk * jnp.exp2(piv - bl))
        s_m = _dot_nt(qt.astype(BF16), kt.astype(BF16))
        if nb > 1:
            s_m = jnp.where((r_i // m) == (c_i // m), s_m, 0.0)
        scores = scores + s_m
    sub = HG_SUB
    t_i = lax.broadcasted_iota(jnp.int32, (sub, 1), 0)
    lane = lax.broadcasted_iota(jnp.int32, (sub, c), 1)
    diag_rows = []
    for j in range(c // sub):
        qb = q[j * sub:(j + 1) * sub]
        kb = kk[j * sub:(j + 1) * sub]
        bb = bl[j * sub:(j + 1) * sub]
        a_j = jnp.zeros((sub, c), F32)
        for s in range(sub):
            w = qb * kb[s:s + 1] * jnp.exp2(bb - bb[s:s + 1])
            a_j = jnp.where(lane == j * sub + s, jnp.sum(w, axis=-1, keepdims=True), a_j)
        diag_rows.append(jnp.where(lane - j * sub <= t_i, a_j, 0.0))
    scores = scores + jnp.concatenate(diag_rows, axis=0)

    ivb = iv.astype(BF16)
    intra = _dot(scores.astype(BF16), ivb)
    b_last = bl[c - 1:c]
    kdec = (kk * jnp.exp2(b_last - bl)).astype(BF16)
    return (q * jnp.exp2(bl)).astype(BF16), intra, jnp.exp2(b_last), _dot_tn(ivb, kdec)


def _hgrn_kernel(q_ref, f_ref, i_ref, g_ref, lbp_ref, nw_ref, o_ref, st_ref):
    @pl.when(pl.program_id(1) == 0)
    def _():
        st_ref[...] = jnp.zeros_like(st_ref)

    lbp = lbp_ref[...]
    e = jnp.exp(lbp - jnp.max(lbp, axis=0, keepdims=True))
    lb_all = e[0:1] / jnp.sum(e, axis=0, keepdims=True)
    heads = []
    for h in range(HG_HEADS):
        sl = slice(h * HG_DK, (h + 1) * HG_DK)
        st_t = st_ref[h]
        outs = []
        for n in range(HG_CHUNKS_PER_STEP):
            rows = slice(n * HG_CHUNK, (n + 1) * HG_CHUNK)
            qdec, intra, dec_last, kv = _hgrn_chunk(q_ref[rows, sl].astype(F32), f_ref[rows, sl],
                                                    i_ref[rows, sl].astype(F32), lb_all[:, sl])
            o = _dot_nt(qdec, st_t.astype(BF16)) + intra
            st_t = st_t * dec_last + kv
            outs.append(o * lax.rsqrt(jnp.mean(o * o, axis=-1, keepdims=True) + RMS_EPS))
        st_ref[h] = st_t
        heads.append(jnp.concatenate(outs, axis=0))
    o_all = jnp.concatenate(heads, axis=-1)
    g = g_ref[...].astype(F32)
    o_ref[...] = (o_all * nw_ref[...] * (g * jax.nn.sigmoid(g))).astype(o_ref.dtype)


def _hgrn2(hq, hf, hi, hg, lb_param, norm_w, bsz, seq):
    t = hq.shape[0]
    rows = HG_CHUNK * HG_CHUNKS_PER_STEP
    nc = seq // rows
    tok = lambda b, n: (b * nc + n, 0)
    spec = pl.BlockSpec((rows, HG_WIDTH), tok)
    return pl.pallas_call(
        _hgrn_kernel,
        grid=(bsz, nc),
        in_specs=[spec, spec, spec, spec,
                  pl.BlockSpec(lb_param.shape, lambda b, n: (0, 0)),
                  pl.BlockSpec((1, HG_WIDTH), lambda b, n: (0, 0))],
        out_specs=spec,
        out_shape=jax.ShapeDtypeStruct((t, HG_WIDTH), BF16),
        scratch_shapes=[pltpu.VMEM((HG_HEADS, HG_DK, HG_DK), F32)],
        compiler_params=_cparams(("parallel", "arbitrary")),
        name="hgrn2_scan",
    )(hq, hf, hi, hg, lb_param, norm_w.reshape(1, HG_WIDTH))


def _t5_bucket_np(dist):
    max_exact = REL_BUCKETS // 2
    n = np.maximum(dist, 0)
    nf = np.maximum(n, 1).astype(np.float32)
    large = max_exact + (np.log(nf / np.float32(max_exact)) / np.float32(math.log(REL_MAX_DIST / max_exact))
                         * np.float32(REL_BUCKETS - max_exact)).astype(np.int32)
    large = np.minimum(large, REL_BUCKETS - 1)
    return np.where(n < max_exact, n, large).astype(np.int32)


def _band_tables():
    w = ATT_BLOCK
    qi = np.arange(w)[:, None]
    ki = np.arange(2 * w)[None, :]
    m = w + qi - ki
    band = (m >= 0) & (m <= w)
    buckets = np.stack([_t5_bucket_np(m * dil) for _, dil in ATT_BRANCHES])
    return buckets, band


def _bias_kernel(rb_ref, bucket_ref, o_ref):
    g = pl.program_id(0)
    w = ATT_BLOCK
    bucket = bucket_ref[...]
    qi = lax.broadcasted_iota(jnp.int32, (w, 2 * w), 0)
    ki = lax.broadcasted_iota(jnp.int32, (w, 2 * w), 1)
    m = w + qi - ki
    band = (m >= 0) & (m <= w)
    for h in range(ATT_HEADS_PER_BRANCH):
        acc = jnp.zeros((w, 2 * w), F32)
        for c in range(REL_BUCKETS):
            acc = jnp.where(bucket == c, rb_ref[c, g * ATT_HEADS_PER_BRANCH + h], acc)
        full = jnp.where(band, acc, NEG_INF)
        o_ref[1, h] = full
        o_ref[0, h] = jnp.where(ki >= w, full, NEG_INF)


def _bias_tables(rel_bias):
    buckets, _ = _band_tables()
    g = len(ATT_BRANCHES)
    w = ATT_BLOCK
    return pl.pallas_call(
        _bias_kernel,
        grid=(g,),
        in_specs=[pl.BlockSpec(memory_space=pltpu.SMEM),
                  pl.BlockSpec((None, w, 2 * w), lambda i: (i, 0, 0))],
        out_specs=pl.BlockSpec((None, 2, ATT_HEADS_PER_BRANCH, w, 2 * w), lambda i: (i, 0, 0, 0, 0)),
        out_shape=jax.ShapeDtypeStruct((g, 2, ATT_HEADS_PER_BRANCH, w, 2 * w), F32),
        compiler_params=_cparams(("parallel",)),
        name="rel_bias_tables",
    )(rel_bias, jnp.asarray(buckets))


ATT_QROWS = 64
ATT_BLOCKS_PER_STEP = 16


def _attn_kernel(q_ref, kp_ref, kc_ref, vp_ref, vc_ref, bias_ref, o_ref, lse_ref):
    m = pl.program_id(2)
    w = ATT_BLOCK
    hb = ATT_HEADS_PER_BRANCH
    qr = ATT_QROWS
    lane = lax.broadcasted_iota(jnp.int32, (qr, ATT_BW), 1) // ATT_HEAD_DIM
    for res, blk, q0 in [(r, b, s) for r in range(q_ref.shape[1] // ATT_BW) for b in range(q_ref.shape[0] // w)
                         for s in range(0, w, qr)]:
        cols = slice(res * ATT_BW, (res + 1) * ATT_BW)
        kall = jnp.concatenate([kp_ref[:, cols], kc_ref[:, cols]], axis=0)
        vall = jnp.concatenate([vp_ref[:, cols], vc_ref[:, cols]], axis=0)
        rows = slice(blk * w + q0, blk * w + q0 + qr)
        q = q_ref[rows, cols]
        q4 = jnp.concatenate([jnp.where(lane == h, q, jnp.zeros_like(q)) for h in range(hb)], axis=0)
        kk = kall[blk * w:(blk + 2) * w]
        vv = vall[blk * w:(blk + 2) * w]
        s4 = _dot_nt(q4, kk)
        bias = bias_ref[jnp.minimum(m, 1)] if blk == 0 else bias_ref[1]
        s4 = s4 + bias[:, q0:q0 + qr, :].reshape(hb * qr, 2 * w)
        mx = jnp.max(s4, axis=-1, keepdims=True)
        p = jnp.exp(s4 - mx)
        l = jnp.sum(p, axis=-1, keepdims=True)
        o4 = _dot((p / l).astype(vv.dtype), vv)
        lse4 = mx + jnp.log(l)
        o = jnp.zeros((qr, ATT_BW), F32)
        lse = jnp.zeros((qr, ATT_BW), F32)
        for h in range(hb):
            o = jnp.where(lane == h, o4[h * qr:(h + 1) * qr], o)
            lse = jnp.where(lane == h, lse4[h * qr:(h + 1) * qr], lse)
        o_ref[rows, cols] = o.astype(o_ref.dtype)
        lse_ref[rows, cols] = lse


def _dilated_attention(q, k, v, bias_g, bsz, seq, dilation):
    w = ATT_BLOCK
    l = seq // dilation
    pstep = math.gcd(ATT_BLOCKS_PER_STEP, l // w)
    nb = l // (w * pstep)
    rstep = math.gcd(ATT_BLOCKS_PER_STEP // pstep, dilation)
    view = lambda a: a.reshape(bsz, l, dilation * ATT_BW)
    cur = pl.BlockSpec((None, pstep * w, rstep * ATT_BW), lambda b, r, n: (b, n, r))
    prev = pl.BlockSpec((None, w, rstep * ATT_BW), lambda b, r, n: (b, jnp.maximum(pstep * n - 1, 0), r))
    o, lse = pl.pallas_call(
        _attn_kernel,
        grid=(bsz, dilation // rstep, nb),
        in_specs=[cur, prev, cur, prev, cur,
                  pl.BlockSpec(bias_g.shape, lambda b, r, n: (0, 0, 0, 0))],
        out_specs=[cur, cur],
        out_shape=[jax.ShapeDtypeStruct((bsz, l, dilation * ATT_BW), BF16),
                   jax.ShapeDtypeStruct((bsz, l, dilation * ATT_BW), F32)],
        compiler_params=_cparams(("parallel", "parallel", "arbitrary")),
        name=f"dilated_attention_d{dilation}",
    )(view(q), view(k), view(k), view(v), view(v), bias_g)
    return o.reshape(bsz * l, dilation * ATT_BW), lse.reshape(bsz * l, dilation * ATT_BW)


H2P_CHUNKS = 4


def _token_order(ref, scr, dil):
    if dil == 1:
        return ref[...].astype(F32)
    n = ref.shape[0]
    halves = ATT_BW // LANES
    for r in range(dil):
        for half in range(halves):
            c0 = r * ATT_BW + half * LANES
            scr[half, pl.ds(r, n, stride=dil), :] = ref[:, c0:c0 + LANES].astype(F32)
    return jnp.concatenate([scr[half] for half in range(halves)], axis=1)


def _outproj_kernel(yhg_ref, o1_ref, o2_ref, o3_ref, l1_ref, l2_ref, l3_ref, x_ref,
                    g1_ref, sc2_ref, sh2_ref, wout_ref, lng_ref, lnb_ref, wrt_ref,
                    x1_ref, h2_ref, h2p_ref, lgt_ref, *scratch):
    dils = [dil for _, dil in ATT_BRANCHES]
    scr = iter(scratch)
    o1, o2, o3 = [_token_order(r, None if dil == 1 else next(scr), dil)
                  for r, dil in zip((o1_ref, o2_ref, o3_ref), dils)]
    l1, l2, l3 = [_token_order(r, None if dil == 1 else next(scr), dil)
                  for r, dil in zip((l1_ref, l2_ref, l3_ref), dils)]
    mx = jnp.maximum(jnp.maximum(l1, l2), l3)
    e1, e2, e3 = jnp.exp(l1 - mx), jnp.exp(l2 - mx), jnp.exp(l3 - mx)
    den = e1 + e2 + e3
    att = (e1 / den) * o1 + (e2 / den) * o2 + (e3 / den) * o3
    mix = _dot(yhg_ref[...], wout_ref[:HG_WIDTH, :]) + _dot(att.astype(BF16), wout_ref[HG_WIDTH:, :])
    x1 = _ln_rows(DN_ALPHA * x_ref[...] + g1_ref[...] * mix) * lng_ref[...] + lnb_ref[...]
    x1_ref[...] = x1
    h2 = _ln_rows(x1) * (1.0 + sc2_ref[...]) + sh2_ref[...]
    h_hi = h2.astype(BF16)
    h_hf = h_hi.astype(F32)
    h2_ref[...] = h_hi
    bits = lax.bitcast_convert_type(h_hf, jnp.uint32)
    for cidx in range(H2P_CHUNKS):
        lo = bits[:, 2 * LANES * cidx:2 * LANES * cidx + LANES]
        hi = bits[:, 2 * LANES * cidx + LANES:2 * LANES * (cidx + 1)]
        h2p_ref[pl.ds(cidx, h2.shape[0], stride=H2P_CHUNKS), :] = lax.bitcast_convert_type((lo >> 16) | hi,
                                                                                          jnp.int32)
    lgt_ref[...] = _dot_nt(wrt_ref[...].astype(BF16), h_hi)


def _out_projection(yhg, os_, ls_, x2, mod4, w_out_bf, ln_g, ln_b, w_router_t, seq, tm):
    t, d = x2.shape
    spb = seq // tm
    ne = w_router_t.shape[0]
    row = lambda w: pl.BlockSpec((tm, w), lambda i: (i, 0))
    mod_spec = lambda r: pl.BlockSpec((None, None, 1, d), lambda i, r=r: (i // spb, r, 0, 0))
    full = lambda a: pl.BlockSpec(a.shape, lambda i: (0,) * a.ndim)
    ln_g2, ln_b2 = ln_g.reshape(1, d), ln_b.reshape(1, d)
    dils = [dil for _, dil in ATT_BRANCHES]
    branch = [pl.BlockSpec((tm // dil, dil * ATT_BW), lambda i: (i, 0)) for dil in dils]
    return pl.pallas_call(
        _outproj_kernel,
        grid=(t // tm,),
        in_specs=[row(HG_WIDTH)] + branch + branch + [row(d),
                  mod_spec(2), mod_spec(4), mod_spec(3),
                  full(w_out_bf), full(ln_g2), full(ln_b2), full(w_router_t)],
        out_specs=[row(d), row(d), pl.BlockSpec((tm * H2P_CHUNKS, LANES), lambda i: (i, 0)),
                   pl.BlockSpec((ne, tm), lambda i: (0, i))],
        out_shape=[jax.ShapeDtypeStruct((t, d), F32), jax.ShapeDtypeStruct((t, d), BF16),
                   jax.ShapeDtypeStruct((t * H2P_CHUNKS, LANES), jnp.int32),
                   jax.ShapeDtypeStruct((ne, t), F32)],
        scratch_shapes=[pltpu.VMEM((ATT_BW // LANES, tm, LANES), F32) for dil in dils + dils if dil > 1],
        compiler_params=_cparams(("parallel",)),
        name="merge_outproj_ln",
    )(yhg, *os_, *ls_, x2, mod4, mod4, mod4, w_out_bf, ln_g2, ln_b2, w_router_t)


def _argmax_rows(cur, iota, nrows):
    m = jnp.max(cur, axis=0, keepdims=True)
    idx = jnp.min(jnp.where(cur == m, iota, nrows), axis=0, keepdims=True)
    return m, idx, iota == idx


def _route_kernel(lgt_ref, rb_ref, eidx_ref, w_ref, rank_ref, cnt_ref, carry):
    ne = N_EXPERTS
    gsz = ne // N_GROUPS
    tt = lgt_ref.shape[1]

    @pl.when(pl.program_id(0) == 0)
    def _():
        carry[...] = jnp.zeros_like(carry)

    sc = jax.nn.sigmoid(lgt_ref[...])
    biased = sc + rb_ref[...]
    g3 = biased.reshape(N_GROUPS, gsz, tt)
    io3 = lax.broadcasted_iota(jnp.int32, (N_GROUPS, gsz, tt), 1)
    m1 = jnp.max(g3, axis=1, keepdims=True)
    first = jnp.min(jnp.where(g3 == m1, io3, gsz), axis=1, keepdims=True)
    m2 = jnp.max(jnp.where(io3 == first, NEG_INF, g3), axis=1, keepdims=True)
    gs = (m1 + m2).reshape(N_GROUPS, tt)
    io8 = lax.broadcasted_iota(jnp.int32, (N_GROUPS, tt), 0)
    sel = jnp.zeros((N_GROUPS, tt), jnp.int32)
    cur = gs
    for _ in range(TOPK_GROUPS):
        _, _, pick = _argmax_rows(cur, io8, N_GROUPS)
        sel = jnp.where(pick, 1, sel)
        cur = jnp.where(pick, NEG_INF, cur)
    masked = jnp.where(sel.reshape(N_GROUPS, 1, tt) > 0, g3, NEG_INF).reshape(ne, tt)
    ioe = lax.broadcasted_iota(jnp.int32, (ne, tt), 0)
    cur = masked
    idxs, ws, picks = [], [], []
    for _ in range(TOP_K):
        _, idx, pick = _argmax_rows(cur, ioe, ne)
        idxs.append(idx)
        picks.append(pick)
        ws.append(jnp.sum(jnp.where(pick, sc, 0.0), axis=0, keepdims=True))
        cur = jnp.where(pick, NEG_INF, cur)
    wk = jnp.concatenate(ws, axis=0)
    eidx_ref[...] = jnp.concatenate(idxs, axis=0)
    w_ref[...] = wk / jnp.sum(wk, axis=0, keepdims=True) * ROUTED_SCALE
    chosen = jnp.where(cur == NEG_INF, jnp.where(masked == NEG_INF, 0.0, 1.0), 0.0)
    r_i = lax.broadcasted_iota(jnp.int32, (tt, tt), 0)
    c_i = lax.broadcasted_iota(jnp.int32, (tt, tt), 1)
    before = jnp.where(r_i < c_i, 1.0, 0.0).astype(BF16)
    pref = _dot(chosen.astype(BF16), before) + carry[...]
    rank_ref[...] = jnp.concatenate(
        [jnp.sum(jnp.where(p, pref, 0.0), axis=0, keepdims=True) for p in picks], axis=0).astype(jnp.int32)
    carry[...] = carry[...] + jnp.sum(chosen, axis=1, keepdims=True)
    cnt_ref[...] = carry[...]


def _route(logits_t, router_bias, tt):
    ne, t = logits_t.shape
    tok = pl.BlockSpec((TOP_K, tt), lambda i: (0, i))
    return pl.pallas_call(
        _route_kernel,
        grid=(t // tt,),
        in_specs=[pl.BlockSpec((ne, tt), lambda i: (0, i)),
                  pl.BlockSpec((ne, 1), lambda i: (0, 0))],
        out_specs=[tok, tok, tok, pl.BlockSpec((ne, 1), lambda i: (0, 0))],
        out_shape=[jax.ShapeDtypeStruct((TOP_K, t), jnp.int32), jax.ShapeDtypeStruct((TOP_K, t), F32),
                   jax.ShapeDtypeStruct((TOP_K, t), jnp.int32), jax.ShapeDtypeStruct((ne, 1), F32)],
        scratch_shapes=[pltpu.VMEM((ne, 1), F32)],
        compiler_params=_cparams(("arbitrary",)),
        name="router_topk",
    )(logits_t, router_bias.reshape(ne, 1))


def _plan_kernel(cnt_ref, eidx_ref, rank_ref, dest_ref, seq_ref, dexp_ref, meta_ref, pstart_ref):
    ne = N_EXPERTS
    tt = eidx_ref.shape[1]
    nblk = seq_ref.shape[1]

    @pl.when(pl.program_id(0) == 0)
    def _():
        cnt = cnt_ref[...].astype(jnp.int32)
        padded = ((cnt + (MOE_BLOCK - 1)) // MOE_BLOCK) * MOE_BLOCK
        r_i = lax.broadcasted_iota(jnp.int32, (ne, ne), 0)
        c_i = lax.broadcasted_iota(jnp.int32, (ne, ne), 1)
        incl = jnp.where(c_i <= r_i, 1.0, 0.0)
        pend = jnp.dot(incl, jnp.broadcast_to(padded.astype(F32), (ne, LANES)),
                       preferred_element_type=F32, precision=lax.Precision.HIGHEST)[:, 0:1]
        pend = pend.astype(jnp.int32)
        pstart_ref[...] = pend - padded
        blk0 = lax.broadcasted_iota(jnp.int32, (ne, nblk), 1) * MOE_BLOCK
        be = jnp.minimum(jnp.sum(jnp.where(pend <= blk0, 1, 0), axis=0, keepdims=True), ne - 1)
        present = cnt > 0
        strict = jnp.where(c_i < r_i, 1.0, 0.0).astype(BF16)
        sidx = _dot(strict, jnp.broadcast_to(jnp.where(present, 1.0, 0.0), (ne, LANES)).astype(BF16))[:, 0:1]
        sidx = sidx.astype(jnp.int32)
        dexp_ref[...] = jnp.sum(jnp.where(jnp.logical_and(present, sidx == c_i), r_i, 0), axis=0, keepdims=True)
        ioeb = lax.broadcasted_iota(jnp.int32, (ne, nblk), 0)
        nu = jnp.max(pend, axis=0, keepdims=True) // MOE_BLOCK
        nd = jnp.sum(jnp.where(present, 1, 0), axis=0, keepdims=True)
        seq_ref[...] = jnp.minimum(jnp.sum(jnp.where(ioeb == be, sidx, 0), axis=0, keepdims=True), nd - 1)
        lane = lax.broadcasted_iota(jnp.int32, (1, LANES), 1)
        meta_ref[...] = jnp.where(lane == 0, nu, jnp.where(lane == 1, nd, 0))

    pstart = pstart_ref[...]
    ioe = lax.broadcasted_iota(jnp.int32, (ne, tt), 0)
    rows = []
    for k in range(TOP_K):
        sel = ioe == eidx_ref[k:k + 1, :]
        rows.append(jnp.sum(jnp.where(sel, pstart, 0), axis=0, keepdims=True))
    dest_ref[...] = jnp.concatenate(rows, axis=0) + rank_ref[...]


def _dispatch_plan(cnt, eidx, rank, tt):
    k, t = eidx.shape
    n_blocks = -(-(t * k) // MOE_BLOCK) + N_EXPERTS
    tok = pl.BlockSpec((k, tt), lambda i: (0, i))
    one = lambda n: pl.BlockSpec((1, n), lambda i: (0, 0))
    dest, seq, dexp, meta = pl.pallas_call(
        _plan_kernel,
        grid=(t // tt,),
        in_specs=[pl.BlockSpec(cnt.shape, lambda i: (0, 0)), tok, tok],
        out_specs=[tok, one(n_blocks), one(N_EXPERTS), one(LANES)],
        out_shape=[jax.ShapeDtypeStruct((k, t), jnp.int32), jax.ShapeDtypeStruct((1, n_blocks), jnp.int32),
                   jax.ShapeDtypeStruct((1, N_EXPERTS), jnp.int32), jax.ShapeDtypeStruct((1, LANES), jnp.int32)],
        scratch_shapes=[pltpu.VMEM((N_EXPERTS, 1), jnp.int32)],
        compiler_params=_cparams(("arbitrary",)),
        name="dispatch_plan",
    )(cnt, eidx, rank)
    return dest, seq.reshape(n_blocks), dexp.reshape(N_EXPERTS), meta.reshape(LANES), n_blocks


SC_CORES = 2
SC_SUBCORES = 16
SC_LANES = 16
SC_CHUNK = 16384
SC_UNROLL = 4


def _row_tables(dest, wk, n_rows):
    k, t = dest.shape
    a = k * t
    per_w = n_rows // SC_SUBCORES
    assert n_rows % (SC_SUBCORES * SC_LANES) == 0 and t % SC_CHUNK == 0 and SC_CORES == 2
    mesh = plsc.VectorSubcoreMesh(core_axis_name="c", subcore_axis_name="s")
    cp = pltpu.CompilerParams()
    if "needs_layout_passes" in pltpu.CompilerParams.__dataclass_fields__:
        cp = dataclasses.replace(cp, needs_layout_passes=False)

    n_chunks = a // SC_CHUNK
    assert n_chunks % 2 == 0

    def body(dest_hbm, wbit_hbm, tok_out, w_out, dbuf_0, dbuf_1, wbuf_0, wbuf_1, tloc,
             sem_0, sem_1, wsem_0, wsem_1):
        base = lax.axis_index("s") * per_w
        role = lax.axis_index("c")
        bufs = ((dbuf_0, wbuf_0, sem_0, wsem_0), (dbuf_1, wbuf_1, sem_1, wsem_1))

        @pl.loop(0, per_w // SC_LANES)
        def _(i):
            tloc[pl.ds(i * SC_LANES, SC_LANES)] = jnp.zeros((SC_LANES,), jnp.int32)

        lane = lax.iota(jnp.int32, SC_LANES)

        def run(weights, out_ref):
            def chunk_copies(c, slot):
                dbuf, wbuf, sem, wsem = bufs[slot]
                rng = pl.ds(c * SC_CHUNK, SC_CHUNK)
                copies = [pltpu.make_async_copy(dest_hbm.at[rng], dbuf, sem)]
                if weights:
                    copies.append(pltpu.make_async_copy(wbit_hbm.at[rng], wbuf, wsem))
                return copies

            def start(c, slot):
                for cp_ in chunk_copies(c, slot):
                    cp_.start()

            def wait(c, slot):
                for cp_ in chunk_copies(c, slot):
                    cp_.wait()

            def scan(c, slot):
                dbuf, wbuf = bufs[slot][:2]
                tok0 = lax.rem(c * SC_CHUNK, t)

                @pl.loop(0, SC_CHUNK // (SC_LANES * SC_UNROLL))
                def _(j):
                    for u in range(SC_UNROLL):
                        off = (j * SC_UNROLL + u) * SC_LANES
                        loc = dbuf[pl.ds(off, SC_LANES)] - base
                        mine = jnp.logical_and(loc >= 0, loc < per_w)
                        loc = jnp.where(mine, loc, 0)
                        val = wbuf[pl.ds(off, SC_LANES)] if weights else tok0 + off + lane
                        plsc.store_scatter(tloc, [loc], val, mask=mine)

            start(0, 0)

            @pl.loop(0, n_chunks // 2)
            def _(i):
                c0 = 2 * i
                start(c0 + 1, 1)
                wait(c0, 0)
                scan(c0, 0)
                start(jnp.minimum(c0 + 2, n_chunks - 1), 0)
                wait(c0 + 1, 1)
                scan(c0 + 1, 1)

            wait(n_chunks - 1, 0)
            pltpu.sync_copy(tloc, out_ref.at[pl.ds(base, per_w)])

        @pl.when(role == 0)
        def _():
            run(False, tok_out)

        @pl.when(role == 1)
        def _():
            run(True, w_out)

    fn = pl.kernel(
        body,
        out_type=[jax.ShapeDtypeStruct((n_rows,), jnp.int32), jax.ShapeDtypeStruct((n_rows,), jnp.int32)],
        mesh=mesh,
        scratch_types=[pltpu.VMEM((SC_CHUNK,), jnp.int32)] * 4 + [pltpu.VMEM((per_w,), jnp.int32)]
        + [pltpu.SemaphoreType.DMA] * 4,
        compiler_params=cp,
        name="row_tables",
    )
    row_token, row_wbits = fn(dest.reshape(a), lax.bitcast_convert_type(wk, jnp.int32).reshape(a))
    return row_token, lax.bitcast_convert_type(row_wbits, F32)


ROW_TILE = (SUBLANES, LANES)


FFN_GROUP = 4
PAIR_ROWS = 2 * MOE_BLOCK
FFN_CHAIN = 128
W_SETS = 3
GATHER_BATCH = 32


def _gather_rows(h2p_ref, tok_ref, row, buf, base):
    for j0 in range(0, MOE_BLOCK, GATHER_BATCH):
        vals = [h2p_ref[pl.ds(pl.multiple_of(tok_ref[row, j] * H2P_CHUNKS, H2P_CHUNKS), H2P_CHUNKS), :]
                for j in range(j0, j0 + GATHER_BATCH)]
        for j, v in zip(range(j0, j0 + GATHER_BATCH), vals):
            buf[pl.ds(H2P_CHUNKS * (base + j), H2P_CHUNKS), :] = v


def _expert_rows(buf, chains, out):
    hms = []
    for row0, nrows, (wg_c, wu_c, _), _, _ in chains:
        parts = []
        for cidx in range(H2P_CHUNKS):
            word = buf[pl.ds(H2P_CHUNKS * row0 + cidx, nrows, stride=H2P_CHUNKS), :]
            parts.append(lax.bitcast_convert_type(word << 16, F32))
            parts.append(lax.bitcast_convert_type(word & jnp.int32(-65536), F32))
        x = jnp.concatenate(parts, axis=1)
        g = _dot(x, wg_c[...])
        u = _dot(x, wu_c[...])
        hms.append((g * jax.nn.sigmoid(g)) * u)
    for (_, nrows, (_, _, wd_c), out_row0, wcol), hm in zip(chains, hms):
        res = _dot(hm, wd_c[...])
        nch = res.shape[1] // LANES
        for c in range(nch):
            out[pl.ds(nch * out_row0 + c, nrows, stride=nch), :] = res[:, c * LANES:(c + 1) * LANES] * wcol


def _ffn_kernel(seq_ref, dexp_ref, meta_ref, tokc_ref, tokn_ref, roww_ref, h2p_ref, wg_hbm, wu_hbm, wd_hbm,
                y_ref, buf_0, buf_1, *rest):
    i = pl.program_id(0)
    n_used = meta_ref[0]
    n_exp = meta_ref[1]
    nblk = seq_ref.shape[0]
    wsets = tuple(tuple(rest[3 * n:3 * n + 3]) for n in range(W_SETS))
    started_ref, sems = rest[3 * W_SETS:]
    bufs = (buf_0, buf_1)

    def weight_copies(s, par):
        e = dexp_ref[s]
        return [pltpu.make_async_copy(src.at[e], dst, sems.at[par, n])
                for n, (src, dst) in enumerate(zip((wg_hbm, wu_hbm, wd_hbm), wsets[par]))]

    def start_expert(s):
        for par in range(W_SETS):
            @pl.when(s % W_SETS == par)
            def _():
                for cp in weight_copies(s, par):
                    cp.start()

    @pl.when(i == 0)
    def _():
        _gather_rows(h2p_ref, tokc_ref, 0, buf_0, 0)
        _gather_rows(h2p_ref, tokc_ref, 1, buf_0, MOE_BLOCK)
        start_expert(0)
        started_ref[0] = 0

    for p in range(FFN_GROUP // 2):
        b_a = i * FFN_GROUP + 2 * p
        used = b_a < n_used
        s_a = seq_ref[jnp.minimum(b_a, nblk - 1)]
        s_b = seq_ref[jnp.minimum(b_a + 1, nblk - 1)]
        first_a = jnp.logical_or(b_a == 0, s_a != seq_ref[jnp.clip(b_a - 1, 0, nblk - 1)])
        same = s_a == s_b
        cur, nxt = bufs[p], bufs[1 - p]
        out_row0 = p * PAIR_ROWS

        def gate_col(r0):
            blk = 2 * p + r0 // MOE_BLOCK
            return jnp.broadcast_to(roww_ref[blk:blk + 1, :], (MOE_BLOCK, MOE_BLOCK)).T

        def gather_next():
            if p == 0:
                _gather_rows(h2p_ref, tokc_ref, 2, nxt, 0)
                _gather_rows(h2p_ref, tokc_ref, 3, nxt, MOE_BLOCK)
            else:
                _gather_rows(h2p_ref, tokn_ref, 0, nxt, 0)
                _gather_rows(h2p_ref, tokn_ref, 1, nxt, MOE_BLOCK)

        @pl.when(used)
        def _():
            started = started_ref[0]
            limit = jnp.minimum(s_a + (W_SETS - 1), n_exp - 1)
            for _unused in range(W_SETS - 1):
                go = started < limit

                @pl.when(go)
                def _():
                    start_expert(started + 1)
                started = jnp.where(go, started + 1, started)
            started_ref[0] = started

        for par in range(W_SETS):
            @pl.when(jnp.logical_and(used, jnp.logical_and(same, s_a % W_SETS == par)))
            def _():
                @pl.when(first_a)
                def _():
                    for cp in weight_copies(s_a, par):
                        cp.wait()
                gather_next()
                _expert_rows(cur, [(r0, FFN_CHAIN, wsets[par], out_row0 + r0, gate_col(r0))
                                   for r0 in range(0, PAIR_ROWS, FFN_CHAIN)], y_ref)

            @pl.when(jnp.logical_and(used, jnp.logical_and(jnp.logical_not(same), s_a % W_SETS == par)))
            def _():
                @pl.when(first_a)
                def _():
                    for cp in weight_copies(s_a, par):
                        cp.wait()
                for cp in weight_copies(s_b, (par + 1) % W_SETS):
                    cp.wait()
                gather_next()
                _expert_rows(cur, [(r0, FFN_CHAIN, wsets[(par + r0 // MOE_BLOCK) % W_SETS], out_row0 + r0,
                                    gate_col(r0)) for r0 in range(0, PAIR_ROWS, FFN_CHAIN)], y_ref)

        @pl.when(jnp.logical_not(used))
        def _():
            y_ref[pl.ds(out_row0 * SUBLANES, PAIR_ROWS * SUBLANES), :] = jnp.zeros(
                (PAIR_ROWS * SUBLANES, LANES), y_ref.dtype)


def _routed_experts(h2p, row_token, row_weight, seq, dexp, meta, n_blocks, wg, wu, wd):
    d = wg.shape[1]
    de = wg.shape[2]
    ng = n_blocks // FFN_GROUP
    assert n_blocks % FFN_GROUP == 0 and FFN_GROUP == 4 and d == SUBLANES * LANES
    tok3 = row_token.reshape(ng, FFN_GROUP, MOE_BLOCK)
    w3 = row_weight.reshape(ng, FFN_GROUP, MOE_BLOCK)
    idle_step = lambda m: jnp.minimum((m[0] + FFN_GROUP - 1) // FFN_GROUP, ng - 1)
    smem = lambda imap: pl.BlockSpec((None, FFN_GROUP, MOE_BLOCK), imap, memory_space=pltpu.SMEM)
    grid_spec = pltpu.PrefetchScalarGridSpec(
        num_scalar_prefetch=3,
        grid=(ng,),
        in_specs=[
            smem(lambda i, sq, dx, m: (jnp.minimum(i, ng - 1), 0, 0)),
            smem(lambda i, sq, dx, m: (jnp.minimum(i + 1, ng - 1), 0, 0)),
            pl.BlockSpec((None, FFN_GROUP, MOE_BLOCK), lambda i, sq, dx, m: (i, 0, 0)),
            pl.BlockSpec(h2p.shape, lambda i, sq, dx, m: (0, 0), pipeline_mode=pl.Buffered(1)),
            pl.BlockSpec(memory_space=pl.ANY),
            pl.BlockSpec(memory_space=pl.ANY),
            pl.BlockSpec(memory_space=pl.ANY),
        ],
        out_specs=pl.BlockSpec((FFN_GROUP * MOE_BLOCK * SUBLANES, LANES),
                               lambda i, sq, dx, m: (jnp.minimum(i, idle_step(m)), 0)),
        scratch_shapes=[pltpu.VMEM((PAIR_ROWS * H2P_CHUNKS, LANES), jnp.int32)] * 2 + [
            pltpu.VMEM((d, de), F32), pltpu.VMEM((d, de), F32), pltpu.VMEM((de, d), F32)] * W_SETS + [
            pltpu.SMEM((1,), jnp.int32), pltpu.SemaphoreType.DMA((W_SETS, 3))],
    )
    y = pl.pallas_call(
        _ffn_kernel,
        grid_spec=grid_spec,
        out_shape=jax.ShapeDtypeStruct((n_blocks * MOE_BLOCK * SUBLANES, LANES), F32),
        compiler_params=_cparams(("arbitrary",)),
        name="routed_experts",
    )(seq, dexp, meta, tok3, tok3, w3, h2p, wg, wu, wd)
    return y.reshape((n_blocks * MOE_BLOCK,) + ROW_TILE)


COMBINE_TOKENS = 128


def _row_copies(src_hbm, idx_ref, buf, sem):
    return [pltpu.make_async_copy(src_hbm.at[idx_ref[k, j]], buf.at[k, j], sem)
            for k in range(TOP_K) for j in range(COMBINE_TOKENS)]


def _shared_kernel(h_ref, wsg_ref, wsu_ref, wsd_ref, o_ref):
    hb = h_ref[...]
    g = _dot(hb, wsg_ref[...])
    u = _dot(hb, wsu_ref[...])
    o_ref[...] = _dot(((g * jax.nn.sigmoid(g)) * u).astype(BF16), wsd_ref[...])


def _shared_expert(h2b, wsg, wsu, wsd, tm):
    t, d = h2b.shape
    row = pl.BlockSpec((tm, d), lambda i: (i, 0))
    full = lambda a: pl.BlockSpec(a.shape, lambda i: (0,) * a.ndim)
    return pl.pallas_call(
        _shared_kernel,
        grid=(t // tm,),
        in_specs=[row, full(wsg), full(wsu), full(wsd)],
        out_specs=row,
        out_shape=jax.ShapeDtypeStruct((t, d), F32),
        compiler_params=_cparams(("parallel",)),
        name="shared_expert",
    )(h2b, wsg, wsu, wsd)


def _final_kernel(dc_ref, dn_ref, x1_ref, sh_ref, g2_ref, lng_ref, lnb_ref,
                  y_hbm, o_ref, ybuf, sems):
    i = pl.program_id(0)
    nsteps = pl.num_programs(0)
    slot = i % 2

    def issue(d_ref, s):
        for n, cp in enumerate(_row_copies(y_hbm, d_ref, ybuf.at[s], sems.at[s])):
            cp.start(priority=n % 2)

    @pl.when(i == 0)
    def _():
        issue(dc_ref, 0)

    @pl.when(i + 1 < nsteps)
    def _():
        issue(dn_ref, 1 - slot)

    for cp in _row_copies(y_hbm, dc_ref, ybuf.at[slot], sems.at[slot]):
        cp.wait()
    rows = []
    for j in range(COMBINE_TOKENS):
        acc = ybuf[slot, 0, j].astype(F32)
        for k in range(1, TOP_K):
            acc = acc + ybuf[slot, k, j].astype(F32)
        rows.append(acc)
    routed = pltpu.einshape("tcl->t(cl)", jnp.stack(rows, axis=0))
    x2 = DN_ALPHA * x1_ref[...] + g2_ref[...] * (routed + sh_ref[...])
    o_ref[...] = _ln_rows(x2) * lng_ref[...] + lnb_ref[...]


def _combine_final(x1, shared, mod4, dest, y_rows, ln_g, ln_b, seq):
    t, d = x1.shape
    tt = COMBINE_TOKENS
    nsteps = t // tt
    spb = seq // tt
    row = pl.BlockSpec((tt, d), lambda i: (i, 0))
    full = lambda a: pl.BlockSpec(a.shape, lambda i: (0,) * a.ndim)
    ln_g2, ln_b2 = ln_g.reshape(1, d), ln_b.reshape(1, d)
    return pl.pallas_call(
        _final_kernel,
        grid=(nsteps,),
        in_specs=[pl.BlockSpec((TOP_K, tt), lambda i: (0, i), memory_space=pltpu.SMEM),
                  pl.BlockSpec((TOP_K, tt), lambda i: (0, jnp.minimum(i + 1, nsteps - 1)),
                               memory_space=pltpu.SMEM),
                  row, row,
                  pl.BlockSpec((None, None, 1, d), lambda i: (i // spb, 5, 0, 0)),
                  full(ln_g2), full(ln_b2),
                  pl.BlockSpec(memory_space=pl.ANY)],
        out_specs=row,
        out_shape=jax.ShapeDtypeStruct((t, d), F32),
        scratch_shapes=[pltpu.VMEM((2, TOP_K, tt) + ROW_TILE, y_rows.dtype),
                        pltpu.SemaphoreType.DMA((2,))],
        compiler_params=_cparams(("arbitrary",)),
        name="shared_combine_ln",
    )(dest, dest, x1, shared, mod4, ln_g2, ln_b2, y_rows)


def kernel(x, c, w_ada, b_ada, w_in, hg_lower_bound, hg_norm_w, rel_bias, w_out, ln1_g, ln1_b, w_router,
           router_bias, w_e_gate, w_e_up, w_e_down, w_sh_gate, w_sh_up, w_sh_down, ln2_g, ln2_b):
    bsz, seq, d = x.shape
    t = bsz * seq
    assert w_ada.shape[0] == DEPTH and seq % (ATT_BRANCHES[-1][0]) == 0
    x2 = x.reshape(t, d)
    bias = _bias_tables(rel_bias)
    for l in range(DEPTH):
        mod4 = _modulation(c, w_ada[l], b_ada[l]).reshape(bsz, 6, 1, d)
        w_in_bf = _cast_bf16(w_in[l], 256)
        hq, hf, hi, hg, *qkv = _in_projection(x2, mod4, w_in_bf, seq, 512)
        nbr = len(ATT_BRANCHES)
        y_hg = _hgrn2(hq, hf, hi, hg, hg_lower_bound, hg_norm_w[l], bsz, seq)
        os_, ls_ = [], []
        for g, (_, dil) in enumerate(ATT_BRANCHES):
            o, lse = _dilated_attention(qkv[g], qkv[nbr + g], qkv[2 * nbr + g], bias[g], bsz, seq, dil)
            os_.append(o)
            ls_.append(lse)
        w_out_bf = _cast_bf16(w_out[l], 256)
        x1, h2b, h2p, logits_t = _out_projection(y_hg, os_, ls_, x2, mod4, w_out_bf, ln1_g[l], ln1_b[l],
                                                 w_router[l].T, seq, 512)
        eidx, wk, rank, cnt = _route(logits_t, router_bias[l], 512)
        dest, blk_seq, dexp, meta, n_blocks = _dispatch_plan(cnt, eidx, rank, 2048)
        row_token, row_weight = _row_tables(dest, wk, n_blocks * MOE_BLOCK)
        shared = _shared_expert(h2b, _cast_bf16(w_sh_gate[l], 256), _cast_bf16(w_sh_up[l], 256),
                                _cast_bf16(w_sh_down[l], 256), 1024)
        y_rows = _routed_experts(h2p, row_token, row_weight, blk_seq, dexp, meta, n_blocks,
                                 w_e_gate[l], w_e_up[l], w_e_down[l])
        x2 = _combine_final(x1, shared, mod4, dest, y_rows, ln2_g[l], ln2_b[l], seq)
    return x2.reshape(bsz, seq, d)
```
